```python
import jax, jax.numpy as jnp
from jax import lax
import numpy as np

D_MODEL = 1024
BATCH = 8
SEQ = 2048
DEPTH = 2
DEC_BATCH = 128
DEC_SEQ = 8
PAST_LEN = 16384
PAGE_SIZE = 128

S5_WIDTH = D_MODEL // 2
S5_GROUP_CH = 16
S5_GROUPS = S5_WIDTH // S5_GROUP_CH
S5_STATE = 64
POOL_WIDTH = D_MODEL - S5_WIDTH
POOL_WINDOWS = (2, 4, 8, 16)
POOL_GROUPS = len(POOL_WINDOWS)
POOL_GROUP_CH = POOL_WIDTH // POOL_GROUPS
POOL_BUF = max(POOL_WINDOWS) - 1
MIX_WIDTH = S5_WIDTH + POOL_WIDTH
N_EXPERT_GROUPS = 4
EXPERTS_PER_GROUP = 4
N_EXPERTS = N_EXPERT_GROUPS * EXPERTS_PER_GROUP
TOP_K_INNER = 2
D_EXPERT = D_MODEL // 4
DT_MIN = 0.001
DT_MAX = 0.1
EPS = 1e-6

kernel_name = "hymba_s5_pool_hmoe_step"


def rms_norm(x, g):
    x32 = x.astype(jnp.float32)
    y = x32 * lax.rsqrt(jnp.mean(x32 * x32, axis=-1, keepdims=True) + EPS)
    return (y * g.astype(jnp.float32)).astype(x.dtype)


def _scan_combine(e1, e2):
    a1r, a1i, b1r, b1i = e1
    a2r, a2i, b2r, b2i = e2
    return (a2r * a1r - a2i * a1i,
            a2r * a1i + a2i * a1r,
            a2r * b1r - a2i * b1i + b2r,
            a2r * b1i + a2i * b1r + b2i)


def s5_mixer(u, h0_re, h0_im, lam_re, lam_im, log_dt, b_re, b_im, c_re, c_im, d_skip, w_glu, b_glu):
    f32 = jnp.float32
    bsz, t, _ = u.shape
    u32 = u.astype(f32)
    ug = u32.reshape(bsz, t, S5_GROUPS, S5_GROUP_CH)
    lr = lam_re.astype(f32)
    li = lam_im.astype(f32)
    dt = jnp.exp(log_dt.astype(f32))[:, None]
    mag = jnp.exp(lr * dt)
    ab_re = mag * jnp.cos(li * dt)
    ab_im = mag * jnp.sin(li * dt)
    den = lr * lr + li * li
    nr = ab_re - 1.0
    coef_re = (nr * lr + ab_im * li) / den
    coef_im = (ab_im * lr - nr * li) / den
    br = b_re.astype(f32)
    bi = b_im.astype(f32)
    bb_re = coef_re[..., None] * br - coef_im[..., None] * bi
    bb_im = coef_re[..., None] * bi + coef_im[..., None] * br
    x_re = jnp.einsum('btgc,gnc->btgn', ug, bb_re)
    x_im = jnp.einsum('btgc,gnc->btgn', ug, bb_im)
    h0r = h0_re.astype(f32)
    h0i = h0_im.astype(f32)
    x_re = x_re.at[:, 0].add(ab_re * h0r - ab_im * h0i)
    x_im = x_im.at[:, 0].add(ab_re * h0i + ab_im * h0r)
    a_re = jnp.broadcast_to(ab_re, x_re.shape)
    a_im = jnp.broadcast_to(ab_im, x_im.shape)
    _, _, s_re, s_im = lax.associative_scan(_scan_combine, (a_re, a_im, x_re, x_im), axis=1)
    y = (jnp.einsum('btgn,gcn->btgc', s_re, c_re.astype(f32))
         - jnp.einsum('btgn,gcn->btgc', s_im, c_im.astype(f32)))
    y = y.reshape(bsz, t, S5_WIDTH) + d_skip.astype(f32) * u32
    g = jax.nn.gelu(y)
    out = g * jax.nn.sigmoid(g @ w_glu.astype(f32) + b_glu.astype(f32))
    return out.astype(u.dtype), s_re[:, -1], s_im[:, -1]


def pool_mixer(z, buf, w_pool, pool_scale):
    f32 = jnp.float32
    bsz, t, _ = z.shape
    full = z if buf is None else jnp.concatenate([buf.astype(z.dtype), z], axis=1)
    p = full.shape[1] - t
    full32 = full.astype(f32)
    cs = jnp.concatenate([jnp.zeros((bsz, 1, POOL_WIDTH), f32), jnp.cumsum(full32, axis=1)], axis=1)
    pos = np.arange(t)
    hi = p + pos + 1
    outs = []
    for gi, w in enumerate(POOL_WINDOWS):
        lo = np.maximum(hi - w, 0)
        cnt = (hi - lo).astype(np.float32)
        csg = cs[..., gi * POOL_GROUP_CH:(gi + 1) * POOL_GROUP_CH]
        mean = (csg[:, hi] - csg[:, lo]) / cnt[None, :, None]
        outs.append(mean - full32[:, p:, gi * POOL_GROUP_CH:(gi + 1) * POOL_GROUP_CH])
    pooled = jnp.stack(outs, axis=2)
    mixed = jnp.einsum('btgc,gcd->btgd', pooled, w_pool.astype(f32)).reshape(bsz, t, POOL_WIDTH)
    out = mixed * pool_scale.astype(f32)
    return out.astype(z.dtype), full[:, -POOL_BUF:]


def hier_moe(h, w_grp, b_grp, w_erouter, b_erouter, w1, w3, w2):
    shp = h.shape
    hf = h.reshape(-1, D_MODEL)
    f32 = jnp.float32
    gp = jax.nn.softmax((hf @ w_grp + b_grp).astype(f32), axis=-1)
    g_w, g_idx = lax.top_k(gp, 1)
    el = (hf @ w_erouter + b_erouter).astype(f32).reshape(-1, N_EXPERT_GROUPS, EXPERTS_PER_GROUP)
    el_sel = jnp.take_along_axis(el, g_idx[:, :, None], axis=1)[:, 0]
    ep = jax.nn.softmax(el_sel, axis=-1)
    e_w, e_idx = lax.top_k(ep, TOP_K_INNER)
    e_w = e_w / jnp.sum(e_w, axis=-1, keepdims=True)
    gate = g_w * e_w
    glob = g_idx * EXPERTS_PER_GROUP + e_idx
    comb = jnp.sum(jax.nn.one_hot(glob, N_EXPERTS, dtype=f32) * gate[..., None], axis=1)
    hid = jax.nn.silu(jnp.einsum('nd,edf->nef', hf, w1)) * jnp.einsum('nd,edf->nef', hf, w3)
    hid = hid * comb[:, :, None].astype(hid.dtype)
    out = jnp.einsum('nef,efd->nd', hid, w2)
    return out.reshape(shp)


def _layer(x, h_re, h_im, pool_buf, norm_mix, w_in, lam_re, lam_im, log_dt, b_re, b_im, c_re, c_im,
           d_skip, w_glu, b_glu, w_pool, pool_scale, w_out, norm_ffn, w_grp, b_grp, w_erouter,
           b_erouter, w1, w3, w2):
    h = rms_norm(x, norm_mix)
    proj = h @ w_in
    u = proj[..., :S5_WIDTH]
    z = proj[..., S5_WIDTH:]
    y_s5, n_re, n_im = s5_mixer(u, h_re, h_im, lam_re, lam_im, log_dt, b_re, b_im, c_re, c_im,
                                d_skip, w_glu, b_glu)
    y_pool, n_buf = pool_mixer(z, pool_buf, w_pool, pool_scale)
    x = x + jnp.concatenate([y_s5, y_pool], axis=-1) @ w_out
    x = x + hier_moe(rms_norm(x, norm_ffn), w_grp, b_grp, w_erouter, b_erouter, w1, w3, w2)
    return x, n_re, n_im, n_buf


def setup_inputs(seed: int = 0) -> dict:
    key = jax.random.key(seed)
    ks = jax.random.split(key, 32)
    f32 = jnp.float32

    def nrm(k, shape, s):
        return s * jax.random.normal(k, shape, f32)

    n_idx = jnp.arange(S5_STATE, dtype=f32)
    G, N, C = S5_GROUPS, S5_STATE, S5_GROUP_CH
    return {
        'x_prompt': nrm(ks[0], (BATCH, SEQ, D_MODEL), 1.0),
        'x_sample': nrm(ks[1], (DEC_BATCH, DEC_SEQ, D_MODEL), 1.0),
        'state_ssm_re': nrm(ks[2], (DEPTH, DEC_BATCH, G, N), 0.1),
        'state_ssm_im': nrm(ks[3], (DEPTH, DEC_BATCH, G, N), 0.1),
        'state_pool': nrm(ks[4], (DEPTH, DEC_BATCH, POOL_BUF, POOL_WIDTH), 1.0),
        'norm_mix': 1.0 + nrm(ks[5], (DEPTH, D_MODEL), 0.02),
        'w_in': nrm(ks[6], (DEPTH, D_MODEL, MIX_WIDTH), D_MODEL ** -0.5),
        'lam_re': -0.5 + nrm(ks[7], (DEPTH, G, N), 0.01),
        'lam_im': jnp.pi * n_idx + nrm(ks[8], (DEPTH, G, N), 0.01),
        'log_dt': jax.random.uniform(ks[9], (DEPTH, G), f32, float(np.log(DT_MIN)), float(np.log(DT_MAX))),
        'b_re': nrm(ks[10], (DEPTH, G, N, C), (2.0 * C) ** -0.5),
        'b_im': nrm(ks[11], (DEPTH, G, N, C), (2.0 * C) ** -0.5),
        'c_re': nrm(ks[12], (DEPTH, G, C, N), (2.0 * N) ** -0.5 * 4.0),
        'c_im': nrm(ks[13], (DEPTH, G, C, N), (2.0 * N) ** -0.5 * 4.0),
        'd_skip': nrm(ks[14], (DEPTH, S5_WIDTH), 1.0),
        'w_glu': nrm(ks[15], (DEPTH, S5_WIDTH, S5_WIDTH), S5_WIDTH ** -0.5),
        'b_glu': nrm(ks[16], (DEPTH, S5_WIDTH), 0.02),
        'w_pool': nrm(ks[17], (DEPTH, POOL_GROUPS, POOL_GROUP_CH, POOL_GROUP_CH), POOL_GROUP_CH ** -0.5),
        'pool_scale': 1.0 + nrm(ks[18], (DEPTH, POOL_WIDTH), 0.1),
        'w_out': nrm(ks[19], (DEPTH, MIX_WIDTH, D_MODEL), MIX_WIDTH ** -0.5),
        'norm_ffn': 1.0 + nrm(ks[20], (DEPTH, D_MODEL), 0.02),
        'w_grp': nrm(ks[21], (DEPTH, D_MODEL, N_EXPERT_GROUPS), D_MODEL ** -0.5),
        'b_grp': nrm(ks[22], (DEPTH, N_EXPERT_GROUPS), 0.01),
        'w_erouter': nrm(ks[23], (DEPTH, D_MODEL, N_EXPERTS), D_MODEL ** -0.5),
        'b_erouter': nrm(ks[24], (DEPTH, N_EXPERTS), 0.01),
        'w1': nrm(ks[25], (DEPTH, N_EXPERTS, D_MODEL, D_EXPERT), D_MODEL ** -0.5),
        'w3': nrm(ks[26], (DEPTH, N_EXPERTS, D_MODEL, D_EXPERT), D_MODEL ** -0.5),
        'w2': nrm(ks[27], (DEPTH, N_EXPERTS, D_EXPERT, D_MODEL), D_EXPERT ** -0.5),
        'norm_final': 1.0 + nrm(ks[28], (D_MODEL,), 0.02),
    }


def reference(x_prompt, x_sample, state_ssm_re, state_ssm_im, state_pool, norm_mix, w_in, lam_re,
              lam_im, log_dt, b_re, b_im, c_re, c_im, d_skip, w_glu, b_glu, w_pool, pool_scale, w_out,
              norm_ffn, w_grp, b_grp, w_erouter, b_erouter, w1, w3, w2, norm_final):
    stacked = (norm_mix, w_in, lam_re, lam_im, log_dt, b_re, b_im, c_re, c_im, d_skip, w_glu, b_glu,
               w_pool, pool_scale, w_out, norm_ffn, w_grp, b_grp, w_erouter, b_erouter, w1, w3, w2)
    h_zero = jnp.zeros((x_prompt.shape[0], S5_GROUPS, S5_STATE), jnp.float32)
    xp = x_prompt
    xs = x_sample
    p_re, p_im, p_pool, s_re, s_im, s_pool = [], [], [], [], [], []
    for l in range(DEPTH):
        lp = [a[l] for a in stacked]
        xp, a_re, a_im, a_buf = _layer(xp, h_zero, h_zero, None, *lp)
        xs, c_re_n, c_im_n, c_buf = _layer(xs, state_ssm_re[l], state_ssm_im[l], state_pool[l], *lp)
        p_re.append(a_re)
        p_im.append(a_im)
        p_pool.append(a_buf)
        s_re.append(c_re_n)
        s_im.append(c_im_n)
        s_pool.append(c_buf)
    y_prompt = rms_norm(xp, norm_final)
    y_sample = rms_norm(xs, norm_final)
    new_ssm_re_prompt = jnp.stack(p_re)
    new_ssm_im_prompt = jnp.stack(p_im)
    new_pool_prompt = jnp.stack(p_pool)
    new_ssm_re_sample = jnp.stack(s_re)
    new_ssm_im_sample = jnp.stack(s_im)
    new_pool_sample = jnp.stack(s_pool)
    return (y_prompt, y_sample, new_ssm_re_prompt, new_ssm_im_prompt, new_pool_prompt,
            new_ssm_re_sample, new_ssm_im_sample, new_pool_sample)
```

```python
import functools

import jax
import jax.numpy as jnp
from jax import lax
from jax.experimental import pallas as pl
from jax.experimental.pallas import tpu as pltpu

D_MODEL = 1024
S5_WIDTH = 512
S5_GROUP_CH = 16
S5_GROUPS = 32
S5_STATE = 64
S5_LANES = S5_GROUPS * S5_STATE
POOL_WIDTH = 512
POOL_WINDOWS = (2, 4, 8, 16)
POOL_GROUP_CH = 128
POOL_BUF = 15
POOL_HIST = 16
N_EXPERTS = 16
EXPERTS_PER_GROUP = 4
N_EXPERT_GROUPS = 4
D_EXPERT = 256
EPS = 1e-6

SUBLANES = 8
LANES = 128
S5_KBLOCK = 128
S5_NBLOCK = S5_KBLOCK // S5_GROUP_CH * S5_STATE
SCAN_LANES = 512
VMEM_LIMIT = 56 * 1024 * 1024

F32 = jnp.float32
BF16 = jnp.bfloat16


def _rms(x, g):
    return x * lax.rsqrt(jnp.mean(x * x, axis=-1, keepdims=True) + EPS) * g


def _gelu_tanh(x):
    return 0.5 * x * (1.0 + jnp.tanh(0.7978845608028654 * (x + 0.044715 * (x * x * x))))


def _disc_kernel(lr_ref, li_ref, ldt_ref, br_ref, bi_ref, are_ref, aim_ref, bbr_ref, bbi_ref):
    lr = lr_ref[...]
    li = li_ref[...]
    dt = jnp.exp(ldt_ref[...])
    mag = jnp.exp(lr * dt)
    ab_re = mag * jnp.cos(li * dt)
    ab_im = mag * jnp.sin(li * dt)
    den = lr * lr + li * li
    nr = ab_re - 1.0
    coef_re = (nr * lr + ab_im * li) / den
    coef_im = (ab_im * lr - nr * li) / den
    br = br_ref[...]
    bi = bi_ref[...]
    are_ref[...] = ab_re
    aim_ref[...] = ab_im
    bbr_ref[...] = coef_re * br - coef_im * bi
    bbi_ref[...] = coef_re * bi + coef_im * br


def _discretise(lam_re, lam_im, log_dt, b_re, b_im):
    depth = lam_re.shape[0]
    rows = depth * S5_GROUPS * S5_GROUP_CH
    rep = lambda a: jnp.repeat(a.reshape(depth * S5_GROUPS, -1), S5_GROUP_CH, axis=0)
    tr = lambda b: jnp.transpose(b, (0, 1, 3, 2)).reshape(rows, S5_STATE)
    out = jax.ShapeDtypeStruct((rows, S5_STATE), F32)
    are, aim, bbr, bbi = pl.pallas_call(
        _disc_kernel, out_shape=(out, out, out, out), name="s5_discretise",
    )(rep(lam_re), rep(lam_im), rep(log_dt[..., None]), tr(b_re), tr(b_im))
    shp = (depth, S5_GROUPS, S5_GROUP_CH, S5_STATE)
    are = are.reshape(shp)[:, :, 0, :].reshape(depth, 1, S5_LANES)
    aim = aim.reshape(shp)[:, :, 0, :].reshape(depth, 1, S5_LANES)
    return are, aim, bbr.reshape(shp), bbi.reshape(shp)


def _blockdiag(w, rows_first):
    depth = w.shape[0]
    gpb = S5_KBLOCK // S5_GROUP_CH
    w5 = w.reshape(depth, S5_GROUPS // gpb, gpb, S5_GROUP_CH, S5_STATE)
    eye = jnp.eye(gpb, dtype=w.dtype)
    if rows_first:
        return jnp.einsum('lkgcn,gh->lkgchn', w5, eye).reshape(depth, -1, S5_KBLOCK, S5_NBLOCK)
    return jnp.einsum('lkgcn,gh->lkgnhc', w5, eye).reshape(depth, -1, S5_NBLOCK, S5_KBLOCK)


def _mixer_kernel(x_ref, h0re_ref, h0im_ref, hist_ref, gmix_ref, win_ref, wx_ref, are_ref, aim_ref,
                  cm_ref, dskip_ref, wglu_ref, bglu_ref, wpool_ref, pscale_ref, wout_ref,
                  x1_ref, hre_out, him_out, pool_out,
                  xre, xim, hre, him, zbuf, *, bsz, tc, past):
    i = pl.program_id(1)
    last = pl.num_programs(1) - 1
    rows = bsz * tc
    hist_rows = POOL_HIST * bsz

    @pl.when(i == 0)
    def _():
        hre[...] = h0re_ref[0]
        him[...] = h0im_ref[0]
        zbuf[0:hist_rows, :] = hist_ref[0]

    x = x_ref[0]
    h = _rms(x, gmix_ref[...])
    proj = jnp.dot(h.astype(BF16), win_ref[...], preferred_element_type=F32)
    u = proj[:, :S5_WIDTH]
    z = proj[:, S5_WIDTH:]

    ub = u.astype(BF16)
    nblk = S5_WIDTH // S5_KBLOCK
    for kb in range(nblk):
        xx = jnp.dot(ub[:, kb * S5_KBLOCK:(kb + 1) * S5_KBLOCK], wx_ref[kb],
                     preferred_element_type=F32)
        xre[:, kb * S5_NBLOCK:(kb + 1) * S5_NBLOCK] = xx[:, :S5_NBLOCK]
        xim[:, kb * S5_NBLOCK:(kb + 1) * S5_NBLOCK] = xx[:, S5_NBLOCK:]

    def scan_block(rb, lb):
        ls = slice(lb * SCAN_LANES, (lb + 1) * SCAN_LANES)
        ar = jnp.broadcast_to(are_ref[:, ls], (SUBLANES, SCAN_LANES))
        ai = jnp.broadcast_to(aim_ref[:, ls], (SUBLANES, SCAN_LANES))
        r0 = pl.multiple_of(rb * SUBLANES, SUBLANES)

        def step(t, carry):
            hr, hi = carry
            row = pl.multiple_of(t * bsz + r0, SUBLANES)
            nr = ar * hr - ai * hi + xre[pl.ds(row, SUBLANES), ls]
            ni = ar * hi + ai * hr + xim[pl.ds(row, SUBLANES), ls]
            xre[pl.ds(row, SUBLANES), ls] = nr
            xim[pl.ds(row, SUBLANES), ls] = ni
            return nr, ni

        hr, hi = lax.fori_loop(0, tc, step, (hre[pl.ds(r0, SUBLANES), ls], him[pl.ds(r0, SUBLANES), ls]),
                               unroll=8)
        hre[pl.ds(r0, SUBLANES), ls] = hr
        him[pl.ds(r0, SUBLANES), ls] = hi

    for lb in range(S5_LANES // SCAN_LANES):
        if bsz == SUBLANES:
            scan_block(0, lb)
        else:
            def body(rb, c, lb=lb):
                scan_block(rb, lb)
                return c
            lax.fori_loop(0, bsz // SUBLANES, body, 0)

    ys = []
    for kb in range(nblk):
        ns = slice(kb * S5_NBLOCK, (kb + 1) * S5_NBLOCK)
        ys.append(jnp.dot(xre[:, ns].astype(BF16), cm_ref[kb, :S5_NBLOCK, :], preferred_element_type=F32)
                  + jnp.dot(xim[:, ns].astype(BF16), cm_ref[kb, S5_NBLOCK:, :], preferred_element_type=F32))
    y = jnp.concatenate(ys, axis=-1) + dskip_ref[...] * u
    g = _gelu_tanh(y)
    s5_out = g * jax.nn.sigmoid(jnp.dot(g.astype(BF16), wglu_ref[...], preferred_element_type=F32)
                                + bglu_ref[...])

    zbuf[hist_rows:hist_rows + rows, :] = z
    step_idx = i * tc + lax.broadcasted_iota(jnp.int32, (rows, 1), 0) // bsz
    pos = (step_idx + (past + 1)).astype(F32)
    pouts = []
    for gi, w in enumerate(POOL_WINDOWS):
        gs = slice(gi * POOL_GROUP_CH, (gi + 1) * POOL_GROUP_CH)
        s = zbuf[:, gs]
        k = 1
        while k < w:
            s = s[k * bsz:] + s[:-k * bsz]
            k *= 2
        inv_cnt = 1.0 / jnp.minimum(pos, float(w))
        pooled = s[-rows:] * inv_cnt - z[:, gs]
        pouts.append(jnp.dot(pooled.astype(BF16), wpool_ref[gi], preferred_element_type=F32))
    pool_mixed = jnp.concatenate(pouts, axis=-1) * pscale_ref[...]

    mix = jnp.concatenate([s5_out, pool_mixed], axis=-1).astype(BF16)
    x1_ref[0] = x + jnp.dot(mix, wout_ref[...], preferred_element_type=F32)

    @pl.when(i == last)
    def _():
        hre_out[0] = hre[...]
        him_out[0] = him[...]
        pool_out[0] = zbuf[(tc + POOL_HIST - POOL_BUF) * bsz:(tc + POOL_HIST) * bsz, :]

    @pl.when(i != last)
    def _():
        zbuf[0:hist_rows, :] = zbuf[tc * bsz:(tc + POOL_HIST) * bsz, :]


def _mixer(x, h0re, h0im, hist, lw, *, bsz, tc, past):
    nb, trows, _ = x.shape
    rows = bsz * tc
    nchunks = trows // rows
    const = lambda shape: pl.BlockSpec(shape, lambda b, i: (0,) * len(shape))
    perb = lambda shape: pl.BlockSpec(shape, lambda b, i: (b,) + (0,) * (len(shape) - 1))
    in_specs = [
        pl.BlockSpec((1, rows, D_MODEL), lambda b, i: (b, i, 0)),
        perb((1, bsz, S5_LANES)), perb((1, bsz, S5_LANES)), perb((1, POOL_HIST * bsz, POOL_WIDTH)),
        const((1, D_MODEL)), const((D_MODEL, D_MODEL)),
        const((S5_WIDTH // S5_KBLOCK, S5_KBLOCK, 2 * S5_NBLOCK)),
        const((1, S5_LANES)), const((1, S5_LANES)),
        const((S5_WIDTH // S5_KBLOCK, 2 * S5_NBLOCK, S5_KBLOCK)),
        const((1, S5_WIDTH)), const((S5_WIDTH, S5_WIDTH)), const((1, S5_WIDTH)),
        const((len(POOL_WINDOWS), POOL_GROUP_CH, POOL_GROUP_CH)), const((1, POOL_WIDTH)),
        const((D_MODEL, D_MODEL)),
    ]
    out_specs = [
        pl.BlockSpec((1, rows, D_MODEL), lambda b, i: (b, i, 0)),
        perb((1, bsz, S5_LANES)), perb((1, bsz, S5_LANES)), perb((1, POOL_BUF * bsz, POOL_WIDTH)),
    ]
    out_shape = [
        jax.ShapeDtypeStruct((nb, trows, D_MODEL), F32),
        jax.ShapeDtypeStruct((nb, bsz, S5_LANES), F32),
        jax.ShapeDtypeStruct((nb, bsz, S5_LANES), F32),
        jax.ShapeDtypeStruct((nb, POOL_BUF * bsz, POOL_WIDTH), F32),
    ]
    scratch = [
        pltpu.VMEM((rows, S5_LANES), F32), pltpu.VMEM((rows, S5_LANES), F32),
        pltpu.VMEM((bsz, S5_LANES), F32), pltpu.VMEM((bsz, S5_LANES), F32),
        pltpu.VMEM(((tc + POOL_HIST) * bsz, POOL_WIDTH), F32),
    ]
    return pl.pallas_call(
        functools.partial(_mixer_kernel, bsz=bsz, tc=tc, past=past),
        grid=(nb, nchunks), in_specs=in_specs, out_specs=out_specs, out_shape=out_shape,
        scratch_shapes=scratch, name="mixer",
        compiler_params=pltpu.CompilerParams(dimension_semantics=("arbitrary", "arbitrary"),
                                             vmem_limit_bytes=VMEM_LIMIT),
    )(x, h0re, h0im, hist, lw['norm_mix'], lw['w_in'], lw['wx'], lw['are'], lw['aim'], lw['cm'],
      lw['d_skip'], lw['w_glu'], lw['b_glu'], lw['w_pool'], lw['pool_scale'], lw['w_out'])


def _gating(logits):
    r = logits.shape[0]
    lane = lax.broadcasted_iota(jnp.int32, (r, LANES), 1)
    lanef = lane.astype(F32)
    neg = -jnp.inf
    lg = jnp.where(lane < N_EXPERT_GROUPS, logits[:, :LANES], neg)
    gmax = jnp.max(lg, axis=-1, keepdims=True)
    g_w = 1.0 / jnp.sum(jnp.exp(lg - gmax), axis=-1, keepdims=True)
    g_idx = jnp.min(jnp.where(lg == gmax, lanef, float(LANES)), axis=-1, keepdims=True)
    in_group = (lane // EXPERTS_PER_GROUP).astype(F32) == g_idx
    el = jnp.where(jnp.logical_and(in_group, lane < N_EXPERTS), logits[:, LANES:], neg)
    m1 = jnp.max(el, axis=-1, keepdims=True)
    i1 = jnp.min(jnp.where(el == m1, lanef, float(LANES)), axis=-1, keepdims=True)
    el2 = jnp.where(lanef == i1, neg, el)
    m2 = jnp.max(el2, axis=-1, keepdims=True)
    i2 = jnp.min(jnp.where(el2 == m2, lanef, float(LANES)), axis=-1, keepdims=True)
    e2 = jnp.exp(m2 - m1)
    den = 1.0 + e2
    gate1 = g_w / den
    gate2 = g_w * e2 / den
    return jnp.where(lanef == i1, gate1, 0.0) + jnp.where(lanef == i2, gate2, 0.0)


def _moe_kernel(x1_ref, gffn_ref, wrh_ref, wrl_ref, br_ref, w1_ref, w3_ref, w2_ref, gfin_ref,
                out_ref, h2_s, comb_s, acc_s, *, final):
    e = pl.program_id(1)

    @pl.when(e == 0)
    def _():
        x = x1_ref[...]
        h2 = _rms(x, gffn_ref[...])
        hi = h2.astype(BF16)
        lo = (h2 - hi.astype(F32)).astype(BF16)
        logits = (jnp.dot(hi, wrh_ref[...], preferred_element_type=F32)
                  + jnp.dot(hi, wrl_ref[...], preferred_element_type=F32)
                  + jnp.dot(lo, wrh_ref[...], preferred_element_type=F32)) + br_ref[...]
        h2_s[...] = hi
        comb_s[...] = _gating(logits)
        acc_s[...] = x

    h2b = h2_s[...]
    a = jnp.dot(h2b, w1_ref[0], preferred_element_type=F32)
    b = jnp.dot(h2b, w3_ref[0], preferred_element_type=F32)
    lane = lax.broadcasted_iota(jnp.int32, comb_s.shape, 1)
    ce = jnp.sum(jnp.where(lane == e, comb_s[...], 0.0), axis=-1, keepdims=True)
    hid = (a * jax.nn.sigmoid(a)) * b * ce
    acc_s[...] += jnp.dot(hid.astype(BF16), w2_ref[0], preferred_element_type=F32)

    @pl.when(e == N_EXPERTS - 1)
    def _():
        acc = acc_s[...]
        out_ref[...] = _rms(acc, gfin_ref[...]) if final else acc


def _moe(x1, lw, norm_final, *, tile, final):
    n = x1.shape[0]
    const = lambda shape: pl.BlockSpec(shape, lambda i, e: (0,) * len(shape))
    in_specs = [
        pl.BlockSpec((tile, D_MODEL), lambda i, e: (i, 0)),
        const((1, D_MODEL)), const((D_MODEL, 2 * LANES)), const((D_MODEL, 2 * LANES)), const((1, 2 * LANES)),
        pl.BlockSpec((1, D_MODEL, D_EXPERT), lambda i, e: (e, 0, 0)),
        pl.BlockSpec((1, D_MODEL, D_EXPERT), lambda i, e: (e, 0, 0)),
        pl.BlockSpec((1, D_EXPERT, D_MODEL), lambda i, e: (e, 0, 0)),
        const((1, D_MODEL)),
    ]
    return pl.pallas_call(
        functools.partial(_moe_kernel, final=final),
        grid=(n // tile, N_EXPERTS), in_specs=in_specs,
        out_specs=pl.BlockSpec((tile, D_MODEL), lambda i, e: (i, 0)),
        out_shape=jax.ShapeDtypeStruct((n, D_MODEL), F32),
        scratch_shapes=[pltpu.VMEM((tile, D_MODEL), BF16), pltpu.VMEM((tile, LANES), F32),
                        pltpu.VMEM((tile, D_MODEL), F32)],
        name="moe",
        compiler_params=pltpu.CompilerParams(dimension_semantics=("arbitrary", "arbitrary"),
                                             vmem_limit_bytes=VMEM_LIMIT),
    )(x1, lw['norm_ffn'], lw['wr_hi'], lw['wr_lo'], lw['b_r'], lw['w1'], lw['w3'], lw['w2'], norm_final)


def _router_weights(w_grp, b_grp, w_er, b_er):
    w = jnp.zeros((D_MODEL, 2 * LANES), F32)
    w = w.at[:, :N_EXPERT_GROUPS].set(w_grp).at[:, LANES:LANES + N_EXPERTS].set(w_er)
    b = jnp.zeros((1, 2 * LANES), F32)
    b = b.at[0, :N_EXPERT_GROUPS].set(b_grp).at[0, LANES:LANES + N_EXPERTS].set(b_er)
    hi = w.astype(BF16)
    lo = (w - hi.astype(F32)).astype(BF16)
    return hi, lo, b


def _to_time_major(x, nb):
    bt, t, w = x.shape
    return jnp.transpose(x.reshape(nb, bt // nb, t, w), (0, 2, 1, 3)).reshape(nb, t * (bt // nb), w)


def _from_time_major(x, t):
    nb, rows, w = x.shape
    bsz = rows // t
    return jnp.transpose(x.reshape(nb, t, bsz, w), (0, 2, 1, 3)).reshape(nb * bsz, t, w)


def kernel(x_prompt, x_sample, state_ssm_re, state_ssm_im, state_pool, norm_mix, w_in, lam_re, lam_im, log_dt, b_re, b_im, c_re, c_im, d_skip, w_glu, b_glu, w_pool, pool_scale, w_out, norm_ffn, w_grp, b_grp, w_erouter, b_erouter, w1, w3, w2, norm_final):
    depth = w_in.shape[0]
    pb, pt, _ = x_prompt.shape
    sb, st, _ = x_sample.shape
    s_nb = 2
    s_bsz = sb // s_nb
    p_tc = 64

    are, aim, bbr, bbi = _discretise(lam_re, lam_im, log_dt, b_re, b_im)
    wx = jnp.concatenate([_blockdiag(bbr, True), _blockdiag(bbi, True)], axis=-1).astype(BF16)
    cm = jnp.concatenate([_blockdiag(c_re, False), _blockdiag(-c_im, False)], axis=-2).astype(BF16)

    layers = []
    for l in range(depth):
        wr_hi, wr_lo, b_r = _router_weights(w_grp[l], b_grp[l], w_erouter[l], b_erouter[l])
        layers.append(dict(
            norm_mix=norm_mix[l][None], w_in=w_in[l].astype(BF16), wx=wx[l], are=are[l], aim=aim[l],
            cm=cm[l], d_skip=d_skip[l][None], w_glu=w_glu[l].astype(BF16), b_glu=b_glu[l][None],
            w_pool=w_pool[l].astype(BF16), pool_scale=pool_scale[l][None], w_out=w_out[l].astype(BF16),
            norm_ffn=norm_ffn[l][None], wr_hi=wr_hi, wr_lo=wr_lo, b_r=b_r,
            w1=w1[l].astype(BF16), w3=w3[l].astype(BF16), w2=w2[l].astype(BF16)))
    nfin = norm_final[None]

    xp = _to_time_major(x_prompt, 1)
    xs = _to_time_major(x_sample, s_nb)
    p_zero_h = jnp.zeros((1, pb, S5_LANES), F32)
    p_zero_hist = jnp.zeros((1, POOL_HIST * pb, POOL_WIDTH), F32)

    outs = {k: [] for k in ('p_re', 'p_im', 'p_pool', 's_re', 's_im', 's_pool')}
    for l, lw in enumerate(layers):
        final = l == depth - 1
        xp1, hre, him, pool = _mixer(xp, p_zero_h, p_zero_h, p_zero_hist, lw, bsz=pb, tc=p_tc, past=0)
        xp = _moe(xp1.reshape(-1, D_MODEL), lw, nfin, tile=1024, final=final).reshape(xp.shape)
        outs['p_re'].append(hre.reshape(pb, S5_GROUPS, S5_STATE))
        outs['p_im'].append(him.reshape(pb, S5_GROUPS, S5_STATE))
        outs['p_pool'].append(_from_time_major(pool, POOL_BUF))
        h0re = state_ssm_re[l].reshape(s_nb, s_bsz, S5_LANES)
        h0im = state_ssm_im[l].reshape(s_nb, s_bsz, S5_LANES)
        hist = jnp.pad(_to_time_major(state_pool[l], s_nb), ((0, 0), ((POOL_HIST - POOL_BUF) * s_bsz, 0), (0, 0)))
        xs1, hre, him, pool = _mixer(xs, h0re, h0im, hist, lw, bsz=s_bsz, tc=st, past=POOL_BUF)
        xs = _moe(xs1.reshape(-1, D_MODEL), lw, nfin, tile=1024, final=final).reshape(xs.shape)
        outs['s_re'].append(hre.reshape(sb, S5_GROUPS, S5_STATE))
        outs['s_im'].append(him.reshape(sb, S5_GROUPS, S5_STATE))
        outs['s_pool'].append(_from_time_major(pool, POOL_BUF))

    y_prompt = _from_time_major(xp, pt)
    y_sample = _from_time_major(xs, st)
    return (y_prompt, y_sample, jnp.stack(outs['p_re']), jnp.stack(outs['p_im']), jnp.stack(outs['p_pool']),
            jnp.stack(outs['s_re']), jnp.stack(outs['s_im']), jnp.stack(outs['s_pool']))
```

```python
import functools

import jax
import jax.numpy as jnp
from jax import lax
from jax.experimental import pallas as pl
from jax.experimental.pallas import tpu as pltpu

D_MODEL = 1024
S5_WIDTH = 512
S5_GROUP_CH = 16
S5_GROUPS = 32
S5_STATE = 64
S5_LANES = S5_GROUPS * S5_STATE
POOL_WIDTH = 512
POOL_WINDOWS = (2, 4, 8, 16)
POOL_GROUP_CH = 128
POOL_BUF = 15
POOL_HIST = 16
N_EXPERTS = 16
EXPERTS_PER_GROUP = 4
N_EXPERT_GROUPS = 4
D_EXPERT = 256
EPS = 1e-6

SUBLANES = 8
LANES = 128
S5_KBLOCK = 128
S5_NBLOCK = S5_KBLOCK // S5_GROUP_CH * S5_STATE
SCAN_LANES = 512
VMEM_LIMIT = 56 * 1024 * 1024

TILE = 512
GRAN = 32
TILE_GRANS = TILE // GRAN + N_EXPERT_GROUPS
SORTED_ROWS = TILE_GRANS * GRAN
ITEM_GRANS = 16
ITEM_ROWS = ITEM_GRANS * GRAN
PAYLOAD = D_MODEL + 2 * LANES

F32 = jnp.float32
BF16 = jnp.bfloat16


def _rms(x, g):
    return x * lax.rsqrt(jnp.mean(x * x, axis=-1, keepdims=True) + EPS) * g


def _gelu_tanh(x):
    return 0.5 * x * (1.0 + jnp.tanh(0.7978845608028654 * (x + 0.044715 * (x * x * x))))


def _disc_kernel(lr_ref, li_ref, ldt_ref, br_ref, bi_ref, are_ref, aim_ref, bbr_ref, bbi_ref):
    lr = lr_ref[...]
    li = li_ref[...]
    dt = jnp.exp(ldt_ref[...])
    mag = jnp.exp(lr * dt)
    ab_re = mag * jnp.cos(li * dt)
    ab_im = mag * jnp.sin(li * dt)
    den = lr * lr + li * li
    nr = ab_re - 1.0
    coef_re = (nr * lr + ab_im * li) / den
    coef_im = (ab_im * lr - nr * li) / den
    br = br_ref[...]
    bi = bi_ref[...]
    are_ref[...] = ab_re
    aim_ref[...] = ab_im
    bbr_ref[...] = coef_re * br - coef_im * bi
    bbi_ref[...] = coef_re * bi + coef_im * br


def _discretise(lam_re, lam_im, log_dt, b_re, b_im):
    depth = lam_re.shape[0]
    rows = depth * S5_GROUPS * S5_GROUP_CH
    rep = lambda a: jnp.repeat(a.reshape(depth * S5_GROUPS, -1), S5_GROUP_CH, axis=0)
    tr = lambda b: jnp.transpose(b, (0, 1, 3, 2)).reshape(rows, S5_STATE)
    out = jax.ShapeDtypeStruct((rows, S5_STATE), F32)
    are, aim, bbr, bbi = pl.pallas_call(
        _disc_kernel, out_shape=(out, out, out, out), name="s5_discretise",
    )(rep(lam_re), rep(lam_im), rep(log_dt[..., None]), tr(b_re), tr(b_im))
    shp = (depth, S5_GROUPS, S5_GROUP_CH, S5_STATE)
    are = are.reshape(shp)[:, :, 0, :].reshape(depth, 1, S5_LANES)
    aim = aim.reshape(shp)[:, :, 0, :].reshape(depth, 1, S5_LANES)
    return are, aim, bbr.reshape(shp), bbi.reshape(shp)


def _blockdiag(w, rows_first):
    depth = w.shape[0]
    gpb = S5_KBLOCK // S5_GROUP_CH
    w5 = w.reshape(depth, S5_GROUPS // gpb, gpb, S5_GROUP_CH, S5_STATE)
    eye = jnp.eye(gpb, dtype=w.dtype)
    if rows_first:
        return jnp.einsum('lkgcn,gh->lkgchn', w5, eye).reshape(depth, -1, S5_KBLOCK, S5_NBLOCK)
    return jnp.einsum('lkgcn,gh->lkgnhc', w5, eye).reshape(depth, -1, S5_NBLOCK, S5_KBLOCK)


def _mixer_kernel(x_ref, h0re_ref, h0im_ref, hist_ref, gmix_ref, win_ref, wx_ref, are_ref, aim_ref,
                  cm_ref, dskip_ref, wglu_ref, bglu_ref, wpool_ref, pscale_ref, wout_ref,
                  *rest, bsz, tc, past):
    x1_ref, hre_out, him_out, pool_out, xre, xim, hre, him, zbuf = rest[-9:]
    i = pl.program_id(1)
    last = pl.num_programs(1) - 1
    rows = bsz * tc
    hist_rows = POOL_HIST * bsz

    @pl.when(i == 0)
    def _():
        hre[...] = h0re_ref[0]
        him[...] = h0im_ref[0]
        zbuf[0:hist_rows, :] = hist_ref[0]

    x = x_ref[...]
    h = _rms(x, gmix_ref[...])
    proj = jnp.dot(h.astype(BF16), win_ref[...], preferred_element_type=F32)
    u = proj[:, :S5_WIDTH]
    z = proj[:, S5_WIDTH:]

    ub = u.astype(BF16)
    nblk = S5_WIDTH // S5_KBLOCK
    for kb in range(nblk):
        xx = jnp.dot(ub[:, kb * S5_KBLOCK:(kb + 1) * S5_KBLOCK], wx_ref[kb],
                     preferred_element_type=F32)
        xre[:, kb * S5_NBLOCK:(kb + 1) * S5_NBLOCK] = xx[:, :S5_NBLOCK]
        xim[:, kb * S5_NBLOCK:(kb + 1) * S5_NBLOCK] = xx[:, S5_NBLOCK:]

    def scan_block(rb, lb):
        ls = slice(lb * SCAN_LANES, (lb + 1) * SCAN_LANES)
        ar = jnp.broadcast_to(are_ref[:, ls], (SUBLANES, SCAN_LANES))
        ai = jnp.broadcast_to(aim_ref[:, ls], (SUBLANES, SCAN_LANES))
        r0 = pl.multiple_of(rb * SUBLANES, SUBLANES)

        def step(t, carry):
            hr, hi = carry
            row = pl.multiple_of(t * bsz + r0, SUBLANES)
            nr = ar * hr - ai * hi + xre[pl.ds(row, SUBLANES), ls]
            ni = ar * hi + ai * hr + xim[pl.ds(row, SUBLANES), ls]
            xre[pl.ds(row, SUBLANES), ls] = nr
            xim[pl.ds(row, SUBLANES), ls] = ni
            return nr, ni

        hr, hi = lax.fori_loop(0, tc, step, (hre[pl.ds(r0, SUBLANES), ls], him[pl.ds(r0, SUBLANES), ls]),
                               unroll=8)
        hre[pl.ds(r0, SUBLANES), ls] = hr
        him[pl.ds(r0, SUBLANES), ls] = hi

    for lb in range(S5_LANES // SCAN_LANES):
        if bsz == SUBLANES:
            scan_block(0, lb)
        else:
            def body(rb, c, lb=lb):
                scan_block(rb, lb)
                return c
            lax.fori_loop(0, bsz // SUBLANES, body, 0)

    ys = []
    for kb in range(nblk):
        ns = slice(kb * S5_NBLOCK, (kb + 1) * S5_NBLOCK)
        ys.append(jnp.dot(xre[:, ns].astype(BF16), cm_ref[kb, :S5_NBLOCK, :], preferred_element_type=F32)
                  + jnp.dot(xim[:, ns].astype(BF16), cm_ref[kb, S5_NBLOCK:, :], preferred_element_type=F32))
    y = jnp.concatenate(ys, axis=-1) + dskip_ref[...] * u
    g = _gelu_tanh(y)
    s5_out = g * jax.nn.sigmoid(jnp.dot(g.astype(BF16), wglu_ref[...], preferred_element_type=F32)
                                + bglu_ref[...])

    zbuf[hist_rows:hist_rows + rows, :] = z
    step_idx = i * tc + lax.broadcasted_iota(jnp.int32, (rows, 1), 0) // bsz
    pos = (step_idx + (past + 1)).astype(F32)
    pouts = []
    for gi, w in enumerate(POOL_WINDOWS):
        gs = slice(gi * POOL_GROUP_CH, (gi + 1) * POOL_GROUP_CH)
        s = zbuf[:, gs]
        k = 1
        while k < w:
            s = s[k * bsz:] + s[:-k * bsz]
            k *= 2
        inv_cnt = 1.0 / jnp.minimum(pos, float(w))
        pooled = s[-rows:] * inv_cnt - z[:, gs]
        pouts.append(jnp.dot(pooled.astype(BF16), wpool_ref[gi], preferred_element_type=F32))
    pool_mixed = jnp.concatenate(pouts, axis=-1) * pscale_ref[...]

    mix = jnp.concatenate([s5_out, pool_mixed], axis=-1).astype(BF16)
    x1_ref[...] = x + jnp.dot(mix, wout_ref[...], preferred_element_type=F32)

    @pl.when(i == last)
    def _():
        hre_out[0] = hre[...]
        him_out[0] = him[...]
        pool_out[0] = zbuf[(tc + POOL_HIST - POOL_BUF) * bsz:(tc + POOL_HIST) * bsz, :]

    @pl.when(i != last)
    def _():
        zbuf[0:hist_rows, :] = zbuf[tc * bsz:(tc + POOL_HIST) * bsz, :]


def _mixer(x, h0re, h0im, hist, lw, *, bsz, tc, past, nb, nchunks, tile_off, into=None):
    rows = bsz * tc
    assert rows == TILE
    ntok = x.shape[0]
    const = lambda shape: pl.BlockSpec(shape, lambda b, i: (0,) * len(shape))
    perb = lambda shape: pl.BlockSpec(shape, lambda b, i: (b,) + (0,) * (len(shape) - 1))
    tok = pl.BlockSpec((rows, D_MODEL), lambda b, i: (tile_off + b * nchunks + i, 0))
    in_specs = [
        tok,
        perb((1, bsz, S5_LANES)), perb((1, bsz, S5_LANES)), perb((1, POOL_HIST * bsz, POOL_WIDTH)),
        const((1, D_MODEL)), const((D_MODEL, D_MODEL)),
        const((S5_WIDTH // S5_KBLOCK, S5_KBLOCK, 2 * S5_NBLOCK)),
        const((1, S5_LANES)), const((1, S5_LANES)),
        const((S5_WIDTH // S5_KBLOCK, 2 * S5_NBLOCK, S5_KBLOCK)),
        const((1, S5_WIDTH)), const((S5_WIDTH, S5_WIDTH)), const((1, S5_WIDTH)),
        const((len(POOL_WINDOWS), POOL_GROUP_CH, POOL_GROUP_CH)), const((1, POOL_WIDTH)),
        const((D_MODEL, D_MODEL)),
    ]
    args = [x, h0re, h0im, hist, lw['norm_mix'], lw['w_in'], lw['wx'], lw['are'], lw['aim'], lw['cm'],
            lw['d_skip'], lw['w_glu'], lw['b_glu'], lw['w_pool'], lw['pool_scale'], lw['w_out']]
    aliases = {}
    if into is not None:
        in_specs.append(pl.BlockSpec(memory_space=pl.ANY))
        args.append(into)
        aliases = {len(args) - 1: 0}
    out_specs = [
        tok,
        perb((1, bsz, S5_LANES)), perb((1, bsz, S5_LANES)), perb((1, POOL_BUF * bsz, POOL_WIDTH)),
    ]
    out_shape = [
        jax.ShapeDtypeStruct((ntok, D_MODEL), F32),
        jax.ShapeDtypeStruct((nb, bsz, S5_LANES), F32),
        jax.ShapeDtypeStruct((nb, bsz, S5_LANES), F32),
        jax.ShapeDtypeStruct((nb, POOL_BUF * bsz, POOL_WIDTH), F32),
    ]
    scratch = [
        pltpu.VMEM((rows, S5_LANES), F32), pltpu.VMEM((rows, S5_LANES), F32),
        pltpu.VMEM((bsz, S5_LANES), F32), pltpu.VMEM((bsz, S5_LANES), F32),
        pltpu.VMEM(((tc + POOL_HIST) * bsz, POOL_WIDTH), F32),
    ]
    return pl.pallas_call(
        functools.partial(_mixer_kernel, bsz=bsz, tc=tc, past=past),
        grid=(nb, nchunks), in_specs=in_specs, out_specs=out_specs, out_shape=out_shape,
        scratch_shapes=scratch, name="mixer", input_output_aliases=aliases,
        compiler_params=pltpu.CompilerParams(dimension_semantics=("arbitrary", "arbitrary"),
                                             vmem_limit_bytes=VMEM_LIMIT),
    )(*args)


def _gating(logits):
    r = logits.shape[0]
    lane = lax.broadcasted_iota(jnp.int32, (r, LANES), 1)
    lanef = lane.astype(F32)
    neg = -jnp.inf
    lg = jnp.where(lane < N_EXPERT_GROUPS, logits[:, :LANES], neg)
    gmax = jnp.max(lg, axis=-1, keepdims=True)
    g_w = 1.0 / jnp.sum(jnp.exp(lg - gmax), axis=-1, keepdims=True)
    g_idx = jnp.min(jnp.where(lg == gmax, lanef, float(LANES)), axis=-1, keepdims=True)
    in_group = (lane // EXPERTS_PER_GROUP).astype(F32) == g_idx
    el = jnp.where(jnp.logical_and(in_group, lane < N_EXPERTS), logits[:, LANES:], neg)
    m1 = jnp.max(el, axis=-1, keepdims=True)
    i1 = jnp.min(jnp.where(el == m1, lanef, float(LANES)), axis=-1, keepdims=True)
    el2 = jnp.where(lanef == i1, neg, el)
    m2 = jnp.max(el2, axis=-1, keepdims=True)
    i2 = jnp.min(jnp.where(el2 == m2, lanef, float(LANES)), axis=-1, keepdims=True)
    e2 = jnp.exp(m2 - m1)
    den = 1.0 + e2
    gate1 = g_w / den
    gate2 = g_w * e2 / den
    comb = jnp.where(lanef == i1, gate1, 0.0) + jnp.where(lanef == i2, gate2, 0.0)
    return comb, g_idx


def _route_sort_kernel(x1_ref, gffn_ref, wrh_ref, wrl_ref, br_ref, sorted_ref, dest_ref, gid_ref):
    x = x1_ref[...]
    h2 = _rms(x, gffn_ref[...])
    hi = h2.astype(BF16)
    lo = (h2 - hi.astype(F32)).astype(BF16)
    logits = (jnp.dot(hi, wrh_ref[...], preferred_element_type=F32)
              + jnp.dot(hi, wrl_ref[...], preferred_element_type=F32)
              + jnp.dot(lo, wrh_ref[...], preferred_element_type=F32)) + br_ref[...]
    comb, g_idx = _gating(logits)

    lanef = lax.broadcasted_iota(jnp.int32, (TILE, LANES), 1).astype(F32)
    onehot = jnp.where(lanef == g_idx, 1.0, 0.0)
    ri = lax.broadcasted_iota(jnp.int32, (TILE, TILE), 0)
    ci = lax.broadcasted_iota(jnp.int32, (TILE, TILE), 1)
    earlier = jnp.where(ci < ri, 1.0, 0.0).astype(BF16)
    rank = jnp.dot(earlier, onehot.astype(BF16), preferred_element_type=F32)
    counts = jnp.sum(onehot, axis=0, keepdims=True)
    padded = jnp.floor((counts + float(GRAN - 1)) * (1.0 / GRAN)) * float(GRAN)
    e0 = padded[:, 0:1]
    e1 = e0 + padded[:, 1:2]
    e2 = e1 + padded[:, 2:3]
    e3 = e2 + padded[:, 3:4]
    lane1 = lax.broadcasted_iota(jnp.int32, (1, LANES), 1)
    base = (jnp.where(lane1 == 1, e0, 0.0) + jnp.where(lane1 == 2, e1, 0.0)
            + jnp.where(lane1 == 3, e2, 0.0))
    dest = jnp.sum(onehot * (base + rank), axis=-1, keepdims=True)
    dest_ref[...] = dest

    dest_row = jnp.sum(jnp.where(ri == ci, dest, 0.0), axis=0, keepdims=True)
    slot = lax.broadcasted_iota(jnp.int32, (SORTED_ROWS, TILE), 0).astype(F32)
    perm = jnp.where(slot == dest_row, 1.0, 0.0).astype(BF16)
    c_hi = comb.astype(BF16)
    c_lo = (comb - c_hi.astype(F32)).astype(BF16)
    payload = jnp.concatenate([hi, c_hi, c_lo], axis=-1)
    sorted_ref[...] = jnp.dot(perm, payload, preferred_element_type=F32).astype(BF16)

    q = lane1.astype(F32) * float(GRAN)
    gid = (jnp.where(q >= e0, 1, 0) + jnp.where(q >= e1, 1, 0)
           + jnp.where(q >= e2, 1, 0) + jnp.where(q >= e3, 1, 0))
    gid_ref[0] = gid.astype(jnp.int32)


def _route_sort(x1, lw):
    ntiles = x1.shape[0] // TILE
    const = lambda shape: pl.BlockSpec(shape, lambda i: (0,) * len(shape))
    return pl.pallas_call(
        _route_sort_kernel, grid=(ntiles,),
        in_specs=[pl.BlockSpec((TILE, D_MODEL), lambda i: (i, 0)), const((1, D_MODEL)),
                  const((D_MODEL, 2 * LANES)), const((D_MODEL, 2 * LANES)), const((1, 2 * LANES))],
        out_specs=[pl.BlockSpec((SORTED_ROWS, PAYLOAD), lambda i: (i, 0)),
                   pl.BlockSpec((TILE, 1), lambda i: (i, 0)),
                   pl.BlockSpec((1, 1, LANES), lambda i: (i, 0, 0))],
        out_shape=[jax.ShapeDtypeStruct((ntiles * SORTED_ROWS, PAYLOAD), BF16),
                   jax.ShapeDtypeStruct((x1.shape[0], 1), F32),
                   jax.ShapeDtypeStruct((ntiles, 1, LANES), jnp.int32)],
        name="moe_route_sort",
        compiler_params=pltpu.CompilerParams(dimension_semantics=("arbitrary",),
                                             vmem_limit_bytes=VMEM_LIMIT),
    )(x1, lw['norm_ffn'], lw['wr_hi'], lw['wr_lo'], lw['b_r'])


def _experts_kernel(src_ref, wgid_ref, wvalid_ref, *refs):
    gran_refs = refs[:ITEM_GRANS]
    w1_ref, w3_ref, w2_ref, out_ref = refs[ITEM_GRANS:]
    j = pl.program_id(0)

    @pl.when(wvalid_ref[j] > 0)
    def _():
        full = jnp.concatenate([g[...] for g in gran_refs], axis=0)
        h = full[:, :D_MODEL]
        comb = (full[:, D_MODEL:D_MODEL + LANES].astype(F32)
                + full[:, D_MODEL + LANES:].astype(F32))
        first = wgid_ref[j] * EXPERTS_PER_GROUP
        lane = lax.broadcasted_iota(jnp.int32, (ITEM_ROWS, LANES), 1)
        hids = []
        for e in range(EXPERTS_PER_GROUP):
            a = jnp.dot(h, w1_ref[e], preferred_element_type=F32)
            b = jnp.dot(h, w3_ref[e], preferred_element_type=F32)
            gate = jnp.sum(jnp.where(lane == first + e, comb, 0.0), axis=-1, keepdims=True)
            hids.append(((a * jax.nn.sigmoid(a)) * b * gate).astype(BF16))
        hid = jnp.concatenate(hids, axis=-1)
        w2g = w2_ref[...].reshape(EXPERTS_PER_GROUP * D_EXPERT, D_MODEL)
        out = jnp.dot(hid, w2g, preferred_element_type=F32)
        o_hi = out.astype(BF16)
        o_lo = (out - o_hi.astype(F32)).astype(BF16)
        out_ref[...] = jnp.concatenate([o_hi, o_lo], axis=-1)

    @pl.when(wvalid_ref[j] == 0)
    def _():
        out_ref[...] = jnp.zeros(out_ref.shape, out_ref.dtype)


def _experts(sorted_tok, src, wgid, wvalid, lw):
    nitems = wgid.shape[0]
    gran_spec = lambda s: pl.BlockSpec((GRAN, PAYLOAD), lambda j, src, wg, wv: (src[j * ITEM_GRANS + s], 0))
    wspec = lambda shape: pl.BlockSpec(shape, lambda j, src, wg, wv: (wg[j], 0, 0))
    grid_spec = pltpu.PrefetchScalarGridSpec(
        num_scalar_prefetch=3, grid=(nitems,),
        in_specs=[gran_spec(s) for s in range(ITEM_GRANS)] + [
            wspec((EXPERTS_PER_GROUP, D_MODEL, D_EXPERT)), wspec((EXPERTS_PER_GROUP, D_MODEL, D_EXPERT)),
            wspec((EXPERTS_PER_GROUP, D_EXPERT, D_MODEL))],
        out_specs=pl.BlockSpec((ITEM_ROWS, 2 * D_MODEL), lambda j, src, wg, wv: (j, 0)))
    return pl.pallas_call(
        _experts_kernel, grid_spec=grid_spec,
        out_shape=jax.ShapeDtypeStruct((nitems * ITEM_ROWS, 2 * D_MODEL), BF16),
        name="moe_experts",
        compiler_params=pltpu.CompilerParams(dimension_semantics=("arbitrary",),
                                             vmem_limit_bytes=VMEM_LIMIT),
    )(src, wgid, wvalid, *([sorted_tok] * ITEM_GRANS), lw['w1'], lw['w3'], lw['w2'])


def _unsort_kernel(pos_ref, x1_ref, dest_ref, *refs, final):
    gran_refs = refs[:TILE_GRANS]
    gfin_ref, out_ref = refs[TILE_GRANS:]
    s = jnp.concatenate([g[...] for g in gran_refs], axis=0)
    slot = lax.broadcasted_iota(jnp.int32, (TILE, SORTED_ROWS), 1).astype(F32)
    perm_t = jnp.where(slot == dest_ref[...], 1.0, 0.0).astype(BF16)
    r = jnp.dot(perm_t, s, preferred_element_type=F32)
    y = x1_ref[...] + r[:, :D_MODEL] + r[:, D_MODEL:]
    out_ref[...] = _rms(y, gfin_ref[...]) if final else y


def _unsort(x1, dest, expert_out, pos, norm_final, *, final):
    ntiles = x1.shape[0] // TILE
    gran_spec = lambda q: pl.BlockSpec((GRAN, 2 * D_MODEL), lambda k, pos: (pos[k * TILE_GRANS + q], 0))
    grid_spec = pltpu.PrefetchScalarGridSpec(
        num_scalar_prefetch=1, grid=(ntiles,),
        in_specs=[pl.BlockSpec((TILE, D_MODEL), lambda k, pos: (k, 0)),
                  pl.BlockSpec((TILE, 1), lambda k, pos: (k, 0))]
                 + [gran_spec(q) for q in range(TILE_GRANS)]
                 + [pl.BlockSpec((1, D_MODEL), lambda k, pos: (0, 0))],
        out_specs=pl.BlockSpec((TILE, D_MODEL), lambda k, pos: (k, 0)))
    return pl.pallas_call(
        functools.partial(_unsort_kernel, final=final), grid_spec=grid_spec,
        out_shape=jax.ShapeDtypeStruct(x1.shape, F32), name="moe_unsort",
        compiler_params=pltpu.CompilerParams(dimension_semantics=("arbitrary",),
                                             vmem_limit_bytes=VMEM_LIMIT),
    )(pos, x1, dest, *([expert_out] * TILE_GRANS), norm_final)


def _dispatch_tables(gid):
    ntiles = gid.shape[0]
    ngran = ntiles * TILE_GRANS
    nitems = -(-ngran // ITEM_GRANS) + N_EXPERT_GROUPS
    g = gid[:, 0, :TILE_GRANS].reshape(ngran)
    onehot = (g[:, None] == jnp.arange(N_EXPERT_GROUPS, dtype=jnp.int32)[None, :]).astype(jnp.int32)
    valid = g < N_EXPERT_GROUPS
    rank = jnp.cumsum(onehot, axis=0) - onehot
    cnt = jnp.sum(onehot, axis=0)
    pcnt = (cnt + (ITEM_GRANS - 1)) // ITEM_GRANS * ITEM_GRANS
    pend = jnp.cumsum(pcnt)
    pstart = pend - pcnt
    pos = jnp.sum(onehot * (pstart[None, :] + rank), axis=1)
    slots = jnp.arange(nitems * ITEM_GRANS, dtype=jnp.int32)
    hit = jnp.logical_and(pos[None, :] == slots[:, None], valid[None, :])
    src = jnp.sum(jnp.where(hit, jnp.arange(ngran, dtype=jnp.int32)[None, :], 0), axis=1)
    item_start = jnp.arange(nitems, dtype=jnp.int32) * ITEM_GRANS
    wgid = jnp.minimum(jnp.sum((item_start[:, None] >= pend[None, :]).astype(jnp.int32), axis=1),
                       N_EXPERT_GROUPS - 1)
    wvalid = (item_start < pend[-1]).astype(jnp.int32)
    return src.astype(jnp.int32), wgid.astype(jnp.int32), wvalid, jnp.where(valid, pos, 0).astype(jnp.int32)


def _moe(x1, lw, norm_final, *, final):
    sorted_tok, dest, gid = _route_sort(x1, lw)
    src, wgid, wvalid, pos = _dispatch_tables(gid)
    expert_out = _experts(sorted_tok, src, wgid, wvalid, lw)
    return _unsort(x1, dest, expert_out, pos, norm_final, final=final)


def _router_weights(w_grp, b_grp, w_er, b_er):
    w = jnp.zeros((D_MODEL, 2 * LANES), F32)
    w = w.at[:, :N_EXPERT_GROUPS].set(w_grp).at[:, LANES:LANES + N_EXPERTS].set(w_er)
    b = jnp.zeros((1, 2 * LANES), F32)
    b = b.at[0, :N_EXPERT_GROUPS].set(b_grp).at[0, LANES:LANES + N_EXPERTS].set(b_er)
    hi = w.astype(BF16)
    lo = (w - hi.astype(F32)).astype(BF16)
    return hi, lo, b


def _to_time_major(x, nb):
    bt, t, w = x.shape
    return jnp.transpose(x.reshape(nb, bt // nb, t, w), (0, 2, 1, 3)).reshape(nb, t * (bt // nb), w)


def _from_time_major(x, t):
    nb, rows, w = x.shape
    bsz = rows // t
    return jnp.transpose(x.reshape(nb, t, bsz, w), (0, 2, 1, 3)).reshape(nb * bsz, t, w)


def kernel(x_prompt, x_sample, state_ssm_re, state_ssm_im, state_pool, norm_mix, w_in, lam_re, lam_im, log_dt, b_re, b_im, c_re, c_im, d_skip, w_glu, b_glu, w_pool, pool_scale, w_out, norm_ffn, w_grp, b_grp, w_erouter, b_erouter, w1, w3, w2, norm_final):
    depth = w_in.shape[0]
    pb, pt, _ = x_prompt.shape
    sb, st, _ = x_sample.shape
    p_tc = TILE // pb
    p_tiles = pt // p_tc
    s_bsz = TILE // st
    s_nb = sb // s_bsz
    p_rows = pb * pt

    are, aim, bbr, bbi = _discretise(lam_re, lam_im, log_dt, b_re, b_im)
    wx = jnp.concatenate([_blockdiag(bbr, True), _blockdiag(bbi, True)], axis=-1).astype(BF16)
    cm = jnp.concatenate([_blockdiag(c_re, False), _blockdiag(-c_im, False)], axis=-2).astype(BF16)

    layers = []
    for l in range(depth):
        wr_hi, wr_lo, b_r = _router_weights(w_grp[l], b_grp[l], w_erouter[l], b_erouter[l])
        layers.append(dict(
            norm_mix=norm_mix[l][None], w_in=w_in[l].astype(BF16), wx=wx[l], are=are[l], aim=aim[l],
            cm=cm[l], d_skip=d_skip[l][None], w_glu=w_glu[l].astype(BF16), b_glu=b_glu[l][None],
            w_pool=w_pool[l].astype(BF16), pool_scale=pool_scale[l][None], w_out=w_out[l].astype(BF16),
            norm_ffn=norm_ffn[l][None], wr_hi=wr_hi, wr_lo=wr_lo, b_r=b_r,
            w1=w1[l].astype(BF16), w3=w3[l].astype(BF16), w2=w2[l].astype(BF16)))
    nfin = norm_final[None]

    x = jnp.concatenate([_to_time_major(x_prompt, 1).reshape(p_rows, D_MODEL),
                         _to_time_major(x_sample, s_nb).reshape(sb * st, D_MODEL)], axis=0)
    p_zero_h = jnp.zeros((1, pb, S5_LANES), F32)
    p_zero_hist = jnp.zeros((1, POOL_HIST * pb, POOL_WIDTH), F32)

    outs = {k: [] for k in ('p_re', 'p_im', 'p_pool', 's_re', 's_im', 's_pool')}
    for l, lw in enumerate(layers):
        x1, hre, him, pool = _mixer(x, p_zero_h, p_zero_h, p_zero_hist, lw, bsz=pb, tc=p_tc, past=0,
                                    nb=1, nchunks=p_tiles, tile_off=0)
        outs['p_re'].append(hre.reshape(pb, S5_GROUPS, S5_STATE))
        outs['p_im'].append(him.reshape(pb, S5_GROUPS, S5_STATE))
        outs['p_pool'].append(_from_time_major(pool, POOL_BUF))
        h0re = state_ssm_re[l].reshape(s_nb, s_bsz, S5_LANES)
        h0im = state_ssm_im[l].reshape(s_nb, s_bsz, S5_LANES)
        hist = jnp.pad(_to_time_major(state_pool[l], s_nb), ((0, 0), ((POOL_HIST - POOL_BUF) * s_bsz, 0), (0, 0)))
        x1, hre, him, pool = _mixer(x, h0re, h0im, hist, lw, bsz=s_bsz, tc=st, past=POOL_BUF,
                                    nb=s_nb, nchunks=1, tile_off=p_tiles, into=x1)
        outs['s_re'].append(hre.reshape(sb, S5_GROUPS, S5_STATE))
        outs['s_im'].append(him.reshape(sb, S5_GROUPS, S5_STATE))
        outs['s_pool'].append(_from_time_major(pool, POOL_BUF))
        x = _moe(x1, lw, nfin, final=(l == depth - 1))

    y_prompt = _from_time_major(x[:p_rows].reshape(1, p_rows, D_MODEL), pt)
    y_sample = _from_time_major(x[p_rows:].reshape(s_nb, s_bsz * st, D_MODEL), st)
    return (y_prompt, y_sample, jnp.stack(outs['p_re']), jnp.stack(outs['p_im']), jnp.stack(outs['p_pool']),
            jnp.stack(outs['s_re']), jnp.stack(outs['s_im']), jnp.stack(outs['s_pool']))
```

```python
import functools

import jax
import jax.numpy as jnp
from jax import lax
from jax.experimental import pallas as pl
from jax.experimental.pallas import tpu as pltpu

D_MODEL = 1024
S5_WIDTH = 512
S5_GROUP_CH = 16
S5_GROUPS = 32
S5_STATE = 64
S5_LANES = S5_GROUPS * S5_STATE
POOL_WIDTH = 512
POOL_WINDOWS = (2, 4, 8, 16)
POOL_GROUP_CH = 128
POOL_BUF = 15
POOL_HIST = 16
N_EXPERTS = 16
EXPERTS_PER_GROUP = 4
N_EXPERT_GROUPS = 4
D_EXPERT = 256
EPS = 1e-6

SUBLANES = 8
LANES = 128
S5_KBLOCK = 128
S5_NBLOCK = S5_KBLOCK // S5_GROUP_CH * S5_STATE
SCAN_LANES = 512
VMEM_LIMIT = 56 * 1024 * 1024

TILE = 512
GRAN = 32
TILE_GRANS = TILE // GRAN + N_EXPERT_GROUPS
SORTED_ROWS = TILE_GRANS * GRAN
ITEM_GRANS = 16
ITEM_ROWS = ITEM_GRANS * GRAN
PAYLOAD = D_MODEL + 2 * LANES

F32 = jnp.float32
BF16 = jnp.bfloat16


def _rms(x, g):
    return x * lax.rsqrt(jnp.mean(x * x, axis=-1, keepdims=True) + EPS) * g


def _gelu_tanh(x):
    return 0.5 * x * (1.0 + jnp.tanh(0.7978845608028654 * (x + 0.044715 * (x * x * x))))


def _disc_kernel(lr_ref, li_ref, ldt_ref, br_ref, bi_ref, are_ref, aim_ref, bbr_ref, bbi_ref):
    lr = lr_ref[...]
    li = li_ref[...]
    dt = jnp.exp(ldt_ref[...])
    mag = jnp.exp(lr * dt)
    ab_re = mag * jnp.cos(li * dt)
    ab_im = mag * jnp.sin(li * dt)
    den = lr * lr + li * li
    nr = ab_re - 1.0
    coef_re = (nr * lr + ab_im * li) / den
    coef_im = (ab_im * lr - nr * li) / den
    br = br_ref[...]
    bi = bi_ref[...]
    are_ref[...] = ab_re
    aim_ref[...] = ab_im
    bbr_ref[...] = coef_re * br - coef_im * bi
    bbi_ref[...] = coef_re * bi + coef_im * br


def _discretise(lam_re, lam_im, log_dt, b_re, b_im):
    depth = lam_re.shape[0]
    rows = depth * S5_GROUPS * S5_GROUP_CH
    rep = lambda a: jnp.repeat(a.reshape(depth * S5_GROUPS, -1), S5_GROUP_CH, axis=0)
    tr = lambda b: jnp.transpose(b, (0, 1, 3, 2)).reshape(rows, S5_STATE)
    out = jax.ShapeDtypeStruct((rows, S5_STATE), F32)
    are, aim, bbr, bbi = pl.pallas_call(
        _disc_kernel, out_shape=(out, out, out, out), name="s5_discretise",
    )(rep(lam_re), rep(lam_im), rep(log_dt[..., None]), tr(b_re), tr(b_im))
    shp = (depth, S5_GROUPS, S5_GROUP_CH, S5_STATE)
    are = are.reshape(shp)[:, :, 0, :].reshape(depth, 1, S5_LANES)
    aim = aim.reshape(shp)[:, :, 0, :].reshape(depth, 1, S5_LANES)
    return are, aim, bbr.reshape(shp), bbi.reshape(shp)


def _blockdiag(w, rows_first):
    depth = w.shape[0]
    gpb = S5_KBLOCK // S5_GROUP_CH
    w5 = w.reshape(depth, S5_GROUPS // gpb, gpb, S5_GROUP_CH, S5_STATE)
    eye = jnp.eye(gpb, dtype=w.dtype)
    if rows_first:
        return jnp.einsum('lkgcn,gh->lkgchn', w5, eye).reshape(depth, -1, S5_KBLOCK, S5_NBLOCK)
    return jnp.einsum('lkgcn,gh->lkgnhc', w5, eye).reshape(depth, -1, S5_NBLOCK, S5_KBLOCK)


def _mixer_kernel(x_ref, h0re_ref, h0im_ref, hist_ref, gmix_ref, win_ref, wx_ref, are_ref, aim_ref,
                  cm_ref, dskip_ref, wglu_ref, bglu_ref, wpool_ref, pscale_ref, wout_ref,
                  *rest, bsz, tc, past):
    x1_ref, hre_out, him_out, pool_out, xre, xim, hre, him, zbuf = rest[-9:]
    i = pl.program_id(1)
    last = pl.num_programs(1) - 1
    rows = bsz * tc
    hist_rows = POOL_HIST * bsz

    @pl.when(i == 0)
    def _():
        hre[...] = h0re_ref[0]
        him[...] = h0im_ref[0]
        zbuf[0:hist_rows, :] = hist_ref[0]

    if len(x_ref.shape) == 3:
        x = jnp.concatenate([x_ref[:, t, :] for t in range(tc)], axis=0)
    else:
        x = x_ref[...]
    h = _rms(x, gmix_ref[...])
    proj = jnp.dot(h.astype(BF16), win_ref[...], preferred_element_type=F32)
    u = proj[:, :S5_WIDTH]
    z = proj[:, S5_WIDTH:]

    ub = u.astype(BF16)
    nblk = S5_WIDTH // S5_KBLOCK
    for kb in range(nblk):
        xx = jnp.dot(ub[:, kb * S5_KBLOCK:(kb + 1) * S5_KBLOCK], wx_ref[kb],
                     preferred_element_type=F32)
        xre[:, kb * S5_NBLOCK:(kb + 1) * S5_NBLOCK] = xx[:, :S5_NBLOCK]
        xim[:, kb * S5_NBLOCK:(kb + 1) * S5_NBLOCK] = xx[:, S5_NBLOCK:]

    def scan_block(rb, lb):
        ls = slice(lb * SCAN_LANES, (lb + 1) * SCAN_LANES)
        ar = jnp.broadcast_to(are_ref[:, ls], (SUBLANES, SCAN_LANES))
        ai = jnp.broadcast_to(aim_ref[:, ls], (SUBLANES, SCAN_LANES))
        r0 = pl.multiple_of(rb * SUBLANES, SUBLANES)

        def step(t, carry):
            hr, hi = carry
            row = pl.multiple_of(t * bsz + r0, SUBLANES)
            nr = ar * hr - ai * hi + xre[pl.ds(row, SUBLANES), ls]
            ni = ar * hi + ai * hr + xim[pl.ds(row, SUBLANES), ls]
            xre[pl.ds(row, SUBLANES), ls] = nr
            xim[pl.ds(row, SUBLANES), ls] = ni
            return nr, ni

        hr, hi = lax.fori_loop(0, tc, step, (hre[pl.ds(r0, SUBLANES), ls], him[pl.ds(r0, SUBLANES), ls]),
                               unroll=8)
        hre[pl.ds(r0, SUBLANES), ls] = hr
        him[pl.ds(r0, SUBLANES), ls] = hi

    for lb in range(S5_LANES // SCAN_LANES):
        if bsz == SUBLANES:
            scan_block(0, lb)
        else:
            def body(rb, c, lb=lb):
                scan_block(rb, lb)
                return c
            lax.fori_loop(0, bsz // SUBLANES, body, 0)

    ys = []
    for kb in range(nblk):
        ns = slice(kb * S5_NBLOCK, (kb + 1) * S5_NBLOCK)
        ys.append(jnp.dot(xre[:, ns].astype(BF16), cm_ref[kb, :S5_NBLOCK, :], preferred_element_type=F32)
                  + jnp.dot(xim[:, ns].astype(BF16), cm_ref[kb, S5_NBLOCK:, :], preferred_element_type=F32))
    y = jnp.concatenate(ys, axis=-1) + dskip_ref[...] * u
    g = _gelu_tanh(y)
    s5_out = g * jax.nn.sigmoid(jnp.dot(g.astype(BF16), wglu_ref[...], preferred_element_type=F32)
                                + bglu_ref[...])

    zbuf[hist_rows:hist_rows + rows, :] = z
    step_idx = i * tc + lax.broadcasted_iota(jnp.int32, (rows, 1), 0) // bsz
    pos = (step_idx + (past + 1)).astype(F32)
    pouts = []
    for gi, w in enumerate(POOL_WINDOWS):
        gs = slice(gi * POOL_GROUP_CH, (gi + 1) * POOL_GROUP_CH)
        s = zbuf[:, gs]
        k = 1
        while k < w:
            s = s[k * bsz:] + s[:-k * bsz]
            k *= 2
        inv_cnt = 1.0 / jnp.minimum(pos, float(w))
        pooled = s[-rows:] * inv_cnt - z[:, gs]
        pouts.append(jnp.dot(pooled.astype(BF16), wpool_ref[gi], preferred_element_type=F32))
    pool_mixed = jnp.concatenate(pouts, axis=-1) * pscale_ref[...]

    mix = jnp.concatenate([s5_out, pool_mixed], axis=-1).astype(BF16)
    x1_ref[...] = x + jnp.dot(mix, wout_ref[...], preferred_element_type=F32)

    @pl.when(i == last)
    def _():
        hre_out[0] = hre[...]
        him_out[0] = him[...]
        pool_out[0] = zbuf[(tc + POOL_HIST - POOL_BUF) * bsz:(tc + POOL_HIST) * bsz, :]

    @pl.when(i != last)
    def _():
        zbuf[0:hist_rows, :] = zbuf[tc * bsz:(tc + POOL_HIST) * bsz, :]


def _mixer(x, h0re, h0im, hist, lw, *, bsz, tc, past, nb, nchunks, tile_off, ntok, into=None):
    rows = bsz * tc
    assert rows == TILE
    const = lambda shape: pl.BlockSpec(shape, lambda b, i: (0,) * len(shape))
    perb = lambda shape: pl.BlockSpec(shape, lambda b, i: (b,) + (0,) * (len(shape) - 1))
    tok = pl.BlockSpec((rows, D_MODEL), lambda b, i: (tile_off + b * nchunks + i, 0))
    in_specs = [
        tok if x.ndim == 2 else pl.BlockSpec((bsz, tc, D_MODEL), lambda b, i: (b, i, 0)),
        perb((1, bsz, S5_LANES)), perb((1, bsz, S5_LANES)), perb((1, POOL_HIST * bsz, POOL_WIDTH)),
        const((1, D_MODEL)), const((D_MODEL, D_MODEL)),
        const((S5_WIDTH // S5_KBLOCK, S5_KBLOCK, 2 * S5_NBLOCK)),
        const((1, S5_LANES)), const((1, S5_LANES)),
        const((S5_WIDTH // S5_KBLOCK, 2 * S5_NBLOCK, S5_KBLOCK)),
        const((1, S5_WIDTH)), const((S5_WIDTH, S5_WIDTH)), const((1, S5_WIDTH)),
        const((len(POOL_WINDOWS), POOL_GROUP_CH, POOL_GROUP_CH)), const((1, POOL_WIDTH)),
        const((D_MODEL, D_MODEL)),
    ]
    args = [x, h0re, h0im, hist, lw['norm_mix'], lw['w_in'], lw['wx'], lw['are'], lw['aim'], lw['cm'],
            lw['d_skip'], lw['w_glu'], lw['b_glu'], lw['w_pool'], lw['pool_scale'], lw['w_out']]
    aliases = {}
    if into is not None:
        in_specs.append(pl.BlockSpec(memory_space=pl.ANY))
        args.append(into)
        aliases = {len(args) - 1: 0}
    out_specs = [
        tok,
        perb((1, bsz, S5_LANES)), perb((1, bsz, S5_LANES)), perb((1, POOL_BUF * bsz, POOL_WIDTH)),
    ]
    out_shape = [
        jax.ShapeDtypeStruct((ntok, D_MODEL), F32),
        jax.ShapeDtypeStruct((nb, bsz, S5_LANES), F32),
        jax.ShapeDtypeStruct((nb, bsz, S5_LANES), F32),
        jax.ShapeDtypeStruct((nb, POOL_BUF * bsz, POOL_WIDTH), F32),
    ]
    scratch = [
        pltpu.VMEM((rows, S5_LANES), F32), pltpu.VMEM((rows, S5_LANES), F32),
        pltpu.VMEM((bsz, S5_LANES), F32), pltpu.VMEM((bsz, S5_LANES), F32),
        pltpu.VMEM(((tc + POOL_HIST) * bsz, POOL_WIDTH), F32),
    ]
    return pl.pallas_call(
        functools.partial(_mixer_kernel, bsz=bsz, tc=tc, past=past),
        grid=(nb, nchunks), in_specs=in_specs, out_specs=out_specs, out_shape=out_shape,
        scratch_shapes=scratch, name="mixer", input_output_aliases=aliases,
        compiler_params=pltpu.CompilerParams(dimension_semantics=("arbitrary", "arbitrary"),
                                             vmem_limit_bytes=VMEM_LIMIT),
    )(*args)


def _gating(logits):
    r = logits.shape[0]
    lane = lax.broadcasted_iota(jnp.int32, (r, LANES), 1)
    lanef = lane.astype(F32)
    neg = -jnp.inf
    lg = jnp.where(lane < N_EXPERT_GROUPS, logits[:, :LANES], neg)
    gmax = jnp.max(lg, axis=-1, keepdims=True)
    g_w = 1.0 / jnp.sum(jnp.exp(lg - gmax), axis=-1, keepdims=True)
    g_idx = jnp.min(jnp.where(lg == gmax, lanef, float(LANES)), axis=-1, keepdims=True)
    in_group = (lane // EXPERTS_PER_GROUP).astype(F32) == g_idx
    el = jnp.where(jnp.logical_and(in_group, lane < N_EXPERTS), logits[:, LANES:], neg)
    m1 = jnp.max(el, axis=-1, keepdims=True)
    i1 = jnp.min(jnp.where(el == m1, lanef, float(LANES)), axis=-1, keepdims=True)
    el2 = jnp.where(lanef == i1, neg, el)
    m2 = jnp.max(el2, axis=-1, keepdims=True)
    i2 = jnp.min(jnp.where(el2 == m2, lanef, float(LANES)), axis=-1, keepdims=True)
    e2 = jnp.exp(m2 - m1)
    den = 1.0 + e2
    gate1 = g_w / den
    gate2 = g_w * e2 / den
    comb = jnp.where(lanef == i1, gate1, 0.0) + jnp.where(lanef == i2, gate2, 0.0)
    return comb, g_idx


def _route_sort_kernel(x1_ref, gffn_ref, wrh_ref, wrl_ref, br_ref, sorted_ref, dest_ref, gid_ref):
    x = x1_ref[...]
    h2 = _rms(x, gffn_ref[...])
    hi = h2.astype(BF16)
    lo = (h2 - hi.astype(F32)).astype(BF16)
    logits = (jnp.dot(hi, wrh_ref[...], preferred_element_type=F32)
              + jnp.dot(hi, wrl_ref[...], preferred_element_type=F32)
              + jnp.dot(lo, wrh_ref[...], preferred_element_type=F32)) + br_ref[...]
    comb, g_idx = _gating(logits)

    lanef = lax.broadcasted_iota(jnp.int32, (TILE, LANES), 1).astype(F32)
    onehot = jnp.where(lanef == g_idx, 1.0, 0.0)
    ri = lax.broadcasted_iota(jnp.int32, (TILE, TILE), 0)
    ci = lax.broadcasted_iota(jnp.int32, (TILE, TILE), 1)
    earlier = jnp.where(ci < ri, 1.0, 0.0).astype(BF16)
    rank = jnp.dot(earlier, onehot.astype(BF16), preferred_element_type=F32)
    counts = jnp.sum(onehot, axis=0, keepdims=True)
    padded = jnp.floor((counts + float(GRAN - 1)) * (1.0 / GRAN)) * float(GRAN)
    e0 = padded[:, 0:1]
    e1 = e0 + padded[:, 1:2]
    e2 = e1 + padded[:, 2:3]
    e3 = e2 + padded[:, 3:4]
    lane1 = lax.broadcasted_iota(jnp.int32, (1, LANES), 1)
    base = (jnp.where(lane1 == 1, e0, 0.0) + jnp.where(lane1 == 2, e1, 0.0)
            + jnp.where(lane1 == 3, e2, 0.0))
    dest = jnp.sum(onehot * (base + rank), axis=-1, keepdims=True)
    dest_ref[...] = dest

    dest_row = jnp.sum(jnp.where(ri == ci, dest, 0.0), axis=0, keepdims=True)
    slot = lax.broadcasted_iota(jnp.int32, (SORTED_ROWS, TILE), 0).astype(F32)
    perm = jnp.where(slot == dest_row, 1.0, 0.0).astype(BF16)
    c_hi = comb.astype(BF16)
    c_lo = (comb - c_hi.astype(F32)).astype(BF16)
    payload = jnp.concatenate([hi, c_hi, c_lo], axis=-1)
    sorted_ref[...] = jnp.dot(perm, payload, preferred_element_type=F32).astype(BF16)

    q = lane1.astype(F32) * float(GRAN)
    gid = (jnp.where(q >= e0, 1, 0) + jnp.where(q >= e1, 1, 0)
           + jnp.where(q >= e2, 1, 0) + jnp.where(q >= e3, 1, 0))
    gid_ref[0] = gid.astype(jnp.int32)


def _route_sort(x1, lw):
    ntiles = x1.shape[0] // TILE
    const = lambda shape: pl.BlockSpec(shape, lambda i: (0,) * len(shape))
    return pl.pallas_call(
        _route_sort_kernel, grid=(ntiles,),
        in_specs=[pl.BlockSpec((TILE, D_MODEL), lambda i: (i, 0)), const((1, D_MODEL)),
                  const((D_MODEL, 2 * LANES)), const((D_MODEL, 2 * LANES)), const((1, 2 * LANES))],
        out_specs=[pl.BlockSpec((SORTED_ROWS, PAYLOAD), lambda i: (i, 0)),
                   pl.BlockSpec((TILE, 1), lambda i: (i, 0)),
                   pl.BlockSpec((1, 1, LANES), lambda i: (i, 0, 0))],
        out_shape=[jax.ShapeDtypeStruct((ntiles * SORTED_ROWS, PAYLOAD), BF16),
                   jax.ShapeDtypeStruct((x1.shape[0], 1), F32),
                   jax.ShapeDtypeStruct((ntiles, 1, LANES), jnp.int32)],
        name="moe_route_sort",
        compiler_params=pltpu.CompilerParams(dimension_semantics=("arbitrary",),
                                             vmem_limit_bytes=VMEM_LIMIT),
    )(x1, lw['norm_ffn'], lw['wr_hi'], lw['wr_lo'], lw['b_r'])


def _experts_kernel(src_ref, wgid_ref, wvalid_ref, *refs):
    gran_refs = refs[:ITEM_GRANS]
    w1_ref, w3_ref, w2_ref, out_ref = refs[ITEM_GRANS:]
    j = pl.program_id(0)

    @pl.when(wvalid_ref[j] > 0)
    def _():
        full = jnp.concatenate([g[...] for g in gran_refs], axis=0)
        h = full[:, :D_MODEL]
        comb = (full[:, D_MODEL:D_MODEL + LANES].astype(F32)
                + full[:, D_MODEL + LANES:].astype(F32))
        first = wgid_ref[j] * EXPERTS_PER_GROUP
        lane = lax.broadcasted_iota(jnp.int32, (ITEM_ROWS, LANES), 1)
        hids = []
        for e in range(EXPERTS_PER_GROUP):
            a = jnp.dot(h, w1_ref[e], preferred_element_type=F32)
            b = jnp.dot(h, w3_ref[e], preferred_element_type=F32)
            gate = jnp.sum(jnp.where(lane == first + e, comb, 0.0), axis=-1, keepdims=True)
            hids.append(((a * jax.nn.sigmoid(a)) * b * gate).astype(BF16))
        hid = jnp.concatenate(hids, axis=-1)
        w2g = w2_ref[...].reshape(EXPERTS_PER_GROUP * D_EXPERT, D_MODEL)
        out = jnp.dot(hid, w2g, preferred_element_type=F32)
        o_hi = out.astype(BF16)
        o_lo = (out - o_hi.astype(F32)).astype(BF16)
        out_ref[...] = jnp.concatenate([o_hi, o_lo], axis=-1)

    @pl.when(wvalid_ref[j] == 0)
    def _():
        out_ref[...] = jnp.zeros(out_ref.shape, out_ref.dtype)


def _experts(sorted_tok, src, wgid, wvalid, lw):
    nitems = wgid.shape[0]
    gran_spec = lambda s: pl.BlockSpec((GRAN, PAYLOAD), lambda j, src, wg, wv: (src[j * ITEM_GRANS + s], 0))
    wspec = lambda shape: pl.BlockSpec(shape, lambda j, src, wg, wv: (wg[j], 0, 0))
    grid_spec = pltpu.PrefetchScalarGridSpec(
        num_scalar_prefetch=3, grid=(nitems,),
        in_specs=[gran_spec(s) for s in range(ITEM_GRANS)] + [
            wspec((EXPERTS_PER_GROUP, D_MODEL, D_EXPERT)), wspec((EXPERTS_PER_GROUP, D_MODEL, D_EXPERT)),
            wspec((EXPERTS_PER_GROUP, D_EXPERT, D_MODEL))],
        out_specs=pl.BlockSpec((ITEM_ROWS, 2 * D_MODEL), lambda j, src, wg, wv: (j, 0)))
    return pl.pallas_call(
        _experts_kernel, grid_spec=grid_spec,
        out_shape=jax.ShapeDtypeStruct((nitems * ITEM_ROWS, 2 * D_MODEL), BF16),
        name="moe_experts",
        compiler_params=pltpu.CompilerParams(dimension_semantics=("arbitrary",),
                                             vmem_limit_bytes=VMEM_LIMIT),
    )(src, wgid, wvalid, *([sorted_tok] * ITEM_GRANS), lw['w1'], lw['w3'], lw['w2'])


def _unsort_kernel(pos_ref, x1_ref, dest_ref, *refs, final):
    gran_refs = refs[:TILE_GRANS]
    gfin_ref, out_ref = refs[TILE_GRANS:]
    s = jnp.concatenate([g[...] for g in gran_refs], axis=0)
    slot = lax.broadcasted_iota(jnp.int32, (TILE, SORTED_ROWS), 1).astype(F32)
    perm_t = jnp.where(slot == dest_ref[...], 1.0, 0.0).astype(BF16)
    r = jnp.dot(perm_t, s, preferred_element_type=F32)
    y = x1_ref[...] + r[:, :D_MODEL] + r[:, D_MODEL:]
    if not final:
        out_ref[...] = y
    else:
        y = _rms(y, gfin_ref[...])
        bsz, tc, _ = out_ref.shape
        for t in range(tc):
            out_ref[:, t, :] = y[t * bsz:(t + 1) * bsz, :]


def _unsort(x1, dest, expert_out, pos, norm_final, *, tile_off=0, ntiles=None, batch_major=None):
    final = batch_major is not None
    if final:
        batch, t_len, bsz, tc = batch_major
        nchunks = t_len // tc
        out_spec = pl.BlockSpec((bsz, tc, D_MODEL), lambda k, pos: (k // nchunks, k % nchunks, 0))
        out_shape = jax.ShapeDtypeStruct((batch, t_len, D_MODEL), F32)
    else:
        ntiles = x1.shape[0] // TILE
        out_spec = pl.BlockSpec((TILE, D_MODEL), lambda k, pos: (k, 0))
        out_shape = jax.ShapeDtypeStruct(x1.shape, F32)
    gran_spec = lambda q: pl.BlockSpec((GRAN, 2 * D_MODEL),
                                       lambda k, pos: (pos[(k + tile_off) * TILE_GRANS + q], 0))
    grid_spec = pltpu.PrefetchScalarGridSpec(
        num_scalar_prefetch=1, grid=(ntiles,),
        in_specs=[pl.BlockSpec((TILE, D_MODEL), lambda k, pos: (k + tile_off, 0)),
                  pl.BlockSpec((TILE, 1), lambda k, pos: (k + tile_off, 0))]
                 + [gran_spec(q) for q in range(TILE_GRANS)]
                 + [pl.BlockSpec((1, D_MODEL), lambda k, pos: (0, 0))],
        out_specs=out_spec)
    return pl.pallas_call(
        functools.partial(_unsort_kernel, final=final), grid_spec=grid_spec,
        out_shape=out_shape, name="moe_unsort",
        compiler_params=pltpu.CompilerParams(dimension_semantics=("arbitrary",),
                                             vmem_limit_bytes=VMEM_LIMIT),
    )(pos, x1, dest, *([expert_out] * TILE_GRANS), norm_final)


def _dispatch_tables(gid):
    ntiles = gid.shape[0]
    ngran = ntiles * TILE_GRANS
    nitems = -(-ngran // ITEM_GRANS) + N_EXPERT_GROUPS
    g = gid[:, 0, :TILE_GRANS].reshape(ngran)
    onehot = (g[:, None] == jnp.arange(N_EXPERT_GROUPS, dtype=jnp.int32)[None, :]).astype(jnp.int32)
    valid = g < N_EXPERT_GROUPS
    rank = jnp.cumsum(onehot, axis=0) - onehot
    cnt = jnp.sum(onehot, axis=0)
    pcnt = (cnt + (ITEM_GRANS - 1)) // ITEM_GRANS * ITEM_GRANS
    pend = jnp.cumsum(pcnt)
    pstart = pend - pcnt
    pos = jnp.sum(onehot * (pstart[None, :] + rank), axis=1)
    slots = jnp.arange(nitems * ITEM_GRANS, dtype=jnp.int32)
    hit = jnp.logical_and(pos[None, :] == slots[:, None], valid[None, :])
    src = jnp.sum(jnp.where(hit, jnp.arange(ngran, dtype=jnp.int32)[None, :], 0), axis=1)
    item_start = jnp.arange(nitems, dtype=jnp.int32) * ITEM_GRANS
    wgid = jnp.minimum(jnp.sum((item_start[:, None] >= pend[None, :]).astype(jnp.int32), axis=1),
                       N_EXPERT_GROUPS - 1)
    wvalid = (item_start < pend[-1]).astype(jnp.int32)
    return src.astype(jnp.int32), wgid.astype(jnp.int32), wvalid, jnp.where(valid, pos, 0).astype(jnp.int32)


def _moe(x1, lw, norm_final, final_layouts=None):
    sorted_tok, dest, gid = _route_sort(x1, lw)
    src, wgid, wvalid, pos = _dispatch_tables(gid)
    expert_out = _experts(sorted_tok, src, wgid, wvalid, lw)
    if final_layouts is None:
        return _unsort(x1, dest, expert_out, pos, norm_final)
    return [_unsort(x1, dest, expert_out, pos, norm_final, tile_off=off, ntiles=n, batch_major=bm)
            for off, n, bm in final_layouts]


def _router_weights(w_grp, b_grp, w_er, b_er):
    w = jnp.zeros((D_MODEL, 2 * LANES), F32)
    w = w.at[:, :N_EXPERT_GROUPS].set(w_grp).at[:, LANES:LANES + N_EXPERTS].set(w_er)
    b = jnp.zeros((1, 2 * LANES), F32)
    b = b.at[0, :N_EXPERT_GROUPS].set(b_grp).at[0, LANES:LANES + N_EXPERTS].set(b_er)
    hi = w.astype(BF16)
    lo = (w - hi.astype(F32)).astype(BF16)
    return hi, lo, b


def _to_time_major(x, nb):
    bt, t, w = x.shape
    return jnp.transpose(x.reshape(nb, bt // nb, t, w), (0, 2, 1, 3)).reshape(nb, t * (bt // nb), w)


def _from_time_major(x, t):
    nb, rows, w = x.shape
    bsz = rows // t
    return jnp.transpose(x.reshape(nb, t, bsz, w), (0, 2, 1, 3)).reshape(nb * bsz, t, w)


def kernel(x_prompt, x_sample, state_ssm_re, state_ssm_im, state_pool, norm_mix, w_in, lam_re, lam_im, log_dt, b_re, b_im, c_re, c_im, d_skip, w_glu, b_glu, w_pool, pool_scale, w_out, norm_ffn, w_grp, b_grp, w_erouter, b_erouter, w1, w3, w2, norm_final):
    depth = w_in.shape[0]
    pb, pt, _ = x_prompt.shape
    sb, st, _ = x_sample.shape
    p_tc = TILE // pb
    p_tiles = pt // p_tc
    s_bsz = TILE // st
    s_nb = sb // s_bsz
    p_rows = pb * pt

    are, aim, bbr, bbi = _discretise(lam_re, lam_im, log_dt, b_re, b_im)
    wx = jnp.concatenate([_blockdiag(bbr, True), _blockdiag(bbi, True)], axis=-1).astype(BF16)
    cm = jnp.concatenate([_blockdiag(c_re, False), _blockdiag(-c_im, False)], axis=-2).astype(BF16)

    layers = []
    for l in range(depth):
        wr_hi, wr_lo, b_r = _router_weights(w_grp[l], b_grp[l], w_erouter[l], b_erouter[l])
        layers.append(dict(
            norm_mix=norm_mix[l][None], w_in=w_in[l].astype(BF16), wx=wx[l], are=are[l], aim=aim[l],
            cm=cm[l], d_skip=d_skip[l][None], w_glu=w_glu[l].astype(BF16), b_glu=b_glu[l][None],
            w_pool=w_pool[l].astype(BF16), pool_scale=pool_scale[l][None], w_out=w_out[l].astype(BF16),
            norm_ffn=norm_ffn[l][None], wr_hi=wr_hi, wr_lo=wr_lo, b_r=b_r,
            w1=w1[l].astype(BF16), w3=w3[l].astype(BF16), w2=w2[l].astype(BF16)))
    nfin = norm_final[None]

    ntok = p_rows + sb * st
    p_zero_h = jnp.zeros((1, pb, S5_LANES), F32)
    p_zero_hist = jnp.zeros((1, POOL_HIST * pb, POOL_WIDTH), F32)
    final_layouts = [(0, p_tiles, (pb, pt, pb, p_tc)), (p_tiles, s_nb, (sb, st, s_bsz, st))]

    outs = {k: [] for k in ('p_re', 'p_im', 'p_pool', 's_re', 's_im', 's_pool')}
    x = None
    for l, lw in enumerate(layers):
        x1, hre, him, pool = _mixer(x_prompt if l == 0 else x, p_zero_h, p_zero_h, p_zero_hist, lw,
                                    bsz=pb, tc=p_tc, past=0, nb=1, nchunks=p_tiles, tile_off=0, ntok=ntok)
        outs['p_re'].append(hre.reshape(pb, S5_GROUPS, S5_STATE))
        outs['p_im'].append(him.reshape(pb, S5_GROUPS, S5_STATE))
        outs['p_pool'].append(_from_time_major(pool, POOL_BUF))
        h0re = state_ssm_re[l].reshape(s_nb, s_bsz, S5_LANES)
        h0im = state_ssm_im[l].reshape(s_nb, s_bsz, S5_LANES)
        hist = jnp.pad(_to_time_major(state_pool[l], s_nb), ((0, 0), ((POOL_HIST - POOL_BUF) * s_bsz, 0), (0, 0)))
        x1, hre, him, pool = _mixer(x_sample if l == 0 else x, h0re, h0im, hist, lw, bsz=s_bsz, tc=st,
                                    past=POOL_BUF, nb=s_nb, nchunks=1, tile_off=p_tiles, ntok=ntok, into=x1)
        outs['s_re'].append(hre.reshape(sb, S5_GROUPS, S5_STATE))
        outs['s_im'].append(him.reshape(sb, S5_GROUPS, S5_STATE))
        outs['s_pool'].append(_from_time_major(pool, POOL_BUF))
        x = _moe(x1, lw, nfin, final_layouts if l == depth - 1 else None)

    y_prompt, y_sample = x
    return (y_prompt, y_sample, jnp.stack(outs['p_re']), jnp.stack(outs['p_im']), jnp.stack(outs['p_pool']),
            jnp.stack(outs['s_re']), jnp.stack(outs['s_im']), jnp.stack(outs['s_pool']))
```

```python
import functools

import jax
import jax.numpy as jnp
from jax import lax
from jax.experimental import pallas as pl
from jax.experimental.pallas import tpu as pltpu

D_MODEL = 1024
S5_WIDTH = 512
S5_GROUP_CH = 16
S5_GROUPS = 32
S5_STATE = 64
S5_LANES = S5_GROUPS * S5_STATE
POOL_WIDTH = 512
POOL_WINDOWS = (2, 4, 8, 16)
POOL_GROUP_CH = 128
POOL_BUF = 15
POOL_HIST = 16
N_EXPERTS = 16
EXPERTS_PER_GROUP = 4
N_EXPERT_GROUPS = 4
D_EXPERT = 256
EPS = 1e-6

SUBLANES = 8
LANES = 128
S5_KBLOCK = 128
S5_NBLOCK = S5_KBLOCK // S5_GROUP_CH * S5_STATE
SCAN_LANES = 512
VMEM_LIMIT = 56 * 1024 * 1024

TILE = 512
GRAN = 32
TILE_GRANS = TILE // GRAN + N_EXPERT_GROUPS
SORTED_ROWS = TILE_GRANS * GRAN
ITEM_GRANS = 16
ITEM_ROWS = ITEM_GRANS * GRAN
PAYLOAD = D_MODEL + 2 * LANES

F32 = jnp.float32
BF16 = jnp.bfloat16

MIXER_PARAMS = ('norm_mix', 'w_in', 'wx', 'are', 'aim', 'cm', 'd_skip', 'w_glu', 'b_glu', 'w_pool',
                'pool_scale', 'w_out')


def _layer_spec(arr, l):
    return pl.BlockSpec((None,) + arr.shape[1:], lambda *_: (l,) + (0,) * (arr.ndim - 1))


def _rms(x, g):
    return x * lax.rsqrt(jnp.mean(x * x, axis=-1, keepdims=True) + EPS) * g


def _gelu_tanh(x):
    return 0.5 * x * (1.0 + jnp.tanh(0.7978845608028654 * (x + 0.044715 * (x * x * x))))


def _disc_kernel(lr_ref, li_ref, ldt_ref, br_ref, bi_ref, are_ref, aim_ref, bbr_ref, bbi_ref):
    lr = lr_ref[...]
    li = li_ref[...]
    dt = jnp.exp(ldt_ref[...])
    mag = jnp.exp(lr * dt)
    ab_re = mag * jnp.cos(li * dt)
    ab_im = mag * jnp.sin(li * dt)
    den = lr * lr + li * li
    nr = ab_re - 1.0
    coef_re = (nr * lr + ab_im * li) / den
    coef_im = (ab_im * lr - nr * li) / den
    br = br_ref[...]
    bi = bi_ref[...]
    are_ref[...] = ab_re
    aim_ref[...] = ab_im
    bbr_ref[...] = coef_re * br - coef_im * bi
    bbi_ref[...] = coef_re * bi + coef_im * br


def _discretise(lam_re, lam_im, log_dt, b_re, b_im):
    depth = lam_re.shape[0]
    rows = depth * S5_GROUPS * S5_GROUP_CH
    rep = lambda a: jnp.repeat(a.reshape(depth * S5_GROUPS, -1), S5_GROUP_CH, axis=0)
    tr = lambda b: jnp.transpose(b, (0, 1, 3, 2)).reshape(rows, S5_STATE)
    out = jax.ShapeDtypeStruct((rows, S5_STATE), F32)
    are, aim, bbr, bbi = pl.pallas_call(
        _disc_kernel, out_shape=(out, out, out, out), name="s5_discretise",
    )(rep(lam_re), rep(lam_im), rep(log_dt[..., None]), tr(b_re), tr(b_im))
    shp = (depth, S5_GROUPS, S5_GROUP_CH, S5_STATE)
    are = are.reshape(shp)[:, :, 0, :].reshape(depth, 1, S5_LANES)
    aim = aim.reshape(shp)[:, :, 0, :].reshape(depth, 1, S5_LANES)
    return are, aim, bbr.reshape(shp), bbi.reshape(shp)


def _blockdiag(w, rows_first):
    depth = w.shape[0]
    gpb = S5_KBLOCK // S5_GROUP_CH
    w5 = w.reshape(depth, S5_GROUPS // gpb, gpb, S5_GROUP_CH, S5_STATE)
    eye = jnp.eye(gpb, dtype=w.dtype)
    if rows_first:
        return jnp.einsum('lkgcn,gh->lkgchn', w5, eye).reshape(depth, -1, S5_KBLOCK, S5_NBLOCK)
    return jnp.einsum('lkgcn,gh->lkgnhc', w5, eye).reshape(depth, -1, S5_NBLOCK, S5_KBLOCK)


MIXER_INPUTS = 16


def _mixer_kernel(*refs, bsz, tc, past, nchunks, has_tail):
    if not has_tail:
        _mixer_body(*refs, bsz=bsz, tc=tc, past=past, last=nchunks - 1)
        return
    tail_ref = refs[MIXER_INPUTS]
    body_refs = refs[:MIXER_INPUTS] + refs[MIXER_INPUTS + 1:]
    x1_ref = body_refs[MIXER_INPUTS]
    i = pl.program_id(1)

    @pl.when(i < nchunks)
    def _():
        _mixer_body(*body_refs, bsz=bsz, tc=tc, past=past, last=nchunks - 1)

    @pl.when(i >= nchunks)
    def _():
        x1_ref[...] = tail_ref[...]


def _mixer_body(x_ref, h0re_ref, h0im_ref, hist_ref, gmix_ref, win_ref, wx_ref, are_ref, aim_ref,
                cm_ref, dskip_ref, wglu_ref, bglu_ref, wpool_ref, pscale_ref, wout_ref,
                x1_ref, hre_out, him_out, pool_out, xre, xim, hre, him, zbuf, *, bsz, tc, past, last):
    i = pl.program_id(1)
    rows = bsz * tc
    hist_rows = POOL_HIST * bsz

    @pl.when(i == 0)
    def _():
        hre[...] = h0re_ref[0]
        him[...] = h0im_ref[0]
        zbuf[0:hist_rows, :] = hist_ref[0]

    if len(x_ref.shape) == 3:
        x = jnp.concatenate([x_ref[:, t, :] for t in range(tc)], axis=0)
    else:
        x = x_ref[...]
    h = _rms(x, gmix_ref[...])
    proj = jnp.dot(h.astype(BF16), win_ref[...], preferred_element_type=F32)
    u = proj[:, :S5_WIDTH]
    z = proj[:, S5_WIDTH:]

    ub = u.astype(BF16)
    nblk = S5_WIDTH // S5_KBLOCK
    for kb in range(nblk):
        xx = jnp.dot(ub[:, kb * S5_KBLOCK:(kb + 1) * S5_KBLOCK], wx_ref[kb],
                     preferred_element_type=F32)
        xre[:, kb * S5_NBLOCK:(kb + 1) * S5_NBLOCK] = xx[:, :S5_NBLOCK]
        xim[:, kb * S5_NBLOCK:(kb + 1) * S5_NBLOCK] = xx[:, S5_NBLOCK:]

    def scan_block(rb, lb):
        ls = slice(lb * SCAN_LANES, (lb + 1) * SCAN_LANES)
        ar = jnp.broadcast_to(are_ref[:, ls], (SUBLANES, SCAN_LANES))
        ai = jnp.broadcast_to(aim_ref[:, ls], (SUBLANES, SCAN_LANES))
        r0 = pl.multiple_of(rb * SUBLANES, SUBLANES)

        def step(t, carry):
            hr, hi = carry
            row = pl.multiple_of(t * bsz + r0, SUBLANES)
            nr = ar * hr - ai * hi + xre[pl.ds(row, SUBLANES), ls]
            ni = ar * hi + ai * hr + xim[pl.ds(row, SUBLANES), ls]
            xre[pl.ds(row, SUBLANES), ls] = nr
            xim[pl.ds(row, SUBLANES), ls] = ni
            return nr, ni

        hr, hi = lax.fori_loop(0, tc, step, (hre[pl.ds(r0, SUBLANES), ls], him[pl.ds(r0, SUBLANES), ls]),
                               unroll=8)
        hre[pl.ds(r0, SUBLANES), ls] = hr
        him[pl.ds(r0, SUBLANES), ls] = hi

    for lb in range(S5_LANES // SCAN_LANES):
        if bsz == SUBLANES:
            scan_block(0, lb)
        else:
            def body(rb, c, lb=lb):
                scan_block(rb, lb)
                return c
            lax.fori_loop(0, bsz // SUBLANES, body, 0)

    ys = []
    for kb in range(nblk):
        ns = slice(kb * S5_NBLOCK, (kb + 1) * S5_NBLOCK)
        ys.append(jnp.dot(xre[:, ns].astype(BF16), cm_ref[kb, :S5_NBLOCK, :], preferred_element_type=F32)
                  + jnp.dot(xim[:, ns].astype(BF16), cm_ref[kb, S5_NBLOCK:, :], preferred_element_type=F32))
    y = jnp.concatenate(ys, axis=-1) + dskip_ref[...] * u
    g = _gelu_tanh(y)
    s5_out = g * jax.nn.sigmoid(jnp.dot(g.astype(BF16), wglu_ref[...], preferred_element_type=F32)
                                + bglu_ref[...])

    zbuf[hist_rows:hist_rows + rows, :] = z
    step_idx = i * tc + lax.broadcasted_iota(jnp.int32, (rows, 1), 0) // bsz
    pos = (step_idx + (past + 1)).astype(F32)
    pouts = []
    for gi, w in enumerate(POOL_WINDOWS):
        gs = slice(gi * POOL_GROUP_CH, (gi + 1) * POOL_GROUP_CH)
        s = zbuf[:, gs]
        k = 1
        while k < w:
            s = s[k * bsz:] + s[:-k * bsz]
            k *= 2
        inv_cnt = 1.0 / jnp.minimum(pos, float(w))
        pooled = s[-rows:] * inv_cnt - z[:, gs]
        pouts.append(jnp.dot(pooled.astype(BF16), wpool_ref[gi], preferred_element_type=F32))
    pool_mixed = jnp.concatenate(pouts, axis=-1) * pscale_ref[...]

    mix = jnp.concatenate([s5_out, pool_mixed], axis=-1).astype(BF16)
    x1_ref[...] = x + jnp.dot(mix, wout_ref[...], preferred_element_type=F32)

    @pl.when(i == last)
    def _():
        hre_out[0] = hre[...]
        him_out[0] = him[...]
        pool_out[0] = zbuf[(tc + POOL_HIST - POOL_BUF) * bsz:(tc + POOL_HIST) * bsz, :]

    @pl.when(i != last)
    def _():
        zbuf[0:hist_rows, :] = zbuf[tc * bsz:(tc + POOL_HIST) * bsz, :]


def _mixer(x, h0re, h0im, hist, lw, l, *, bsz, tc, past, nb, nchunks, in_off, tail=None):
    rows = bsz * tc
    assert rows == TILE
    ntail = 0 if tail is None else tail.shape[0] // TILE
    assert ntail == 0 or nb == 1
    steps = nchunks + ntail
    chunk = lambda i: jnp.minimum(i, nchunks - 1)
    perb = lambda shape: pl.BlockSpec(shape, lambda b, i: (b,) + (0,) * (len(shape) - 1))
    if x.ndim == 2:
        x_spec = pl.BlockSpec((rows, D_MODEL), lambda b, i: (in_off + b * nchunks + chunk(i), 0))
    else:
        x_spec = pl.BlockSpec((bsz, tc, D_MODEL), lambda b, i: (b, chunk(i), 0))
    in_specs = [
        x_spec,
        perb((1, bsz, S5_LANES)), perb((1, bsz, S5_LANES)), perb((1, POOL_HIST * bsz, POOL_WIDTH)),
    ] + [_layer_spec(lw[k], l) for k in MIXER_PARAMS]
    args = [x, h0re, h0im, hist] + [lw[k] for k in MIXER_PARAMS]
    assert len(args) == MIXER_INPUTS
    if ntail:
        in_specs.append(pl.BlockSpec((TILE, D_MODEL), lambda b, i: (jnp.maximum(i - nchunks, 0), 0)))
        args.append(tail)
    out_specs = [
        pl.BlockSpec((rows, D_MODEL), lambda b, i: (b * steps + i, 0)),
        perb((1, bsz, S5_LANES)), perb((1, bsz, S5_LANES)), perb((1, POOL_BUF * bsz, POOL_WIDTH)),
    ]
    out_shape = [
        jax.ShapeDtypeStruct((nb * steps * TILE, D_MODEL), F32),
        jax.ShapeDtypeStruct((nb, bsz, S5_LANES), F32),
        jax.ShapeDtypeStruct((nb, bsz, S5_LANES), F32),
        jax.ShapeDtypeStruct((nb, POOL_BUF * bsz, POOL_WIDTH), F32),
    ]
    scratch = [
        pltpu.VMEM((rows, S5_LANES), F32), pltpu.VMEM((rows, S5_LANES), F32),
        pltpu.VMEM((bsz, S5_LANES), F32), pltpu.VMEM((bsz, S5_LANES), F32),
        pltpu.VMEM(((tc + POOL_HIST) * bsz, POOL_WIDTH), F32),
    ]
    return pl.pallas_call(
        functools.partial(_mixer_kernel, bsz=bsz, tc=tc, past=past, nchunks=nchunks, has_tail=ntail > 0),
        grid=(nb, steps), in_specs=in_specs, out_specs=out_specs, out_shape=out_shape,
        scratch_shapes=scratch, name="mixer",
        compiler_params=pltpu.CompilerParams(dimension_semantics=("arbitrary", "arbitrary"),
                                             vmem_limit_bytes=VMEM_LIMIT),
    )(*args)


def _gating(logits):
    r = logits.shape[0]
    lane = lax.broadcasted_iota(jnp.int32, (r, LANES), 1)
    lanef = lane.astype(F32)
    neg = -jnp.inf
    is_group = jnp.logical_and(lane >= N_EXPERTS, lane < N_EXPERTS + N_EXPERT_GROUPS)
    lg = jnp.where(is_group, logits, neg)
    gmax = jnp.max(lg, axis=-1, keepdims=True)
    g_w = 1.0 / jnp.sum(jnp.exp(lg - gmax), axis=-1, keepdims=True)
    g_idx = jnp.min(jnp.where(lg == gmax, lanef, float(LANES)), axis=-1, keepdims=True) - float(N_EXPERTS)
    in_group = (lane // EXPERTS_PER_GROUP).astype(F32) == g_idx
    el = jnp.where(jnp.logical_and(in_group, lane < N_EXPERTS), logits, neg)
    m1 = jnp.max(el, axis=-1, keepdims=True)
    i1 = jnp.min(jnp.where(el == m1, lanef, float(LANES)), axis=-1, keepdims=True)
    el2 = jnp.where(lanef == i1, neg, el)
    m2 = jnp.max(el2, axis=-1, keepdims=True)
    i2 = jnp.min(jnp.where(el2 == m2, lanef, float(LANES)), axis=-1, keepdims=True)
    e2 = jnp.exp(m2 - m1)
    den = 1.0 + e2
    gate1 = g_w / den
    gate2 = g_w * e2 / den
    comb = jnp.where(lanef == i1, gate1, 0.0) + jnp.where(lanef == i2, gate2, 0.0)
    return comb, g_idx


def _route_sort_kernel(x1_ref, gffn_ref, wrh_ref, wrl_ref, br_ref, sorted_ref, dest_ref, gid_ref):
    x = x1_ref[...]
    h2 = _rms(x, gffn_ref[...])
    hi = h2.astype(BF16)
    lo = (h2 - hi.astype(F32)).astype(BF16)
    logits = (jnp.dot(hi, wrh_ref[...], preferred_element_type=F32)
              + jnp.dot(hi, wrl_ref[...], preferred_element_type=F32)
              + jnp.dot(lo, wrh_ref[...], preferred_element_type=F32)) + br_ref[...]
    comb, g_idx = _gating(logits)

    lanef = lax.broadcasted_iota(jnp.int32, (TILE, LANES), 1).astype(F32)
    onehot = jnp.where(lanef == g_idx, 1.0, 0.0)
    ri = lax.broadcasted_iota(jnp.int32, (TILE, TILE), 0)
    ci = lax.broadcasted_iota(jnp.int32, (TILE, TILE), 1)
    earlier = jnp.where(ci < ri, 1.0, 0.0).astype(BF16)
    rank = jnp.dot(earlier, onehot.astype(BF16), preferred_element_type=F32)
    counts = jnp.sum(onehot, axis=0, keepdims=True)
    padded = jnp.floor((counts + float(GRAN - 1)) * (1.0 / GRAN)) * float(GRAN)
    e0 = padded[:, 0:1]
    e1 = e0 + padded[:, 1:2]
    e2 = e1 + padded[:, 2:3]
    e3 = e2 + padded[:, 3:4]
    lane1 = lax.broadcasted_iota(jnp.int32, (1, LANES), 1)
    base = (jnp.where(lane1 == 1, e0, 0.0) + jnp.where(lane1 == 2, e1, 0.0)
            + jnp.where(lane1 == 3, e2, 0.0))
    dest = jnp.sum(onehot * (base + rank), axis=-1, keepdims=True)
    dest_ref[...] = dest

    dest_row = jnp.sum(jnp.where(ri == ci, dest, 0.0), axis=0, keepdims=True)
    slot = lax.broadcasted_iota(jnp.int32, (SORTED_ROWS, TILE), 0).astype(F32)
    perm = jnp.where(slot == dest_row, 1.0, 0.0).astype(BF16)
    c_hi = comb.astype(BF16)
    c_lo = (comb - c_hi.astype(F32)).astype(BF16)
    payload = jnp.concatenate([hi, c_hi, c_lo], axis=-1)
    sorted_ref[...] = jnp.dot(perm, payload, preferred_element_type=F32).astype(BF16)

    q = lane1.astype(F32) * float(GRAN)
    gid = (jnp.where(q >= e0, 1, 0) + jnp.where(q >= e1, 1, 0)
           + jnp.where(q >= e2, 1, 0) + jnp.where(q >= e3, 1, 0))
    gid_ref[0] = gid.astype(jnp.int32)


def _route_sort(x1, lw, l):
    ntiles = x1.shape[0] // TILE
    params = [lw[k] for k in ('norm_ffn', 'wr_hi', 'wr_lo', 'b_r')]
    return pl.pallas_call(
        _route_sort_kernel, grid=(ntiles,),
        in_specs=[pl.BlockSpec((TILE, D_MODEL), lambda i: (i, 0))] + [_layer_spec(p, l) for p in params],
        out_specs=[pl.BlockSpec((SORTED_ROWS, PAYLOAD), lambda i: (i, 0)),
                   pl.BlockSpec((TILE, 1), lambda i: (i, 0)),
                   pl.BlockSpec((1, 1, LANES), lambda i: (i, 0, 0))],
        out_shape=[jax.ShapeDtypeStruct((ntiles * SORTED_ROWS, PAYLOAD), BF16),
                   jax.ShapeDtypeStruct((x1.shape[0], 1), F32),
                   jax.ShapeDtypeStruct((ntiles, 1, LANES), jnp.int32)],
        name="moe_route_sort",
        compiler_params=pltpu.CompilerParams(dimension_semantics=("arbitrary",),
                                             vmem_limit_bytes=VMEM_LIMIT),
    )(x1, *params)


def _experts_kernel(src_ref, wgid_ref, wvalid_ref, *refs):
    gran_refs = refs[:ITEM_GRANS]
    w1_ref, w3_ref, w2_ref, out_ref = refs[ITEM_GRANS:]
    j = pl.program_id(0)

    @pl.when(wvalid_ref[j] > 0)
    def _():
        full = jnp.concatenate([g[...] for g in gran_refs], axis=0)
        h = full[:, :D_MODEL]
        comb = (full[:, D_MODEL:D_MODEL + LANES].astype(F32)
                + full[:, D_MODEL + LANES:].astype(F32))
        first = wgid_ref[j] * EXPERTS_PER_GROUP
        lane = lax.broadcasted_iota(jnp.int32, (ITEM_ROWS, LANES), 1)
        hids = []
        for e in range(EXPERTS_PER_GROUP):
            a = jnp.dot(h, w1_ref[e], preferred_element_type=F32)
            b = jnp.dot(h, w3_ref[e], preferred_element_type=F32)
            gate = jnp.sum(jnp.where(lane == first + e, comb, 0.0), axis=-1, keepdims=True)
            hids.append(((a * jax.nn.sigmoid(a)) * b * gate).astype(BF16))
        hid = jnp.concatenate(hids, axis=-1)
        w2g = w2_ref[...].reshape(EXPERTS_PER_GROUP * D_EXPERT, D_MODEL)
        out_ref[...] = jnp.dot(hid, w2g, preferred_element_type=F32).astype(BF16)

    @pl.when(wvalid_ref[j] == 0)
    def _():
        out_ref[...] = jnp.zeros(out_ref.shape, out_ref.dtype)


def _experts(sorted_tok, src, wgid, wvalid, lw, l):
    nitems = wgid.shape[0]
    gran_spec = lambda s: pl.BlockSpec((GRAN, PAYLOAD), lambda j, src, wg, wv: (src[j * ITEM_GRANS + s], 0))
    wspec = lambda shape: pl.BlockSpec(shape, lambda j, src, wg, wv: (l * N_EXPERT_GROUPS + wg[j], 0, 0))
    grid_spec = pltpu.PrefetchScalarGridSpec(
        num_scalar_prefetch=3, grid=(nitems,),
        in_specs=[gran_spec(s) for s in range(ITEM_GRANS)] + [
            wspec((EXPERTS_PER_GROUP, D_MODEL, D_EXPERT)), wspec((EXPERTS_PER_GROUP, D_MODEL, D_EXPERT)),
            wspec((EXPERTS_PER_GROUP, D_EXPERT, D_MODEL))],
        out_specs=pl.BlockSpec((ITEM_ROWS, D_MODEL), lambda j, src, wg, wv: (j, 0)))
    return pl.pallas_call(
        _experts_kernel, grid_spec=grid_spec,
        out_shape=jax.ShapeDtypeStruct((nitems * ITEM_ROWS, D_MODEL), BF16),
        name="moe_experts",
        compiler_params=pltpu.CompilerParams(dimension_semantics=("arbitrary",),
                                             vmem_limit_bytes=VMEM_LIMIT),
    )(src, wgid, wvalid, *([sorted_tok] * ITEM_GRANS), lw['w1'], lw['w3'], lw['w2'])


def _unsort_kernel(pos_ref, x1_ref, dest_ref, *refs, final):
    gran_refs = refs[:TILE_GRANS]
    gfin_ref, out_ref = refs[TILE_GRANS:]
    s = jnp.concatenate([g[...] for g in gran_refs], axis=0)
    slot = lax.broadcasted_iota(jnp.int32, (TILE, SORTED_ROWS), 1).astype(F32)
    perm_t = jnp.where(slot == dest_ref[...], 1.0, 0.0).astype(BF16)
    y = x1_ref[...] + jnp.dot(perm_t, s, preferred_element_type=F32)
    if not final:
        out_ref[...] = y
    else:
        y = _rms(y, gfin_ref[...])
        bsz, tc, _ = out_ref.shape
        for t in range(tc):
            out_ref[:, t, :] = y[t * bsz:(t + 1) * bsz, :]


def _unsort(x1, dest, expert_out, pos, norm_final, *, tile_off=0, ntiles=None, batch_major=None):
    final = batch_major is not None
    if final:
        batch, t_len, bsz, tc = batch_major
        nchunks = t_len // tc
        out_spec = pl.BlockSpec((bsz, tc, D_MODEL), lambda k, pos: (k // nchunks, k % nchunks, 0))
        out_shape = jax.ShapeDtypeStruct((batch, t_len, D_MODEL), F32)
    else:
        ntiles = x1.shape[0] // TILE
        out_spec = pl.BlockSpec((TILE, D_MODEL), lambda k, pos: (k, 0))
        out_shape = jax.ShapeDtypeStruct(x1.shape, F32)
    gran_spec = lambda q: pl.BlockSpec((GRAN, D_MODEL),
                                       lambda k, pos: (pos[(k + tile_off) * TILE_GRANS + q], 0))
    grid_spec = pltpu.PrefetchScalarGridSpec(
        num_scalar_prefetch=1, grid=(ntiles,),
        in_specs=[pl.BlockSpec((TILE, D_MODEL), lambda k, pos: (k + tile_off, 0)),
                  pl.BlockSpec((TILE, 1), lambda k, pos: (k + tile_off, 0))]
                 + [gran_spec(q) for q in range(TILE_GRANS)]
                 + [pl.BlockSpec((1, D_MODEL), lambda k, pos: (0, 0))],
        out_specs=out_spec)
    return pl.pallas_call(
        functools.partial(_unsort_kernel, final=final), grid_spec=grid_spec,
        out_shape=out_shape, name="moe_unsort",
        compiler_params=pltpu.CompilerParams(dimension_semantics=("arbitrary",),
                                             vmem_limit_bytes=VMEM_LIMIT),
    )(pos, x1, dest, *([expert_out] * TILE_GRANS), norm_final)


def _dispatch_tables(gid):
    ntiles = gid.shape[0]
    ngran = ntiles * TILE_GRANS
    nitems = -(-ngran // ITEM_GRANS) + N_EXPERT_GROUPS
    g = gid[:, 0, :TILE_GRANS].reshape(ngran)
    onehot = (g[:, None] == jnp.arange(N_EXPERT_GROUPS, dtype=jnp.int32)[None, :]).astype(jnp.int32)
    valid = g < N_EXPERT_GROUPS
    rank = jnp.cumsum(onehot, axis=0) - onehot
    cnt = jnp.sum(onehot, axis=0)
    pcnt = (cnt + (ITEM_GRANS - 1)) // ITEM_GRANS * ITEM_GRANS
    pend = jnp.cumsum(pcnt)
    pstart = pend - pcnt
    pos = jnp.sum(onehot * (pstart[None, :] + rank), axis=1)
    slots = jnp.arange(nitems * ITEM_GRANS, dtype=jnp.int32)
    hit = jnp.logical_and(pos[None, :] == slots[:, None], valid[None, :])
    src = jnp.sum(jnp.where(hit, jnp.arange(ngran, dtype=jnp.int32)[None, :], 0), axis=1)
    item_start = jnp.arange(nitems, dtype=jnp.int32) * ITEM_GRANS
    wgid = jnp.minimum(jnp.sum((item_start[:, None] >= pend[None, :]).astype(jnp.int32), axis=1),
                       N_EXPERT_GROUPS - 1)
    wvalid = (item_start < pend[-1]).astype(jnp.int32)
    return src.astype(jnp.int32), wgid.astype(jnp.int32), wvalid, jnp.where(valid, pos, 0).astype(jnp.int32)


def _moe(x1, lw, l, norm_final, final_layouts=None):
    sorted_tok, dest, gid = _route_sort(x1, lw, l)
    src, wgid, wvalid, pos = _dispatch_tables(gid)
    expert_out = _experts(sorted_tok, src, wgid, wvalid, lw, l)
    if final_layouts is None:
        return _unsort(x1, dest, expert_out, pos, norm_final)
    return [_unsort(x1, dest, expert_out, pos, norm_final, tile_off=off, ntiles=n, batch_major=bm)
            for off, n, bm in final_layouts]


def _router_weights(w_grp, b_grp, w_er, b_er):
    pad = LANES - N_EXPERTS - N_EXPERT_GROUPS
    w = jnp.pad(jnp.concatenate([w_er, w_grp], axis=-1), ((0, 0), (0, 0), (0, pad)))
    b = jnp.pad(jnp.concatenate([b_er, b_grp], axis=-1), ((0, 0), (0, pad)))
    hi = w.astype(BF16)
    lo = (w - hi.astype(F32)).astype(BF16)
    return hi, lo, b


def _to_time_major(x, nb):
    bt, t, w = x.shape
    return jnp.transpose(x.reshape(nb, bt // nb, t, w), (0, 2, 1, 3)).reshape(nb, t * (bt // nb), w)


def _from_time_major(x, t):
    nb, rows, w = x.shape
    bsz = rows // t
    return jnp.transpose(x.reshape(nb, t, bsz, w), (0, 2, 1, 3)).reshape(nb * bsz, t, w)


def kernel(x_prompt, x_sample, state_ssm_re, state_ssm_im, state_pool, norm_mix, w_in, lam_re, lam_im, log_dt, b_re, b_im, c_re, c_im, d_skip, w_glu, b_glu, w_pool, pool_scale, w_out, norm_ffn, w_grp, b_grp, w_erouter, b_erouter, w1, w3, w2, norm_final):
    depth = w_in.shape[0]
    pb, pt, _ = x_prompt.shape
    sb, st, _ = x_sample.shape
    p_tc = TILE // pb
    p_tiles = pt // p_tc
    s_bsz = TILE // st
    s_nb = sb // s_bsz
    p_rows = pb * pt

    are, aim, bbr, bbi = _discretise(lam_re, lam_im, log_dt, b_re, b_im)
    wx = jnp.concatenate([_blockdiag(bbr, True), _blockdiag(bbi, True)], axis=-1).astype(BF16)
    cm = jnp.concatenate([_blockdiag(c_re, False), _blockdiag(-c_im, False)], axis=-2).astype(BF16)

    wr_hi, wr_lo, b_r = _router_weights(w_grp, b_grp, w_erouter, b_erouter)
    stack_experts = lambda w: w.astype(BF16).reshape((depth * N_EXPERTS,) + w.shape[2:])
    vec = lambda a: a[:, None, :]
    lw = dict(norm_mix=vec(norm_mix), w_in=w_in.astype(BF16), wx=wx, are=are, aim=aim, cm=cm,
              d_skip=vec(d_skip), w_glu=w_glu.astype(BF16), b_glu=vec(b_glu), w_pool=w_pool.astype(BF16),
              pool_scale=vec(pool_scale), w_out=w_out.astype(BF16), norm_ffn=vec(norm_ffn),
              wr_hi=wr_hi, wr_lo=wr_lo, b_r=vec(b_r),
              w1=stack_experts(w1), w3=stack_experts(w3), w2=stack_experts(w2))
    nfin = norm_final[None]

    p_zero_h = jnp.zeros((1, pb, S5_LANES), F32)
    p_zero_hist = jnp.zeros((1, POOL_HIST * pb, POOL_WIDTH), F32)
    final_layouts = [(0, p_tiles, (pb, pt, pb, p_tc)), (p_tiles, s_nb, (sb, st, s_bsz, st))]

    outs = {k: [] for k in ('p_re', 'p_im', 'p_pool', 's_re', 's_im', 's_pool')}
    x = None
    for l in range(depth):
        h0re = state_ssm_re[l].reshape(s_nb, s_bsz, S5_LANES)
        h0im = state_ssm_im[l].reshape(s_nb, s_bsz, S5_LANES)
        hist = jnp.pad(_to_time_major(state_pool[l], s_nb), ((0, 0), ((POOL_HIST - POOL_BUF) * s_bsz, 0), (0, 0)))
        xs1, hre, him, pool = _mixer(x_sample if l == 0 else x, h0re, h0im, hist, lw, l, bsz=s_bsz, tc=st,
                                     past=POOL_BUF, nb=s_nb, nchunks=1, in_off=p_tiles)
        outs['s_re'].append(hre.reshape(sb, S5_GROUPS, S5_STATE))
        outs['s_im'].append(him.reshape(sb, S5_GROUPS, S5_STATE))
        outs['s_pool'].append(_from_time_major(pool, POOL_BUF))
        x1, hre, him, pool = _mixer(x_prompt if l == 0 else x, p_zero_h, p_zero_h, p_zero_hist, lw, l,
                                    bsz=pb, tc=p_tc, past=0, nb=1, nchunks=p_tiles, in_off=0, tail=xs1)
        outs['p_re'].append(hre.reshape(pb, S5_GROUPS, S5_STATE))
        outs['p_im'].append(him.reshape(pb, S5_GROUPS, S5_STATE))
        outs['p_pool'].append(_from_time_major(pool, POOL_BUF))
        x = _moe(x1, lw, l, nfin, final_layouts if l == depth - 1 else None)

    y_prompt, y_sample = x
    return (y_prompt, y_sample, jnp.stack(outs['p_re']), jnp.stack(outs['p_im']), jnp.stack(outs['p_pool']),
            jnp.stack(outs['s_re']), jnp.stack(outs['s_im']), jnp.stack(outs['s_pool']))
```

```python
import functools

import jax
import jax.numpy as jnp
from jax import lax
from jax.experimental import pallas as pl
from jax.experimental.pallas import tpu as pltpu

D_MODEL = 1024
S5_WIDTH = 512
S5_GROUP_CH = 16
S5_GROUPS = 32
S5_STATE = 64
S5_LANES = S5_GROUPS * S5_STATE
POOL_WIDTH = 512
POOL_WINDOWS = (2, 4, 8, 16)
POOL_GROUP_CH = 128
POOL_BUF = 15
POOL_HIST = 16
N_EXPERTS = 16
EXPERTS_PER_GROUP = 4
N_EXPERT_GROUPS = 4
D_EXPERT = 256
EPS = 1e-6

SUBLANES = 8
LANES = 128
S5_KBLOCK = 128
S5_NBLOCK = S5_KBLOCK // S5_GROUP_CH * S5_STATE
SCAN_LANES = 512
VMEM_LIMIT = 56 * 1024 * 1024

TILE = 512
GRAN = 32
TILE_GRANS = TILE // GRAN + N_EXPERT_GROUPS
SORTED_ROWS = TILE_GRANS * GRAN
ITEM_GRANS = 16
ITEM_ROWS = ITEM_GRANS * GRAN
PAYLOAD = D_MODEL + 2 * LANES

F32 = jnp.float32
BF16 = jnp.bfloat16

MIXER_PARAMS = ('norm_mix', 'w_in', 'wx', 'are', 'aim', 'cm', 'd_skip', 'w_glu', 'b_glu', 'w_pool',
                'pool_scale', 'w_out')


def _layer_spec(arr, l):
    return pl.BlockSpec((None,) + arr.shape[1:], lambda *_: (l,) + (0,) * (arr.ndim - 1))


def _rms(x, g):
    return x * lax.rsqrt(jnp.mean(x * x, axis=-1, keepdims=True) + EPS) * g


def _gelu_tanh(x):
    return 0.5 * x * (1.0 + jnp.tanh(0.7978845608028654 * (x + 0.044715 * (x * x * x))))


def _disc_kernel(lr_ref, li_ref, ldt_ref, br_ref, bi_ref, are_ref, aim_ref, bbr_ref, bbi_ref):
    lr = lr_ref[...]
    li = li_ref[...]
    dt = jnp.exp(ldt_ref[...])
    mag = jnp.exp(lr * dt)
    ab_re = mag * jnp.cos(li * dt)
    ab_im = mag * jnp.sin(li * dt)
    den = lr * lr + li * li
    nr = ab_re - 1.0
    coef_re = (nr * lr + ab_im * li) / den
    coef_im = (ab_im * lr - nr * li) / den
    br = br_ref[...]
    bi = bi_ref[...]
    are_ref[...] = ab_re
    aim_ref[...] = ab_im
    bbr_ref[...] = coef_re * br - coef_im * bi
    bbi_ref[...] = coef_re * bi + coef_im * br


def _discretise(lam_re, lam_im, log_dt, b_re, b_im):
    depth = lam_re.shape[0]
    rows = depth * S5_GROUPS * S5_GROUP_CH
    rep = lambda a: jnp.repeat(a.reshape(depth * S5_GROUPS, -1), S5_GROUP_CH, axis=0)
    tr = lambda b: jnp.transpose(b, (0, 1, 3, 2)).reshape(rows, S5_STATE)
    out = jax.ShapeDtypeStruct((rows, S5_STATE), F32)
    are, aim, bbr, bbi = pl.pallas_call(
        _disc_kernel, out_shape=(out, out, out, out), name="s5_discretise",
    )(rep(lam_re), rep(lam_im), rep(log_dt[..., None]), tr(b_re), tr(b_im))
    shp = (depth, S5_GROUPS, S5_GROUP_CH, S5_STATE)
    are = are.reshape(shp)[:, :, 0, :].reshape(depth, 1, S5_LANES)
    aim = aim.reshape(shp)[:, :, 0, :].reshape(depth, 1, S5_LANES)
    return are, aim, bbr.reshape(shp), bbi.reshape(shp)


def _blockdiag(w, rows_first):
    depth = w.shape[0]
    gpb = S5_KBLOCK // S5_GROUP_CH
    w5 = w.reshape(depth, S5_GROUPS // gpb, gpb, S5_GROUP_CH, S5_STATE)
    eye = jnp.eye(gpb, dtype=w.dtype)
    if rows_first:
        return jnp.einsum('lkgcn,gh->lkgchn', w5, eye).reshape(depth, -1, S5_KBLOCK, S5_NBLOCK)
    return jnp.einsum('lkgcn,gh->lkgnhc', w5, eye).reshape(depth, -1, S5_NBLOCK, S5_KBLOCK)


MIXER_INPUTS = 16


def _mixer_kernel(*refs, bsz, tc, past, nchunks, has_tail):
    if not has_tail:
        _mixer_body(*refs, bsz=bsz, tc=tc, past=past, last=nchunks - 1)
        return
    tail_ref = refs[MIXER_INPUTS]
    body_refs = refs[:MIXER_INPUTS] + refs[MIXER_INPUTS + 1:]
    x1_ref = body_refs[MIXER_INPUTS]
    i = pl.program_id(1)

    @pl.when(i < nchunks)
    def _():
        _mixer_body(*body_refs, bsz=bsz, tc=tc, past=past, last=nchunks - 1)

    @pl.when(i >= nchunks)
    def _():
        x1_ref[...] = tail_ref[...]


def _mixer_body(x_ref, h0re_ref, h0im_ref, hist_ref, gmix_ref, win_ref, wx_ref, are_ref, aim_ref,
                cm_ref, dskip_ref, wglu_ref, bglu_ref, wpool_ref, pscale_ref, wout_ref,
                x1_ref, hre_out, him_out, pool_out, xre, xim, sre, sim, hre, him, zbuf, *, bsz, tc, past, last):
    i = pl.program_id(1)
    rows = bsz * tc
    hist_rows = POOL_HIST * bsz

    @pl.when(i == 0)
    def _():
        hre[...] = h0re_ref[0]
        him[...] = h0im_ref[0]
        zbuf[0:hist_rows, :] = hist_ref[0]

    if len(x_ref.shape) == 3:
        x = jnp.concatenate([x_ref[:, t, :] for t in range(tc)], axis=0)
    else:
        x = x_ref[...]
    h = _rms(x, gmix_ref[...])
    proj = jnp.dot(h.astype(BF16), win_ref[...], preferred_element_type=F32)
    u = proj[:, :S5_WIDTH]
    z = proj[:, S5_WIDTH:]

    ub = u.astype(BF16)
    nblk = S5_WIDTH // S5_KBLOCK
    for kb in range(nblk):
        xx = jnp.dot(ub[:, kb * S5_KBLOCK:(kb + 1) * S5_KBLOCK], wx_ref[kb],
                     preferred_element_type=F32)
        xre[:, kb * S5_NBLOCK:(kb + 1) * S5_NBLOCK] = xx[:, :S5_NBLOCK]
        xim[:, kb * S5_NBLOCK:(kb + 1) * S5_NBLOCK] = xx[:, S5_NBLOCK:]

    def scan_block(rb, lb):
        ls = slice(lb * SCAN_LANES, (lb + 1) * SCAN_LANES)
        ar = jnp.broadcast_to(are_ref[:, ls], (SUBLANES, SCAN_LANES))
        ai = jnp.broadcast_to(aim_ref[:, ls], (SUBLANES, SCAN_LANES))
        r0 = pl.multiple_of(rb * SUBLANES, SUBLANES)

        def step(t, carry):
            hr, hi = carry
            row = pl.multiple_of(t * bsz + r0, SUBLANES)
            nr = ar * hr + (xre[pl.ds(row, SUBLANES), ls] - ai * hi)
            ni = ar * hi + (xim[pl.ds(row, SUBLANES), ls] + ai * hr)
            sre[pl.ds(row, SUBLANES), ls] = nr
            sim[pl.ds(row, SUBLANES), ls] = ni
            return nr, ni

        hr, hi = lax.fori_loop(0, tc, step, (hre[pl.ds(r0, SUBLANES), ls], him[pl.ds(r0, SUBLANES), ls]),
                               unroll=8)
        hre[pl.ds(r0, SUBLANES), ls] = hr
        him[pl.ds(r0, SUBLANES), ls] = hi

    for lb in range(S5_LANES // SCAN_LANES):
        if bsz == SUBLANES:
            scan_block(0, lb)
        else:
            def body(rb, c, lb=lb):
                scan_block(rb, lb)
                return c
            lax.fori_loop(0, bsz // SUBLANES, body, 0)

    ys = []
    for kb in range(nblk):
        ns = slice(kb * S5_NBLOCK, (kb + 1) * S5_NBLOCK)
        ys.append(jnp.dot(sre[:, ns].astype(BF16), cm_ref[kb, :S5_NBLOCK, :], preferred_element_type=F32)
                  + jnp.dot(sim[:, ns].astype(BF16), cm_ref[kb, S5_NBLOCK:, :], preferred_element_type=F32))
    y = jnp.concatenate(ys, axis=-1) + dskip_ref[...] * u
    g = _gelu_tanh(y)
    s5_out = g * jax.nn.sigmoid(jnp.dot(g.astype(BF16), wglu_ref[...], preferred_element_type=F32)
                                + bglu_ref[...])

    zbuf[hist_rows:hist_rows + rows, :] = z
    step_idx = i * tc + lax.broadcasted_iota(jnp.int32, (rows, 1), 0) // bsz
    pos = (step_idx + (past + 1)).astype(F32)
    pouts = []
    for gi, w in enumerate(POOL_WINDOWS):
        gs = slice(gi * POOL_GROUP_CH, (gi + 1) * POOL_GROUP_CH)
        s = zbuf[:, gs]
        k = 1
        while k < w:
            s = s[k * bsz:] + s[:-k * bsz]
            k *= 2
        inv_cnt = 1.0 / jnp.minimum(pos, float(w))
        pooled = s[-rows:] * inv_cnt - z[:, gs]
        pouts.append(jnp.dot(pooled.astype(BF16), wpool_ref[gi], preferred_element_type=F32))
    pool_mixed = jnp.concatenate(pouts, axis=-1) * pscale_ref[...]

    mix = jnp.concatenate([s5_out, pool_mixed], axis=-1).astype(BF16)
    x1_ref[...] = x + jnp.dot(mix, wout_ref[...], preferred_element_type=F32)

    @pl.when(i == last)
    def _():
        hre_out[0] = hre[...]
        him_out[0] = him[...]
        pool_out[0] = zbuf[(tc + POOL_HIST - POOL_BUF) * bsz:(tc + POOL_HIST) * bsz, :]

    @pl.when(i != last)
    def _():
        zbuf[0:hist_rows, :] = zbuf[tc * bsz:(tc + POOL_HIST) * bsz, :]


def _mixer(x, h0re, h0im, hist, lw, l, *, bsz, tc, past, nb, nchunks, in_off, tail=None):
    rows = bsz * tc
    assert rows == TILE
    ntail = 0 if tail is None else tail.shape[0] // TILE
    assert ntail == 0 or nb == 1
    steps = nchunks + ntail
    chunk = lambda i: jnp.minimum(i, nchunks - 1)
    perb = lambda shape: pl.BlockSpec(shape, lambda b, i: (b,) + (0,) * (len(shape) - 1))
    if x.ndim == 2:
        x_spec = pl.BlockSpec((rows, D_MODEL), lambda b, i: (in_off + b * nchunks + chunk(i), 0))
    else:
        x_spec = pl.BlockSpec((bsz, tc, D_MODEL), lambda b, i: (b, chunk(i), 0))
    in_specs = [
        x_spec,
        perb((1, bsz, S5_LANES)), perb((1, bsz, S5_LANES)), perb((1, POOL_HIST * bsz, POOL_WIDTH)),
    ] + [_layer_spec(lw[k], l) for k in MIXER_PARAMS]
    args = [x, h0re, h0im, hist] + [lw[k] for k in MIXER_PARAMS]
    assert len(args) == MIXER_INPUTS
    if ntail:
        in_specs.append(pl.BlockSpec((TILE, D_MODEL), lambda b, i: (jnp.maximum(i - nchunks, 0), 0)))
        args.append(tail)
    out_specs = [
        pl.BlockSpec((rows, D_MODEL), lambda b, i: (b * steps + i, 0)),
        perb((1, bsz, S5_LANES)), perb((1, bsz, S5_LANES)), perb((1, POOL_BUF * bsz, POOL_WIDTH)),
    ]
    out_shape = [
        jax.ShapeDtypeStruct((nb * steps * TILE, D_MODEL), F32),
        jax.ShapeDtypeStruct((nb, bsz, S5_LANES), F32),
        jax.ShapeDtypeStruct((nb, bsz, S5_LANES), F32),
        jax.ShapeDtypeStruct((nb, POOL_BUF * bsz, POOL_WIDTH), F32),
    ]
    scratch = [
        pltpu.VMEM((rows, S5_LANES), F32), pltpu.VMEM((rows, S5_LANES), F32),
        pltpu.VMEM((rows, S5_LANES), F32), pltpu.VMEM((rows, S5_LANES), F32),
        pltpu.VMEM((bsz, S5_LANES), F32), pltpu.VMEM((bsz, S5_LANES), F32),
        pltpu.VMEM(((tc + POOL_HIST) * bsz, POOL_WIDTH), F32),
    ]
    return pl.pallas_call(
        functools.partial(_mixer_kernel, bsz=bsz, tc=tc, past=past, nchunks=nchunks, has_tail=ntail > 0),
        grid=(nb, steps), in_specs=in_specs, out_specs=out_specs, out_shape=out_shape,
        scratch_shapes=scratch, name="mixer",
        compiler_params=pltpu.CompilerParams(dimension_semantics=("arbitrary", "arbitrary"),
                                             vmem_limit_bytes=VMEM_LIMIT),
    )(*args)


def _gating(logits):
    r = logits.shape[0]
    lane = lax.broadcasted_iota(jnp.int32, (r, LANES), 1)
    lanef = lane.astype(F32)
    neg = -jnp.inf
    is_group = jnp.logical_and(lane >= N_EXPERTS, lane < N_EXPERTS + N_EXPERT_GROUPS)
    lg = jnp.where(is_group, logits, neg)
    gmax = jnp.max(lg, axis=-1, keepdims=True)
    g_w = 1.0 / jnp.sum(jnp.exp(lg - gmax), axis=-1, keepdims=True)
    g_idx = jnp.min(jnp.where(lg == gmax, lanef, float(LANES)), axis=-1, keepdims=True) - float(N_EXPERTS)
    in_group = (lane // EXPERTS_PER_GROUP).astype(F32) == g_idx
    el = jnp.where(jnp.logical_and(in_group, lane < N_EXPERTS), logits, neg)
    m1 = jnp.max(el, axis=-1, keepdims=True)
    i1 = jnp.min(jnp.where(el == m1, lanef, float(LANES)), axis=-1, keepdims=True)
    el2 = jnp.where(lanef == i1, neg, el)
    m2 = jnp.max(el2, axis=-1, keepdims=True)
    i2 = jnp.min(jnp.where(el2 == m2, lanef, float(LANES)), axis=-1, keepdims=True)
    e2 = jnp.exp(m2 - m1)
    den = 1.0 + e2
    gate1 = g_w / den
    gate2 = g_w * e2 / den
    comb = jnp.where(lanef == i1, gate1, 0.0) + jnp.where(lanef == i2, gate2, 0.0)
    return comb, g_idx


def _route_sort_kernel(x1_ref, gffn_ref, wrh_ref, wrl_ref, br_ref, sorted_ref, dest_ref, gid_ref):
    x = x1_ref[...]
    h2 = _rms(x, gffn_ref[...])
    hi = h2.astype(BF16)
    lo = (h2 - hi.astype(F32)).astype(BF16)
    logits = (jnp.dot(hi, wrh_ref[...], preferred_element_type=F32)
              + jnp.dot(hi, wrl_ref[...], preferred_element_type=F32)
              + jnp.dot(lo, wrh_ref[...], preferred_element_type=F32)) + br_ref[...]
    comb, g_idx = _gating(logits)

    lanef = lax.broadcasted_iota(jnp.int32, (TILE, LANES), 1).astype(F32)
    onehot = jnp.where(lanef == g_idx, 1.0, 0.0)
    ri = lax.broadcasted_iota(jnp.int32, (TILE, TILE), 0)
    ci = lax.broadcasted_iota(jnp.int32, (TILE, TILE), 1)
    earlier = jnp.where(ci < ri, 1.0, 0.0).astype(BF16)
    rank = jnp.dot(earlier, onehot.astype(BF16), preferred_element_type=F32)
    counts = jnp.sum(onehot, axis=0, keepdims=True)
    padded = jnp.floor((counts + float(GRAN - 1)) * (1.0 / GRAN)) * float(GRAN)
    e0 = padded[:, 0:1]
    e1 = e0 + padded[:, 1:2]
    e2 = e1 + padded[:, 2:3]
    e3 = e2 + padded[:, 3:4]
    lane1 = lax.broadcasted_iota(jnp.int32, (1, LANES), 1)
    base = (jnp.where(lane1 == 1, e0, 0.0) + jnp.where(lane1 == 2, e1, 0.0)
            + jnp.where(lane1 == 3, e2, 0.0))
    dest = jnp.sum(onehot * (base + rank), axis=-1, keepdims=True)
    dest_ref[...] = dest

    dest_row = jnp.sum(jnp.where(ri == ci, dest, 0.0), axis=0, keepdims=True)
    slot = lax.broadcasted_iota(jnp.int32, (SORTED_ROWS, TILE), 0).astype(F32)
    perm = jnp.where(slot == dest_row, 1.0, 0.0).astype(BF16)
    c_hi = comb.astype(BF16)
    c_lo = (comb - c_hi.astype(F32)).astype(BF16)
    payload = jnp.concatenate([hi, c_hi, c_lo], axis=-1)
    sorted_ref[...] = jnp.dot(perm, payload, preferred_element_type=F32).astype(BF16)

    q = lane1.astype(F32) * float(GRAN)
    gid = (jnp.where(q >= e0, 1, 0) + jnp.where(q >= e1, 1, 0)
           + jnp.where(q >= e2, 1, 0) + jnp.where(q >= e3, 1, 0))
    gid_ref[0] = gid.astype(jnp.int32)


def _route_sort(x1, lw, l):
    ntiles = x1.shape[0] // TILE
    params = [lw[k] for k in ('norm_ffn', 'wr_hi', 'wr_lo', 'b_r')]
    return pl.pallas_call(
        _route_sort_kernel, grid=(ntiles,),
        in_specs=[pl.BlockSpec((TILE, D_MODEL), lambda i: (i, 0))] + [_layer_spec(p, l) for p in params],
        out_specs=[pl.BlockSpec((SORTED_ROWS, PAYLOAD), lambda i: (i, 0)),
                   pl.BlockSpec((TILE, 1), lambda i: (i, 0)),
                   pl.BlockSpec((1, 1, LANES), lambda i: (i, 0, 0))],
        out_shape=[jax.ShapeDtypeStruct((ntiles * SORTED_ROWS, PAYLOAD), BF16),
                   jax.ShapeDtypeStruct((x1.shape[0], 1), F32),
                   jax.ShapeDtypeStruct((ntiles, 1, LANES), jnp.int32)],
        name="moe_route_sort",
        compiler_params=pltpu.CompilerParams(dimension_semantics=("arbitrary",),
                                             vmem_limit_bytes=VMEM_LIMIT),
    )(x1, *params)


def _experts_kernel(src_ref, wgid_ref, wvalid_ref, *refs):
    gran_refs = refs[:ITEM_GRANS]
    w1f_ref, w3f_ref, w2f_ref, out_ref, w1_ref, w3_ref, w2_ref = refs[ITEM_GRANS:]
    j = pl.program_id(0)

    @pl.when(jnp.logical_or(j == 0, wgid_ref[j] != wgid_ref[jnp.maximum(j - 1, 0)]))
    def _():
        w1_ref[...] = w1f_ref[...].astype(BF16)
        w3_ref[...] = w3f_ref[...].astype(BF16)
        w2_ref[...] = w2f_ref[...].astype(BF16)

    @pl.when(wvalid_ref[j] > 0)
    def _():
        full = jnp.concatenate([g[...] for g in gran_refs], axis=0)
        h = full[:, :D_MODEL]
        comb = (full[:, D_MODEL:D_MODEL + LANES].astype(F32)
                + full[:, D_MODEL + LANES:].astype(F32))
        first = wgid_ref[j] * EXPERTS_PER_GROUP
        lane = lax.broadcasted_iota(jnp.int32, (ITEM_ROWS, LANES), 1)
        hids = []
        for e in range(EXPERTS_PER_GROUP):
            a = jnp.dot(h, w1_ref[e], preferred_element_type=F32)
            b = jnp.dot(h, w3_ref[e], preferred_element_type=F32)
            gate = jnp.sum(jnp.where(lane == first + e, comb, 0.0), axis=-1, keepdims=True)
            hids.append(((a * jax.nn.sigmoid(a)) * b * gate).astype(BF16))
        hid = jnp.concatenate(hids, axis=-1)
        w2g = w2_ref[...].reshape(EXPERTS_PER_GROUP * D_EXPERT, D_MODEL)
        out_ref[...] = jnp.dot(hid, w2g, preferred_element_type=F32).astype(BF16)

    @pl.when(wvalid_ref[j] == 0)
    def _():
        out_ref[...] = jnp.zeros(out_ref.shape, out_ref.dtype)


def _experts(sorted_tok, src, wgid, wvalid, lw, l):
    nitems = wgid.shape[0]
    gran_spec = lambda s: pl.BlockSpec((GRAN, PAYLOAD), lambda j, src, wg, wv: (src[j * ITEM_GRANS + s], 0))
    wspec = lambda shape: pl.BlockSpec(shape, lambda j, src, wg, wv: (l * N_EXPERT_GROUPS + wg[j], 0, 0))
    grid_spec = pltpu.PrefetchScalarGridSpec(
        num_scalar_prefetch=3, grid=(nitems,),
        in_specs=[gran_spec(s) for s in range(ITEM_GRANS)] + [
            wspec((EXPERTS_PER_GROUP, D_MODEL, D_EXPERT)), wspec((EXPERTS_PER_GROUP, D_MODEL, D_EXPERT)),
            wspec((EXPERTS_PER_GROUP, D_EXPERT, D_MODEL))],
        out_specs=pl.BlockSpec((ITEM_ROWS, D_MODEL), lambda j, src, wg, wv: (j, 0)),
        scratch_shapes=[pltpu.VMEM((EXPERTS_PER_GROUP, D_MODEL, D_EXPERT), BF16),
                        pltpu.VMEM((EXPERTS_PER_GROUP, D_MODEL, D_EXPERT), BF16),
                        pltpu.VMEM((EXPERTS_PER_GROUP, D_EXPERT, D_MODEL), BF16)])
    return pl.pallas_call(
        _experts_kernel, grid_spec=grid_spec,
        out_shape=jax.ShapeDtypeStruct((nitems * ITEM_ROWS, D_MODEL), BF16),
        name="moe_experts",
        compiler_params=pltpu.CompilerParams(dimension_semantics=("arbitrary",),
                                             vmem_limit_bytes=VMEM_LIMIT),
    )(src, wgid, wvalid, *([sorted_tok] * ITEM_GRANS), lw['w1'], lw['w3'], lw['w2'])


def _unsort_kernel(pos_ref, x1_ref, dest_ref, *refs, final):
    gran_refs = refs[:TILE_GRANS]
    gfin_ref, out_ref = refs[TILE_GRANS:]
    s = jnp.concatenate([g[...] for g in gran_refs], axis=0)
    slot = lax.broadcasted_iota(jnp.int32, (TILE, SORTED_ROWS), 1).astype(F32)
    perm_t = jnp.where(slot == dest_ref[...], 1.0, 0.0).astype(BF16)
    y = x1_ref[...] + jnp.dot(perm_t, s, preferred_element_type=F32)
    if not final:
        out_ref[...] = y
    else:
        y = _rms(y, gfin_ref[...])
        bsz, tc, _ = out_ref.shape
        for t in range(tc):
            out_ref[:, t, :] = y[t * bsz:(t + 1) * bsz, :]


def _unsort(x1, dest, expert_out, pos, norm_final, *, tile_off=0, ntiles=None, batch_major=None):
    final = batch_major is not None
    if final:
        batch, t_len, bsz, tc = batch_major
        nchunks = t_len // tc
        out_spec = pl.BlockSpec((bsz, tc, D_MODEL), lambda k, pos: (k // nchunks, k % nchunks, 0))
        out_shape = jax.ShapeDtypeStruct((batch, t_len, D_MODEL), F32)
    else:
        ntiles = x1.shape[0] // TILE
        out_spec = pl.BlockSpec((TILE, D_MODEL), lambda k, pos: (k, 0))
        out_shape = jax.ShapeDtypeStruct(x1.shape, F32)
    gran_spec = lambda q: pl.BlockSpec((GRAN, D_MODEL),
                                       lambda k, pos: (pos[(k + tile_off) * TILE_GRANS + q], 0))
    grid_spec = pltpu.PrefetchScalarGridSpec(
        num_scalar_prefetch=1, grid=(ntiles,),
        in_specs=[pl.BlockSpec((TILE, D_MODEL), lambda k, pos: (k + tile_off, 0)),
                  pl.BlockSpec((TILE, 1), lambda k, pos: (k + tile_off, 0))]
                 + [gran_spec(q) for q in range(TILE_GRANS)]
                 + [pl.BlockSpec((1, D_MODEL), lambda k, pos: (0, 0))],
        out_specs=out_spec)
    return pl.pallas_call(
        functools.partial(_unsort_kernel, final=final), grid_spec=grid_spec,
        out_shape=out_shape, name="moe_unsort",
        compiler_params=pltpu.CompilerParams(dimension_semantics=("arbitrary",),
                                             vmem_limit_bytes=VMEM_LIMIT),
    )(pos, x1, dest, *([expert_out] * TILE_GRANS), norm_final)


def _dispatch_tables(gid):
    ntiles = gid.shape[0]
    ngran = ntiles * TILE_GRANS
    nitems = -(-ngran // ITEM_GRANS) + N_EXPERT_GROUPS
    g = gid[:, 0, :TILE_GRANS].reshape(ngran)
    onehot = (g[:, None] == jnp.arange(N_EXPERT_GROUPS, dtype=jnp.int32)[None, :]).astype(jnp.int32)
    valid = g < N_EXPERT_GROUPS
    rank = jnp.cumsum(onehot, axis=0) - onehot
    cnt = jnp.sum(onehot, axis=0)
    pcnt = (cnt + (ITEM_GRANS - 1)) // ITEM_GRANS * ITEM_GRANS
    pend = jnp.cumsum(pcnt)
    pstart = pend - pcnt
    pos = jnp.sum(onehot * (pstart[None, :] + rank), axis=1)
    slots = jnp.arange(nitems * ITEM_GRANS, dtype=jnp.int32)
    hit = jnp.logical_and(pos[None, :] == slots[:, None], valid[None, :])
    src = jnp.sum(jnp.where(hit, jnp.arange(ngran, dtype=jnp.int32)[None, :], 0), axis=1)
    item_start = jnp.arange(nitems, dtype=jnp.int32) * ITEM_GRANS
    last_group = jnp.max(jnp.where(pcnt > 0, jnp.arange(N_EXPERT_GROUPS, dtype=jnp.int32), 0))
    wgid = jnp.minimum(jnp.sum((item_start[:, None] >= pend[None, :]).astype(jnp.int32), axis=1), last_group)
    wvalid = (item_start < pend[-1]).astype(jnp.int32)
    return src.astype(jnp.int32), wgid.astype(jnp.int32), wvalid, jnp.where(valid, pos, 0).astype(jnp.int32)


def _moe(x1, lw, l, norm_final, final_layouts=None):
    sorted_tok, dest, gid = _route_sort(x1, lw, l)
    src, wgid, wvalid, pos = _dispatch_tables(gid)
    expert_out = _experts(sorted_tok, src, wgid, wvalid, lw, l)
    if final_layouts is None:
        return _unsort(x1, dest, expert_out, pos, norm_final)
    return [_unsort(x1, dest, expert_out, pos, norm_final, tile_off=off, ntiles=n, batch_major=bm)
            for off, n, bm in final_layouts]


def _router_weights(w_grp, b_grp, w_er, b_er):
    pad = LANES - N_EXPERTS - N_EXPERT_GROUPS
    w = jnp.pad(jnp.concatenate([w_er, w_grp], axis=-1), ((0, 0), (0, 0), (0, pad)))
    b = jnp.pad(jnp.concatenate([b_er, b_grp], axis=-1), ((0, 0), (0, pad)))
    hi = w.astype(BF16)
    lo = (w - hi.astype(F32)).astype(BF16)
    return hi, lo, b


def _to_time_major(x, nb):
    bt, t, w = x.shape
    return jnp.transpose(x.reshape(nb, bt // nb, t, w), (0, 2, 1, 3)).reshape(nb, t * (bt // nb), w)


def _from_time_major(x, t):
    nb, rows, w = x.shape
    bsz = rows // t
    return jnp.transpose(x.reshape(nb, t, bsz, w), (0, 2, 1, 3)).reshape(nb * bsz, t, w)


def kernel(x_prompt, x_sample, state_ssm_re, state_ssm_im, state_pool, norm_mix, w_in, lam_re, lam_im, log_dt, b_re, b_im, c_re, c_im, d_skip, w_glu, b_glu, w_pool, pool_scale, w_out, norm_ffn, w_grp, b_grp, w_erouter, b_erouter, w1, w3, w2, norm_final):
    depth = w_in.shape[0]
    pb, pt, _ = x_prompt.shape
    sb, st, _ = x_sample.shape
    p_tc = TILE // pb
    p_tiles = pt // p_tc
    s_bsz = TILE // st
    s_nb = sb // s_bsz
    p_rows = pb * pt

    are, aim, bbr, bbi = _discretise(lam_re, lam_im, log_dt, b_re, b_im)
    wx = jnp.concatenate([_blockdiag(bbr, True), _blockdiag(bbi, True)], axis=-1).astype(BF16)
    cm = jnp.concatenate([_blockdiag(c_re, False), _blockdiag(-c_im, False)], axis=-2).astype(BF16)

    wr_hi, wr_lo, b_r = _router_weights(w_grp, b_grp, w_erouter, b_erouter)
    stack_experts = lambda w: w.reshape((depth * N_EXPERTS,) + w.shape[2:])
    vec = lambda a: a[:, None, :]
    lw = dict(norm_mix=vec(norm_mix), w_in=w_in.astype(BF16), wx=wx, are=are, aim=aim, cm=cm,
              d_skip=vec(d_skip), w_glu=w_glu.astype(BF16), b_glu=vec(b_glu), w_pool=w_pool.astype(BF16),
              pool_scale=vec(pool_scale), w_out=w_out.astype(BF16), norm_ffn=vec(norm_ffn),
              wr_hi=wr_hi, wr_lo=wr_lo, b_r=vec(b_r),
              w1=stack_experts(w1), w3=stack_experts(w3), w2=stack_experts(w2))
    nfin = norm_final[None]

    p_zero_h = jnp.zeros((1, pb, S5_LANES), F32)
    p_zero_hist = jnp.zeros((1, POOL_HIST * pb, POOL_WIDTH), F32)
    final_layouts = [(0, p_tiles, (pb, pt, pb, p_tc)), (p_tiles, s_nb, (sb, st, s_bsz, st))]

    outs = {k: [] for k in ('p_re', 'p_im', 'p_pool', 's_re', 's_im', 's_pool')}
    x = None
    for l in range(depth):
        h0re = state_ssm_re[l].reshape(s_nb, s_bsz, S5_LANES)
        h0im = state_ssm_im[l].reshape(s_nb, s_bsz, S5_LANES)
        hist = jnp.pad(_to_time_major(state_pool[l], s_nb), ((0, 0), ((POOL_HIST - POOL_BUF) * s_bsz, 0), (0, 0)))
        xs1, hre, him, pool = _mixer(x_sample if l == 0 else x, h0re, h0im, hist, lw, l, bsz=s_bsz, tc=st,
                                     past=POOL_BUF, nb=s_nb, nchunks=1, in_off=p_tiles)
        outs['s_re'].append(hre.reshape(sb, S5_GROUPS, S5_STATE))
        outs['s_im'].append(him.reshape(sb, S5_GROUPS, S5_STATE))
        outs['s_pool'].append(_from_time_major(pool, POOL_BUF))
        x1, hre, him, pool = _mixer(x_prompt if l == 0 else x, p_zero_h, p_zero_h, p_zero_hist, lw, l,
                                    bsz=pb, tc=p_tc, past=0, nb=1, nchunks=p_tiles, in_off=0, tail=xs1)
        outs['p_re'].append(hre.reshape(pb, S5_GROUPS, S5_STATE))
        outs['p_im'].append(him.reshape(pb, S5_GROUPS, S5_STATE))
        outs['p_pool'].append(_from_time_major(pool, POOL_BUF))
        x = _moe(x1, lw, l, nfin, final_layouts if l == depth - 1 else None)

    y_prompt, y_sample = x
    return (y_prompt, y_sample, jnp.stack(outs['p_re']), jnp.stack(outs['p_im']), jnp.stack(outs['p_pool']),
            jnp.stack(outs['s_re']), jnp.stack(outs['s_im']), jnp.stack(outs['s_pool']))
```

```python
import functools

import jax
import jax.numpy as jnp
from jax import lax
from jax.experimental import pallas as pl
from jax.experimental.pallas import tpu as pltpu

D_MODEL = 1024
S5_WIDTH = 512
S5_GROUP_CH = 16
S5_GROUPS = 32
S5_STATE = 64
S5_LANES = S5_GROUPS * S5_STATE
POOL_WIDTH = 512
POOL_WINDOWS = (2, 4, 8, 16)
POOL_GROUP_CH = 128
POOL_BUF = 15
POOL_HIST = 16
N_EXPERTS = 16
EXPERTS_PER_GROUP = 4
N_EXPERT_GROUPS = 4
D_EXPERT = 256
EPS = 1e-6

SUBLANES = 8
LANES = 128
S5_KBLOCK = 128
S5_NBLOCK = S5_KBLOCK // S5_GROUP_CH * S5_STATE
SCAN_LANES = 512
VMEM_LIMIT = 56 * 1024 * 1024

TILE = 512
GRAN = 32
TILE_GRANS = TILE // GRAN + N_EXPERT_GROUPS
SORTED_ROWS = TILE_GRANS * GRAN
ITEM_GRANS = 16
ITEM_ROWS = ITEM_GRANS * GRAN
PAYLOAD = D_MODEL + 2 * LANES

F32 = jnp.float32
BF16 = jnp.bfloat16

MIXER_PARAMS = ('norm_mix', 'w_in', 'wx', 'are', 'aim', 'cm', 'd_skip', 'w_glu', 'b_glu', 'w_pool',
                'pool_scale', 'w_out')


def _layer_spec(arr, l):
    return pl.BlockSpec((None,) + arr.shape[1:], lambda *_: (l,) + (0,) * (arr.ndim - 1))


def _rms(x, g):
    return x * lax.rsqrt(jnp.mean(x * x, axis=-1, keepdims=True) + EPS) * g


def _gelu_tanh(x):
    return 0.5 * x * (1.0 + jnp.tanh(0.7978845608028654 * (x + 0.044715 * (x * x * x))))


def _disc_kernel(lr_ref, li_ref, ldt_ref, br_ref, bi_ref, are_ref, aim_ref, bbr_ref, bbi_ref):
    lr = lr_ref[...]
    li = li_ref[...]
    dt = jnp.exp(ldt_ref[...])
    mag = jnp.exp(lr * dt)
    ab_re = mag * jnp.cos(li * dt)
    ab_im = mag * jnp.sin(li * dt)
    den = lr * lr + li * li
    nr = ab_re - 1.0
    coef_re = (nr * lr + ab_im * li) / den
    coef_im = (ab_im * lr - nr * li) / den
    br = br_ref[...]
    bi = bi_ref[...]
    are_ref[...] = ab_re
    aim_ref[...] = ab_im
    bbr_ref[...] = coef_re * br - coef_im * bi
    bbi_ref[...] = coef_re * bi + coef_im * br


def _discretise(lam_re, lam_im, log_dt, b_re, b_im):
    depth = lam_re.shape[0]
    rows = depth * S5_GROUPS * S5_GROUP_CH
    rep = lambda a: jnp.repeat(a.reshape(depth * S5_GROUPS, -1), S5_GROUP_CH, axis=0)
    tr = lambda b: jnp.transpose(b, (0, 1, 3, 2)).reshape(rows, S5_STATE)
    out = jax.ShapeDtypeStruct((rows, S5_STATE), F32)
    are, aim, bbr, bbi = pl.pallas_call(
        _disc_kernel, out_shape=(out, out, out, out), name="s5_discretise",
    )(rep(lam_re), rep(lam_im), rep(log_dt[..., None]), tr(b_re), tr(b_im))
    shp = (depth, S5_GROUPS, S5_GROUP_CH, S5_STATE)
    are = are.reshape(shp)[:, :, 0, :].reshape(depth, 1, S5_LANES)
    aim = aim.reshape(shp)[:, :, 0, :].reshape(depth, 1, S5_LANES)
    return are, aim, bbr.reshape(shp), bbi.reshape(shp)


def _blockdiag(w, rows_first):
    depth = w.shape[0]
    gpb = S5_KBLOCK // S5_GROUP_CH
    w5 = w.reshape(depth, S5_GROUPS // gpb, gpb, S5_GROUP_CH, S5_STATE)
    eye = jnp.eye(gpb, dtype=w.dtype)
    if rows_first:
        return jnp.einsum('lkgcn,gh->lkgchn', w5, eye).reshape(depth, -1, S5_KBLOCK, S5_NBLOCK)
    return jnp.einsum('lkgcn,gh->lkgnhc', w5, eye).reshape(depth, -1, S5_NBLOCK, S5_KBLOCK)


MIXER_INPUTS = 16


def _mixer_kernel(*refs, bsz, tc, past, nchunks, has_tail):
    if not has_tail:
        _mixer_body(*refs, bsz=bsz, tc=tc, past=past, last=nchunks - 1)
        return
    tail_ref = refs[MIXER_INPUTS]
    body_refs = refs[:MIXER_INPUTS] + refs[MIXER_INPUTS + 1:]
    x1_ref = body_refs[MIXER_INPUTS]
    i = pl.program_id(1)

    @pl.when(i < nchunks)
    def _():
        _mixer_body(*body_refs, bsz=bsz, tc=tc, past=past, last=nchunks - 1)

    @pl.when(i >= nchunks)
    def _():
        x1_ref[...] = tail_ref[...]


def _mixer_body(x_ref, h0re_ref, h0im_ref, hist_ref, gmix_ref, win_ref, wx_ref, are_ref, aim_ref,
                cm_ref, dskip_ref, wglu_ref, bglu_ref, wpool_ref, pscale_ref, wout_ref,
                x1_ref, hre_out, him_out, pool_out, xre, xim, sre, sim, hre, him, zbuf, *, bsz, tc, past, last):
    i = pl.program_id(1)
    rows = bsz * tc
    hist_rows = POOL_HIST * bsz

    @pl.when(i == 0)
    def _():
        hre[...] = h0re_ref[0]
        him[...] = h0im_ref[0]
        zbuf[0:hist_rows, :] = hist_ref[0]

    if len(x_ref.shape) == 3:
        x = jnp.concatenate([x_ref[:, t, :] for t in range(tc)], axis=0)
    else:
        x = x_ref[...]
    h = _rms(x, gmix_ref[...])
    proj = jnp.dot(h.astype(BF16), win_ref[...], preferred_element_type=F32)
    u = proj[:, :S5_WIDTH]
    z = proj[:, S5_WIDTH:]

    assert SCAN_LANES == S5_NBLOCK
    ub = u.astype(BF16)

    def scan_rows(r0, ls, static):
        ar = jnp.broadcast_to(are_ref[:, ls], (SUBLANES, SCAN_LANES))
        ai = jnp.broadcast_to(aim_ref[:, ls], (SUBLANES, SCAN_LANES))

        def step(t, carry):
            hr, hi = carry
            row = t * bsz + r0
            if not static:
                row = pl.multiple_of(row, SUBLANES)
            nr = ar * hr + (xre[pl.ds(row, SUBLANES), ls] - ai * hi)
            ni = ar * hi + (xim[pl.ds(row, SUBLANES), ls] + ai * hr)
            sre[pl.ds(row, SUBLANES), ls] = nr
            sim[pl.ds(row, SUBLANES), ls] = ni
            return nr, ni

        carry = (hre[pl.ds(r0, SUBLANES), ls], him[pl.ds(r0, SUBLANES), ls])
        if static:
            for t in range(tc):
                carry = step(t, carry)
        else:
            carry = lax.fori_loop(0, tc, step, carry, unroll=8)
        hre[pl.ds(r0, SUBLANES), ls] = carry[0]
        him[pl.ds(r0, SUBLANES), ls] = carry[1]

    ys = []
    for kb in range(S5_WIDTH // S5_KBLOCK):
        ns = slice(kb * S5_NBLOCK, (kb + 1) * S5_NBLOCK)
        xx = jnp.dot(ub[:, kb * S5_KBLOCK:(kb + 1) * S5_KBLOCK], wx_ref[kb],
                     preferred_element_type=F32)
        xre[:, ns] = xx[:, :S5_NBLOCK]
        xim[:, ns] = xx[:, S5_NBLOCK:]
        if bsz == SUBLANES:
            scan_rows(0, ns, True)
        else:
            def body(rb, c, ns=ns):
                scan_rows(pl.multiple_of(rb * SUBLANES, SUBLANES), ns, False)
                return c
            lax.fori_loop(0, bsz // SUBLANES, body, 0)
        ys.append(jnp.dot(sre[:, ns].astype(BF16), cm_ref[kb, :S5_NBLOCK, :], preferred_element_type=F32)
                  + jnp.dot(sim[:, ns].astype(BF16), cm_ref[kb, S5_NBLOCK:, :], preferred_element_type=F32))
    y = jnp.concatenate(ys, axis=-1) + dskip_ref[...] * u
    g = _gelu_tanh(y)
    s5_out = g * jax.nn.sigmoid(jnp.dot(g.astype(BF16), wglu_ref[...], preferred_element_type=F32)
                                + bglu_ref[...])

    zbuf[hist_rows:hist_rows + rows, :] = z
    step_idx = i * tc + lax.broadcasted_iota(jnp.int32, (rows, 1), 0) // bsz
    pos = (step_idx + (past + 1)).astype(F32)
    pouts = []
    for gi, w in enumerate(POOL_WINDOWS):
        gs = slice(gi * POOL_GROUP_CH, (gi + 1) * POOL_GROUP_CH)
        s = zbuf[:, gs]
        k = 1
        while k < w:
            s = s[k * bsz:] + s[:-k * bsz]
            k *= 2
        inv_cnt = 1.0 / jnp.minimum(pos, float(w))
        pooled = s[-rows:] * inv_cnt - z[:, gs]
        pouts.append(jnp.dot(pooled.astype(BF16), wpool_ref[gi], preferred_element_type=F32))
    pool_mixed = jnp.concatenate(pouts, axis=-1) * pscale_ref[...]

    mix = jnp.concatenate([s5_out, pool_mixed], axis=-1).astype(BF16)
    x1_ref[...] = x + jnp.dot(mix, wout_ref[...], preferred_element_type=F32)

    @pl.when(i == last)
    def _():
        hre_out[0] = hre[...]
        him_out[0] = him[...]
        pool_out[0] = zbuf[(tc + POOL_HIST - POOL_BUF) * bsz:(tc + POOL_HIST) * bsz, :]

    @pl.when(i != last)
    def _():
        zbuf[0:hist_rows, :] = zbuf[tc * bsz:(tc + POOL_HIST) * bsz, :]


def _mixer(x, h0re, h0im, hist, lw, l, *, bsz, tc, past, nb, nchunks, in_off, tail=None):
    rows = bsz * tc
    assert rows == TILE
    ntail = 0 if tail is None else tail.shape[0] // TILE
    assert ntail == 0 or nb == 1
    steps = nchunks + ntail
    chunk = lambda i: jnp.minimum(i, nchunks - 1)
    perb = lambda shape: pl.BlockSpec(shape, lambda b, i: (b,) + (0,) * (len(shape) - 1))
    if x.ndim == 2:
        x_spec = pl.BlockSpec((rows, D_MODEL), lambda b, i: (in_off + b * nchunks + chunk(i), 0))
    else:
        x_spec = pl.BlockSpec((bsz, tc, D_MODEL), lambda b, i: (b, chunk(i), 0))
    in_specs = [
        x_spec,
        perb((1, bsz, S5_LANES)), perb((1, bsz, S5_LANES)), perb((1, POOL_HIST * bsz, POOL_WIDTH)),
    ] + [_layer_spec(lw[k], l) for k in MIXER_PARAMS]
    args = [x, h0re, h0im, hist] + [lw[k] for k in MIXER_PARAMS]
    assert len(args) == MIXER_INPUTS
    if ntail:
        in_specs.append(pl.BlockSpec((TILE, D_MODEL), lambda b, i: (jnp.maximum(i - nchunks, 0), 0)))
        args.append(tail)
    out_specs = [
        pl.BlockSpec((rows, D_MODEL), lambda b, i: (b * steps + i, 0)),
        perb((1, bsz, S5_LANES)), perb((1, bsz, S5_LANES)), perb((1, POOL_BUF * bsz, POOL_WIDTH)),
    ]
    out_shape = [
        jax.ShapeDtypeStruct((nb * steps * TILE, D_MODEL), F32),
        jax.ShapeDtypeStruct((nb, bsz, S5_LANES), F32),
        jax.ShapeDtypeStruct((nb, bsz, S5_LANES), F32),
        jax.ShapeDtypeStruct((nb, POOL_BUF * bsz, POOL_WIDTH), F32),
    ]
    scratch = [
        pltpu.VMEM((rows, S5_LANES), F32), pltpu.VMEM((rows, S5_LANES), F32),
        pltpu.VMEM((rows, S5_LANES), F32), pltpu.VMEM((rows, S5_LANES), F32),
        pltpu.VMEM((bsz, S5_LANES), F32), pltpu.VMEM((bsz, S5_LANES), F32),
        pltpu.VMEM(((tc + POOL_HIST) * bsz, POOL_WIDTH), F32),
    ]
    return pl.pallas_call(
        functools.partial(_mixer_kernel, bsz=bsz, tc=tc, past=past, nchunks=nchunks, has_tail=ntail > 0),
        grid=(nb, steps), in_specs=in_specs, out_specs=out_specs, out_shape=out_shape,
        scratch_shapes=scratch, name="mixer",
        compiler_params=pltpu.CompilerParams(dimension_semantics=("arbitrary", "arbitrary"),
                                             vmem_limit_bytes=VMEM_LIMIT),
    )(*args)


def _gating(logits):
    r = logits.shape[0]
    lane = lax.broadcasted_iota(jnp.int32, (r, LANES), 1)
    lanef = lane.astype(F32)
    neg = -jnp.inf
    is_group = jnp.logical_and(lane >= N_EXPERTS, lane < N_EXPERTS + N_EXPERT_GROUPS)
    lg = jnp.where(is_group, logits, neg)
    gmax = jnp.max(lg, axis=-1, keepdims=True)
    g_w = 1.0 / jnp.sum(jnp.exp(lg - gmax), axis=-1, keepdims=True)
    g_idx = jnp.min(jnp.where(lg == gmax, lanef, float(LANES)), axis=-1, keepdims=True) - float(N_EXPERTS)
    in_group = (lane // EXPERTS_PER_GROUP).astype(F32) == g_idx
    el = jnp.where(jnp.logical_and(in_group, lane < N_EXPERTS), logits, neg)
    m1 = jnp.max(el, axis=-1, keepdims=True)
    i1 = jnp.min(jnp.where(el == m1, lanef, float(LANES)), axis=-1, keepdims=True)
    el2 = jnp.where(lanef == i1, neg, el)
    m2 = jnp.max(el2, axis=-1, keepdims=True)
    i2 = jnp.min(jnp.where(el2 == m2, lanef, float(LANES)), axis=-1, keepdims=True)
    e2 = jnp.exp(m2 - m1)
    den = 1.0 + e2
    gate1 = g_w / den
    gate2 = g_w * e2 / den
    comb = jnp.where(lanef == i1, gate1, 0.0) + jnp.where(lanef == i2, gate2, 0.0)
    return comb, g_idx


def _route_sort_kernel(x1_ref, gffn_ref, wrh_ref, wrl_ref, br_ref, sorted_ref, dest_ref, gid_ref):
    x = x1_ref[...]
    h2 = _rms(x, gffn_ref[...])
    hi = h2.astype(BF16)
    lo = (h2 - hi.astype(F32)).astype(BF16)
    logits = (jnp.dot(hi, wrh_ref[...], preferred_element_type=F32)
              + jnp.dot(hi, wrl_ref[...], preferred_element_type=F32)
              + jnp.dot(lo, wrh_ref[...], preferred_element_type=F32)) + br_ref[...]
    comb, g_idx = _gating(logits)

    lanef = lax.broadcasted_iota(jnp.int32, (TILE, LANES), 1).astype(F32)
    onehot = jnp.where(lanef == g_idx, 1.0, 0.0)
    ri = lax.broadcasted_iota(jnp.int32, (TILE, TILE), 0)
    ci = lax.broadcasted_iota(jnp.int32, (TILE, TILE), 1)
    earlier = jnp.where(ci < ri, 1.0, 0.0).astype(BF16)
    rank = jnp.dot(earlier, onehot.astype(BF16), preferred_element_type=F32)
    counts = jnp.sum(onehot, axis=0, keepdims=True)
    padded = jnp.floor((counts + float(GRAN - 1)) * (1.0 / GRAN)) * float(GRAN)
    e0 = padded[:, 0:1]
    e1 = e0 + padded[:, 1:2]
    e2 = e1 + padded[:, 2:3]
    e3 = e2 + padded[:, 3:4]
    lane1 = lax.broadcasted_iota(jnp.int32, (1, LANES), 1)
    base = (jnp.where(lane1 == 1, e0, 0.0) + jnp.where(lane1 == 2, e1, 0.0)
            + jnp.where(lane1 == 3, e2, 0.0))
    dest = jnp.sum(onehot * (base + rank), axis=-1, keepdims=True)
    dest_ref[...] = dest

    dest_row = jnp.sum(jnp.where(ri == ci, dest, 0.0), axis=0, keepdims=True)
    slot = lax.broadcasted_iota(jnp.int32, (SORTED_ROWS, TILE), 0).astype(F32)
    perm = jnp.where(slot == dest_row, 1.0, 0.0).astype(BF16)
    c_hi = comb.astype(BF16)
    c_lo = (comb - c_hi.astype(F32)).astype(BF16)
    payload = jnp.concatenate([hi, c_hi, c_lo], axis=-1)
    sorted_ref[...] = jnp.dot(perm, payload, preferred_element_type=F32).astype(BF16)

    q = lane1.astype(F32) * float(GRAN)
    gid = (jnp.where(q >= e0, 1, 0) + jnp.where(q >= e1, 1, 0)
           + jnp.where(q >= e2, 1, 0) + jnp.where(q >= e3, 1, 0))
    gid_ref[0] = gid.astype(jnp.int32)


def _route_sort(x1, lw, l):
    ntiles = x1.shape[0] // TILE
    params = [lw[k] for k in ('norm_ffn', 'wr_hi', 'wr_lo', 'b_r')]
    return pl.pallas_call(
        _route_sort_kernel, grid=(ntiles,),
        in_specs=[pl.BlockSpec((TILE, D_MODEL), lambda i: (i, 0))] + [_layer_spec(p, l) for p in params],
        out_specs=[pl.BlockSpec((SORTED_ROWS, PAYLOAD), lambda i: (i, 0)),
                   pl.BlockSpec((TILE, 1), lambda i: (i, 0)),
                   pl.BlockSpec((1, 1, LANES), lambda i: (i, 0, 0))],
        out_shape=[jax.ShapeDtypeStruct((ntiles * SORTED_ROWS, PAYLOAD), BF16),
                   jax.ShapeDtypeStruct((x1.shape[0], 1), F32),
                   jax.ShapeDtypeStruct((ntiles, 1, LANES), jnp.int32)],
        name="moe_route_sort",
        compiler_params=pltpu.CompilerParams(dimension_semantics=("arbitrary",),
                                             vmem_limit_bytes=VMEM_LIMIT),
    )(x1, *params)


def _experts_kernel(src_ref, wgid_ref, wvalid_ref, *refs):
    gran_refs = refs[:ITEM_GRANS]
    w1f_ref, w3f_ref, w2f_ref, out_ref, w1_ref, w3_ref, w2_ref = refs[ITEM_GRANS:]
    j = pl.program_id(0)

    @pl.when(jnp.logical_or(j == 0, wgid_ref[j] != wgid_ref[jnp.maximum(j - 1, 0)]))
    def _():
        w1_ref[...] = w1f_ref[...].astype(BF16)
        w3_ref[...] = w3f_ref[...].astype(BF16)
        w2_ref[...] = w2f_ref[...].astype(BF16)

    @pl.when(wvalid_ref[j] > 0)
    def _():
        full = jnp.concatenate([g[...] for g in gran_refs], axis=0)
        h = full[:, :D_MODEL]
        comb = (full[:, D_MODEL:D_MODEL + LANES].astype(F32)
                + full[:, D_MODEL + LANES:].astype(F32))
        first = wgid_ref[j] * EXPERTS_PER_GROUP
        lane = lax.broadcasted_iota(jnp.int32, (ITEM_ROWS, LANES), 1)
        hids = []
        for e in range(EXPERTS_PER_GROUP):
            a = jnp.dot(h, w1_ref[e], preferred_element_type=F32)
            b = jnp.dot(h, w3_ref[e], preferred_element_type=F32)
            gate = jnp.sum(jnp.where(lane == first + e, comb, 0.0), axis=-1, keepdims=True)
            hids.append(((a * jax.nn.sigmoid(a)) * b * gate).astype(BF16))
        hid = jnp.concatenate(hids, axis=-1)
        w2g = w2_ref[...].reshape(EXPERTS_PER_GROUP * D_EXPERT, D_MODEL)
        out_ref[...] = jnp.dot(hid, w2g, preferred_element_type=F32).astype(BF16)

    @pl.when(wvalid_ref[j] == 0)
    def _():
        out_ref[...] = jnp.zeros(out_ref.shape, out_ref.dtype)


def _experts(sorted_tok, src, wgid, wvalid, lw, l):
    nitems = wgid.shape[0]
    gran_spec = lambda s: pl.BlockSpec((GRAN, PAYLOAD), lambda j, src, wg, wv: (src[j * ITEM_GRANS + s], 0))
    wspec = lambda shape: pl.BlockSpec(shape, lambda j, src, wg, wv: (l * N_EXPERT_GROUPS + wg[j], 0, 0))
    grid_spec = pltpu.PrefetchScalarGridSpec(
        num_scalar_prefetch=3, grid=(nitems,),
        in_specs=[gran_spec(s) for s in range(ITEM_GRANS)] + [
            wspec((EXPERTS_PER_GROUP, D_MODEL, D_EXPERT)), wspec((EXPERTS_PER_GROUP, D_MODEL, D_EXPERT)),
            wspec((EXPERTS_PER_GROUP, D_EXPERT, D_MODEL))],
        out_specs=pl.BlockSpec((ITEM_ROWS, D_MODEL), lambda j, src, wg, wv: (j, 0)),
        scratch_shapes=[pltpu.VMEM((EXPERTS_PER_GROUP, D_MODEL, D_EXPERT), BF16),
                        pltpu.VMEM((EXPERTS_PER_GROUP, D_MODEL, D_EXPERT), BF16),
                        pltpu.VMEM((EXPERTS_PER_GROUP, D_EXPERT, D_MODEL), BF16)])
    return pl.pallas_call(
        _experts_kernel, grid_spec=grid_spec,
        out_shape=jax.ShapeDtypeStruct((nitems * ITEM_ROWS, D_MODEL), BF16),
        name="moe_experts",
        compiler_params=pltpu.CompilerParams(dimension_semantics=("arbitrary",),
                                             vmem_limit_bytes=VMEM_LIMIT),
    )(src, wgid, wvalid, *([sorted_tok] * ITEM_GRANS), lw['w1'], lw['w3'], lw['w2'])


def _unsort_kernel(pos_ref, x1_ref, dest_ref, *refs, final):
    gran_refs = refs[:TILE_GRANS]
    gfin_ref, out_ref = refs[TILE_GRANS:]
    s = jnp.concatenate([g[...] for g in gran_refs], axis=0)
    slot = lax.broadcasted_iota(jnp.int32, (TILE, SORTED_ROWS), 1).astype(F32)
    perm_t = jnp.where(slot == dest_ref[...], 1.0, 0.0).astype(BF16)
    y = x1_ref[...] + jnp.dot(perm_t, s, preferred_element_type=F32)
    if not final:
        out_ref[...] = y
    else:
        y = _rms(y, gfin_ref[...])
        bsz, tc, _ = out_ref.shape
        for t in range(tc):
            out_ref[:, t, :] = y[t * bsz:(t + 1) * bsz, :]


def _unsort(x1, dest, expert_out, pos, norm_final, *, tile_off=0, ntiles=None, batch_major=None):
    final = batch_major is not None
    if final:
        batch, t_len, bsz, tc = batch_major
        nchunks = t_len // tc
        out_spec = pl.BlockSpec((bsz, tc, D_MODEL), lambda k, pos: (k // nchunks, k % nchunks, 0))
        out_shape = jax.ShapeDtypeStruct((batch, t_len, D_MODEL), F32)
    else:
        ntiles = x1.shape[0] // TILE
        out_spec = pl.BlockSpec((TILE, D_MODEL), lambda k, pos: (k, 0))
        out_shape = jax.ShapeDtypeStruct(x1.shape, F32)
    gran_spec = lambda q: pl.BlockSpec((GRAN, D_MODEL),
                                       lambda k, pos: (pos[(k + tile_off) * TILE_GRANS + q], 0))
    grid_spec = pltpu.PrefetchScalarGridSpec(
        num_scalar_prefetch=1, grid=(ntiles,),
        in_specs=[pl.BlockSpec((TILE, D_MODEL), lambda k, pos: (k + tile_off, 0)),
                  pl.BlockSpec((TILE, 1), lambda k, pos: (k + tile_off, 0))]
                 + [gran_spec(q) for q in range(TILE_GRANS)]
                 + [pl.BlockSpec((1, D_MODEL), lambda k, pos: (0, 0))],
        out_specs=out_spec)
    return pl.pallas_call(
        functools.partial(_unsort_kernel, final=final), grid_spec=grid_spec,
        out_shape=out_shape, name="moe_unsort",
        compiler_params=pltpu.CompilerParams(dimension_semantics=("arbitrary",),
                                             vmem_limit_bytes=VMEM_LIMIT),
    )(pos, x1, dest, *([expert_out] * TILE_GRANS), norm_final)


def _dispatch_tables(gid):
    ntiles = gid.shape[0]
    ngran = ntiles * TILE_GRANS
    nitems = -(-ngran // ITEM_GRANS) + N_EXPERT_GROUPS
    g = gid[:, 0, :TILE_GRANS].reshape(ngran)
    onehot = (g[:, None] == jnp.arange(N_EXPERT_GROUPS, dtype=jnp.int32)[None, :]).astype(jnp.int32)
    valid = g < N_EXPERT_GROUPS
    rank = jnp.cumsum(onehot, axis=0) - onehot
    cnt = jnp.sum(onehot, axis=0)
    pcnt = (cnt + (ITEM_GRANS - 1)) // ITEM_GRANS * ITEM_GRANS
    pend = jnp.cumsum(pcnt)
    pstart = pend - pcnt
    pos = jnp.sum(onehot * (pstart[None, :] + rank), axis=1)
    slots = jnp.arange(nitems * ITEM_GRANS, dtype=jnp.int32)
    hit = jnp.logical_and(pos[None, :] == slots[:, None], valid[None, :])
    src = jnp.sum(jnp.where(hit, jnp.arange(ngran, dtype=jnp.int32)[None, :], 0), axis=1)
    item_start = jnp.arange(nitems, dtype=jnp.int32) * ITEM_GRANS
    last_group = jnp.max(jnp.where(pcnt > 0, jnp.arange(N_EXPERT_GROUPS, dtype=jnp.int32), 0))
    wgid = jnp.minimum(jnp.sum((item_start[:, None] >= pend[None, :]).astype(jnp.int32), axis=1), last_group)
    wvalid = (item_start < pend[-1]).astype(jnp.int32)
    return src.astype(jnp.int32), wgid.astype(jnp.int32), wvalid, jnp.where(valid, pos, 0).astype(jnp.int32)


def _moe(x1, lw, l, norm_final, final_layouts=None):
    sorted_tok, dest, gid = _route_sort(x1, lw, l)
    src, wgid, wvalid, pos = _dispatch_tables(gid)
    expert_out = _experts(sorted_tok, src, wgid, wvalid, lw, l)
    if final_layouts is None:
        return _unsort(x1, dest, expert_out, pos, norm_final)
    return [_unsort(x1, dest, expert_out, pos, norm_final, tile_off=off, ntiles=n, batch_major=bm)
            for off, n, bm in final_layouts]


def _router_weights(w_grp, b_grp, w_er, b_er):
    pad = LANES - N_EXPERTS - N_EXPERT_GROUPS
    w = jnp.pad(jnp.concatenate([w_er, w_grp], axis=-1), ((0, 0), (0, 0), (0, pad)))
    b = jnp.pad(jnp.concatenate([b_er, b_grp], axis=-1), ((0, 0), (0, pad)))
    hi = w.astype(BF16)
    lo = (w - hi.astype(F32)).astype(BF16)
    return hi, lo, b


def _to_time_major(x, nb):
    bt, t, w = x.shape
    return jnp.transpose(x.reshape(nb, bt // nb, t, w), (0, 2, 1, 3)).reshape(nb, t * (bt // nb), w)


def _from_time_major(x, t):
    nb, rows, w = x.shape
    bsz = rows // t
    return jnp.transpose(x.reshape(nb, t, bsz, w), (0, 2, 1, 3)).reshape(nb * bsz, t, w)


def kernel(x_prompt, x_sample, state_ssm_re, state_ssm_im, state_pool, norm_mix, w_in, lam_re, lam_im, log_dt, b_re, b_im, c_re, c_im, d_skip, w_glu, b_glu, w_pool, pool_scale, w_out, norm_ffn, w_grp, b_grp, w_erouter, b_erouter, w1, w3, w2, norm_final):
    depth = w_in.shape[0]
    pb, pt, _ = x_prompt.shape
    sb, st, _ = x_sample.shape
    p_tc = TILE // pb
    p_tiles = pt // p_tc
    s_bsz = TILE // st
    s_nb = sb // s_bsz
    p_rows = pb * pt

    are, aim, bbr, bbi = _discretise(lam_re, lam_im, log_dt, b_re, b_im)
    wx = jnp.concatenate([_blockdiag(bbr, True), _blockdiag(bbi, True)], axis=-1).astype(BF16)
    cm = jnp.concatenate([_blockdiag(c_re, False), _blockdiag(-c_im, False)], axis=-2).astype(BF16)

    wr_hi, wr_lo, b_r = _router_weights(w_grp, b_grp, w_erouter, b_erouter)
    stack_experts = lambda w: w.reshape((depth * N_EXPERTS,) + w.shape[2:])
    vec = lambda a: a[:, None, :]
    lw = dict(norm_mix=vec(norm_mix), w_in=w_in.astype(BF16), wx=wx, are=are, aim=aim, cm=cm,
              d_skip=vec(d_skip), w_glu=w_glu.astype(BF16), b_glu=vec(b_glu), w_pool=w_pool.astype(BF16),
              pool_scale=vec(pool_scale), w_out=w_out.astype(BF16), norm_ffn=vec(norm_ffn),
              wr_hi=wr_hi, wr_lo=wr_lo, b_r=vec(b_r),
              w1=stack_experts(w1), w3=stack_experts(w3), w2=stack_experts(w2))
    nfin = norm_final[None]

    p_zero_h = jnp.zeros((1, pb, S5_LANES), F32)
    p_zero_hist = jnp.zeros((1, POOL_HIST * pb, POOL_WIDTH), F32)
    final_layouts = [(0, p_tiles, (pb, pt, pb, p_tc)), (p_tiles, s_nb, (sb, st, s_bsz, st))]

    outs = {k: [] for k in ('p_re', 'p_im', 'p_pool', 's_re', 's_im', 's_pool')}
    x = None
    for l in range(depth):
        h0re = state_ssm_re[l].reshape(s_nb, s_bsz, S5_LANES)
        h0im = state_ssm_im[l].reshape(s_nb, s_bsz, S5_LANES)
        hist = jnp.pad(_to_time_major(state_pool[l], s_nb), ((0, 0), ((POOL_HIST - POOL_BUF) * s_bsz, 0), (0, 0)))
        xs1, hre, him, pool = _mixer(x_sample if l == 0 else x, h0re, h0im, hist, lw, l, bsz=s_bsz, tc=st,
                                     past=POOL_BUF, nb=s_nb, nchunks=1, in_off=p_tiles)
        outs['s_re'].append(hre.reshape(sb, S5_GROUPS, S5_STATE))
        outs['s_im'].append(him.reshape(sb, S5_GROUPS, S5_STATE))
        outs['s_pool'].append(_from_time_major(pool, POOL_BUF))
        x1, hre, him, pool = _mixer(x_prompt if l == 0 else x, p_zero_h, p_zero_h, p_zero_hist, lw, l,
                                    bsz=pb, tc=p_tc, past=0, nb=1, nchunks=p_tiles, in_off=0, tail=xs1)
        outs['p_re'].append(hre.reshape(pb, S5_GROUPS, S5_STATE))
        outs['p_im'].append(him.reshape(pb, S5_GROUPS, S5_STATE))
        outs['p_pool'].append(_from_time_major(pool, POOL_BUF))
        x = _moe(x1, lw, l, nfin, final_layouts if l == depth - 1 else None)

    y_prompt, y_sample = x
    return (y_prompt, y_sample, jnp.stack(outs['p_re']), jnp.stack(outs['p_im']), jnp.stack(outs['p_pool']),
            jnp.stack(outs['s_re']), jnp.stack(outs['s_im']), jnp.stack(outs['s_pool']))
```

```python
import functools

import jax
import jax.numpy as jnp
from jax import lax
from jax.experimental import pallas as pl
from jax.experimental.pallas import tpu as pltpu

D_MODEL = 1024
S5_WIDTH = 512
S5_GROUP_CH = 16
S5_GROUPS = 32
S5_STATE = 64
S5_LANES = S5_GROUPS * S5_STATE
POOL_WIDTH = 512
POOL_WINDOWS = (2, 4, 8, 16)
POOL_GROUP_CH = 128
POOL_BUF = 15
POOL_HIST = 16
N_EXPERTS = 16
EXPERTS_PER_GROUP = 4
N_EXPERT_GROUPS = 4
D_EXPERT = 256
EPS = 1e-6

SUBLANES = 8
LANES = 128
S5_KBLOCK = 128
S5_NBLOCK = S5_KBLOCK // S5_GROUP_CH * S5_STATE
SCAN_LANES = 512
VMEM_LIMIT = 56 * 1024 * 1024

TILE = 512
GRAN = 32
TILE_GRANS = TILE // GRAN + N_EXPERT_GROUPS
SORTED_ROWS = TILE_GRANS * GRAN
ITEM_GRANS = 16
ITEM_ROWS = ITEM_GRANS * GRAN
ROUTE_TILES = 2
PAYLOAD = D_MODEL + 2 * LANES

F32 = jnp.float32
BF16 = jnp.bfloat16

MIXER_PARAMS = ('norm_mix', 'w_in', 'wx', 'are', 'aim', 'cm', 'd_skip', 'w_glu', 'b_glu', 'w_pool',
                'pool_scale', 'w_out')


def _layer_spec(arr, l):
    return pl.BlockSpec((None,) + arr.shape[1:], lambda *_: (l,) + (0,) * (arr.ndim - 1))


def _rms(x, g):
    return x * lax.rsqrt(jnp.mean(x * x, axis=-1, keepdims=True) + EPS) * g


def _gelu_tanh(x):
    return 0.5 * x * (1.0 + jnp.tanh(0.7978845608028654 * (x + 0.044715 * (x * x * x))))


def _disc_kernel(lr_ref, li_ref, ldt_ref, br_ref, bi_ref, are_ref, aim_ref, bbr_ref, bbi_ref):
    lr = lr_ref[...]
    li = li_ref[...]
    dt = jnp.exp(ldt_ref[...])
    mag = jnp.exp(lr * dt)
    ab_re = mag * jnp.cos(li * dt)
    ab_im = mag * jnp.sin(li * dt)
    den = lr * lr + li * li
    nr = ab_re - 1.0
    coef_re = (nr * lr + ab_im * li) / den
    coef_im = (ab_im * lr - nr * li) / den
    br = br_ref[...]
    bi = bi_ref[...]
    are_ref[...] = ab_re
    aim_ref[...] = ab_im
    bbr_ref[...] = coef_re * br - coef_im * bi
    bbi_ref[...] = coef_re * bi + coef_im * br


def _discretise(lam_re, lam_im, log_dt, b_re, b_im):
    depth = lam_re.shape[0]
    rows = depth * S5_GROUPS * S5_GROUP_CH
    rep = lambda a: jnp.repeat(a.reshape(depth * S5_GROUPS, -1), S5_GROUP_CH, axis=0)
    tr = lambda b: jnp.transpose(b, (0, 1, 3, 2)).reshape(rows, S5_STATE)
    out = jax.ShapeDtypeStruct((rows, S5_STATE), F32)
    are, aim, bbr, bbi = pl.pallas_call(
        _disc_kernel, out_shape=(out, out, out, out), name="s5_discretise",
    )(rep(lam_re), rep(lam_im), rep(log_dt[..., None]), tr(b_re), tr(b_im))
    shp = (depth, S5_GROUPS, S5_GROUP_CH, S5_STATE)
    are = are.reshape(shp)[:, :, 0, :].reshape(depth, 1, S5_LANES)
    aim = aim.reshape(shp)[:, :, 0, :].reshape(depth, 1, S5_LANES)
    return are, aim, bbr.reshape(shp), bbi.reshape(shp)


def _blockdiag(w, rows_first):
    depth = w.shape[0]
    gpb = S5_KBLOCK // S5_GROUP_CH
    w5 = w.reshape(depth, S5_GROUPS // gpb, gpb, S5_GROUP_CH, S5_STATE)
    eye = jnp.eye(gpb, dtype=w.dtype)
    if rows_first:
        return jnp.einsum('lkgcn,gh->lkgchn', w5, eye).reshape(depth, -1, S5_KBLOCK, S5_NBLOCK)
    return jnp.einsum('lkgcn,gh->lkgnhc', w5, eye).reshape(depth, -1, S5_NBLOCK, S5_KBLOCK)


MIXER_INPUTS = 16


def _mixer_kernel(*refs, bsz, tc, past, nchunks, has_tail):
    if not has_tail:
        _mixer_body(*refs, bsz=bsz, tc=tc, past=past, last=nchunks - 1)
        return
    tail_ref = refs[MIXER_INPUTS]
    body_refs = refs[:MIXER_INPUTS] + refs[MIXER_INPUTS + 1:]
    x1_ref = body_refs[MIXER_INPUTS]
    i = pl.program_id(1)

    @pl.when(i < nchunks)
    def _():
        _mixer_body(*body_refs, bsz=bsz, tc=tc, past=past, last=nchunks - 1)

    @pl.when(i >= nchunks)
    def _():
        x1_ref[...] = tail_ref[...]


def _mixer_body(x_ref, h0re_ref, h0im_ref, hist_ref, gmix_ref, win_ref, wx_ref, are_ref, aim_ref,
                cm_ref, dskip_ref, wglu_ref, bglu_ref, wpool_ref, pscale_ref, wout_ref,
                x1_ref, hre_out, him_out, pool_out, xre, xim, sre, sim, hre, him, zbuf, *, bsz, tc, past, last):
    i = pl.program_id(1)
    rows = bsz * tc
    hist_rows = POOL_HIST * bsz

    @pl.when(i == 0)
    def _():
        hre[...] = h0re_ref[0]
        him[...] = h0im_ref[0]
        zbuf[0:hist_rows, :] = hist_ref[0]

    if len(x_ref.shape) == 3:
        x = jnp.concatenate([x_ref[:, t, :] for t in range(tc)], axis=0)
    else:
        x = x_ref[...]
    h = _rms(x, gmix_ref[...])
    proj = jnp.dot(h.astype(BF16), win_ref[...], preferred_element_type=F32)
    u = proj[:, :S5_WIDTH]
    z = proj[:, S5_WIDTH:]

    zbuf[hist_rows:hist_rows + rows, :] = z
    step_idx = i * tc + lax.broadcasted_iota(jnp.int32, (rows, 1), 0) // bsz
    pos = (step_idx + (past + 1)).astype(F32)
    pouts = []
    for gi, w in enumerate(POOL_WINDOWS):
        gs = slice(gi * POOL_GROUP_CH, (gi + 1) * POOL_GROUP_CH)
        s = zbuf[:, gs]
        k = 1
        while k < w:
            s = s[k * bsz:] + s[:-k * bsz]
            k *= 2
        inv_cnt = 1.0 / jnp.minimum(pos, float(w))
        pooled = s[-rows:] * inv_cnt - z[:, gs]
        pouts.append(jnp.dot(pooled.astype(BF16), wpool_ref[gi], preferred_element_type=F32))
    pool_mixed = jnp.concatenate(pouts, axis=-1) * pscale_ref[...]

    res = x + jnp.dot(pool_mixed.astype(BF16), wout_ref[S5_WIDTH:, :], preferred_element_type=F32)

    assert SCAN_LANES == S5_NBLOCK
    ub = u.astype(BF16)

    def scan_rows(r0, ls, static):
        ar = jnp.broadcast_to(are_ref[:, ls], (SUBLANES, SCAN_LANES))
        ai = jnp.broadcast_to(aim_ref[:, ls], (SUBLANES, SCAN_LANES))

        def step(t, carry):
            hr, hi = carry
            row = t * bsz + r0
            if not static:
                row = pl.multiple_of(row, SUBLANES)
            nr = ar * hr + (xre[pl.ds(row, SUBLANES), ls] - ai * hi)
            ni = ar * hi + (xim[pl.ds(row, SUBLANES), ls] + ai * hr)
            sre[pl.ds(row, SUBLANES), ls] = nr
            sim[pl.ds(row, SUBLANES), ls] = ni
            return nr, ni

        carry = (hre[pl.ds(r0, SUBLANES), ls], him[pl.ds(r0, SUBLANES), ls])
        if static:
            for t in range(tc):
                carry = step(t, carry)
        else:
            carry = lax.fori_loop(0, tc, step, carry, unroll=8)
        hre[pl.ds(r0, SUBLANES), ls] = carry[0]
        him[pl.ds(r0, SUBLANES), ls] = carry[1]

    ys = []
    for kb in range(S5_WIDTH // S5_KBLOCK):
        ns = slice(kb * S5_NBLOCK, (kb + 1) * S5_NBLOCK)
        xx = jnp.dot(ub[:, kb * S5_KBLOCK:(kb + 1) * S5_KBLOCK], wx_ref[kb],
                     preferred_element_type=F32)
        xre[:, ns] = xx[:, :S5_NBLOCK]
        xim[:, ns] = xx[:, S5_NBLOCK:]
        if bsz == SUBLANES:
            scan_rows(0, ns, True)
        else:
            def body(rb, c, ns=ns):
                scan_rows(pl.multiple_of(rb * SUBLANES, SUBLANES), ns, False)
                return c
            lax.fori_loop(0, bsz // SUBLANES, body, 0)
        ys.append(jnp.dot(sre[:, ns].astype(BF16), cm_ref[kb, :S5_NBLOCK, :], preferred_element_type=F32)
                  + jnp.dot(sim[:, ns].astype(BF16), cm_ref[kb, S5_NBLOCK:, :], preferred_element_type=F32))
    y = jnp.concatenate(ys, axis=-1) + dskip_ref[...] * u
    g = _gelu_tanh(y)
    s5_out = g * jax.nn.sigmoid(jnp.dot(g.astype(BF16), wglu_ref[...], preferred_element_type=F32)
                                + bglu_ref[...])

    x1_ref[...] = res + jnp.dot(s5_out.astype(BF16), wout_ref[:S5_WIDTH, :], preferred_element_type=F32)

    @pl.when(i == last)
    def _():
        hre_out[0] = hre[...]
        him_out[0] = him[...]
        pool_out[0] = zbuf[(tc + POOL_HIST - POOL_BUF) * bsz:(tc + POOL_HIST) * bsz, :]

    @pl.when(i != last)
    def _():
        zbuf[0:hist_rows, :] = zbuf[tc * bsz:(tc + POOL_HIST) * bsz, :]


def _mixer(x, h0re, h0im, hist, lw, l, *, bsz, tc, past, nb, nchunks, in_off, st_off=0, tail=None):
    rows = bsz * tc
    assert rows == TILE
    ntail = 0 if tail is None else tail.shape[0] // TILE
    assert ntail == 0 or nb == 1
    steps = nchunks + ntail
    chunk = lambda i: jnp.minimum(i, nchunks - 1)
    perb = lambda shape: pl.BlockSpec(shape, lambda b, i: (b,) + (0,) * (len(shape) - 1))
    state = lambda shape: pl.BlockSpec(shape, lambda b, i: (st_off + b,) + (0,) * (len(shape) - 1))
    if x.ndim == 2:
        x_spec = pl.BlockSpec((rows, D_MODEL), lambda b, i: (in_off + b * nchunks + chunk(i), 0))
    else:
        x_spec = pl.BlockSpec((bsz, tc, D_MODEL), lambda b, i: (b, chunk(i), 0))
    in_specs = [
        x_spec,
        state((1, bsz, S5_LANES)), state((1, bsz, S5_LANES)), state((1, POOL_HIST * bsz, POOL_WIDTH)),
    ] + [_layer_spec(lw[k], l) for k in MIXER_PARAMS]
    args = [x, h0re, h0im, hist] + [lw[k] for k in MIXER_PARAMS]
    assert len(args) == MIXER_INPUTS
    if ntail:
        in_specs.append(pl.BlockSpec((TILE, D_MODEL), lambda b, i: (jnp.maximum(i - nchunks, 0), 0)))
        args.append(tail)
    out_specs = [
        pl.BlockSpec((rows, D_MODEL), lambda b, i: (b * steps + i, 0)),
        perb((1, bsz, S5_LANES)), perb((1, bsz, S5_LANES)), perb((1, POOL_BUF * bsz, POOL_WIDTH)),
    ]
    out_shape = [
        jax.ShapeDtypeStruct((nb * steps * TILE, D_MODEL), F32),
        jax.ShapeDtypeStruct((nb, bsz, S5_LANES), F32),
        jax.ShapeDtypeStruct((nb, bsz, S5_LANES), F32),
        jax.ShapeDtypeStruct((nb, POOL_BUF * bsz, POOL_WIDTH), F32),
    ]
    scratch = [
        pltpu.VMEM((rows, S5_LANES), F32), pltpu.VMEM((rows, S5_LANES), F32),
        pltpu.VMEM((rows, S5_LANES), F32), pltpu.VMEM((rows, S5_LANES), F32),
        pltpu.VMEM((bsz, S5_LANES), F32), pltpu.VMEM((bsz, S5_LANES), F32),
        pltpu.VMEM(((tc + POOL_HIST) * bsz, POOL_WIDTH), F32),
    ]
    return pl.pallas_call(
        functools.partial(_mixer_kernel, bsz=bsz, tc=tc, past=past, nchunks=nchunks, has_tail=ntail > 0),
        grid=(nb, steps), in_specs=in_specs, out_specs=out_specs, out_shape=out_shape,
        scratch_shapes=scratch, name="mixer",
        compiler_params=pltpu.CompilerParams(dimension_semantics=("arbitrary", "arbitrary"),
                                             vmem_limit_bytes=VMEM_LIMIT),
    )(*args)


def _gating(logits):
    r = logits.shape[0]
    lane = lax.broadcasted_iota(jnp.int32, (r, LANES), 1)
    lanef = lane.astype(F32)
    neg = -jnp.inf
    is_group = jnp.logical_and(lane >= N_EXPERTS, lane < N_EXPERTS + N_EXPERT_GROUPS)
    lg = jnp.where(is_group, logits, neg)
    gmax = jnp.max(lg, axis=-1, keepdims=True)
    g_w = 1.0 / jnp.sum(jnp.exp(lg - gmax), axis=-1, keepdims=True)
    g_idx = jnp.min(jnp.where(lg == gmax, lanef, float(LANES)), axis=-1, keepdims=True) - float(N_EXPERTS)
    in_group = (lane // EXPERTS_PER_GROUP).astype(F32) == g_idx
    el = jnp.where(jnp.logical_and(in_group, lane < N_EXPERTS), logits, neg)
    m1 = jnp.max(el, axis=-1, keepdims=True)
    i1 = jnp.min(jnp.where(el == m1, lanef, float(LANES)), axis=-1, keepdims=True)
    el2 = jnp.where(lanef == i1, neg, el)
    m2 = jnp.max(el2, axis=-1, keepdims=True)
    i2 = jnp.min(jnp.where(el2 == m2, lanef, float(LANES)), axis=-1, keepdims=True)
    e2 = jnp.exp(m2 - m1)
    den = 1.0 + e2
    gate1 = g_w / den
    gate2 = g_w * e2 / den
    comb = jnp.where(lanef == i1, gate1, 0.0) + jnp.where(lanef == i2, gate2, 0.0)
    return comb, g_idx


def _route_sort_kernel(x1_ref, gffn_ref, wrh_ref, wrl_ref, br_ref, sorted_ref, dest_ref, gid_ref):
    for k in range(ROUTE_TILES):
        _route_sort_tile(k, x1_ref, gffn_ref, wrh_ref, wrl_ref, br_ref, sorted_ref, dest_ref, gid_ref)


def _route_sort_tile(k, x1_ref, gffn_ref, wrh_ref, wrl_ref, br_ref, sorted_ref, dest_ref, gid_ref):
    x = x1_ref[k * TILE:(k + 1) * TILE, :]
    h2 = _rms(x, gffn_ref[...])
    hi = h2.astype(BF16)
    lo = (h2 - hi.astype(F32)).astype(BF16)
    logits = (jnp.dot(hi, wrh_ref[...], preferred_element_type=F32)
              + jnp.dot(hi, wrl_ref[...], preferred_element_type=F32)
              + jnp.dot(lo, wrh_ref[...], preferred_element_type=F32)) + br_ref[...]
    comb, g_idx = _gating(logits)

    lanef = lax.broadcasted_iota(jnp.int32, (TILE, LANES), 1).astype(F32)
    onehot = jnp.where(lanef == g_idx, 1.0, 0.0)
    ri = lax.broadcasted_iota(jnp.int32, (TILE, TILE), 0)
    ci = lax.broadcasted_iota(jnp.int32, (TILE, TILE), 1)
    earlier = jnp.where(ci < ri, 1.0, 0.0).astype(BF16)
    rank = jnp.dot(earlier, onehot.astype(BF16), preferred_element_type=F32)
    counts = jnp.sum(onehot, axis=0, keepdims=True)
    padded = jnp.floor((counts + float(GRAN - 1)) * (1.0 / GRAN)) * float(GRAN)
    e0 = padded[:, 0:1]
    e1 = e0 + padded[:, 1:2]
    e2 = e1 + padded[:, 2:3]
    e3 = e2 + padded[:, 3:4]
    lane1 = lax.broadcasted_iota(jnp.int32, (1, LANES), 1)
    base = (jnp.where(lane1 == 1, e0, 0.0) + jnp.where(lane1 == 2, e1, 0.0)
            + jnp.where(lane1 == 3, e2, 0.0))
    dest = jnp.sum(onehot * (base + rank), axis=-1, keepdims=True)
    dest_ref[k * TILE:(k + 1) * TILE, :] = dest

    dest_row = jnp.sum(jnp.where(ri == ci, dest, 0.0), axis=0, keepdims=True)
    slot = lax.broadcasted_iota(jnp.int32, (SORTED_ROWS, TILE), 0).astype(F32)
    perm = jnp.where(slot == dest_row, 1.0, 0.0).astype(BF16)
    c_hi = comb.astype(BF16)
    c_lo = (comb - c_hi.astype(F32)).astype(BF16)
    payload = jnp.concatenate([hi, c_hi, c_lo], axis=-1)
    sorted_ref[k * SORTED_ROWS:(k + 1) * SORTED_ROWS, :] = jnp.dot(
        perm, payload, preferred_element_type=F32).astype(BF16)

    q = lane1.astype(F32) * float(GRAN)
    gid = (jnp.where(q >= e0, 1, 0) + jnp.where(q >= e1, 1, 0)
           + jnp.where(q >= e2, 1, 0) + jnp.where(q >= e3, 1, 0))
    gid_ref[k] = gid.astype(jnp.int32)


def _route_sort(x1, lw, l):
    ntiles = x1.shape[0] // TILE
    params = [lw[k] for k in ('norm_ffn', 'wr_hi', 'wr_lo', 'b_r')]
    return pl.pallas_call(
        _route_sort_kernel, grid=(ntiles // ROUTE_TILES,),
        in_specs=[pl.BlockSpec((ROUTE_TILES * TILE, D_MODEL), lambda i: (i, 0))]
                 + [_layer_spec(p, l) for p in params],
        out_specs=[pl.BlockSpec((ROUTE_TILES * SORTED_ROWS, PAYLOAD), lambda i: (i, 0)),
                   pl.BlockSpec((ROUTE_TILES * TILE, 1), lambda i: (i, 0)),
                   pl.BlockSpec((ROUTE_TILES, 1, LANES), lambda i: (i, 0, 0))],
        out_shape=[jax.ShapeDtypeStruct((ntiles * SORTED_ROWS, PAYLOAD), BF16),
                   jax.ShapeDtypeStruct((x1.shape[0], 1), F32),
                   jax.ShapeDtypeStruct((ntiles, 1, LANES), jnp.int32)],
        name="moe_route_sort",
        compiler_params=pltpu.CompilerParams(dimension_semantics=("arbitrary",),
                                             vmem_limit_bytes=VMEM_LIMIT),
    )(x1, *params)


def _experts_kernel(src_ref, wgid_ref, wvalid_ref, *refs):
    gran_refs = refs[:ITEM_GRANS]
    w1f_ref, w3f_ref, w2f_ref, out_ref, w1_ref, w3_ref, w2_ref = refs[ITEM_GRANS:]
    j = pl.program_id(0)

    @pl.when(jnp.logical_or(j == 0, wgid_ref[j] != wgid_ref[jnp.maximum(j - 1, 0)]))
    def _():
        w1_ref[...] = w1f_ref[...].astype(BF16)
        w3_ref[...] = w3f_ref[...].astype(BF16)
        w2_ref[...] = w2f_ref[...].astype(BF16)

    @pl.when(wvalid_ref[j] > 0)
    def _():
        full = jnp.concatenate([g[...] for g in gran_refs], axis=0)
        h = full[:, :D_MODEL]
        comb = (full[:, D_MODEL:D_MODEL + LANES].astype(F32)
                + full[:, D_MODEL + LANES:].astype(F32))
        first = wgid_ref[j] * EXPERTS_PER_GROUP
        lane = lax.broadcasted_iota(jnp.int32, (ITEM_ROWS, LANES), 1)
        hids = []
        for e in range(EXPERTS_PER_GROUP):
            a = jnp.dot(h, w1_ref[e], preferred_element_type=F32)
            b = jnp.dot(h, w3_ref[e], preferred_element_type=F32)
            gate = jnp.sum(jnp.where(lane == first + e, comb, 0.0), axis=-1, keepdims=True)
            hids.append(((a * jax.nn.sigmoid(a)) * b * gate).astype(BF16))
        hid = jnp.concatenate(hids, axis=-1)
        w2g = w2_ref[...].reshape(EXPERTS_PER_GROUP * D_EXPERT, D_MODEL)
        out_ref[...] = jnp.dot(hid, w2g, preferred_element_type=F32).astype(BF16)

    @pl.when(wvalid_ref[j] == 0)
    def _():
        out_ref[...] = jnp.zeros(out_ref.shape, out_ref.dtype)


def _experts(sorted_tok, src, wgid, wvalid, lw, l):
    nitems = wgid.shape[0]
    gran_spec = lambda s: pl.BlockSpec((GRAN, PAYLOAD), lambda j, src, wg, wv: (src[j * ITEM_GRANS + s], 0))
    wspec = lambda shape: pl.BlockSpec(shape, lambda j, src, wg, wv: (l * N_EXPERT_GROUPS + wg[j], 0, 0))
    grid_spec = pltpu.PrefetchScalarGridSpec(
        num_scalar_prefetch=3, grid=(nitems,),
        in_specs=[gran_spec(s) for s in range(ITEM_GRANS)] + [
            wspec((EXPERTS_PER_GROUP, D_MODEL, D_EXPERT)), wspec((EXPERTS_PER_GROUP, D_MODEL, D_EXPERT)),
            wspec((EXPERTS_PER_GROUP, D_EXPERT, D_MODEL))],
        out_specs=pl.BlockSpec((ITEM_ROWS, D_MODEL), lambda j, src, wg, wv: (j, 0)),
        scratch_shapes=[pltpu.VMEM((EXPERTS_PER_GROUP, D_MODEL, D_EXPERT), BF16),
                        pltpu.VMEM((EXPERTS_PER_GROUP, D_MODEL, D_EXPERT), BF16),
                        pltpu.VMEM((EXPERTS_PER_GROUP, D_EXPERT, D_MODEL), BF16)])
    return pl.pallas_call(
        _experts_kernel, grid_spec=grid_spec,
        out_shape=jax.ShapeDtypeStruct((nitems * ITEM_ROWS, D_MODEL), BF16),
        name="moe_experts",
        compiler_params=pltpu.CompilerParams(dimension_semantics=("arbitrary",),
                                             vmem_limit_bytes=VMEM_LIMIT),
    )(src, wgid, wvalid, *([sorted_tok] * ITEM_GRANS), lw['w1'], lw['w3'], lw['w2'])


def _unsort_kernel(pos_ref, x1_ref, dest_ref, *refs, final):
    gran_refs = refs[:TILE_GRANS]
    gfin_ref, out_ref = refs[TILE_GRANS:]
    s = jnp.concatenate([g[...] for g in gran_refs], axis=0)
    slot = lax.broadcasted_iota(jnp.int32, (TILE, SORTED_ROWS), 1).astype(F32)
    perm_t = jnp.where(slot == dest_ref[...], 1.0, 0.0).astype(BF16)
    y = x1_ref[...] + jnp.dot(perm_t, s, preferred_element_type=F32)
    if not final:
        out_ref[...] = y
    else:
        y = _rms(y, gfin_ref[...])
        bsz, tc, _ = out_ref.shape
        for t in range(tc):
            out_ref[:, t, :] = y[t * bsz:(t + 1) * bsz, :]


def _unsort(x1, dest, expert_out, pos, norm_final, *, tile_off=0, ntiles=None, batch_major=None):
    final = batch_major is not None
    if final:
        batch, t_len, bsz, tc = batch_major
        nchunks = t_len // tc
        out_spec = pl.BlockSpec((bsz, tc, D_MODEL), lambda k, pos: (k // nchunks, k % nchunks, 0))
        out_shape = jax.ShapeDtypeStruct((batch, t_len, D_MODEL), F32)
    else:
        ntiles = x1.shape[0] // TILE
        out_spec = pl.BlockSpec((TILE, D_MODEL), lambda k, pos: (k, 0))
        out_shape = jax.ShapeDtypeStruct(x1.shape, F32)
    gran_spec = lambda q: pl.BlockSpec((GRAN, D_MODEL),
                                       lambda k, pos: (pos[(k + tile_off) * TILE_GRANS + q], 0))
    grid_spec = pltpu.PrefetchScalarGridSpec(
        num_scalar_prefetch=1, grid=(ntiles,),
        in_specs=[pl.BlockSpec((TILE, D_MODEL), lambda k, pos: (k + tile_off, 0)),
                  pl.BlockSpec((TILE, 1), lambda k, pos: (k + tile_off, 0))]
                 + [gran_spec(q) for q in range(TILE_GRANS)]
                 + [pl.BlockSpec((1, D_MODEL), lambda k, pos: (0, 0))],
        out_specs=out_spec)
    return pl.pallas_call(
        functools.partial(_unsort_kernel, final=final), grid_spec=grid_spec,
        out_shape=out_shape, name="moe_unsort",
        compiler_params=pltpu.CompilerParams(dimension_semantics=("arbitrary",),
                                             vmem_limit_bytes=VMEM_LIMIT),
    )(pos, x1, dest, *([expert_out] * TILE_GRANS), norm_final)


def _dispatch_tables(gid):
    ntiles = gid.shape[0]
    ngran = ntiles * TILE_GRANS
    nitems = -(-ngran // ITEM_GRANS) + N_EXPERT_GROUPS
    g = gid[:, 0, :TILE_GRANS].reshape(ngran)
    onehot = (g[:, None] == jnp.arange(N_EXPERT_GROUPS, dtype=jnp.int32)[None, :]).astype(jnp.int32)
    valid = g < N_EXPERT_GROUPS
    rank = jnp.cumsum(onehot, axis=0) - onehot
    cnt = jnp.sum(onehot, axis=0)
    pcnt = (cnt + (ITEM_GRANS - 1)) // ITEM_GRANS * ITEM_GRANS
    pend = jnp.cumsum(pcnt)
    pstart = pend - pcnt
    pos = jnp.sum(onehot * (pstart[None, :] + rank), axis=1)
    slots = jnp.arange(nitems * ITEM_GRANS, dtype=jnp.int32)
    hit = jnp.logical_and(pos[None, :] == slots[:, None], valid[None, :])
    src = jnp.sum(jnp.where(hit, jnp.arange(ngran, dtype=jnp.int32)[None, :], 0), axis=1)
    item_start = jnp.arange(nitems, dtype=jnp.int32) * ITEM_GRANS
    last_group = jnp.max(jnp.where(pcnt > 0, jnp.arange(N_EXPERT_GROUPS, dtype=jnp.int32), 0))
    wgid = jnp.minimum(jnp.sum((item_start[:, None] >= pend[None, :]).astype(jnp.int32), axis=1), last_group)
    wvalid = (item_start < pend[-1]).astype(jnp.int32)
    return src.astype(jnp.int32), wgid.astype(jnp.int32), wvalid, jnp.where(valid, pos, 0).astype(jnp.int32)


def _moe(x1, lw, l, norm_final, final_layouts=None):
    sorted_tok, dest, gid = _route_sort(x1, lw, l)
    src, wgid, wvalid, pos = _dispatch_tables(gid)
    expert_out = _experts(sorted_tok, src, wgid, wvalid, lw, l)
    if final_layouts is None:
        return _unsort(x1, dest, expert_out, pos, norm_final)
    return [_unsort(x1, dest, expert_out, pos, norm_final, tile_off=off, ntiles=n, batch_major=bm)
            for off, n, bm in final_layouts]


def _router_weights(w_grp, b_grp, w_er, b_er):
    pad = LANES - N_EXPERTS - N_EXPERT_GROUPS
    w = jnp.pad(jnp.concatenate([w_er, w_grp], axis=-1), ((0, 0), (0, 0), (0, pad)))
    b = jnp.pad(jnp.concatenate([b_er, b_grp], axis=-1), ((0, 0), (0, pad)))
    hi = w.astype(BF16)
    lo = (w - hi.astype(F32)).astype(BF16)
    return hi, lo, b


def _to_time_major(x, nb):
    bt, t, w = x.shape
    return jnp.transpose(x.reshape(nb, bt // nb, t, w), (0, 2, 1, 3)).reshape(nb, t * (bt // nb), w)


def _from_time_major(x, t):
    nb, rows, w = x.shape
    bsz = rows // t
    return jnp.transpose(x.reshape(nb, t, bsz, w), (0, 2, 1, 3)).reshape(nb * bsz, t, w)


def kernel(x_prompt, x_sample, state_ssm_re, state_ssm_im, state_pool, norm_mix, w_in, lam_re, lam_im, log_dt, b_re, b_im, c_re, c_im, d_skip, w_glu, b_glu, w_pool, pool_scale, w_out, norm_ffn, w_grp, b_grp, w_erouter, b_erouter, w1, w3, w2, norm_final):
    depth = w_in.shape[0]
    pb, pt, _ = x_prompt.shape
    sb, st, _ = x_sample.shape
    p_tc = TILE // pb
    p_tiles = pt // p_tc
    s_bsz = TILE // st
    s_nb = sb // s_bsz
    p_rows = pb * pt

    are, aim, bbr, bbi = _discretise(lam_re, lam_im, log_dt, b_re, b_im)
    wx = jnp.concatenate([_blockdiag(bbr, True), _blockdiag(bbi, True)], axis=-1).astype(BF16)
    cm = jnp.concatenate([_blockdiag(c_re, False), _blockdiag(-c_im, False)], axis=-2).astype(BF16)

    wr_hi, wr_lo, b_r = _router_weights(w_grp, b_grp, w_erouter, b_erouter)
    stack_experts = lambda w: w.reshape((depth * N_EXPERTS,) + w.shape[2:])
    vec = lambda a: a[:, None, :]
    lw = dict(norm_mix=vec(norm_mix), w_in=w_in.astype(BF16), wx=wx, are=are, aim=aim, cm=cm,
              d_skip=vec(d_skip), w_glu=w_glu.astype(BF16), b_glu=vec(b_glu), w_pool=w_pool.astype(BF16),
              pool_scale=vec(pool_scale), w_out=w_out.astype(BF16), norm_ffn=vec(norm_ffn),
              wr_hi=wr_hi, wr_lo=wr_lo, b_r=vec(b_r),
              w1=stack_experts(w1), w3=stack_experts(w3), w2=stack_experts(w2))
    nfin = norm_final[None]

    p_zero_h = jnp.zeros((1, pb, S5_LANES), F32)
    p_zero_hist = jnp.zeros((1, POOL_HIST * pb, POOL_WIDTH), F32)
    final_layouts = [(0, p_tiles, (pb, pt, pb, p_tc)), (p_tiles, s_nb, (sb, st, s_bsz, st))]

    s_h0re = state_ssm_re.reshape(depth * s_nb, s_bsz, S5_LANES)
    s_h0im = state_ssm_im.reshape(depth * s_nb, s_bsz, S5_LANES)
    s_hist = jnp.pad(_to_time_major(state_pool.reshape(depth * sb, POOL_BUF, POOL_WIDTH), depth * s_nb),
                     ((0, 0), ((POOL_HIST - POOL_BUF) * s_bsz, 0), (0, 0)))

    outs = {k: [] for k in ('p_re', 'p_im', 'p_pool', 's_re', 's_im', 's_pool')}
    x = None
    for l in range(depth):
        xs1, hre, him, pool = _mixer(x_sample if l == 0 else x, s_h0re, s_h0im, s_hist, lw, l, bsz=s_bsz,
                                     tc=st, past=POOL_BUF, nb=s_nb, nchunks=1, in_off=p_tiles,
                                     st_off=l * s_nb)
        outs['s_re'].append(hre)
        outs['s_im'].append(him)
        outs['s_pool'].append(pool)
        x1, hre, him, pool = _mixer(x_prompt if l == 0 else x, p_zero_h, p_zero_h, p_zero_hist, lw, l,
                                    bsz=pb, tc=p_tc, past=0, nb=1, nchunks=p_tiles, in_off=0, tail=xs1)
        outs['p_re'].append(hre)
        outs['p_im'].append(him)
        outs['p_pool'].append(pool)
        x = _moe(x1, lw, l, nfin, final_layouts if l == depth - 1 else None)

    def states(k, batch):
        return jnp.stack(outs[k]).reshape(depth, batch, S5_GROUPS, S5_STATE)

    def pools(k, batch):
        p = jnp.stack(outs[k])
        p = _from_time_major(p.reshape((-1,) + p.shape[2:]), POOL_BUF)
        return p.reshape(depth, batch, POOL_BUF, POOL_WIDTH)

    y_prompt, y_sample = x
    return (y_prompt, y_sample, states('p_re', pb), states('p_im', pb), pools('p_pool', pb),
            states('s_re', sb), states('s_im', sb), pools('s_pool', sb))
```

```python
import functools

import jax
import jax.numpy as jnp
from jax import lax
from jax.experimental import pallas as pl
from jax.experimental.pallas import tpu as pltpu

D_MODEL = 1024
S5_WIDTH = 512
S5_GROUP_CH = 16
S5_GROUPS = 32
S5_STATE = 64
S5_LANES = S5_GROUPS * S5_STATE
POOL_WIDTH = 512
POOL_WINDOWS = (2, 4, 8, 16)
POOL_GROUP_CH = 128
POOL_BUF = 15
POOL_HIST = 16
N_EXPERTS = 16
EXPERTS_PER_GROUP = 4
N_EXPERT_GROUPS = 4
D_EXPERT = 256
EPS = 1e-6

SUBLANES = 8
LANES = 128
S5_KBLOCK = 128
S5_NBLOCK = S5_KBLOCK // S5_GROUP_CH * S5_STATE
SCAN_LANES = 512
VMEM_LIMIT = 56 * 1024 * 1024

TILE = 512
GRAN = 32
TILE_GRANS = TILE // GRAN + N_EXPERT_GROUPS
SORTED_ROWS = TILE_GRANS * GRAN
ITEM_GRANS = 16
ITEM_ROWS = ITEM_GRANS * GRAN
ROUTE_TILES = 2
PAYLOAD = D_MODEL + 2 * LANES

F32 = jnp.float32
BF16 = jnp.bfloat16

MIXER_PARAMS = ('norm_mix', 'w_in', 'wx', 'are', 'aim', 'cm', 'd_skip', 'w_glu', 'b_glu', 'w_pool',
                'pool_scale', 'w_out')


def _layer_spec(arr, l):
    return pl.BlockSpec((None,) + arr.shape[1:], lambda *_: (l,) + (0,) * (arr.ndim - 1))


def _rms(x, g):
    return x * lax.rsqrt(jnp.mean(x * x, axis=-1, keepdims=True) + EPS) * g


def _gelu_tanh(x):
    return 0.5 * x * (1.0 + jnp.tanh(0.7978845608028654 * (x + 0.044715 * (x * x * x))))


def _disc_kernel(lr_ref, li_ref, ldt_ref, br_ref, bi_ref, are_ref, aim_ref, bbr_ref, bbi_ref):
    lr = lr_ref[...]
    li = li_ref[...]
    dt = jnp.exp(ldt_ref[...])
    mag = jnp.exp(lr * dt)
    ab_re = mag * jnp.cos(li * dt)
    ab_im = mag * jnp.sin(li * dt)
    den = lr * lr + li * li
    nr = ab_re - 1.0
    coef_re = (nr * lr + ab_im * li) / den
    coef_im = (ab_im * lr - nr * li) / den
    br = br_ref[...]
    bi = bi_ref[...]
    are_ref[...] = ab_re
    aim_ref[...] = ab_im
    bbr_ref[...] = coef_re * br - coef_im * bi
    bbi_ref[...] = coef_re * bi + coef_im * br


def _discretise(lam_re, lam_im, log_dt, b_re, b_im):
    depth = lam_re.shape[0]
    rows = depth * S5_GROUPS * S5_GROUP_CH
    rep = lambda a: jnp.repeat(a.reshape(depth * S5_GROUPS, -1), S5_GROUP_CH, axis=0)
    tr = lambda b: jnp.transpose(b, (0, 1, 3, 2)).reshape(rows, S5_STATE)
    out = jax.ShapeDtypeStruct((rows, S5_STATE), F32)
    are, aim, bbr, bbi = pl.pallas_call(
        _disc_kernel, out_shape=(out, out, out, out), name="s5_discretise",
    )(rep(lam_re), rep(lam_im), rep(log_dt[..., None]), tr(b_re), tr(b_im))
    shp = (depth, S5_GROUPS, S5_GROUP_CH, S5_STATE)
    are = are.reshape(shp)[:, :, 0, :].reshape(depth, 1, S5_LANES)
    aim = aim.reshape(shp)[:, :, 0, :].reshape(depth, 1, S5_LANES)
    return are, aim, bbr.reshape(shp), bbi.reshape(shp)


def _blockdiag(w, rows_first):
    depth = w.shape[0]
    gpb = S5_KBLOCK // S5_GROUP_CH
    w5 = w.reshape(depth, S5_GROUPS // gpb, gpb, S5_GROUP_CH, S5_STATE)
    eye = jnp.eye(gpb, dtype=w.dtype)
    if rows_first:
        return jnp.einsum('lkgcn,gh->lkgchn', w5, eye).reshape(depth, -1, S5_KBLOCK, S5_NBLOCK)
    return jnp.einsum('lkgcn,gh->lkgnhc', w5, eye).reshape(depth, -1, S5_NBLOCK, S5_KBLOCK)


MIXER_INPUTS = 16


def _mixer_kernel(*refs, bsz, tc, past, nchunks, has_tail):
    if not has_tail:
        _mixer_body(*refs, bsz=bsz, tc=tc, past=past, last=nchunks - 1)
        return
    tail_ref = refs[MIXER_INPUTS]
    body_refs = refs[:MIXER_INPUTS] + refs[MIXER_INPUTS + 1:]
    x1_ref = body_refs[MIXER_INPUTS]
    i = pl.program_id(1)

    @pl.when(i < nchunks)
    def _():
        _mixer_body(*body_refs, bsz=bsz, tc=tc, past=past, last=nchunks - 1)

    @pl.when(i >= nchunks)
    def _():
        x1_ref[...] = tail_ref[...]


def _mixer_body(x_ref, h0re_ref, h0im_ref, hist_ref, gmix_ref, win_ref, wx_ref, are_ref, aim_ref,
                cm_ref, dskip_ref, wglu_ref, bglu_ref, wpool_ref, pscale_ref, wout_ref,
                x1_ref, hre_out, him_out, pool_out, xre, xim, sre, sim, hre, him, zbuf, *, bsz, tc, past, last):
    i = pl.program_id(1)
    rows = bsz * tc
    hist_rows = POOL_HIST * bsz

    @pl.when(i == 0)
    def _():
        hre[...] = h0re_ref[0]
        him[...] = h0im_ref[0]
        zbuf[0:hist_rows, :] = hist_ref[0]

    if len(x_ref.shape) == 3:
        x = jnp.concatenate([x_ref[:, t, :] for t in range(tc)], axis=0)
    else:
        x = x_ref[...]
    h = _rms(x, gmix_ref[...])
    proj = jnp.dot(h.astype(BF16), win_ref[...], preferred_element_type=F32)
    u = proj[:, :S5_WIDTH]
    z = proj[:, S5_WIDTH:]

    zbuf[hist_rows:hist_rows + rows, :] = z
    step_idx = i * tc + lax.broadcasted_iota(jnp.int32, (rows, 1), 0) // bsz
    pos = (step_idx + (past + 1)).astype(F32)
    pouts = []
    for gi, w in enumerate(POOL_WINDOWS):
        gs = slice(gi * POOL_GROUP_CH, (gi + 1) * POOL_GROUP_CH)
        s = zbuf[:, gs]
        k = 1
        while k < w:
            s = s[k * bsz:] + s[:-k * bsz]
            k *= 2
        inv_cnt = 1.0 / jnp.minimum(pos, float(w))
        pooled = s[-rows:] * inv_cnt - z[:, gs]
        pouts.append(jnp.dot(pooled.astype(BF16), wpool_ref[gi], preferred_element_type=F32))
    pool_mixed = jnp.concatenate(pouts, axis=-1) * pscale_ref[...]

    res = x + jnp.dot(pool_mixed.astype(BF16), wout_ref[S5_WIDTH:, :], preferred_element_type=F32)

    assert SCAN_LANES == S5_NBLOCK
    ub = u.astype(BF16)

    def scan_rows(r0, ls, static):
        ar = jnp.broadcast_to(are_ref[:, ls], (SUBLANES, SCAN_LANES))
        ai = jnp.broadcast_to(aim_ref[:, ls], (SUBLANES, SCAN_LANES))

        def step(t, carry):
            hr, hi = carry
            row = t * bsz + r0
            if not static:
                row = pl.multiple_of(row, SUBLANES)
            nr = ar * hr + (xre[pl.ds(row, SUBLANES), ls] - ai * hi)
            ni = ar * hi + (xim[pl.ds(row, SUBLANES), ls] + ai * hr)
            sre[pl.ds(row, SUBLANES), ls] = nr
            sim[pl.ds(row, SUBLANES), ls] = ni
            return nr, ni

        carry = (hre[pl.ds(r0, SUBLANES), ls], him[pl.ds(r0, SUBLANES), ls])
        if static:
            for t in range(tc):
                carry = step(t, carry)
        else:
            carry = lax.fori_loop(0, tc, step, carry, unroll=8)
        hre[pl.ds(r0, SUBLANES), ls] = carry[0]
        him[pl.ds(r0, SUBLANES), ls] = carry[1]

    ys = []
    for kb in range(S5_WIDTH // S5_KBLOCK):
        ns = slice(kb * S5_NBLOCK, (kb + 1) * S5_NBLOCK)
        xx = jnp.dot(ub[:, kb * S5_KBLOCK:(kb + 1) * S5_KBLOCK], wx_ref[kb],
                     preferred_element_type=F32)
        xre[:, ns] = xx[:, :S5_NBLOCK]
        xim[:, ns] = xx[:, S5_NBLOCK:]
        if bsz == SUBLANES:
            scan_rows(0, ns, True)
        else:
            def body(rb, c, ns=ns):
                scan_rows(pl.multiple_of(rb * SUBLANES, SUBLANES), ns, False)
                return c
            lax.fori_loop(0, bsz // SUBLANES, body, 0)
        ys.append(jnp.dot(sre[:, ns].astype(BF16), cm_ref[kb, :S5_NBLOCK, :], preferred_element_type=F32)
                  + jnp.dot(sim[:, ns].astype(BF16), cm_ref[kb, S5_NBLOCK:, :], preferred_element_type=F32))
    y = jnp.concatenate(ys, axis=-1) + dskip_ref[...] * u
    g = _gelu_tanh(y)
    s5_out = g * jax.nn.sigmoid(jnp.dot(g.astype(BF16), wglu_ref[...], preferred_element_type=F32)
                                + bglu_ref[...])

    x1_ref[...] = res + jnp.dot(s5_out.astype(BF16), wout_ref[:S5_WIDTH, :], preferred_element_type=F32)

    @pl.when(i == last)
    def _():
        hre_out[0] = hre[...]
        him_out[0] = him[...]
        pool_out[0] = zbuf[(tc + POOL_HIST - POOL_BUF) * bsz:(tc + POOL_HIST) * bsz, :]

    @pl.when(i != last)
    def _():
        zbuf[0:hist_rows, :] = zbuf[tc * bsz:(tc + POOL_HIST) * bsz, :]


def _mixer(x, h0re, h0im, hist, lw, l, *, bsz, tc, past, nb, nchunks, in_off, tail=None):
    rows = bsz * tc
    assert rows == TILE
    ntail = 0 if tail is None else tail.shape[0] // TILE
    assert ntail == 0 or nb == 1
    steps = nchunks + ntail
    chunk = lambda i: jnp.minimum(i, nchunks - 1)
    perb = lambda shape: pl.BlockSpec(shape, lambda b, i: (b,) + (0,) * (len(shape) - 1))
    if x.ndim == 2:
        x_spec = pl.BlockSpec((rows, D_MODEL), lambda b, i: (in_off + b * nchunks + chunk(i), 0))
    else:
        x_spec = pl.BlockSpec((bsz, tc, D_MODEL), lambda b, i: (b, chunk(i), 0))
    in_specs = [
        x_spec,
        perb((1, bsz, S5_LANES)), perb((1, bsz, S5_LANES)), perb((1, POOL_HIST * bsz, POOL_WIDTH)),
    ] + [_layer_spec(lw[k], l) for k in MIXER_PARAMS]
    args = [x, h0re, h0im, hist] + [lw[k] for k in MIXER_PARAMS]
    assert len(args) == MIXER_INPUTS
    if ntail:
        in_specs.append(pl.BlockSpec((TILE, D_MODEL), lambda b, i: (jnp.maximum(i - nchunks, 0), 0)))
        args.append(tail)
    out_specs = [
        pl.BlockSpec((rows, D_MODEL), lambda b, i: (b * steps + i, 0)),
        perb((1, bsz, S5_LANES)), perb((1, bsz, S5_LANES)), perb((1, POOL_BUF * bsz, POOL_WIDTH)),
    ]
    out_shape = [
        jax.ShapeDtypeStruct((nb * steps * TILE, D_MODEL), F32),
        jax.ShapeDtypeStruct((nb, bsz, S5_LANES), F32),
        jax.ShapeDtypeStruct((nb, bsz, S5_LANES), F32),
        jax.ShapeDtypeStruct((nb, POOL_BUF * bsz, POOL_WIDTH), F32),
    ]
    scratch = [
        pltpu.VMEM((rows, S5_LANES), F32), pltpu.VMEM((rows, S5_LANES), F32),
        pltpu.VMEM((rows, S5_LANES), F32), pltpu.VMEM((rows, S5_LANES), F32),
        pltpu.VMEM((bsz, S5_LANES), F32), pltpu.VMEM((bsz, S5_LANES), F32),
        pltpu.VMEM(((tc + POOL_HIST) * bsz, POOL_WIDTH), F32),
    ]
    return pl.pallas_call(
        functools.partial(_mixer_kernel, bsz=bsz, tc=tc, past=past, nchunks=nchunks, has_tail=ntail > 0),
        grid=(nb, steps), in_specs=in_specs, out_specs=out_specs, out_shape=out_shape,
        scratch_shapes=scratch, name="mixer",
        compiler_params=pltpu.CompilerParams(dimension_semantics=("arbitrary", "arbitrary"),
                                             vmem_limit_bytes=VMEM_LIMIT),
    )(*args)


ROUTER_ROWS = 32


def _dot_nt(a, b):
    return lax.dot_general(a, b, (((1,), (1,)), ((), ())), preferred_element_type=F32)


def _argmax_rows(rows):
    best, idx = rows[0], jnp.zeros_like(rows[0])
    for k in range(1, len(rows)):
        gt = rows[k] > best
        best = jnp.where(gt, rows[k], best)
        idx = jnp.where(gt, float(k), idx)
    return best, idx


def _gating_t(lt):
    row = lambda r: lt[r:r + 1, :]
    g_rows = [row(N_EXPERTS + g) for g in range(N_EXPERT_GROUPS)]
    gmax, gidx = _argmax_rows(g_rows)
    gsum = g_rows[0] * 0.0
    for r in g_rows:
        gsum = gsum + jnp.exp(r - gmax)
    g_w = 1.0 / gsum
    el = []
    for k in range(EXPERTS_PER_GROUP):
        v = row((N_EXPERT_GROUPS - 1) * EXPERTS_PER_GROUP + k)
        for g in range(N_EXPERT_GROUPS - 2, -1, -1):
            v = jnp.where(gidx == float(g), row(g * EXPERTS_PER_GROUP + k), v)
        el.append(v)
    m1, i1 = _argmax_rows(el)
    m2, i2 = _argmax_rows([jnp.where(i1 == float(k), -jnp.inf, el[k]) for k in range(EXPERTS_PER_GROUP)])
    e2 = jnp.exp(m2 - m1)
    den = 1.0 + e2
    first = gidx * float(EXPERTS_PER_GROUP)
    return gidx, first + i1, first + i2, g_w / den, g_w * e2 / den


def _route_sort_kernel(x1_ref, tri_ref, gffn_ref, wrh_ref, wrl_ref, br_ref, sorted_ref, dest_ref, gid_ref):
    tiles = [_route_sort_tile(k, x1_ref, tri_ref, gffn_ref, wrh_ref, wrl_ref, br_ref, sorted_ref, dest_ref,
                              gid_ref) for k in range(ROUTE_TILES)]
    while tiles:
        tiles = [t for t in tiles if next(t, None) is not None]


def _route_sort_tile(k, x1_ref, tri_ref, gffn_ref, wrh_ref, wrl_ref, br_ref, sorted_ref, dest_ref, gid_ref):
    x = x1_ref[k * TILE:(k + 1) * TILE, :]
    h2 = _rms(x, gffn_ref[...])
    hi = h2.astype(BF16)
    lo = (h2 - hi.astype(F32)).astype(BF16)
    yield 1
    lt = (_dot_nt(wrh_ref[...], hi) + _dot_nt(wrl_ref[...], hi) + _dot_nt(wrh_ref[...], lo)) + br_ref[...]
    yield 2
    gidx, ex1, ex2, gate1, gate2 = _gating_t(lt)
    yield 3

    sub8 = lax.broadcasted_iota(jnp.int32, (SUBLANES, TILE), 0).astype(F32)
    onehot = jnp.where(sub8 == gidx, 1.0, 0.0)
    rank = jnp.dot(onehot.astype(BF16), tri_ref[...], preferred_element_type=F32)
    counts = jnp.sum(onehot, axis=1, keepdims=True)
    padded = jnp.floor((counts + float(GRAN - 1)) * (1.0 / GRAN)) * float(GRAN)
    e0 = padded[0:1, :]
    e1 = e0 + padded[1:2, :]
    e2 = e1 + padded[2:3, :]
    e3 = e2 + padded[3:4, :]
    dest = jnp.where(gidx == 0.0, rank[0:1, :],
                     jnp.where(gidx == 1.0, e0 + rank[1:2, :],
                               jnp.where(gidx == 2.0, e1 + rank[2:3, :], e2 + rank[3:4, :])))
    dest_ref[k * TILE:(k + 1) * TILE, :] = jnp.broadcast_to(dest, (LANES, TILE)).T[:, 0:1]

    slot = lax.broadcasted_iota(jnp.int32, (SORTED_ROWS, TILE), 0).astype(F32)
    perm = jnp.where(slot == dest, 1.0, 0.0).astype(BF16)
    sub = lax.broadcasted_iota(jnp.int32, (N_EXPERTS, TILE), 0).astype(F32)
    comb = jnp.where(sub == ex1, gate1, 0.0) + jnp.where(sub == ex2, gate2, 0.0)
    c_hi = comb.astype(BF16)
    c_lo = (comb - c_hi.astype(F32)).astype(BF16)
    zeros = jnp.zeros((LANES - N_EXPERTS, TILE), BF16)
    gates = jnp.concatenate([c_hi, zeros, c_lo, zeros], axis=0)
    yield 4
    rows = slice(k * SORTED_ROWS, (k + 1) * SORTED_ROWS)
    sorted_ref[rows, :D_MODEL] = jnp.dot(perm, hi, preferred_element_type=F32).astype(BF16)
    sorted_ref[rows, D_MODEL:] = _dot_nt(perm, gates).astype(BF16)

    q = lax.broadcasted_iota(jnp.int32, (1, LANES), 1).astype(F32) * float(GRAN)
    gid = (jnp.where(q >= e0, 1, 0) + jnp.where(q >= e1, 1, 0)
           + jnp.where(q >= e2, 1, 0) + jnp.where(q >= e3, 1, 0))
    gid_ref[k] = gid.astype(jnp.int32)


def _route_sort(x1, lw, l):
    ntiles = x1.shape[0] // TILE
    params = [lw[k] for k in ('norm_ffn', 'wr_hi', 'wr_lo', 'b_r')]
    tri = jnp.triu(jnp.ones((TILE, TILE), BF16), k=1)
    return pl.pallas_call(
        _route_sort_kernel, grid=(ntiles // ROUTE_TILES,),
        in_specs=[pl.BlockSpec((ROUTE_TILES * TILE, D_MODEL), lambda i: (i, 0)),
                  pl.BlockSpec((TILE, TILE), lambda i: (0, 0))] + [_layer_spec(p, l) for p in params],
        out_specs=[pl.BlockSpec((ROUTE_TILES * SORTED_ROWS, PAYLOAD), lambda i: (i, 0)),
                   pl.BlockSpec((ROUTE_TILES * TILE, 1), lambda i: (i, 0)),
                   pl.BlockSpec((ROUTE_TILES, 1, LANES), lambda i: (i, 0, 0))],
        out_shape=[jax.ShapeDtypeStruct((ntiles * SORTED_ROWS, PAYLOAD), BF16),
                   jax.ShapeDtypeStruct((x1.shape[0], 1), F32),
                   jax.ShapeDtypeStruct((ntiles, 1, LANES), jnp.int32)],
        name="moe_route_sort",
        compiler_params=pltpu.CompilerParams(dimension_semantics=("arbitrary",),
                                             vmem_limit_bytes=VMEM_LIMIT),
    )(x1, tri, *params)


def _experts_kernel(src_ref, wgid_ref, wvalid_ref, *refs):
    gran_refs = refs[:ITEM_GRANS]
    w1f_ref, w3f_ref, w2f_ref, out_ref, w1_ref, w3_ref, w2_ref = refs[ITEM_GRANS:]
    j = pl.program_id(0)

    @pl.when(jnp.logical_or(j == 0, wgid_ref[j] != wgid_ref[jnp.maximum(j - 1, 0)]))
    def _():
        w1_ref[...] = w1f_ref[...].astype(BF16)
        w3_ref[...] = w3f_ref[...].astype(BF16)
        w2_ref[...] = w2f_ref[...].astype(BF16)

    @pl.when(wvalid_ref[j] > 0)
    def _():
        full = jnp.concatenate([g[...] for g in gran_refs], axis=0)
        h = full[:, :D_MODEL]
        comb = (full[:, D_MODEL:D_MODEL + LANES].astype(F32)
                + full[:, D_MODEL + LANES:].astype(F32))
        first = wgid_ref[j] * EXPERTS_PER_GROUP
        lane = lax.broadcasted_iota(jnp.int32, (ITEM_ROWS, LANES), 1)
        hids = []
        for e in range(EXPERTS_PER_GROUP):
            a = jnp.dot(h, w1_ref[e], preferred_element_type=F32)
            b = jnp.dot(h, w3_ref[e], preferred_element_type=F32)
            gate = jnp.sum(jnp.where(lane == first + e, comb, 0.0), axis=-1, keepdims=True)
            hids.append(((a * jax.nn.sigmoid(a)) * b * gate).astype(BF16))
        hid = jnp.concatenate(hids, axis=-1)
        w2g = w2_ref[...].reshape(EXPERTS_PER_GROUP * D_EXPERT, D_MODEL)
        out_ref[...] = jnp.dot(hid, w2g, preferred_element_type=F32).astype(BF16)

    @pl.when(wvalid_ref[j] == 0)
    def _():
        out_ref[...] = jnp.zeros(out_ref.shape, out_ref.dtype)


def _experts(sorted_tok, src, wgid, wvalid, lw, l):
    nitems = wgid.shape[0]
    gran_spec = lambda s: pl.BlockSpec((GRAN, PAYLOAD), lambda j, src, wg, wv: (src[j * ITEM_GRANS + s], 0))
    wspec = lambda shape: pl.BlockSpec(shape, lambda j, src, wg, wv: (l * N_EXPERT_GROUPS + wg[j], 0, 0))
    grid_spec = pltpu.PrefetchScalarGridSpec(
        num_scalar_prefetch=3, grid=(nitems,),
        in_specs=[gran_spec(s) for s in range(ITEM_GRANS)] + [
            wspec((EXPERTS_PER_GROUP, D_MODEL, D_EXPERT)), wspec((EXPERTS_PER_GROUP, D_MODEL, D_EXPERT)),
            wspec((EXPERTS_PER_GROUP, D_EXPERT, D_MODEL))],
        out_specs=pl.BlockSpec((ITEM_ROWS, D_MODEL), lambda j, src, wg, wv: (j, 0)),
        scratch_shapes=[pltpu.VMEM((EXPERTS_PER_GROUP, D_MODEL, D_EXPERT), BF16),
                        pltpu.VMEM((EXPERTS_PER_GROUP, D_MODEL, D_EXPERT), BF16),
                        pltpu.VMEM((EXPERTS_PER_GROUP, D_EXPERT, D_MODEL), BF16)])
    return pl.pallas_call(
        _experts_kernel, grid_spec=grid_spec,
        out_shape=jax.ShapeDtypeStruct((nitems * ITEM_ROWS, D_MODEL), BF16),
        name="moe_experts",
        compiler_params=pltpu.CompilerParams(dimension_semantics=("arbitrary",),
                                             vmem_limit_bytes=VMEM_LIMIT),
    )(src, wgid, wvalid, *([sorted_tok] * ITEM_GRANS), lw['w1'], lw['w3'], lw['w2'])


def _unsort_kernel(pos_ref, x1_ref, dest_ref, *refs, final):
    gran_refs = refs[:TILE_GRANS]
    gfin_ref, out_ref = refs[TILE_GRANS:]
    s = jnp.concatenate([g[...] for g in gran_refs], axis=0)
    slot = lax.broadcasted_iota(jnp.int32, (TILE, SORTED_ROWS), 1).astype(F32)
    perm_t = jnp.where(slot == dest_ref[...], 1.0, 0.0).astype(BF16)
    y = x1_ref[...] + jnp.dot(perm_t, s, preferred_element_type=F32)
    if not final:
        out_ref[...] = y
    else:
        y = _rms(y, gfin_ref[...])
        bsz, tc, _ = out_ref.shape
        for t in range(tc):
            out_ref[:, t, :] = y[t * bsz:(t + 1) * bsz, :]


def _unsort(x1, dest, expert_out, pos, norm_final, *, tile_off=0, ntiles=None, batch_major=None):
    final = batch_major is not None
    if final:
        batch, t_len, bsz, tc = batch_major
        nchunks = t_len // tc
        out_spec = pl.BlockSpec((bsz, tc, D_MODEL), lambda k, pos: (k // nchunks, k % nchunks, 0))
        out_shape = jax.ShapeDtypeStruct((batch, t_len, D_MODEL), F32)
    else:
        ntiles = x1.shape[0] // TILE
        out_spec = pl.BlockSpec((TILE, D_MODEL), lambda k, pos: (k, 0))
        out_shape = jax.ShapeDtypeStruct(x1.shape, F32)
    gran_spec = lambda q: pl.BlockSpec((GRAN, D_MODEL),
                                       lambda k, pos: (pos[(k + tile_off) * TILE_GRANS + q], 0))
    grid_spec = pltpu.PrefetchScalarGridSpec(
        num_scalar_prefetch=1, grid=(ntiles,),
        in_specs=[pl.BlockSpec((TILE, D_MODEL), lambda k, pos: (k + tile_off, 0)),
                  pl.BlockSpec((TILE, 1), lambda k, pos: (k + tile_off, 0))]
                 + [gran_spec(q) for q in range(TILE_GRANS)]
                 + [pl.BlockSpec((1, D_MODEL), lambda k, pos: (0, 0))],
        out_specs=out_spec)
    return pl.pallas_call(
        functools.partial(_unsort_kernel, final=final), grid_spec=grid_spec,
        out_shape=out_shape, name="moe_unsort",
        compiler_params=pltpu.CompilerParams(dimension_semantics=("arbitrary",),
                                             vmem_limit_bytes=VMEM_LIMIT),
    )(pos, x1, dest, *([expert_out] * TILE_GRANS), norm_final)


def _dispatch_tables(gid):
    ntiles = gid.shape[0]
    ngran = ntiles * TILE_GRANS
    nitems = -(-ngran // ITEM_GRANS) + N_EXPERT_GROUPS
    g = gid[:, 0, :TILE_GRANS].reshape(ngran)
    onehot = (g[:, None] == jnp.arange(N_EXPERT_GROUPS, dtype=jnp.int32)[None, :]).astype(jnp.int32)
    valid = g < N_EXPERT_GROUPS
    rank = jnp.cumsum(onehot, axis=0) - onehot
    cnt = jnp.sum(onehot, axis=0)
    pcnt = (cnt + (ITEM_GRANS - 1)) // ITEM_GRANS * ITEM_GRANS
    pend = jnp.cumsum(pcnt)
    pstart = pend - pcnt
    pos = jnp.sum(onehot * (pstart[None, :] + rank), axis=1)
    slots = jnp.arange(nitems * ITEM_GRANS, dtype=jnp.int32)
    hit = jnp.logical_and(pos[None, :] == slots[:, None], valid[None, :])
    src = jnp.sum(jnp.where(hit, jnp.arange(ngran, dtype=jnp.int32)[None, :], 0), axis=1)
    item_start = jnp.arange(nitems, dtype=jnp.int32) * ITEM_GRANS
    last_group = jnp.max(jnp.where(pcnt > 0, jnp.arange(N_EXPERT_GROUPS, dtype=jnp.int32), 0))
    wgid = jnp.minimum(jnp.sum((item_start[:, None] >= pend[None, :]).astype(jnp.int32), axis=1), last_group)
    wvalid = (item_start < pend[-1]).astype(jnp.int32)
    return src.astype(jnp.int32), wgid.astype(jnp.int32), wvalid, jnp.where(valid, pos, 0).astype(jnp.int32)


def _moe(x1, lw, l, norm_final, final_layouts=None):
    sorted_tok, dest, gid = _route_sort(x1, lw, l)
    src, wgid, wvalid, pos = _dispatch_tables(gid)
    expert_out = _experts(sorted_tok, src, wgid, wvalid, lw, l)
    if final_layouts is None:
        return _unsort(x1, dest, expert_out, pos, norm_final)
    return [_unsort(x1, dest, expert_out, pos, norm_final, tile_off=off, ntiles=n, batch_major=bm)
            for off, n, bm in final_layouts]


def _router_weights(w_grp, b_grp, w_er, b_er):
    pad = ROUTER_ROWS - N_EXPERTS - N_EXPERT_GROUPS
    w = jnp.pad(jnp.concatenate([w_er, w_grp], axis=-1), ((0, 0), (0, 0), (0, pad)))
    w = jnp.transpose(w, (0, 2, 1))
    b = jnp.pad(jnp.concatenate([b_er, b_grp], axis=-1), ((0, 0), (0, pad)))[:, :, None]
    hi = w.astype(BF16)
    lo = (w - hi.astype(F32)).astype(BF16)
    return hi, lo, b


def _to_time_major(x, nb):
    bt, t, w = x.shape
    return jnp.transpose(x.reshape(nb, bt // nb, t, w), (0, 2, 1, 3)).reshape(nb, t * (bt // nb), w)


def _from_time_major(x, t):
    nb, rows, w = x.shape
    bsz = rows // t
    return jnp.transpose(x.reshape(nb, t, bsz, w), (0, 2, 1, 3)).reshape(nb * bsz, t, w)


def kernel(x_prompt, x_sample, state_ssm_re, state_ssm_im, state_pool, norm_mix, w_in, lam_re, lam_im, log_dt, b_re, b_im, c_re, c_im, d_skip, w_glu, b_glu, w_pool, pool_scale, w_out, norm_ffn, w_grp, b_grp, w_erouter, b_erouter, w1, w3, w2, norm_final):
    depth = w_in.shape[0]
    pb, pt, _ = x_prompt.shape
    sb, st, _ = x_sample.shape
    p_tc = TILE // pb
    p_tiles = pt // p_tc
    s_bsz = TILE // st
    s_nb = sb // s_bsz
    p_rows = pb * pt

    are, aim, bbr, bbi = _discretise(lam_re, lam_im, log_dt, b_re, b_im)
    wx = jnp.concatenate([_blockdiag(bbr, True), _blockdiag(bbi, True)], axis=-1).astype(BF16)
    cm = jnp.concatenate([_blockdiag(c_re, False), _blockdiag(-c_im, False)], axis=-2).astype(BF16)

    wr_hi, wr_lo, b_r = _router_weights(w_grp, b_grp, w_erouter, b_erouter)
    stack_experts = lambda w: w.reshape((depth * N_EXPERTS,) + w.shape[2:])
    vec = lambda a: a[:, None, :]
    lw = dict(norm_mix=vec(norm_mix), w_in=w_in.astype(BF16), wx=wx, are=are, aim=aim, cm=cm,
              d_skip=vec(d_skip), w_glu=w_glu.astype(BF16), b_glu=vec(b_glu), w_pool=w_pool.astype(BF16),
              pool_scale=vec(pool_scale), w_out=w_out.astype(BF16), norm_ffn=vec(norm_ffn),
              wr_hi=wr_hi, wr_lo=wr_lo, b_r=b_r,
              w1=stack_experts(w1), w3=stack_experts(w3), w2=stack_experts(w2))
    nfin = norm_final[None]

    p_zero_h = jnp.zeros((1, pb, S5_LANES), F32)
    p_zero_hist = jnp.zeros((1, POOL_HIST * pb, POOL_WIDTH), F32)
    final_layouts = [(0, p_tiles, (pb, pt, pb, p_tc)), (p_tiles, s_nb, (sb, st, s_bsz, st))]

    outs = {k: [] for k in ('p_re', 'p_im', 'p_pool', 's_re', 's_im', 's_pool')}
    x = None
    for l in range(depth):
        h0re = state_ssm_re[l].reshape(s_nb, s_bsz, S5_LANES)
        h0im = state_ssm_im[l].reshape(s_nb, s_bsz, S5_LANES)
        hist = jnp.pad(_to_time_major(state_pool[l], s_nb), ((0, 0), ((POOL_HIST - POOL_BUF) * s_bsz, 0), (0, 0)))
        xs1, hre, him, pool = _mixer(x_sample if l == 0 else x, h0re, h0im, hist, lw, l, bsz=s_bsz, tc=st,
                                     past=POOL_BUF, nb=s_nb, nchunks=1, in_off=p_tiles)
        outs['s_re'].append(hre.reshape(sb, S5_GROUPS, S5_STATE))
        outs['s_im'].append(him.reshape(sb, S5_GROUPS, S5_STATE))
        outs['s_pool'].append(_from_time_major(pool, POOL_BUF))
        x1, hre, him, pool = _mixer(x_prompt if l == 0 else x, p_zero_h, p_zero_h, p_zero_hist, lw, l,
                                    bsz=pb, tc=p_tc, past=0, nb=1, nchunks=p_tiles, in_off=0, tail=xs1)
        outs['p_re'].append(hre.reshape(pb, S5_GROUPS, S5_STATE))
        outs['p_im'].append(him.reshape(pb, S5_GROUPS, S5_STATE))
        outs['p_pool'].append(_from_time_major(pool, POOL_BUF))
        x = _moe(x1, lw, l, nfin, final_layouts if l == depth - 1 else None)

    y_prompt, y_sample = x
    return (y_prompt, y_sample, jnp.stack(outs['p_re']), jnp.stack(outs['p_im']), jnp.stack(outs['p_pool']),
            jnp.stack(outs['s_re']), jnp.stack(outs['s_im']), jnp.stack(outs['s_pool']))
```

```python
import functools

import jax
import jax.numpy as jnp
from jax import lax
from jax.experimental import pallas as pl
from jax.experimental.pallas import tpu as pltpu

D_MODEL = 1024
S5_WIDTH = 512
S5_GROUP_CH = 16
S5_GROUPS = 32
S5_STATE = 64
S5_LANES = S5_GROUPS * S5_STATE
POOL_WIDTH = 512
POOL_WINDOWS = (2, 4, 8, 16)
POOL_GROUP_CH = 128
POOL_BUF = 15
POOL_HIST = 16
N_EXPERTS = 16
EXPERTS_PER_GROUP = 4
N_EXPERT_GROUPS = 4
D_EXPERT = 256
EPS = 1e-6

SUBLANES = 8
LANES = 128
S5_KBLOCK = 128
S5_NBLOCK = S5_KBLOCK // S5_GROUP_CH * S5_STATE
SCAN_LANES = 512
VMEM_LIMIT = 56 * 1024 * 1024

TILE = 512
GRAN = 32
TILE_GRANS = TILE // GRAN + N_EXPERT_GROUPS
SORTED_ROWS = TILE_GRANS * GRAN
ITEM_GRANS = 16
ITEM_ROWS = ITEM_GRANS * GRAN
ROUTE_TILES = 2
PAYLOAD = D_MODEL + 2 * LANES

F32 = jnp.float32
BF16 = jnp.bfloat16

MIXER_PARAMS = ('norm_mix', 'w_in', 'wx', 'are', 'aim', 'cm', 'd_skip', 'w_glu', 'b_glu', 'w_pool',
                'pool_scale', 'w_out')


def _layer_spec(arr, l):
    return pl.BlockSpec((None,) + arr.shape[1:], lambda *_: (l,) + (0,) * (arr.ndim - 1))


def _rms(x, g):
    return x * lax.rsqrt(jnp.mean(x * x, axis=-1, keepdims=True) + EPS) * g


def _gelu_tanh(x):
    return 0.5 * x * (1.0 + jnp.tanh(0.7978845608028654 * (x + 0.044715 * (x * x * x))))


def _disc_kernel(lr_ref, li_ref, ldt_ref, br_ref, bi_ref, are_ref, aim_ref, bbr_ref, bbi_ref):
    lr = lr_ref[...]
    li = li_ref[...]
    dt = jnp.exp(ldt_ref[...])
    mag = jnp.exp(lr * dt)
    ab_re = mag * jnp.cos(li * dt)
    ab_im = mag * jnp.sin(li * dt)
    den = lr * lr + li * li
    nr = ab_re - 1.0
    coef_re = (nr * lr + ab_im * li) / den
    coef_im = (ab_im * lr - nr * li) / den
    br = br_ref[...]
    bi = bi_ref[...]
    are_ref[...] = ab_re
    aim_ref[...] = ab_im
    bbr_ref[...] = coef_re * br - coef_im * bi
    bbi_ref[...] = coef_re * bi + coef_im * br


def _discretise(lam_re, lam_im, log_dt, b_re, b_im):
    depth = lam_re.shape[0]
    rows = depth * S5_GROUPS * S5_GROUP_CH
    rep = lambda a: jnp.repeat(a.reshape(depth * S5_GROUPS, -1), S5_GROUP_CH, axis=0)
    tr = lambda b: jnp.transpose(b, (0, 1, 3, 2)).reshape(rows, S5_STATE)
    out = jax.ShapeDtypeStruct((rows, S5_STATE), F32)
    are, aim, bbr, bbi = pl.pallas_call(
        _disc_kernel, out_shape=(out, out, out, out), name="s5_discretise",
    )(rep(lam_re), rep(lam_im), rep(log_dt[..., None]), tr(b_re), tr(b_im))
    shp = (depth, S5_GROUPS, S5_GROUP_CH, S5_STATE)
    are = are.reshape(shp)[:, :, 0, :].reshape(depth, 1, S5_LANES)
    aim = aim.reshape(shp)[:, :, 0, :].reshape(depth, 1, S5_LANES)
    return are, aim, bbr.reshape(shp), bbi.reshape(shp)


def _blockdiag(w, rows_first):
    depth = w.shape[0]
    gpb = S5_KBLOCK // S5_GROUP_CH
    w5 = w.reshape(depth, S5_GROUPS // gpb, gpb, S5_GROUP_CH, S5_STATE)
    eye = jnp.eye(gpb, dtype=w.dtype)
    if rows_first:
        return jnp.einsum('lkgcn,gh->lkgchn', w5, eye).reshape(depth, -1, S5_KBLOCK, S5_NBLOCK)
    return jnp.einsum('lkgcn,gh->lkgnhc', w5, eye).reshape(depth, -1, S5_NBLOCK, S5_KBLOCK)


MIXER_INPUTS = 16


def _interleave(gens):
    while gens:
        gens = [g for g in gens if next(g, None) is not None]


def _mixer_kernel(*refs, bsz, tc, past, nsteps, halves, has_tail):
    if not has_tail:
        _mixer_body(*refs, bsz=bsz, tc=tc, past=past, last=nsteps - 1, halves=halves)
        return
    tail_ref = refs[MIXER_INPUTS]
    body_refs = refs[:MIXER_INPUTS] + refs[MIXER_INPUTS + 1:]
    x1_ref = body_refs[MIXER_INPUTS]
    i = pl.program_id(1)

    @pl.when(i < nsteps)
    def _():
        _mixer_body(*body_refs, bsz=bsz, tc=tc, past=past, last=nsteps - 1, halves=halves)

    @pl.when(i >= nsteps)
    def _():
        x1_ref[...] = tail_ref[...]


def _mixer_body(x_ref, h0re_ref, h0im_ref, hist_ref, gmix_ref, win_ref, wx_ref, are_ref, aim_ref,
                cm_ref, dskip_ref, wglu_ref, bglu_ref, wpool_ref, pscale_ref, wout_ref,
                x1_ref, hre_out, him_out, pool_out, xre, xim, sre, sim, hre, him, zbuf,
                *, bsz, tc, past, last, halves):
    i = pl.program_id(1)
    rows = bsz * tc
    hist_rows = POOL_HIST * bsz

    @pl.when(i == 0)
    def _():
        hre[...] = h0re_ref[0]
        him[...] = h0im_ref[0]
        zbuf[0:hist_rows, :] = hist_ref[0]

    def chunk(hf):
        r0 = hf * rows
        rsl = slice(r0, r0 + rows)
        if len(x_ref.shape) == 3:
            x = jnp.concatenate([x_ref[:, hf * tc + t, :] for t in range(tc)], axis=0)
        else:
            x = x_ref[rsl, :]
        h = _rms(x, gmix_ref[...]).astype(BF16)
        yield 1
        proj = jnp.dot(h, win_ref[...], preferred_element_type=F32)
        u = proj[:, :S5_WIDTH]
        z = proj[:, S5_WIDTH:]
        zbuf[hist_rows + r0:hist_rows + r0 + rows, :] = z
        ub = u.astype(BF16)
        yield 2

        step_idx = (i * halves + hf) * tc + lax.broadcasted_iota(jnp.int32, (rows, 1), 0) // bsz
        pos = (step_idx + (past + 1)).astype(F32)
        pooled = []
        for gi, w in enumerate(POOL_WINDOWS):
            gs = slice(gi * POOL_GROUP_CH, (gi + 1) * POOL_GROUP_CH)
            s = zbuf[r0:r0 + hist_rows + rows, gs]
            k = 1
            while k < w:
                s = s[k * bsz:] + s[:-k * bsz]
                k *= 2
            inv_cnt = 1.0 / jnp.minimum(pos, float(w))
            pooled.append((s[-rows:] * inv_cnt - z[:, gs]).astype(BF16))
        yield 3
        pouts = [jnp.dot(pooled[gi], wpool_ref[gi], preferred_element_type=F32)
                 for gi in range(len(POOL_WINDOWS))]
        pool_mixed = (jnp.concatenate(pouts, axis=-1) * pscale_ref[...]).astype(BF16)
        x1_ref[rsl, :] = x + jnp.dot(pool_mixed, wout_ref[S5_WIDTH:, :], preferred_element_type=F32)
        yield 4

        ys = []
        for kb in range(S5_WIDTH // S5_KBLOCK):
            ns = slice(kb * S5_NBLOCK, (kb + 1) * S5_NBLOCK)
            xx = jnp.dot(ub[:, kb * S5_KBLOCK:(kb + 1) * S5_KBLOCK], wx_ref[kb],
                         preferred_element_type=F32)
            xre[rsl, ns] = xx[:, :S5_NBLOCK]
            xim[rsl, ns] = xx[:, S5_NBLOCK:]
            yield 5
            if bsz == SUBLANES:
                scan_rows(r0, 0, ns, True)
            else:
                def body(rb, c, ns=ns):
                    scan_rows(r0, pl.multiple_of(rb * SUBLANES, SUBLANES), ns, False)
                    return c
                lax.fori_loop(0, bsz // SUBLANES, body, 0)
            yield 6
            ys.append(jnp.dot(sre[rsl, ns].astype(BF16), cm_ref[kb, :S5_NBLOCK, :], preferred_element_type=F32)
                      + jnp.dot(sim[rsl, ns].astype(BF16), cm_ref[kb, S5_NBLOCK:, :],
                                preferred_element_type=F32))
        y = jnp.concatenate(ys, axis=-1) + dskip_ref[...] * u
        g = _gelu_tanh(y)
        gb = g.astype(BF16)
        yield 7
        s5_out = g * jax.nn.sigmoid(jnp.dot(gb, wglu_ref[...], preferred_element_type=F32) + bglu_ref[...])
        sb = s5_out.astype(BF16)
        yield 8
        x1_ref[rsl, :] += jnp.dot(sb, wout_ref[:S5_WIDTH, :], preferred_element_type=F32)

    assert SCAN_LANES == S5_NBLOCK

    def scan_rows(r0, b0, ls, static):
        ar = jnp.broadcast_to(are_ref[:, ls], (SUBLANES, SCAN_LANES))
        ai = jnp.broadcast_to(aim_ref[:, ls], (SUBLANES, SCAN_LANES))

        def step(t, carry):
            hr, hi = carry
            row = r0 + t * bsz + b0
            if not static:
                row = pl.multiple_of(row, SUBLANES)
            nr = ar * hr + (xre[pl.ds(row, SUBLANES), ls] - ai * hi)
            ni = ar * hi + (xim[pl.ds(row, SUBLANES), ls] + ai * hr)
            sre[pl.ds(row, SUBLANES), ls] = nr
            sim[pl.ds(row, SUBLANES), ls] = ni
            return nr, ni

        carry = (hre[pl.ds(b0, SUBLANES), ls], him[pl.ds(b0, SUBLANES), ls])
        if static:
            for t in range(tc):
                carry = step(t, carry)
        else:
            carry = lax.fori_loop(0, tc, step, carry, unroll=8)
        hre[pl.ds(b0, SUBLANES), ls] = carry[0]
        him[pl.ds(b0, SUBLANES), ls] = carry[1]

    _interleave([chunk(hf) for hf in range(halves)])

    all_rows = halves * rows

    @pl.when(i == last)
    def _():
        hre_out[0] = hre[...]
        him_out[0] = him[...]
        pool_out[0] = zbuf[all_rows + hist_rows - POOL_BUF * bsz:all_rows + hist_rows, :]

    @pl.when(i != last)
    def _():
        zbuf[0:hist_rows, :] = zbuf[all_rows:all_rows + hist_rows, :]


def _mixer(x, h0re, h0im, hist, lw, l, *, bsz, tc, past, nb, nchunks, halves, in_off, tail=None):
    rows = bsz * tc
    assert rows == TILE and nchunks % halves == 0
    blk = halves * rows
    nsteps = nchunks // halves
    ntail = 0 if tail is None else tail.shape[0] // blk
    assert ntail == 0 or (nb == 1 and tail.shape[0] % blk == 0)
    steps = nsteps + ntail
    step = lambda i: jnp.minimum(i, nsteps - 1)
    perb = lambda shape: pl.BlockSpec(shape, lambda b, i: (b,) + (0,) * (len(shape) - 1))
    if x.ndim == 2:
        x_spec = pl.BlockSpec((blk, D_MODEL), lambda b, i: (in_off + b * nsteps + step(i), 0))
    else:
        x_spec = pl.BlockSpec((bsz, halves * tc, D_MODEL), lambda b, i: (b, step(i), 0))
    in_specs = [
        x_spec,
        perb((1, bsz, S5_LANES)), perb((1, bsz, S5_LANES)), perb((1, POOL_HIST * bsz, POOL_WIDTH)),
    ] + [_layer_spec(lw[k], l) for k in MIXER_PARAMS]
    args = [x, h0re, h0im, hist] + [lw[k] for k in MIXER_PARAMS]
    assert len(args) == MIXER_INPUTS
    if ntail:
        in_specs.append(pl.BlockSpec((blk, D_MODEL), lambda b, i: (jnp.maximum(i - nsteps, 0), 0)))
        args.append(tail)
    out_specs = [
        pl.BlockSpec((blk, D_MODEL), lambda b, i: (b * steps + i, 0)),
        perb((1, bsz, S5_LANES)), perb((1, bsz, S5_LANES)), perb((1, POOL_BUF * bsz, POOL_WIDTH)),
    ]
    out_shape = [
        jax.ShapeDtypeStruct((nb * steps * blk, D_MODEL), F32),
        jax.ShapeDtypeStruct((nb, bsz, S5_LANES), F32),
        jax.ShapeDtypeStruct((nb, bsz, S5_LANES), F32),
        jax.ShapeDtypeStruct((nb, POOL_BUF * bsz, POOL_WIDTH), F32),
    ]
    scratch = [
        pltpu.VMEM((blk, S5_LANES), F32), pltpu.VMEM((blk, S5_LANES), F32),
        pltpu.VMEM((blk, S5_LANES), F32), pltpu.VMEM((blk, S5_LANES), F32),
        pltpu.VMEM((bsz, S5_LANES), F32), pltpu.VMEM((bsz, S5_LANES), F32),
        pltpu.VMEM(((halves * tc + POOL_HIST) * bsz, POOL_WIDTH), F32),
    ]
    return pl.pallas_call(
        functools.partial(_mixer_kernel, bsz=bsz, tc=tc, past=past, nsteps=nsteps, halves=halves,
                          has_tail=ntail > 0),
        grid=(nb, steps), in_specs=in_specs, out_specs=out_specs, out_shape=out_shape,
        scratch_shapes=scratch, name="mixer",
        compiler_params=pltpu.CompilerParams(dimension_semantics=("arbitrary", "arbitrary"),
                                             vmem_limit_bytes=VMEM_LIMIT),
    )(*args)


ROUTER_ROWS = 32


def _dot_nt(a, b):
    return lax.dot_general(a, b, (((1,), (1,)), ((), ())), preferred_element_type=F32)


def _argmax_rows(rows):
    best, idx = rows[0], jnp.zeros_like(rows[0])
    for k in range(1, len(rows)):
        gt = rows[k] > best
        best = jnp.where(gt, rows[k], best)
        idx = jnp.where(gt, float(k), idx)
    return best, idx


def _gating_t(lt):
    row = lambda r: lt[r:r + 1, :]
    g_rows = [row(N_EXPERTS + g) for g in range(N_EXPERT_GROUPS)]
    gmax, gidx = _argmax_rows(g_rows)
    gsum = g_rows[0] * 0.0
    for r in g_rows:
        gsum = gsum + jnp.exp(r - gmax)
    g_w = 1.0 / gsum
    el = []
    for k in range(EXPERTS_PER_GROUP):
        v = row((N_EXPERT_GROUPS - 1) * EXPERTS_PER_GROUP + k)
        for g in range(N_EXPERT_GROUPS - 2, -1, -1):
            v = jnp.where(gidx == float(g), row(g * EXPERTS_PER_GROUP + k), v)
        el.append(v)
    m1, i1 = _argmax_rows(el)
    m2, i2 = _argmax_rows([jnp.where(i1 == float(k), -jnp.inf, el[k]) for k in range(EXPERTS_PER_GROUP)])
    e2 = jnp.exp(m2 - m1)
    den = 1.0 + e2
    first = gidx * float(EXPERTS_PER_GROUP)
    return gidx, first + i1, first + i2, g_w / den, g_w * e2 / den


def _route_sort_kernel(x1_ref, tri_ref, gffn_ref, wrh_ref, wrl_ref, br_ref, sorted_ref, dest_ref, gid_ref):
    _interleave([_route_sort_tile(k, x1_ref, tri_ref, gffn_ref, wrh_ref, wrl_ref, br_ref, sorted_ref, dest_ref,
                                  gid_ref) for k in range(ROUTE_TILES)])


def _route_sort_tile(k, x1_ref, tri_ref, gffn_ref, wrh_ref, wrl_ref, br_ref, sorted_ref, dest_ref, gid_ref):
    x = x1_ref[k * TILE:(k + 1) * TILE, :]
    h2 = _rms(x, gffn_ref[...])
    hi = h2.astype(BF16)
    lo = (h2 - hi.astype(F32)).astype(BF16)
    yield 1
    lt = (_dot_nt(wrh_ref[...], hi) + _dot_nt(wrl_ref[...], hi) + _dot_nt(wrh_ref[...], lo)) + br_ref[...]
    yield 2
    gidx, ex1, ex2, gate1, gate2 = _gating_t(lt)
    yield 3

    sub8 = lax.broadcasted_iota(jnp.int32, (SUBLANES, TILE), 0).astype(F32)
    onehot = jnp.where(sub8 == gidx, 1.0, 0.0)
    rank = jnp.dot(onehot.astype(BF16), tri_ref[...], preferred_element_type=F32)
    counts = jnp.sum(onehot, axis=1, keepdims=True)
    padded = jnp.floor((counts + float(GRAN - 1)) * (1.0 / GRAN)) * float(GRAN)
    e0 = padded[0:1, :]
    e1 = e0 + padded[1:2, :]
    e2 = e1 + padded[2:3, :]
    e3 = e2 + padded[3:4, :]
    dest = jnp.where(gidx == 0.0, rank[0:1, :],
                     jnp.where(gidx == 1.0, e0 + rank[1:2, :],
                               jnp.where(gidx == 2.0, e1 + rank[2:3, :], e2 + rank[3:4, :])))
    dest_ref[k * TILE:(k + 1) * TILE, :] = jnp.broadcast_to(dest, (LANES, TILE)).T[:, 0:1]

    slot = lax.broadcasted_iota(jnp.int32, (SORTED_ROWS, TILE), 0).astype(F32)
    perm = jnp.where(slot == dest, 1.0, 0.0).astype(BF16)
    sub = lax.broadcasted_iota(jnp.int32, (N_EXPERTS, TILE), 0).astype(F32)
    comb = jnp.where(sub == ex1, gate1, 0.0) + jnp.where(sub == ex2, gate2, 0.0)
    c_hi = comb.astype(BF16)
    c_lo = (comb - c_hi.astype(F32)).astype(BF16)
    zeros = jnp.zeros((LANES - N_EXPERTS, TILE), BF16)
    gates = jnp.concatenate([c_hi, zeros, c_lo, zeros], axis=0)
    yield 4
    rows = slice(k * SORTED_ROWS, (k + 1) * SORTED_ROWS)
    sorted_ref[rows, :D_MODEL] = jnp.dot(perm, hi, preferred_element_type=F32).astype(BF16)
    sorted_ref[rows, D_MODEL:] = _dot_nt(perm, gates).astype(BF16)

    q = lax.broadcasted_iota(jnp.int32, (1, LANES), 1).astype(F32) * float(GRAN)
    gid = (jnp.where(q >= e0, 1, 0) + jnp.where(q >= e1, 1, 0)
           + jnp.where(q >= e2, 1, 0) + jnp.where(q >= e3, 1, 0))
    gid_ref[k] = gid.astype(jnp.int32)


def _route_sort(x1, lw, l):
    ntiles = x1.shape[0] // TILE
    params = [lw[k] for k in ('norm_ffn', 'wr_hi', 'wr_lo', 'b_r')]
    tri = jnp.triu(jnp.ones((TILE, TILE), BF16), k=1)
    return pl.pallas_call(
        _route_sort_kernel, grid=(ntiles // ROUTE_TILES,),
        in_specs=[pl.BlockSpec((ROUTE_TILES * TILE, D_MODEL), lambda i: (i, 0)),
                  pl.BlockSpec((TILE, TILE), lambda i: (0, 0))] + [_layer_spec(p, l) for p in params],
        out_specs=[pl.BlockSpec((ROUTE_TILES * SORTED_ROWS, PAYLOAD), lambda i: (i, 0)),
                   pl.BlockSpec((ROUTE_TILES * TILE, 1), lambda i: (i, 0)),
                   pl.BlockSpec((ROUTE_TILES, 1, LANES), lambda i: (i, 0, 0))],
        out_shape=[jax.ShapeDtypeStruct((ntiles * SORTED_ROWS, PAYLOAD), BF16),
                   jax.ShapeDtypeStruct((x1.shape[0], 1), F32),
                   jax.ShapeDtypeStruct((ntiles, 1, LANES), jnp.int32)],
        name="moe_route_sort",
        compiler_params=pltpu.CompilerParams(dimension_semantics=("arbitrary",),
                                             vmem_limit_bytes=VMEM_LIMIT),
    )(x1, tri, *params)


def _experts_kernel(src_ref, wgid_ref, wvalid_ref, *refs):
    gran_refs = refs[:ITEM_GRANS]
    w1f_ref, w3f_ref, w2f_ref, out_ref, w1_ref, w3_ref, w2_ref = refs[ITEM_GRANS:]
    j = pl.program_id(0)

    @pl.when(jnp.logical_or(j == 0, wgid_ref[j] != wgid_ref[jnp.maximum(j - 1, 0)]))
    def _():
        w1_ref[...] = w1f_ref[...].astype(BF16)
        w3_ref[...] = w3f_ref[...].astype(BF16)
        w2_ref[...] = w2f_ref[...].astype(BF16)

    @pl.when(wvalid_ref[j] > 0)
    def _():
        full = jnp.concatenate([g[...] for g in gran_refs], axis=0)
        h = full[:, :D_MODEL]
        comb = (full[:, D_MODEL:D_MODEL + LANES].astype(F32)
                + full[:, D_MODEL + LANES:].astype(F32))
        first = wgid_ref[j] * EXPERTS_PER_GROUP
        lane = lax.broadcasted_iota(jnp.int32, (ITEM_ROWS, LANES), 1)
        hids = []
        for e in range(EXPERTS_PER_GROUP):
            a = jnp.dot(h, w1_ref[e], preferred_element_type=F32)
            b = jnp.dot(h, w3_ref[e], preferred_element_type=F32)
            gate = jnp.sum(jnp.where(lane == first + e, comb, 0.0), axis=-1, keepdims=True)
            hids.append(((a * jax.nn.sigmoid(a)) * b * gate).astype(BF16))
        hid = jnp.concatenate(hids, axis=-1)
        w2g = w2_ref[...].reshape(EXPERTS_PER_GROUP * D_EXPERT, D_MODEL)
        out_ref[...] = jnp.dot(hid, w2g, preferred_element_type=F32).astype(BF16)

    @pl.when(wvalid_ref[j] == 0)
    def _():
        out_ref[...] = jnp.zeros(out_ref.shape, out_ref.dtype)


def _experts(sorted_tok, src, wgid, wvalid, lw, l):
    nitems = wgid.shape[0]
    gran_spec = lambda s: pl.BlockSpec((GRAN, PAYLOAD), lambda j, src, wg, wv: (src[j * ITEM_GRANS + s], 0))
    wspec = lambda shape: pl.BlockSpec(shape, lambda j, src, wg, wv: (l * N_EXPERT_GROUPS + wg[j], 0, 0))
    grid_spec = pltpu.PrefetchScalarGridSpec(
        num_scalar_prefetch=3, grid=(nitems,),
        in_specs=[gran_spec(s) for s in range(ITEM_GRANS)] + [
            wspec((EXPERTS_PER_GROUP, D_MODEL, D_EXPERT)), wspec((EXPERTS_PER_GROUP, D_MODEL, D_EXPERT)),
            wspec((EXPERTS_PER_GROUP, D_EXPERT, D_MODEL))],
        out_specs=pl.BlockSpec((ITEM_ROWS, D_MODEL), lambda j, src, wg, wv: (j, 0)),
        scratch_shapes=[pltpu.VMEM((EXPERTS_PER_GROUP, D_MODEL, D_EXPERT), BF16),
                        pltpu.VMEM((EXPERTS_PER_GROUP, D_MODEL, D_EXPERT), BF16),
                        pltpu.VMEM((EXPERTS_PER_GROUP, D_EXPERT, D_MODEL), BF16)])
    return pl.pallas_call(
        _experts_kernel, grid_spec=grid_spec,
        out_shape=jax.ShapeDtypeStruct((nitems * ITEM_ROWS, D_MODEL), BF16),
        name="moe_experts",
        compiler_params=pltpu.CompilerParams(dimension_semantics=("arbitrary",),
                                             vmem_limit_bytes=VMEM_LIMIT),
    )(src, wgid, wvalid, *([sorted_tok] * ITEM_GRANS), lw['w1'], lw['w3'], lw['w2'])


def _unsort_kernel(pos_ref, x1_ref, dest_ref, *refs, final):
    gran_refs = refs[:TILE_GRANS]
    gfin_ref, out_ref = refs[TILE_GRANS:]
    s = jnp.concatenate([g[...] for g in gran_refs], axis=0)
    slot = lax.broadcasted_iota(jnp.int32, (TILE, SORTED_ROWS), 1).astype(F32)
    perm_t = jnp.where(slot == dest_ref[...], 1.0, 0.0).astype(BF16)
    y = x1_ref[...] + jnp.dot(perm_t, s, preferred_element_type=F32)
    if not final:
        out_ref[...] = y
    else:
        y = _rms(y, gfin_ref[...])
        bsz, tc, _ = out_ref.shape
        for t in range(tc):
            out_ref[:, t, :] = y[t * bsz:(t + 1) * bsz, :]


def _unsort(x1, dest, expert_out, pos, norm_final, *, tile_off=0, ntiles=None, batch_major=None):
    final = batch_major is not None
    if final:
        batch, t_len, bsz, tc = batch_major
        nchunks = t_len // tc
        out_spec = pl.BlockSpec((bsz, tc, D_MODEL), lambda k, pos: (k // nchunks, k % nchunks, 0))
        out_shape = jax.ShapeDtypeStruct((batch, t_len, D_MODEL), F32)
    else:
        ntiles = x1.shape[0] // TILE
        out_spec = pl.BlockSpec((TILE, D_MODEL), lambda k, pos: (k, 0))
        out_shape = jax.ShapeDtypeStruct(x1.shape, F32)
    gran_spec = lambda q: pl.BlockSpec((GRAN, D_MODEL),
                                       lambda k, pos: (pos[(k + tile_off) * TILE_GRANS + q], 0))
    grid_spec = pltpu.PrefetchScalarGridSpec(
        num_scalar_prefetch=1, grid=(ntiles,),
        in_specs=[pl.BlockSpec((TILE, D_MODEL), lambda k, pos: (k + tile_off, 0)),
                  pl.BlockSpec((TILE, 1), lambda k, pos: (k + tile_off, 0))]
                 + [gran_spec(q) for q in range(TILE_GRANS)]
                 + [pl.BlockSpec((1, D_MODEL), lambda k, pos: (0, 0))],
        out_specs=out_spec)
    return pl.pallas_call(
        functools.partial(_unsort_kernel, final=final), grid_spec=grid_spec,
        out_shape=out_shape, name="moe_unsort",
        compiler_params=pltpu.CompilerParams(dimension_semantics=("arbitrary",),
                                             vmem_limit_bytes=VMEM_LIMIT),
    )(pos, x1, dest, *([expert_out] * TILE_GRANS), norm_final)


def _dispatch_tables(gid):
    ntiles = gid.shape[0]
    ngran = ntiles * TILE_GRANS
    nitems = -(-ngran // ITEM_GRANS) + N_EXPERT_GROUPS
    g = gid[:, 0, :TILE_GRANS].reshape(ngran)
    onehot = (g[:, None] == jnp.arange(N_EXPERT_GROUPS, dtype=jnp.int32)[None, :]).astype(jnp.int32)
    valid = g < N_EXPERT_GROUPS
    rank = jnp.cumsum(onehot, axis=0) - onehot
    cnt = jnp.sum(onehot, axis=0)
    pcnt = (cnt + (ITEM_GRANS - 1)) // ITEM_GRANS * ITEM_GRANS
    pend = jnp.cumsum(pcnt)
    pstart = pend - pcnt
    pos = jnp.sum(onehot * (pstart[None, :] + rank), axis=1)
    slots = jnp.arange(nitems * ITEM_GRANS, dtype=jnp.int32)
    hit = jnp.logical_and(pos[None, :] == slots[:, None], valid[None, :])
    src = jnp.sum(jnp.where(hit, jnp.arange(ngran, dtype=jnp.int32)[None, :], 0), axis=1)
    item_start = jnp.arange(nitems, dtype=jnp.int32) * ITEM_GRANS
    last_group = jnp.max(jnp.where(pcnt > 0, jnp.arange(N_EXPERT_GROUPS, dtype=jnp.int32), 0))
    wgid = jnp.minimum(jnp.sum((item_start[:, None] >= pend[None, :]).astype(jnp.int32), axis=1), last_group)
    wvalid = (item_start < pend[-1]).astype(jnp.int32)
    return src.astype(jnp.int32), wgid.astype(jnp.int32), wvalid, jnp.where(valid, pos, 0).astype(jnp.int32)


def _moe(x1, lw, l, norm_final, final_layouts=None):
    sorted_tok, dest, gid = _route_sort(x1, lw, l)
    src, wgid, wvalid, pos = _dispatch_tables(gid)
    expert_out = _experts(sorted_tok, src, wgid, wvalid, lw, l)
    if final_layouts is None:
        return _unsort(x1, dest, expert_out, pos, norm_final)
    return [_unsort(x1, dest, expert_out, pos, norm_final, tile_off=off, ntiles=n, batch_major=bm)
            for off, n, bm in final_layouts]


def _router_weights(w_grp, b_grp, w_er, b_er):
    pad = ROUTER_ROWS - N_EXPERTS - N_EXPERT_GROUPS
    w = jnp.pad(jnp.concatenate([w_er, w_grp], axis=-1), ((0, 0), (0, 0), (0, pad)))
    w = jnp.transpose(w, (0, 2, 1))
    b = jnp.pad(jnp.concatenate([b_er, b_grp], axis=-1), ((0, 0), (0, pad)))[:, :, None]
    hi = w.astype(BF16)
    lo = (w - hi.astype(F32)).astype(BF16)
    return hi, lo, b


def _to_time_major(x, nb):
    bt, t, w = x.shape
    return jnp.transpose(x.reshape(nb, bt // nb, t, w), (0, 2, 1, 3)).reshape(nb, t * (bt // nb), w)


def _from_time_major(x, t):
    nb, rows, w = x.shape
    bsz = rows // t
    return jnp.transpose(x.reshape(nb, t, bsz, w), (0, 2, 1, 3)).reshape(nb * bsz, t, w)


def kernel(x_prompt, x_sample, state_ssm_re, state_ssm_im, state_pool, norm_mix, w_in, lam_re, lam_im, log_dt, b_re, b_im, c_re, c_im, d_skip, w_glu, b_glu, w_pool, pool_scale, w_out, norm_ffn, w_grp, b_grp, w_erouter, b_erouter, w1, w3, w2, norm_final):
    depth = w_in.shape[0]
    pb, pt, _ = x_prompt.shape
    sb, st, _ = x_sample.shape
    p_tc = TILE // pb
    p_tiles = pt // p_tc
    s_bsz = TILE // st
    s_nb = sb // s_bsz
    p_rows = pb * pt

    are, aim, bbr, bbi = _discretise(lam_re, lam_im, log_dt, b_re, b_im)
    wx = jnp.concatenate([_blockdiag(bbr, True), _blockdiag(bbi, True)], axis=-1).astype(BF16)
    cm = jnp.concatenate([_blockdiag(c_re, False), _blockdiag(-c_im, False)], axis=-2).astype(BF16)

    wr_hi, wr_lo, b_r = _router_weights(w_grp, b_grp, w_erouter, b_erouter)
    stack_experts = lambda w: w.reshape((depth * N_EXPERTS,) + w.shape[2:])
    vec = lambda a: a[:, None, :]
    lw = dict(norm_mix=vec(norm_mix), w_in=w_in.astype(BF16), wx=wx, are=are, aim=aim, cm=cm,
              d_skip=vec(d_skip), w_glu=w_glu.astype(BF16), b_glu=vec(b_glu), w_pool=w_pool.astype(BF16),
              pool_scale=vec(pool_scale), w_out=w_out.astype(BF16), norm_ffn=vec(norm_ffn),
              wr_hi=wr_hi, wr_lo=wr_lo, b_r=b_r,
              w1=stack_experts(w1), w3=stack_experts(w3), w2=stack_experts(w2))
    nfin = norm_final[None]

    p_zero_h = jnp.zeros((1, pb, S5_LANES), F32)
    p_zero_hist = jnp.zeros((1, POOL_HIST * pb, POOL_WIDTH), F32)
    final_layouts = [(0, p_tiles, (pb, pt, pb, p_tc)), (p_tiles, s_nb, (sb, st, s_bsz, st))]

    outs = {k: [] for k in ('p_re', 'p_im', 'p_pool', 's_re', 's_im', 's_pool')}
    x = None
    for l in range(depth):
        h0re = state_ssm_re[l].reshape(s_nb, s_bsz, S5_LANES)
        h0im = state_ssm_im[l].reshape(s_nb, s_bsz, S5_LANES)
        hist = jnp.pad(_to_time_major(state_pool[l], s_nb), ((0, 0), ((POOL_HIST - POOL_BUF) * s_bsz, 0), (0, 0)))
        xs1, hre, him, pool = _mixer(x_sample if l == 0 else x, h0re, h0im, hist, lw, l, bsz=s_bsz, tc=st,
                                     past=POOL_BUF, nb=s_nb, nchunks=1, halves=1, in_off=p_tiles)
        outs['s_re'].append(hre.reshape(sb, S5_GROUPS, S5_STATE))
        outs['s_im'].append(him.reshape(sb, S5_GROUPS, S5_STATE))
        outs['s_pool'].append(_from_time_major(pool, POOL_BUF))
        x1, hre, him, pool = _mixer(x_prompt if l == 0 else x, p_zero_h, p_zero_h, p_zero_hist, lw, l,
                                    bsz=pb, tc=p_tc, past=0, nb=1, nchunks=p_tiles, halves=2, in_off=0, tail=xs1)
        outs['p_re'].append(hre.reshape(pb, S5_GROUPS, S5_STATE))
        outs['p_im'].append(him.reshape(pb, S5_GROUPS, S5_STATE))
        outs['p_pool'].append(_from_time_major(pool, POOL_BUF))
        x = _moe(x1, lw, l, nfin, final_layouts if l == depth - 1 else None)

    y_prompt, y_sample = x
    return (y_prompt, y_sample, jnp.stack(outs['p_re']), jnp.stack(outs['p_im']), jnp.stack(outs['p_pool']),
            jnp.stack(outs['s_re']), jnp.stack(outs['s_im']), jnp.stack(outs['s_pool']))
```

```python
import functools

import jax
import jax.numpy as jnp
from jax import lax
from jax.experimental import pallas as pl
from jax.experimental.pallas import tpu as pltpu

D_MODEL = 1024
S5_WIDTH = 512
S5_GROUP_CH = 16
S5_GROUPS = 32
S5_STATE = 64
S5_LANES = S5_GROUPS * S5_STATE
POOL_WIDTH = 512
POOL_WINDOWS = (2, 4, 8, 16)
POOL_GROUP_CH = 128
POOL_BUF = 15
POOL_HIST = 16
N_EXPERTS = 16
EXPERTS_PER_GROUP = 4
N_EXPERT_GROUPS = 4
D_EXPERT = 256
EPS = 1e-6

SUBLANES = 8
LANES = 128
S5_KBLOCK = 128
S5_NBLOCK = S5_KBLOCK // S5_GROUP_CH * S5_STATE
SCAN_LANES = 512
VMEM_LIMIT = 56 * 1024 * 1024

TILE = 512
GRAN = 32
TILE_GRANS = TILE // GRAN + N_EXPERT_GROUPS
SORTED_ROWS = TILE_GRANS * GRAN
ITEM_GRANS = 16
ITEM_ROWS = ITEM_GRANS * GRAN
ROUTE_TILES = 2
PAYLOAD = D_MODEL + 2 * LANES

F32 = jnp.float32
BF16 = jnp.bfloat16

MIXER_PARAMS = ('norm_mix', 'w_in', 'wx', 'are', 'aim', 'cm', 'd_skip', 'w_glu', 'b_glu', 'w_pool',
                'pool_scale', 'w_out')


def _layer_spec(arr, l):
    return pl.BlockSpec((None,) + arr.shape[1:], lambda *_: (l,) + (0,) * (arr.ndim - 1))


def _rms(x, g):
    return x * lax.rsqrt(jnp.mean(x * x, axis=-1, keepdims=True) + EPS) * g


def _gelu_tanh(x):
    return 0.5 * x * (1.0 + jnp.tanh(0.7978845608028654 * (x + 0.044715 * (x * x * x))))


def _disc_kernel(lr_ref, li_ref, ldt_ref, br_ref, bi_ref, are_ref, aim_ref, bbr_ref, bbi_ref):
    lr = lr_ref[...]
    li = li_ref[...]
    dt = jnp.exp(ldt_ref[...])
    mag = jnp.exp(lr * dt)
    ab_re = mag * jnp.cos(li * dt)
    ab_im = mag * jnp.sin(li * dt)
    den = lr * lr + li * li
    nr = ab_re - 1.0
    coef_re = (nr * lr + ab_im * li) / den
    coef_im = (ab_im * lr - nr * li) / den
    br = br_ref[...]
    bi = bi_ref[...]
    are_ref[...] = ab_re
    aim_ref[...] = ab_im
    bbr_ref[...] = coef_re * br - coef_im * bi
    bbi_ref[...] = coef_re * bi + coef_im * br


def _discretise(lam_re, lam_im, log_dt, b_re, b_im):
    depth = lam_re.shape[0]
    rows = depth * S5_GROUPS * S5_GROUP_CH
    rep = lambda a: jnp.repeat(a.reshape(depth * S5_GROUPS, -1), S5_GROUP_CH, axis=0)
    tr = lambda b: jnp.transpose(b, (0, 1, 3, 2)).reshape(rows, S5_STATE)
    out = jax.ShapeDtypeStruct((rows, S5_STATE), F32)
    are, aim, bbr, bbi = pl.pallas_call(
        _disc_kernel, out_shape=(out, out, out, out), name="s5_discretise",
    )(rep(lam_re), rep(lam_im), rep(log_dt[..., None]), tr(b_re), tr(b_im))
    shp = (depth, S5_GROUPS, S5_GROUP_CH, S5_STATE)
    are = are.reshape(shp)[:, :, 0, :].reshape(depth, 1, S5_LANES)
    aim = aim.reshape(shp)[:, :, 0, :].reshape(depth, 1, S5_LANES)
    return are, aim, bbr.reshape(shp), bbi.reshape(shp)


def _blockdiag(w, rows_first):
    depth = w.shape[0]
    gpb = S5_KBLOCK // S5_GROUP_CH
    w5 = w.reshape(depth, S5_GROUPS // gpb, gpb, S5_GROUP_CH, S5_STATE)
    eye = jnp.eye(gpb, dtype=w.dtype)
    if rows_first:
        return jnp.einsum('lkgcn,gh->lkgchn', w5, eye).reshape(depth, -1, S5_KBLOCK, S5_NBLOCK)
    return jnp.einsum('lkgcn,gh->lkgnhc', w5, eye).reshape(depth, -1, S5_NBLOCK, S5_KBLOCK)


def _unsorted(dest, gran_refs):
    s = jnp.concatenate([g[...] for g in gran_refs], axis=0)
    slot = lax.broadcasted_iota(jnp.int32, (TILE, SORTED_ROWS), 1).astype(F32)
    perm_t = jnp.where(slot == dest, 1.0, 0.0).astype(BF16)
    return jnp.dot(perm_t, s, preferred_element_type=F32)


MIXER_INPUTS = 16


def _interleave(gens):
    while gens:
        gens = [g for g in gens if next(g, None) is not None]


def _mixer_kernel(*refs, bsz, tc, past, nsteps, halves, has_tail, unsort):
    refs = refs[1:] if unsort else refs
    inputs, rest = refs[:MIXER_INPUTS], refs[MIXER_INPUTS:]
    tail_ref = None
    if has_tail:
        tail_ref, rest = rest[0], rest[1:]
    moe = None
    if unsort:
        ngran = halves * TILE_GRANS
        moe, rest = (rest[0], rest[1:1 + ngran]), rest[1 + ngran:]
    body = functools.partial(_mixer_body, *inputs, *rest, moe=moe, bsz=bsz, tc=tc, past=past,
                             last=nsteps - 1, halves=halves)
    if not has_tail:
        body()
        return
    x1_ref = rest[0]
    i = pl.program_id(1)
    pl.when(i < nsteps)(body)

    @pl.when(i >= nsteps)
    def _():
        x1_ref[...] = tail_ref[...]


def _mixer_body(x_ref, h0re_ref, h0im_ref, hist_ref, gmix_ref, win_ref, wx_ref, are_ref, aim_ref,
                cm_ref, dskip_ref, wglu_ref, bglu_ref, wpool_ref, pscale_ref, wout_ref,
                x1_ref, hre_out, him_out, pool_out, xre, xim, sre, sim, hre, him, zbuf,
                *, moe, bsz, tc, past, last, halves):
    i = pl.program_id(1)
    rows = bsz * tc
    hist_rows = POOL_HIST * bsz

    @pl.when(i == 0)
    def _():
        hre[...] = h0re_ref[0]
        him[...] = h0im_ref[0]
        zbuf[0:hist_rows, :] = hist_ref[0]

    def chunk(hf):
        r0 = hf * rows
        rsl = slice(r0, r0 + rows)
        if len(x_ref.shape) == 3:
            x = jnp.concatenate([x_ref[:, hf * tc + t, :] for t in range(tc)], axis=0)
        else:
            x = x_ref[rsl, :]
        if moe is not None:
            dest_ref, gran_refs = moe
            x = x + _unsorted(dest_ref[rsl, :], gran_refs[hf * TILE_GRANS:(hf + 1) * TILE_GRANS])
        h = _rms(x, gmix_ref[...]).astype(BF16)
        yield 1
        proj = jnp.dot(h, win_ref[...], preferred_element_type=F32)
        u = proj[:, :S5_WIDTH]
        z = proj[:, S5_WIDTH:]
        zbuf[hist_rows + r0:hist_rows + r0 + rows, :] = z
        ub = u.astype(BF16)
        yield 2

        step_idx = (i * halves + hf) * tc + lax.broadcasted_iota(jnp.int32, (rows, 1), 0) // bsz
        pos = (step_idx + (past + 1)).astype(F32)
        pooled = []
        for gi, w in enumerate(POOL_WINDOWS):
            gs = slice(gi * POOL_GROUP_CH, (gi + 1) * POOL_GROUP_CH)
            s = zbuf[r0:r0 + hist_rows + rows, gs]
            k = 1
            while k < w:
                s = s[k * bsz:] + s[:-k * bsz]
                k *= 2
            inv_cnt = 1.0 / jnp.minimum(pos, float(w))
            pooled.append((s[-rows:] * inv_cnt - z[:, gs]).astype(BF16))
        yield 3
        pouts = [jnp.dot(pooled[gi], wpool_ref[gi], preferred_element_type=F32)
                 for gi in range(len(POOL_WINDOWS))]
        pool_mixed = (jnp.concatenate(pouts, axis=-1) * pscale_ref[...]).astype(BF16)
        x1_ref[rsl, :] = x + jnp.dot(pool_mixed, wout_ref[S5_WIDTH:, :], preferred_element_type=F32)
        yield 4

        ys = []
        for kb in range(S5_WIDTH // S5_KBLOCK):
            ns = slice(kb * S5_NBLOCK, (kb + 1) * S5_NBLOCK)
            xx = jnp.dot(ub[:, kb * S5_KBLOCK:(kb + 1) * S5_KBLOCK], wx_ref[kb],
                         preferred_element_type=F32)
            xre[rsl, ns] = xx[:, :S5_NBLOCK]
            xim[rsl, ns] = xx[:, S5_NBLOCK:]
            yield 5
            if bsz == SUBLANES:
                scan_rows(r0, 0, ns, True)
            else:
                def body(rb, c, ns=ns):
                    scan_rows(r0, pl.multiple_of(rb * SUBLANES, SUBLANES), ns, False)
                    return c
                lax.fori_loop(0, bsz // SUBLANES, body, 0)
            yield 6
            ys.append(jnp.dot(sre[rsl, ns].astype(BF16), cm_ref[kb, :S5_NBLOCK, :], preferred_element_type=F32)
                      + jnp.dot(sim[rsl, ns].astype(BF16), cm_ref[kb, S5_NBLOCK:, :],
                                preferred_element_type=F32))
        y = jnp.concatenate(ys, axis=-1) + dskip_ref[...] * u
        g = _gelu_tanh(y)
        gb = g.astype(BF16)
        yield 7
        s5_out = g * jax.nn.sigmoid(jnp.dot(gb, wglu_ref[...], preferred_element_type=F32) + bglu_ref[...])
        sb = s5_out.astype(BF16)
        yield 8
        x1_ref[rsl, :] += jnp.dot(sb, wout_ref[:S5_WIDTH, :], preferred_element_type=F32)

    assert SCAN_LANES == S5_NBLOCK

    def scan_rows(r0, b0, ls, static):
        ar = jnp.broadcast_to(are_ref[:, ls], (SUBLANES, SCAN_LANES))
        ai = jnp.broadcast_to(aim_ref[:, ls], (SUBLANES, SCAN_LANES))

        def step(t, carry):
            hr, hi = carry
            row = r0 + t * bsz + b0
            if not static:
                row = pl.multiple_of(row, SUBLANES)
            nr = ar * hr + (xre[pl.ds(row, SUBLANES), ls] - ai * hi)
            ni = ar * hi + (xim[pl.ds(row, SUBLANES), ls] + ai * hr)
            sre[pl.ds(row, SUBLANES), ls] = nr
            sim[pl.ds(row, SUBLANES), ls] = ni
            return nr, ni

        carry = (hre[pl.ds(b0, SUBLANES), ls], him[pl.ds(b0, SUBLANES), ls])
        if static:
            for t in range(tc):
                carry = step(t, carry)
        else:
            carry = lax.fori_loop(0, tc, step, carry, unroll=8)
        hre[pl.ds(b0, SUBLANES), ls] = carry[0]
        him[pl.ds(b0, SUBLANES), ls] = carry[1]

    _interleave([chunk(hf) for hf in range(halves)])

    all_rows = halves * rows

    @pl.when(i == last)
    def _():
        hre_out[0] = hre[...]
        him_out[0] = him[...]
        pool_out[0] = zbuf[all_rows + hist_rows - POOL_BUF * bsz:all_rows + hist_rows, :]

    @pl.when(i != last)
    def _():
        zbuf[0:hist_rows, :] = zbuf[all_rows:all_rows + hist_rows, :]


def _mixer(x, h0re, h0im, hist, lw, l, *, bsz, tc, past, nb, nchunks, halves, in_off, tail=None, moe=None):
    rows = bsz * tc
    assert rows == TILE and nchunks % halves == 0
    blk = halves * rows
    nsteps = nchunks // halves
    ntail = 0 if tail is None else tail.shape[0] // blk
    assert ntail == 0 or (nb == 1 and tail.shape[0] % blk == 0)
    steps = nsteps + ntail
    step = lambda i: jnp.minimum(i, nsteps - 1)
    perb = lambda shape: pl.BlockSpec(shape, lambda b, i, *_: (b,) + (0,) * (len(shape) - 1))
    in_blk = lambda b, i: in_off + b * nsteps + step(i)
    if x.ndim == 2:
        x_spec = pl.BlockSpec((blk, D_MODEL), lambda b, i, *_: (in_blk(b, i), 0))
    else:
        x_spec = pl.BlockSpec((bsz, halves * tc, D_MODEL), lambda b, i, *_: (b, step(i), 0))
    in_specs = [
        x_spec,
        perb((1, bsz, S5_LANES)), perb((1, bsz, S5_LANES)), perb((1, POOL_HIST * bsz, POOL_WIDTH)),
    ] + [_layer_spec(lw[k], l) for k in MIXER_PARAMS]
    args = [x, h0re, h0im, hist] + [lw[k] for k in MIXER_PARAMS]
    assert len(args) == MIXER_INPUTS
    if ntail:
        in_specs.append(pl.BlockSpec((blk, D_MODEL), lambda b, i, *_: (jnp.maximum(i - nsteps, 0), 0)))
        args.append(tail)
    prefetch = []
    if moe is not None:
        dest, expert_out, pos = moe
        prefetch = [pos]
        in_specs.append(pl.BlockSpec((blk, 1), lambda b, i, pos: (in_blk(b, i), 0)))
        args.append(dest)
        for hf in range(halves):
            for q in range(TILE_GRANS):
                in_specs.append(pl.BlockSpec(
                    (GRAN, D_MODEL),
                    lambda b, i, pos, hf=hf, q=q: (pos[(in_blk(b, i) * halves + hf) * TILE_GRANS + q], 0)))
                args.append(expert_out)
    out_specs = [
        pl.BlockSpec((blk, D_MODEL), lambda b, i, *_: (b * steps + i, 0)),
        perb((1, bsz, S5_LANES)), perb((1, bsz, S5_LANES)), perb((1, POOL_BUF * bsz, POOL_WIDTH)),
    ]
    out_shape = [
        jax.ShapeDtypeStruct((nb * steps * blk, D_MODEL), F32),
        jax.ShapeDtypeStruct((nb, bsz, S5_LANES), F32),
        jax.ShapeDtypeStruct((nb, bsz, S5_LANES), F32),
        jax.ShapeDtypeStruct((nb, POOL_BUF * bsz, POOL_WIDTH), F32),
    ]
    scratch = [
        pltpu.VMEM((blk, S5_LANES), F32), pltpu.VMEM((blk, S5_LANES), F32),
        pltpu.VMEM((blk, S5_LANES), F32), pltpu.VMEM((blk, S5_LANES), F32),
        pltpu.VMEM((bsz, S5_LANES), F32), pltpu.VMEM((bsz, S5_LANES), F32),
        pltpu.VMEM(((halves * tc + POOL_HIST) * bsz, POOL_WIDTH), F32),
    ]
    grid_spec = pltpu.PrefetchScalarGridSpec(
        num_scalar_prefetch=len(prefetch), grid=(nb, steps), in_specs=in_specs, out_specs=out_specs,
        scratch_shapes=scratch)
    return pl.pallas_call(
        functools.partial(_mixer_kernel, bsz=bsz, tc=tc, past=past, nsteps=nsteps, halves=halves,
                          has_tail=ntail > 0, unsort=moe is not None),
        grid_spec=grid_spec, out_shape=out_shape, name="mixer",
        compiler_params=pltpu.CompilerParams(dimension_semantics=("arbitrary", "arbitrary"),
                                             vmem_limit_bytes=VMEM_LIMIT),
    )(*prefetch, *args)


ROUTER_ROWS = 32


def _dot_nt(a, b):
    return lax.dot_general(a, b, (((1,), (1,)), ((), ())), preferred_element_type=F32)


def _argmax_rows(rows):
    best, idx = rows[0], jnp.zeros_like(rows[0])
    for k in range(1, len(rows)):
        gt = rows[k] > best
        best = jnp.where(gt, rows[k], best)
        idx = jnp.where(gt, float(k), idx)
    return best, idx


def _gating_t(lt):
    row = lambda r: lt[r:r + 1, :]
    g_rows = [row(N_EXPERTS + g) for g in range(N_EXPERT_GROUPS)]
    gmax, gidx = _argmax_rows(g_rows)
    gsum = g_rows[0] * 0.0
    for r in g_rows:
        gsum = gsum + jnp.exp(r - gmax)
    g_w = 1.0 / gsum
    el = []
    for k in range(EXPERTS_PER_GROUP):
        v = row((N_EXPERT_GROUPS - 1) * EXPERTS_PER_GROUP + k)
        for g in range(N_EXPERT_GROUPS - 2, -1, -1):
            v = jnp.where(gidx == float(g), row(g * EXPERTS_PER_GROUP + k), v)
        el.append(v)
    m1, i1 = _argmax_rows(el)
    m2, i2 = _argmax_rows([jnp.where(i1 == float(k), -jnp.inf, el[k]) for k in range(EXPERTS_PER_GROUP)])
    e2 = jnp.exp(m2 - m1)
    den = 1.0 + e2
    first = gidx * float(EXPERTS_PER_GROUP)
    return gidx, first + i1, first + i2, g_w / den, g_w * e2 / den


def _route_sort_kernel(x1_ref, tri_ref, gffn_ref, wrh_ref, wrl_ref, br_ref, sorted_ref, dest_ref, gid_ref):
    _interleave([_route_sort_tile(k, x1_ref, tri_ref, gffn_ref, wrh_ref, wrl_ref, br_ref, sorted_ref, dest_ref,
                                  gid_ref) for k in range(ROUTE_TILES)])


def _route_sort_tile(k, x1_ref, tri_ref, gffn_ref, wrh_ref, wrl_ref, br_ref, sorted_ref, dest_ref, gid_ref):
    x = x1_ref[k * TILE:(k + 1) * TILE, :]
    h2 = _rms(x, gffn_ref[...])
    hi = h2.astype(BF16)
    lo = (h2 - hi.astype(F32)).astype(BF16)
    yield 1
    lt = (_dot_nt(wrh_ref[...], hi) + _dot_nt(wrl_ref[...], hi) + _dot_nt(wrh_ref[...], lo)) + br_ref[...]
    yield 2
    gidx, ex1, ex2, gate1, gate2 = _gating_t(lt)
    yield 3

    sub8 = lax.broadcasted_iota(jnp.int32, (SUBLANES, TILE), 0).astype(F32)
    onehot = jnp.where(sub8 == gidx, 1.0, 0.0)
    rank = jnp.dot(onehot.astype(BF16), tri_ref[...], preferred_element_type=F32)
    counts = jnp.sum(onehot, axis=1, keepdims=True)
    padded = jnp.floor((counts + float(GRAN - 1)) * (1.0 / GRAN)) * float(GRAN)
    e0 = padded[0:1, :]
    e1 = e0 + padded[1:2, :]
    e2 = e1 + padded[2:3, :]
    e3 = e2 + padded[3:4, :]
    dest = jnp.where(gidx == 0.0, rank[0:1, :],
                     jnp.where(gidx == 1.0, e0 + rank[1:2, :],
                               jnp.where(gidx == 2.0, e1 + rank[2:3, :], e2 + rank[3:4, :])))
    dest_ref[k * TILE:(k + 1) * TILE, :] = jnp.broadcast_to(dest, (LANES, TILE)).T[:, 0:1]

    slot = lax.broadcasted_iota(jnp.int32, (SORTED_ROWS, TILE), 0).astype(F32)
    perm = jnp.where(slot == dest, 1.0, 0.0).astype(BF16)
    sub = lax.broadcasted_iota(jnp.int32, (N_EXPERTS, TILE), 0).astype(F32)
    comb = jnp.where(sub == ex1, gate1, 0.0) + jnp.where(sub == ex2, gate2, 0.0)
    c_hi = comb.astype(BF16)
    c_lo = (comb - c_hi.astype(F32)).astype(BF16)
    zeros = jnp.zeros((LANES - N_EXPERTS, TILE), BF16)
    gates = jnp.concatenate([c_hi, zeros, c_lo, zeros], axis=0)
    yield 4
    rows = slice(k * SORTED_ROWS, (k + 1) * SORTED_ROWS)
    sorted_ref[rows, :D_MODEL] = jnp.dot(perm, hi, preferred_element_type=F32).astype(BF16)
    sorted_ref[rows, D_MODEL:] = _dot_nt(perm, gates).astype(BF16)

    q = lax.broadcasted_iota(jnp.int32, (1, LANES), 1).astype(F32) * float(GRAN)
    gid = (jnp.where(q >= e0, 1, 0) + jnp.where(q >= e1, 1, 0)
           + jnp.where(q >= e2, 1, 0) + jnp.where(q >= e3, 1, 0))
    gid_ref[k] = gid.astype(jnp.int32)


def _route_sort(x1, lw, l):
    ntiles = x1.shape[0] // TILE
    params = [lw[k] for k in ('norm_ffn', 'wr_hi', 'wr_lo', 'b_r')]
    tri = jnp.triu(jnp.ones((TILE, TILE), BF16), k=1)
    return pl.pallas_call(
        _route_sort_kernel, grid=(ntiles // ROUTE_TILES,),
        in_specs=[pl.BlockSpec((ROUTE_TILES * TILE, D_MODEL), lambda i: (i, 0)),
                  pl.BlockSpec((TILE, TILE), lambda i: (0, 0))] + [_layer_spec(p, l) for p in params],
        out_specs=[pl.BlockSpec((ROUTE_TILES * SORTED_ROWS, PAYLOAD), lambda i: (i, 0)),
                   pl.BlockSpec((ROUTE_TILES * TILE, 1), lambda i: (i, 0)),
                   pl.BlockSpec((ROUTE_TILES, 1, LANES), lambda i: (i, 0, 0))],
        out_shape=[jax.ShapeDtypeStruct((ntiles * SORTED_ROWS, PAYLOAD), BF16),
                   jax.ShapeDtypeStruct((x1.shape[0], 1), F32),
                   jax.ShapeDtypeStruct((ntiles, 1, LANES), jnp.int32)],
        name="moe_route_sort",
        compiler_params=pltpu.CompilerParams(dimension_semantics=("arbitrary",),
                                             vmem_limit_bytes=VMEM_LIMIT),
    )(x1, tri, *params)


def _experts_kernel(src_ref, wgid_ref, wvalid_ref, *refs):
    gran_refs = refs[:ITEM_GRANS]
    w1f_ref, w3f_ref, w2f_ref, out_ref, w1_ref, w3_ref, w2_ref = refs[ITEM_GRANS:]
    j = pl.program_id(0)

    @pl.when(jnp.logical_or(j == 0, wgid_ref[j] != wgid_ref[jnp.maximum(j - 1, 0)]))
    def _():
        w1_ref[...] = w1f_ref[...].astype(BF16)
        w3_ref[...] = w3f_ref[...].astype(BF16)
        w2_ref[...] = w2f_ref[...].astype(BF16)

    @pl.when(wvalid_ref[j] > 0)
    def _():
        full = jnp.concatenate([g[...] for g in gran_refs], axis=0)
        h = full[:, :D_MODEL]
        comb = (full[:, D_MODEL:D_MODEL + LANES].astype(F32)
                + full[:, D_MODEL + LANES:].astype(F32))
        first = wgid_ref[j] * EXPERTS_PER_GROUP
        lane = lax.broadcasted_iota(jnp.int32, (ITEM_ROWS, LANES), 1)
        hids = []
        for e in range(EXPERTS_PER_GROUP):
            a = jnp.dot(h, w1_ref[e], preferred_element_type=F32)
            b = jnp.dot(h, w3_ref[e], preferred_element_type=F32)
            gate = jnp.sum(jnp.where(lane == first + e, comb, 0.0), axis=-1, keepdims=True)
            hids.append(((a * jax.nn.sigmoid(a)) * b * gate).astype(BF16))
        hid = jnp.concatenate(hids, axis=-1)
        w2g = w2_ref[...].reshape(EXPERTS_PER_GROUP * D_EXPERT, D_MODEL)
        out_ref[...] = jnp.dot(hid, w2g, preferred_element_type=F32).astype(BF16)

    @pl.when(wvalid_ref[j] == 0)
    def _():
        out_ref[...] = jnp.zeros(out_ref.shape, out_ref.dtype)


def _experts(sorted_tok, src, wgid, wvalid, lw, l):
    nitems = wgid.shape[0]
    gran_spec = lambda s: pl.BlockSpec((GRAN, PAYLOAD), lambda j, src, wg, wv: (src[j * ITEM_GRANS + s], 0))
    wspec = lambda shape: pl.BlockSpec(shape, lambda j, src, wg, wv: (l * N_EXPERT_GROUPS + wg[j], 0, 0))
    grid_spec = pltpu.PrefetchScalarGridSpec(
        num_scalar_prefetch=3, grid=(nitems,),
        in_specs=[gran_spec(s) for s in range(ITEM_GRANS)] + [
            wspec((EXPERTS_PER_GROUP, D_MODEL, D_EXPERT)), wspec((EXPERTS_PER_GROUP, D_MODEL, D_EXPERT)),
            wspec((EXPERTS_PER_GROUP, D_EXPERT, D_MODEL))],
        out_specs=pl.BlockSpec((ITEM_ROWS, D_MODEL), lambda j, src, wg, wv: (j, 0)),
        scratch_shapes=[pltpu.VMEM((EXPERTS_PER_GROUP, D_MODEL, D_EXPERT), BF16),
                        pltpu.VMEM((EXPERTS_PER_GROUP, D_MODEL, D_EXPERT), BF16),
                        pltpu.VMEM((EXPERTS_PER_GROUP, D_EXPERT, D_MODEL), BF16)])
    return pl.pallas_call(
        _experts_kernel, grid_spec=grid_spec,
        out_shape=jax.ShapeDtypeStruct((nitems * ITEM_ROWS, D_MODEL), BF16),
        name="moe_experts",
        compiler_params=pltpu.CompilerParams(dimension_semantics=("arbitrary",),
                                             vmem_limit_bytes=VMEM_LIMIT),
    )(src, wgid, wvalid, *([sorted_tok] * ITEM_GRANS), lw['w1'], lw['w3'], lw['w2'])


def _unsort_kernel(pos_ref, x1_ref, dest_ref, *refs, final):
    gran_refs = refs[:TILE_GRANS]
    gfin_ref, out_ref = refs[TILE_GRANS:]
    y = x1_ref[...] + _unsorted(dest_ref[...], gran_refs)
    if not final:
        out_ref[...] = y
    else:
        y = _rms(y, gfin_ref[...])
        bsz, tc, _ = out_ref.shape
        for t in range(tc):
            out_ref[:, t, :] = y[t * bsz:(t + 1) * bsz, :]


def _unsort(x1, dest, expert_out, pos, norm_final, *, tile_off=0, ntiles=None, batch_major=None):
    final = batch_major is not None
    if final:
        batch, t_len, bsz, tc = batch_major
        nchunks = t_len // tc
        out_spec = pl.BlockSpec((bsz, tc, D_MODEL), lambda k, pos: (k // nchunks, k % nchunks, 0))
        out_shape = jax.ShapeDtypeStruct((batch, t_len, D_MODEL), F32)
    else:
        ntiles = x1.shape[0] // TILE
        out_spec = pl.BlockSpec((TILE, D_MODEL), lambda k, pos: (k, 0))
        out_shape = jax.ShapeDtypeStruct(x1.shape, F32)
    gran_spec = lambda q: pl.BlockSpec((GRAN, D_MODEL),
                                       lambda k, pos: (pos[(k + tile_off) * TILE_GRANS + q], 0))
    grid_spec = pltpu.PrefetchScalarGridSpec(
        num_scalar_prefetch=1, grid=(ntiles,),
        in_specs=[pl.BlockSpec((TILE, D_MODEL), lambda k, pos: (k + tile_off, 0)),
                  pl.BlockSpec((TILE, 1), lambda k, pos: (k + tile_off, 0))]
                 + [gran_spec(q) for q in range(TILE_GRANS)]
                 + [pl.BlockSpec((1, D_MODEL), lambda k, pos: (0, 0))],
        out_specs=out_spec)
    return pl.pallas_call(
        functools.partial(_unsort_kernel, final=final), grid_spec=grid_spec,
        out_shape=out_shape, name="moe_unsort",
        compiler_params=pltpu.CompilerParams(dimension_semantics=("arbitrary",),
                                             vmem_limit_bytes=VMEM_LIMIT),
    )(pos, x1, dest, *([expert_out] * TILE_GRANS), norm_final)


def _dispatch_tables(gid):
    ntiles = gid.shape[0]
    ngran = ntiles * TILE_GRANS
    nitems = -(-ngran // ITEM_GRANS) + N_EXPERT_GROUPS
    g = gid[:, 0, :TILE_GRANS].reshape(ngran)
    onehot = (g[:, None] == jnp.arange(N_EXPERT_GROUPS, dtype=jnp.int32)[None, :]).astype(jnp.int32)
    valid = g < N_EXPERT_GROUPS
    rank = jnp.cumsum(onehot, axis=0) - onehot
    cnt = jnp.sum(onehot, axis=0)
    pcnt = (cnt + (ITEM_GRANS - 1)) // ITEM_GRANS * ITEM_GRANS
    pend = jnp.cumsum(pcnt)
    pstart = pend - pcnt
    pos = jnp.sum(onehot * (pstart[None, :] + rank), axis=1)
    slots = jnp.arange(nitems * ITEM_GRANS, dtype=jnp.int32)
    hit = jnp.logical_and(pos[None, :] == slots[:, None], valid[None, :])
    src = jnp.sum(jnp.where(hit, jnp.arange(ngran, dtype=jnp.int32)[None, :], 0), axis=1)
    item_start = jnp.arange(nitems, dtype=jnp.int32) * ITEM_GRANS
    last_group = jnp.max(jnp.where(pcnt > 0, jnp.arange(N_EXPERT_GROUPS, dtype=jnp.int32), 0))
    wgid = jnp.minimum(jnp.sum((item_start[:, None] >= pend[None, :]).astype(jnp.int32), axis=1), last_group)
    wvalid = (item_start < pend[-1]).astype(jnp.int32)
    return src.astype(jnp.int32), wgid.astype(jnp.int32), wvalid, jnp.where(valid, pos, 0).astype(jnp.int32)


def _moe(x1, lw, l, norm_final, final_layouts=None):
    sorted_tok, dest, gid = _route_sort(x1, lw, l)
    src, wgid, wvalid, pos = _dispatch_tables(gid)
    expert_out = _experts(sorted_tok, src, wgid, wvalid, lw, l)
    if final_layouts is None:
        return dest, expert_out, pos
    return [_unsort(x1, dest, expert_out, pos, norm_final, tile_off=off, ntiles=n, batch_major=bm)
            for off, n, bm in final_layouts]


def _router_weights(w_grp, b_grp, w_er, b_er):
    pad = ROUTER_ROWS - N_EXPERTS - N_EXPERT_GROUPS
    w = jnp.pad(jnp.concatenate([w_er, w_grp], axis=-1), ((0, 0), (0, 0), (0, pad)))
    w = jnp.transpose(w, (0, 2, 1))
    b = jnp.pad(jnp.concatenate([b_er, b_grp], axis=-1), ((0, 0), (0, pad)))[:, :, None]
    hi = w.astype(BF16)
    lo = (w - hi.astype(F32)).astype(BF16)
    return hi, lo, b


def _to_time_major(x, nb):
    bt, t, w = x.shape
    return jnp.transpose(x.reshape(nb, bt // nb, t, w), (0, 2, 1, 3)).reshape(nb, t * (bt // nb), w)


def _from_time_major(x, t):
    nb, rows, w = x.shape
    bsz = rows // t
    return jnp.transpose(x.reshape(nb, t, bsz, w), (0, 2, 1, 3)).reshape(nb * bsz, t, w)


def kernel(x_prompt, x_sample, state_ssm_re, state_ssm_im, state_pool, norm_mix, w_in, lam_re, lam_im, log_dt, b_re, b_im, c_re, c_im, d_skip, w_glu, b_glu, w_pool, pool_scale, w_out, norm_ffn, w_grp, b_grp, w_erouter, b_erouter, w1, w3, w2, norm_final):
    depth = w_in.shape[0]
    pb, pt, _ = x_prompt.shape
    sb, st, _ = x_sample.shape
    p_tc = TILE // pb
    p_tiles = pt // p_tc
    s_bsz = TILE // st
    s_nb = sb // s_bsz
    p_rows = pb * pt

    are, aim, bbr, bbi = _discretise(lam_re, lam_im, log_dt, b_re, b_im)
    wx = jnp.concatenate([_blockdiag(bbr, True), _blockdiag(bbi, True)], axis=-1).astype(BF16)
    cm = jnp.concatenate([_blockdiag(c_re, False), _blockdiag(-c_im, False)], axis=-2).astype(BF16)

    wr_hi, wr_lo, b_r = _router_weights(w_grp, b_grp, w_erouter, b_erouter)
    stack_experts = lambda w: w.reshape((depth * N_EXPERTS,) + w.shape[2:])
    vec = lambda a: a[:, None, :]
    lw = dict(norm_mix=vec(norm_mix), w_in=w_in.astype(BF16), wx=wx, are=are, aim=aim, cm=cm,
              d_skip=vec(d_skip), w_glu=w_glu.astype(BF16), b_glu=vec(b_glu), w_pool=w_pool.astype(BF16),
              pool_scale=vec(pool_scale), w_out=w_out.astype(BF16), norm_ffn=vec(norm_ffn),
              wr_hi=wr_hi, wr_lo=wr_lo, b_r=b_r,
              w1=stack_experts(w1), w3=stack_experts(w3), w2=stack_experts(w2))
    nfin = norm_final[None]

    p_zero_h = jnp.zeros((1, pb, S5_LANES), F32)
    p_zero_hist = jnp.zeros((1, POOL_HIST * pb, POOL_WIDTH), F32)
    final_layouts = [(0, p_tiles, (pb, pt, pb, p_tc)), (p_tiles, s_nb, (sb, st, s_bsz, st))]

    outs = {k: [] for k in ('p_re', 'p_im', 'p_pool', 's_re', 's_im', 's_pool')}
    x, moe = None, None
    for l in range(depth):
        h0re = state_ssm_re[l].reshape(s_nb, s_bsz, S5_LANES)
        h0im = state_ssm_im[l].reshape(s_nb, s_bsz, S5_LANES)
        hist = jnp.pad(_to_time_major(state_pool[l], s_nb), ((0, 0), ((POOL_HIST - POOL_BUF) * s_bsz, 0), (0, 0)))
        xs1, hre, him, pool = _mixer(x_sample if l == 0 else x, h0re, h0im, hist, lw, l, bsz=s_bsz, tc=st,
                                     past=POOL_BUF, nb=s_nb, nchunks=1, halves=1, in_off=p_tiles, moe=moe)
        outs['s_re'].append(hre.reshape(sb, S5_GROUPS, S5_STATE))
        outs['s_im'].append(him.reshape(sb, S5_GROUPS, S5_STATE))
        outs['s_pool'].append(_from_time_major(pool, POOL_BUF))
        x1, hre, him, pool = _mixer(x_prompt if l == 0 else x, p_zero_h, p_zero_h, p_zero_hist, lw, l,
                                    bsz=pb, tc=p_tc, past=0, nb=1, nchunks=p_tiles, halves=2, in_off=0, tail=xs1, moe=moe)
        outs['p_re'].append(hre.reshape(pb, S5_GROUPS, S5_STATE))
        outs['p_im'].append(him.reshape(pb, S5_GROUPS, S5_STATE))
        outs['p_pool'].append(_from_time_major(pool, POOL_BUF))
        if l < depth - 1:
            x, moe = x1, _moe(x1, lw, l, nfin)
        else:
            y_prompt, y_sample = _moe(x1, lw, l, nfin, final_layouts)

    return (y_prompt, y_sample, jnp.stack(outs['p_re']), jnp.stack(outs['p_im']), jnp.stack(outs['p_pool']),
            jnp.stack(outs['s_re']), jnp.stack(outs['s_im']), jnp.stack(outs['s_pool']))
```

```python
import functools

import jax
import jax.numpy as jnp
from jax import lax
from jax.experimental import pallas as pl
from jax.experimental.pallas import tpu as pltpu

D_MODEL = 1024
S5_WIDTH = 512
S5_GROUP_CH = 16
S5_GROUPS = 32
S5_STATE = 64
S5_LANES = S5_GROUPS * S5_STATE
POOL_WIDTH = 512
POOL_WINDOWS = (2, 4, 8, 16)
POOL_GROUP_CH = 128
POOL_BUF = 15
POOL_HIST = 16
N_EXPERTS = 16
EXPERTS_PER_GROUP = 4
N_EXPERT_GROUPS = 4
D_EXPERT = 256
EPS = 1e-6

SUBLANES = 8
LANES = 128
S5_KBLOCK = 128
S5_NBLOCK = S5_KBLOCK // S5_GROUP_CH * S5_STATE
SCAN_LANES = 512
VMEM_LIMIT = 56 * 1024 * 1024

TILE = 512
GRAN = 32
TILE_GRANS = TILE // GRAN + N_EXPERT_GROUPS
SORTED_ROWS = TILE_GRANS * GRAN
ITEM_GRANS = 16
ITEM_ROWS = ITEM_GRANS * GRAN
ROUTE_TILES = 2
PAYLOAD = D_MODEL + 2 * LANES

F32 = jnp.float32
BF16 = jnp.bfloat16

MIXER_PARAMS = ('norm_mix', 'w_in', 'wx', 'are', 'aim', 'cm', 'd_skip', 'w_glu', 'b_glu', 'w_pool',
                'pool_scale', 'w_out')


def _layer_spec(arr, l):
    return pl.BlockSpec((None,) + arr.shape[1:], lambda *_: (l,) + (0,) * (arr.ndim - 1))


def _rms(x, g):
    return x * lax.rsqrt(jnp.mean(x * x, axis=-1, keepdims=True) + EPS) * g


def _gelu_tanh(x):
    return 0.5 * x * (1.0 + jnp.tanh(0.7978845608028654 * (x + 0.044715 * (x * x * x))))


def _disc_kernel(lr_ref, li_ref, ldt_ref, br_ref, bi_ref, are_ref, aim_ref, bbr_ref, bbi_ref):
    lr = lr_ref[...]
    li = li_ref[...]
    dt = jnp.exp(ldt_ref[...])
    mag = jnp.exp(lr * dt)
    ab_re = mag * jnp.cos(li * dt)
    ab_im = mag * jnp.sin(li * dt)
    den = lr * lr + li * li
    nr = ab_re - 1.0
    coef_re = (nr * lr + ab_im * li) / den
    coef_im = (ab_im * lr - nr * li) / den
    br = br_ref[...]
    bi = bi_ref[...]
    are_ref[...] = ab_re
    aim_ref[...] = ab_im
    bbr_ref[...] = coef_re * br - coef_im * bi
    bbi_ref[...] = coef_re * bi + coef_im * br


def _discretise(lam_re, lam_im, log_dt, b_re, b_im):
    depth = lam_re.shape[0]
    rows = depth * S5_GROUPS * S5_GROUP_CH
    rep = lambda a: jnp.repeat(a.reshape(depth * S5_GROUPS, -1), S5_GROUP_CH, axis=0)
    tr = lambda b: jnp.transpose(b, (0, 1, 3, 2)).reshape(rows, S5_STATE)
    out = jax.ShapeDtypeStruct((rows, S5_STATE), F32)
    are, aim, bbr, bbi = pl.pallas_call(
        _disc_kernel, out_shape=(out, out, out, out), name="s5_discretise",
    )(rep(lam_re), rep(lam_im), rep(log_dt[..., None]), tr(b_re), tr(b_im))
    shp = (depth, S5_GROUPS, S5_GROUP_CH, S5_STATE)
    are = are.reshape(shp)[:, :, 0, :].reshape(depth, 1, S5_LANES)
    aim = aim.reshape(shp)[:, :, 0, :].reshape(depth, 1, S5_LANES)
    return are, aim, bbr.reshape(shp), bbi.reshape(shp)


def _blockdiag(w, rows_first):
    depth = w.shape[0]
    gpb = S5_KBLOCK // S5_GROUP_CH
    w5 = w.reshape(depth, S5_GROUPS // gpb, gpb, S5_GROUP_CH, S5_STATE)
    eye = jnp.eye(gpb, dtype=w.dtype)
    if rows_first:
        return jnp.einsum('lkgcn,gh->lkgchn', w5, eye).reshape(depth, -1, S5_KBLOCK, S5_NBLOCK)
    return jnp.einsum('lkgcn,gh->lkgnhc', w5, eye).reshape(depth, -1, S5_NBLOCK, S5_KBLOCK)


def _unsorted(dest, gran_refs):
    s = jnp.concatenate([g[...] for g in gran_refs], axis=0)
    slot = lax.broadcasted_iota(jnp.int32, (TILE, SORTED_ROWS), 1).astype(F32)
    perm_t = jnp.where(slot == dest, 1.0, 0.0).astype(BF16)
    return jnp.dot(perm_t, s, preferred_element_type=F32)


def _interleave(gens):
    while gens:
        gens = [g for g in gens if next(g, None) is not None]


def _mixer_kernel(*refs, bsz, tc, past, nsteps, halves, has_state, has_tail, unsort):
    refs = refs[1:] if unsort else refs
    n_in = 1 + (3 if has_state else 0) + len(MIXER_PARAMS)
    inputs, rest = refs[:n_in], refs[n_in:]
    tail_ref = None
    if has_tail:
        tail_ref, rest = rest[0], rest[1:]
    moe = None
    if unsort:
        ngran = halves * TILE_GRANS
        moe, rest = (rest[0], rest[1:1 + ngran]), rest[1 + ngran:]
    state = inputs[1:4] if has_state else None
    body = functools.partial(_mixer_body, inputs[0], *inputs[n_in - len(MIXER_PARAMS):], *rest, state=state,
                             moe=moe, bsz=bsz, tc=tc, past=past, last=nsteps - 1, halves=halves)
    if not has_tail:
        body()
        return
    x1_ref = rest[0]
    i = pl.program_id(1)
    pl.when(i < nsteps)(body)

    @pl.when(i >= nsteps)
    def _():
        x1_ref[...] = tail_ref[...]


def _mixer_body(x_ref, gmix_ref, win_ref, wx_ref, are_ref, aim_ref,
                cm_ref, dskip_ref, wglu_ref, bglu_ref, wpool_ref, pscale_ref, wout_ref,
                x1_ref, hre_out, him_out, pool_out, xre, xim, sre, sim, hre, him, zbuf,
                *, state, moe, bsz, tc, past, last, halves):
    i = pl.program_id(1)
    rows = bsz * tc
    hist_rows = POOL_HIST * bsz
    glanes = lambda g: slice(g * S5_STATE, (g + 1) * S5_STATE)

    @pl.when(i == 0)
    def _():
        zbuf[0:hist_rows, :] = jnp.zeros((hist_rows, POOL_WIDTH), F32)
        if state is None:
            hre[...] = jnp.zeros(hre.shape, F32)
            him[...] = jnp.zeros(him.shape, F32)
        else:
            h0re_ref, h0im_ref, hist_ref = state
            for g in range(S5_GROUPS):
                hre[:, glanes(g)] = h0re_ref[:, g, :]
                him[:, glanes(g)] = h0im_ref[:, g, :]
            for s in range(POOL_BUF):
                zbuf[(POOL_HIST - POOL_BUF + s) * bsz:(POOL_HIST - POOL_BUF + s + 1) * bsz, :] = hist_ref[:, s, :]

    def chunk(hf):
        r0 = hf * rows
        rsl = slice(r0, r0 + rows)
        if len(x_ref.shape) == 3:
            x = jnp.concatenate([x_ref[:, hf * tc + t, :] for t in range(tc)], axis=0)
        else:
            x = x_ref[rsl, :]
        if moe is not None:
            dest_ref, gran_refs = moe
            x = x + _unsorted(dest_ref[rsl, :], gran_refs[hf * TILE_GRANS:(hf + 1) * TILE_GRANS])
        h = _rms(x, gmix_ref[...]).astype(BF16)
        yield 1
        proj = jnp.dot(h, win_ref[...], preferred_element_type=F32)
        u = proj[:, :S5_WIDTH]
        z = proj[:, S5_WIDTH:]
        zbuf[hist_rows + r0:hist_rows + r0 + rows, :] = z
        ub = u.astype(BF16)
        yield 2

        step_idx = (i * halves + hf) * tc + lax.broadcasted_iota(jnp.int32, (rows, 1), 0) // bsz
        pos = (step_idx + (past + 1)).astype(F32)
        pooled = []
        for gi, w in enumerate(POOL_WINDOWS):
            gs = slice(gi * POOL_GROUP_CH, (gi + 1) * POOL_GROUP_CH)
            s = zbuf[r0:r0 + hist_rows + rows, gs]
            k = 1
            while k < w:
                s = s[k * bsz:] + s[:-k * bsz]
                k *= 2
            inv_cnt = 1.0 / jnp.minimum(pos, float(w))
            pooled.append((s[-rows:] * inv_cnt - z[:, gs]).astype(BF16))
        yield 3
        pouts = [jnp.dot(pooled[gi], wpool_ref[gi], preferred_element_type=F32)
                 for gi in range(len(POOL_WINDOWS))]
        pool_mixed = (jnp.concatenate(pouts, axis=-1) * pscale_ref[...]).astype(BF16)
        x1_ref[rsl, :] = x + jnp.dot(pool_mixed, wout_ref[S5_WIDTH:, :], preferred_element_type=F32)
        yield 4

        ys = []
        for kb in range(S5_WIDTH // S5_KBLOCK):
            ns = slice(kb * S5_NBLOCK, (kb + 1) * S5_NBLOCK)
            xx = jnp.dot(ub[:, kb * S5_KBLOCK:(kb + 1) * S5_KBLOCK], wx_ref[kb],
                         preferred_element_type=F32)
            xre[rsl, ns] = xx[:, :S5_NBLOCK]
            xim[rsl, ns] = xx[:, S5_NBLOCK:]
            yield 5
            for b0 in range(0, bsz, SUBLANES):
                scan_rows(r0, b0, ns)
            yield 6
            ys.append(jnp.dot(sre[rsl, ns].astype(BF16), cm_ref[kb, :S5_NBLOCK, :], preferred_element_type=F32)
                      + jnp.dot(sim[rsl, ns].astype(BF16), cm_ref[kb, S5_NBLOCK:, :],
                                preferred_element_type=F32))
        y = jnp.concatenate(ys, axis=-1) + dskip_ref[...] * u
        g = _gelu_tanh(y)
        gb = g.astype(BF16)
        yield 7
        s5_out = g * jax.nn.sigmoid(jnp.dot(gb, wglu_ref[...], preferred_element_type=F32) + bglu_ref[...])
        sb = s5_out.astype(BF16)
        yield 8
        x1_ref[rsl, :] += jnp.dot(sb, wout_ref[:S5_WIDTH, :], preferred_element_type=F32)

    assert SCAN_LANES == S5_NBLOCK

    def scan_rows(r0, b0, ls):
        ar = jnp.broadcast_to(are_ref[:, ls], (SUBLANES, SCAN_LANES))
        ai = jnp.broadcast_to(aim_ref[:, ls], (SUBLANES, SCAN_LANES))
        hr, hi = hre[b0:b0 + SUBLANES, ls], him[b0:b0 + SUBLANES, ls]
        for t in range(tc):
            row = slice(r0 + t * bsz + b0, r0 + t * bsz + b0 + SUBLANES)
            hr, hi = ar * hr + (xre[row, ls] - ai * hi), ar * hi + (xim[row, ls] + ai * hr)
            sre[row, ls] = hr
            sim[row, ls] = hi
        hre[b0:b0 + SUBLANES, ls] = hr
        him[b0:b0 + SUBLANES, ls] = hi

    _interleave([chunk(hf) for hf in range(halves)])

    all_rows = halves * rows

    @pl.when(i == last)
    def _():
        for g in range(S5_GROUPS):
            hre_out[:, g, :] = hre[:, glanes(g)]
            him_out[:, g, :] = him[:, glanes(g)]
        for s in range(POOL_BUF):
            r = all_rows + hist_rows + (s - POOL_BUF) * bsz
            pool_out[:, s, :] = zbuf[r:r + bsz, :]

    @pl.when(i != last)
    def _():
        zbuf[0:hist_rows, :] = zbuf[all_rows:all_rows + hist_rows, :]


def _mixer(x, state, lw, l, *, bsz, tc, past, nb, nchunks, halves, in_off, tail=None, moe=None):
    rows = bsz * tc
    assert rows == TILE and nchunks % halves == 0
    blk = halves * rows
    nsteps = nchunks // halves
    ntail = 0 if tail is None else tail.shape[0] // blk
    assert ntail == 0 or (nb == 1 and tail.shape[0] % blk == 0)
    steps = nsteps + ntail
    step = lambda i: jnp.minimum(i, nsteps - 1)
    perb = lambda shape: pl.BlockSpec(shape, lambda b, i, *_: (b,) + (0,) * (len(shape) - 1))
    in_blk = lambda b, i: in_off + b * nsteps + step(i)
    if x.ndim == 2:
        x_spec = pl.BlockSpec((blk, D_MODEL), lambda b, i, *_: (in_blk(b, i), 0))
    else:
        x_spec = pl.BlockSpec((bsz, halves * tc, D_MODEL), lambda b, i, *_: (b, step(i), 0))
    in_specs, args = [x_spec], [x]
    if state is not None:
        for a in state:
            in_specs.append(pl.BlockSpec((None, bsz) + a.shape[2:], lambda b, i, *_: (l, b, 0, 0)))
            args.append(a)
    in_specs += [_layer_spec(lw[k], l) for k in MIXER_PARAMS]
    args += [lw[k] for k in MIXER_PARAMS]
    if ntail:
        in_specs.append(pl.BlockSpec((blk, D_MODEL), lambda b, i, *_: (jnp.maximum(i - nsteps, 0), 0)))
        args.append(tail)
    prefetch = []
    if moe is not None:
        dest, expert_out, pos = moe
        prefetch = [pos]
        in_specs.append(pl.BlockSpec((blk, 1), lambda b, i, pos: (in_blk(b, i), 0)))
        args.append(dest)
        for hf in range(halves):
            for q in range(TILE_GRANS):
                in_specs.append(pl.BlockSpec(
                    (GRAN, D_MODEL),
                    lambda b, i, pos, hf=hf, q=q: (pos[(in_blk(b, i) * halves + hf) * TILE_GRANS + q], 0)))
                args.append(expert_out)
    out_specs = [
        pl.BlockSpec((blk, D_MODEL), lambda b, i, *_: (b * steps + i, 0)),
        perb((bsz, S5_GROUPS, S5_STATE)), perb((bsz, S5_GROUPS, S5_STATE)), perb((bsz, POOL_BUF, POOL_WIDTH)),
    ]
    out_shape = [
        jax.ShapeDtypeStruct((nb * steps * blk, D_MODEL), F32),
        jax.ShapeDtypeStruct((nb * bsz, S5_GROUPS, S5_STATE), F32),
        jax.ShapeDtypeStruct((nb * bsz, S5_GROUPS, S5_STATE), F32),
        jax.ShapeDtypeStruct((nb * bsz, POOL_BUF, POOL_WIDTH), F32),
    ]
    scratch = [
        pltpu.VMEM((blk, S5_LANES), F32), pltpu.VMEM((blk, S5_LANES), F32),
        pltpu.VMEM((blk, S5_LANES), F32), pltpu.VMEM((blk, S5_LANES), F32),
        pltpu.VMEM((bsz, S5_LANES), F32), pltpu.VMEM((bsz, S5_LANES), F32),
        pltpu.VMEM(((halves * tc + POOL_HIST) * bsz, POOL_WIDTH), F32),
    ]
    grid_spec = pltpu.PrefetchScalarGridSpec(
        num_scalar_prefetch=len(prefetch), grid=(nb, steps), in_specs=in_specs, out_specs=out_specs,
        scratch_shapes=scratch)
    return pl.pallas_call(
        functools.partial(_mixer_kernel, bsz=bsz, tc=tc, past=past, nsteps=nsteps, halves=halves,
                          has_state=state is not None, has_tail=ntail > 0, unsort=moe is not None),
        grid_spec=grid_spec, out_shape=out_shape, name="mixer",
        compiler_params=pltpu.CompilerParams(dimension_semantics=("arbitrary", "arbitrary"),
                                             vmem_limit_bytes=VMEM_LIMIT),
    )(*prefetch, *args)


ROUTER_ROWS = 32


def _dot_nt(a, b):
    return lax.dot_general(a, b, (((1,), (1,)), ((), ())), preferred_element_type=F32)


def _argmax_rows(rows):
    best, idx = rows[0], jnp.zeros_like(rows[0])
    for k in range(1, len(rows)):
        gt = rows[k] > best
        best = jnp.where(gt, rows[k], best)
        idx = jnp.where(gt, float(k), idx)
    return best, idx


def _gating_t(lt):
    row = lambda r: lt[r:r + 1, :]
    g_rows = [row(N_EXPERTS + g) for g in range(N_EXPERT_GROUPS)]
    gmax, gidx = _argmax_rows(g_rows)
    gsum = g_rows[0] * 0.0
    for r in g_rows:
        gsum = gsum + jnp.exp(r - gmax)
    g_w = 1.0 / gsum
    el = []
    for k in range(EXPERTS_PER_GROUP):
        v = row((N_EXPERT_GROUPS - 1) * EXPERTS_PER_GROUP + k)
        for g in range(N_EXPERT_GROUPS - 2, -1, -1):
            v = jnp.where(gidx == float(g), row(g * EXPERTS_PER_GROUP + k), v)
        el.append(v)
    m1, i1 = _argmax_rows(el)
    m2, i2 = _argmax_rows([jnp.where(i1 == float(k), -jnp.inf, el[k]) for k in range(EXPERTS_PER_GROUP)])
    e2 = jnp.exp(m2 - m1)
    den = 1.0 + e2
    first = gidx * float(EXPERTS_PER_GROUP)
    return gidx, first + i1, first + i2, g_w / den, g_w * e2 / den


def _route_sort_kernel(x1_ref, tri_ref, gffn_ref, wrh_ref, wrl_ref, br_ref, sorted_ref, dest_ref, gid_ref):
    _interleave([_route_sort_tile(k, x1_ref, tri_ref, gffn_ref, wrh_ref, wrl_ref, br_ref, sorted_ref, dest_ref,
                                  gid_ref) for k in range(ROUTE_TILES)])


def _route_sort_tile(k, x1_ref, tri_ref, gffn_ref, wrh_ref, wrl_ref, br_ref, sorted_ref, dest_ref, gid_ref):
    x = x1_ref[k * TILE:(k + 1) * TILE, :]
    h2 = _rms(x, gffn_ref[...])
    hi = h2.astype(BF16)
    lo = (h2 - hi.astype(F32)).astype(BF16)
    yield 1
    lt = (_dot_nt(wrh_ref[...], hi) + _dot_nt(wrl_ref[...], hi) + _dot_nt(wrh_ref[...], lo)) + br_ref[...]
    yield 2
    gidx, ex1, ex2, gate1, gate2 = _gating_t(lt)
    yield 3

    sub8 = lax.broadcasted_iota(jnp.int32, (SUBLANES, TILE), 0).astype(F32)
    onehot = jnp.where(sub8 == gidx, 1.0, 0.0)
    rank = jnp.dot(onehot.astype(BF16), tri_ref[...], preferred_element_type=F32)
    counts = jnp.sum(onehot, axis=1, keepdims=True)
    padded = jnp.floor((counts + float(GRAN - 1)) * (1.0 / GRAN)) * float(GRAN)
    e0 = padded[0:1, :]
    e1 = e0 + padded[1:2, :]
    e2 = e1 + padded[2:3, :]
    e3 = e2 + padded[3:4, :]
    dest = jnp.where(gidx == 0.0, rank[0:1, :],
                     jnp.where(gidx == 1.0, e0 + rank[1:2, :],
                               jnp.where(gidx == 2.0, e1 + rank[2:3, :], e2 + rank[3:4, :])))
    dest_ref[k * TILE:(k + 1) * TILE, :] = jnp.broadcast_to(dest, (LANES, TILE)).T[:, 0:1]

    slot = lax.broadcasted_iota(jnp.int32, (SORTED_ROWS, TILE), 0).astype(F32)
    perm = jnp.where(slot == dest, 1.0, 0.0).astype(BF16)
    sub = lax.broadcasted_iota(jnp.int32, (N_EXPERTS, TILE), 0).astype(F32)
    comb = jnp.where(sub == ex1, gate1, 0.0) + jnp.where(sub == ex2, gate2, 0.0)
    c_hi = comb.astype(BF16)
    c_lo = (comb - c_hi.astype(F32)).astype(BF16)
    zeros = jnp.zeros((LANES - N_EXPERTS, TILE), BF16)
    gates = jnp.concatenate([c_hi, zeros, c_lo, zeros], axis=0)
    yield 4
    rows = slice(k * SORTED_ROWS, (k + 1) * SORTED_ROWS)
    sorted_ref[rows, :D_MODEL] = jnp.dot(perm, hi, preferred_element_type=F32).astype(BF16)
    sorted_ref[rows, D_MODEL:] = _dot_nt(perm, gates).astype(BF16)

    q = lax.broadcasted_iota(jnp.int32, (1, LANES), 1).astype(F32) * float(GRAN)
    gid = (jnp.where(q >= e0, 1, 0) + jnp.where(q >= e1, 1, 0)
           + jnp.where(q >= e2, 1, 0) + jnp.where(q >= e3, 1, 0))
    gid_ref[k] = gid.astype(jnp.int32)


def _route_sort(x1, lw, l):
    ntiles = x1.shape[0] // TILE
    params = [lw[k] for k in ('norm_ffn', 'wr_hi', 'wr_lo', 'b_r')]
    tri = jnp.triu(jnp.ones((TILE, TILE), BF16), k=1)
    return pl.pallas_call(
        _route_sort_kernel, grid=(ntiles // ROUTE_TILES,),
        in_specs=[pl.BlockSpec((ROUTE_TILES * TILE, D_MODEL), lambda i: (i, 0)),
                  pl.BlockSpec((TILE, TILE), lambda i: (0, 0))] + [_layer_spec(p, l) for p in params],
        out_specs=[pl.BlockSpec((ROUTE_TILES * SORTED_ROWS, PAYLOAD), lambda i: (i, 0)),
                   pl.BlockSpec((ROUTE_TILES * TILE, 1), lambda i: (i, 0)),
                   pl.BlockSpec((ROUTE_TILES, 1, LANES), lambda i: (i, 0, 0))],
        out_shape=[jax.ShapeDtypeStruct((ntiles * SORTED_ROWS, PAYLOAD), BF16),
                   jax.ShapeDtypeStruct((x1.shape[0], 1), F32),
                   jax.ShapeDtypeStruct((ntiles, 1, LANES), jnp.int32)],
        name="moe_route_sort",
        compiler_params=pltpu.CompilerParams(dimension_semantics=("arbitrary",),
                                             vmem_limit_bytes=VMEM_LIMIT),
    )(x1, tri, *params)


def _experts_kernel(src_ref, wgid_ref, wvalid_ref, *refs):
    gran_refs = refs[:ITEM_GRANS]
    w1f_ref, w3f_ref, w2f_ref, out_ref, w1_ref, w3_ref, w2_ref = refs[ITEM_GRANS:]
    j = pl.program_id(0)

    @pl.when(jnp.logical_or(j == 0, wgid_ref[j] != wgid_ref[jnp.maximum(j - 1, 0)]))
    def _():
        w1_ref[...] = w1f_ref[...].astype(BF16)
        w3_ref[...] = w3f_ref[...].astype(BF16)
        w2_ref[...] = w2f_ref[...].astype(BF16)

    @pl.when(wvalid_ref[j] > 0)
    def _():
        full = jnp.concatenate([g[...] for g in gran_refs], axis=0)
        h = full[:, :D_MODEL]
        comb = (full[:, D_MODEL:D_MODEL + LANES].astype(F32)
                + full[:, D_MODEL + LANES:].astype(F32))
        first = wgid_ref[j] * EXPERTS_PER_GROUP
        lane = lax.broadcasted_iota(jnp.int32, (ITEM_ROWS, LANES), 1)
        hids = []
        for e in range(EXPERTS_PER_GROUP):
            a = jnp.dot(h, w1_ref[e], preferred_element_type=F32)
            b = jnp.dot(h, w3_ref[e], preferred_element_type=F32)
            gate = jnp.sum(jnp.where(lane == first + e, comb, 0.0), axis=-1, keepdims=True)
            hids.append(((a * jax.nn.sigmoid(a)) * b * gate).astype(BF16))
        hid = jnp.concatenate(hids, axis=-1)
        w2g = w2_ref[...].reshape(EXPERTS_PER_GROUP * D_EXPERT, D_MODEL)
        out_ref[...] = jnp.dot(hid, w2g, preferred_element_type=F32).astype(BF16)

    @pl.when(wvalid_ref[j] == 0)
    def _():
        out_ref[...] = jnp.zeros(out_ref.shape, out_ref.dtype)


def _experts(sorted_tok, src, wgid, wvalid, lw, l):
    nitems = wgid.shape[0]
    gran_spec = lambda s: pl.BlockSpec((GRAN, PAYLOAD), lambda j, src, wg, wv: (src[j * ITEM_GRANS + s], 0))
    wspec = lambda shape: pl.BlockSpec(shape, lambda j, src, wg, wv: (l * N_EXPERT_GROUPS + wg[j], 0, 0))
    grid_spec = pltpu.PrefetchScalarGridSpec(
        num_scalar_prefetch=3, grid=(nitems,),
        in_specs=[gran_spec(s) for s in range(ITEM_GRANS)] + [
            wspec((EXPERTS_PER_GROUP, D_MODEL, D_EXPERT)), wspec((EXPERTS_PER_GROUP, D_MODEL, D_EXPERT)),
            wspec((EXPERTS_PER_GROUP, D_EXPERT, D_MODEL))],
        out_specs=pl.BlockSpec((ITEM_ROWS, D_MODEL), lambda j, src, wg, wv: (j, 0)),
        scratch_shapes=[pltpu.VMEM((EXPERTS_PER_GROUP, D_MODEL, D_EXPERT), BF16),
                        pltpu.VMEM((EXPERTS_PER_GROUP, D_MODEL, D_EXPERT), BF16),
                        pltpu.VMEM((EXPERTS_PER_GROUP, D_EXPERT, D_MODEL), BF16)])
    return pl.pallas_call(
        _experts_kernel, grid_spec=grid_spec,
        out_shape=jax.ShapeDtypeStruct((nitems * ITEM_ROWS, D_MODEL), BF16),
        name="moe_experts",
        compiler_params=pltpu.CompilerParams(dimension_semantics=("arbitrary",),
                                             vmem_limit_bytes=VMEM_LIMIT),
    )(src, wgid, wvalid, *([sorted_tok] * ITEM_GRANS), lw['w1'], lw['w3'], lw['w2'])


def _unsort_kernel(pos_ref, x1_ref, dest_ref, *refs, final):
    gran_refs = refs[:TILE_GRANS]
    gfin_ref, out_ref = refs[TILE_GRANS:]
    y = x1_ref[...] + _unsorted(dest_ref[...], gran_refs)
    if not final:
        out_ref[...] = y
    else:
        y = _rms(y, gfin_ref[...])
        bsz, tc, _ = out_ref.shape
        for t in range(tc):
            out_ref[:, t, :] = y[t * bsz:(t + 1) * bsz, :]


def _unsort(x1, dest, expert_out, pos, norm_final, *, tile_off=0, ntiles=None, batch_major=None):
    final = batch_major is not None
    if final:
        batch, t_len, bsz, tc = batch_major
        nchunks = t_len // tc
        out_spec = pl.BlockSpec((bsz, tc, D_MODEL), lambda k, pos: (k // nchunks, k % nchunks, 0))
        out_shape = jax.ShapeDtypeStruct((batch, t_len, D_MODEL), F32)
    else:
        ntiles = x1.shape[0] // TILE
        out_spec = pl.BlockSpec((TILE, D_MODEL), lambda k, pos: (k, 0))
        out_shape = jax.ShapeDtypeStruct(x1.shape, F32)
    gran_spec = lambda q: pl.BlockSpec((GRAN, D_MODEL),
                                       lambda k, pos: (pos[(k + tile_off) * TILE_GRANS + q], 0))
    grid_spec = pltpu.PrefetchScalarGridSpec(
        num_scalar_prefetch=1, grid=(ntiles,),
        in_specs=[pl.BlockSpec((TILE, D_MODEL), lambda k, pos: (k + tile_off, 0)),
                  pl.BlockSpec((TILE, 1), lambda k, pos: (k + tile_off, 0))]
                 + [gran_spec(q) for q in range(TILE_GRANS)]
                 + [pl.BlockSpec((1, D_MODEL), lambda k, pos: (0, 0))],
        out_specs=out_spec)
    return pl.pallas_call(
        functools.partial(_unsort_kernel, final=final), grid_spec=grid_spec,
        out_shape=out_shape, name="moe_unsort",
        compiler_params=pltpu.CompilerParams(dimension_semantics=("arbitrary",),
                                             vmem_limit_bytes=VMEM_LIMIT),
    )(pos, x1, dest, *([expert_out] * TILE_GRANS), norm_final)


def _dispatch_tables(gid):
    ntiles = gid.shape[0]
    ngran = ntiles * TILE_GRANS
    nitems = -(-ngran // ITEM_GRANS) + N_EXPERT_GROUPS
    g = gid[:, 0, :TILE_GRANS].reshape(ngran)
    groups = jnp.arange(N_EXPERT_GROUPS, dtype=jnp.int32)
    onehot = (g[:, None] == groups[None, :]).astype(F32)
    valid = g < N_EXPERT_GROUPS
    idx = jnp.arange(ngran, dtype=jnp.int32)
    earlier = (idx[None, :] < idx[:, None]).astype(F32)
    rank = jnp.dot(earlier, onehot, precision=lax.Precision.HIGHEST).astype(jnp.int32)
    cnt = jnp.sum(onehot, axis=0).astype(jnp.int32)
    pcnt = (cnt + (ITEM_GRANS - 1)) // ITEM_GRANS * ITEM_GRANS
    pstart = jnp.sum(jnp.where(groups[:, None] < groups[None, :], pcnt[:, None], 0), axis=0)
    pend = pstart + pcnt
    pos = jnp.sum(onehot.astype(jnp.int32) * (pstart[None, :] + rank), axis=1)
    slots = jnp.arange(nitems * ITEM_GRANS, dtype=jnp.int32)
    hit = jnp.logical_and(pos[None, :] == slots[:, None], valid[None, :])
    src = jnp.sum(jnp.where(hit, jnp.arange(ngran, dtype=jnp.int32)[None, :], 0), axis=1)
    item_start = jnp.arange(nitems, dtype=jnp.int32) * ITEM_GRANS
    last_group = jnp.max(jnp.where(pcnt > 0, jnp.arange(N_EXPERT_GROUPS, dtype=jnp.int32), 0))
    wgid = jnp.minimum(jnp.sum((item_start[:, None] >= pend[None, :]).astype(jnp.int32), axis=1), last_group)
    wvalid = (item_start < pend[-1]).astype(jnp.int32)
    return src.astype(jnp.int32), wgid.astype(jnp.int32), wvalid, jnp.where(valid, pos, 0).astype(jnp.int32)


def _moe(x1, lw, l, norm_final, final_layouts=None):
    sorted_tok, dest, gid = _route_sort(x1, lw, l)
    src, wgid, wvalid, pos = _dispatch_tables(gid)
    expert_out = _experts(sorted_tok, src, wgid, wvalid, lw, l)
    if final_layouts is None:
        return dest, expert_out, pos
    return [_unsort(x1, dest, expert_out, pos, norm_final, tile_off=off, ntiles=n, batch_major=bm)
            for off, n, bm in final_layouts]


def _router_weights(w_grp, b_grp, w_er, b_er):
    pad = ROUTER_ROWS - N_EXPERTS - N_EXPERT_GROUPS
    w = jnp.pad(jnp.concatenate([w_er, w_grp], axis=-1), ((0, 0), (0, 0), (0, pad)))
    w = jnp.transpose(w, (0, 2, 1))
    b = jnp.pad(jnp.concatenate([b_er, b_grp], axis=-1), ((0, 0), (0, pad)))[:, :, None]
    hi = w.astype(BF16)
    lo = (w - hi.astype(F32)).astype(BF16)
    return hi, lo, b


def kernel(x_prompt, x_sample, state_ssm_re, state_ssm_im, state_pool, norm_mix, w_in, lam_re, lam_im, log_dt, b_re, b_im, c_re, c_im, d_skip, w_glu, b_glu, w_pool, pool_scale, w_out, norm_ffn, w_grp, b_grp, w_erouter, b_erouter, w1, w3, w2, norm_final):
    depth = w_in.shape[0]
    pb, pt, _ = x_prompt.shape
    sb, st, _ = x_sample.shape
    p_tc = TILE // pb
    p_tiles = pt // p_tc
    s_bsz = TILE // st
    s_nb = sb // s_bsz
    p_rows = pb * pt

    are, aim, bbr, bbi = _discretise(lam_re, lam_im, log_dt, b_re, b_im)
    wx = jnp.concatenate([_blockdiag(bbr, True), _blockdiag(bbi, True)], axis=-1).astype(BF16)
    cm = jnp.concatenate([_blockdiag(c_re, False), _blockdiag(-c_im, False)], axis=-2).astype(BF16)

    wr_hi, wr_lo, b_r = _router_weights(w_grp, b_grp, w_erouter, b_erouter)
    stack_experts = lambda w: w.reshape((depth * N_EXPERTS,) + w.shape[2:])
    vec = lambda a: a[:, None, :]
    lw = dict(norm_mix=vec(norm_mix), w_in=w_in.astype(BF16), wx=wx, are=are, aim=aim, cm=cm,
              d_skip=vec(d_skip), w_glu=w_glu.astype(BF16), b_glu=vec(b_glu), w_pool=w_pool.astype(BF16),
              pool_scale=vec(pool_scale), w_out=w_out.astype(BF16), norm_ffn=vec(norm_ffn),
              wr_hi=wr_hi, wr_lo=wr_lo, b_r=b_r,
              w1=stack_experts(w1), w3=stack_experts(w3), w2=stack_experts(w2))
    nfin = norm_final[None]

    final_layouts = [(0, p_tiles, (pb, pt, pb, p_tc)), (p_tiles, s_nb, (sb, st, s_bsz, st))]
    s_state = (state_ssm_re, state_ssm_im, state_pool)

    outs = {k: [] for k in ('p_re', 'p_im', 'p_pool', 's_re', 's_im', 's_pool')}
    x, moe = None, None
    for l in range(depth):
        xs1, hre, him, pool = _mixer(x_sample if l == 0 else x, s_state, lw, l, bsz=s_bsz, tc=st,
                                     past=POOL_BUF, nb=s_nb, nchunks=1, halves=1, in_off=p_tiles, moe=moe)
        outs['s_re'].append(hre)
        outs['s_im'].append(him)
        outs['s_pool'].append(pool)
        x1, hre, him, pool = _mixer(x_prompt if l == 0 else x, None, lw, l,
                                    bsz=pb, tc=p_tc, past=0, nb=1, nchunks=p_tiles, halves=2, in_off=0, tail=xs1, moe=moe)
        outs['p_re'].append(hre)
        outs['p_im'].append(him)
        outs['p_pool'].append(pool)
        if l < depth - 1:
            x, moe = x1, _moe(x1, lw, l, nfin)
        else:
            y_prompt, y_sample = _moe(x1, lw, l, nfin, final_layouts)

    return (y_prompt, y_sample, jnp.stack(outs['p_re']), jnp.stack(outs['p_im']), jnp.stack(outs['p_pool']),
            jnp.stack(outs['s_re']), jnp.stack(outs['s_im']), jnp.stack(outs['s_pool']))
```

```python
import functools

import jax
import jax.numpy as jnp
from jax import lax
from jax.experimental import pallas as pl
from jax.experimental.pallas import tpu as pltpu

D_MODEL = 1024
S5_WIDTH = 512
S5_GROUP_CH = 16
S5_GROUPS = 32
S5_STATE = 64
S5_LANES = S5_GROUPS * S5_STATE
POOL_WIDTH = 512
POOL_WINDOWS = (2, 4, 8, 16)
POOL_GROUP_CH = 128
POOL_BUF = 15
POOL_HIST = 16
N_EXPERTS = 16
EXPERTS_PER_GROUP = 4
N_EXPERT_GROUPS = 4
D_EXPERT = 256
EPS = 1e-6

SUBLANES = 8
LANES = 128
S5_KBLOCK = 128
S5_NBLOCK = S5_KBLOCK // S5_GROUP_CH * S5_STATE
SCAN_LANES = 512
VMEM_LIMIT = 56 * 1024 * 1024

TILE = 512
GRAN = 32
TILE_GRANS = TILE // GRAN + N_EXPERT_GROUPS
SORTED_ROWS = TILE_GRANS * GRAN
ITEM_GRANS = 16
ITEM_ROWS = ITEM_GRANS * GRAN
ROUTE_TILES = 2
PAYLOAD = D_MODEL + 2 * LANES

F32 = jnp.float32
BF16 = jnp.bfloat16

MIXER_PARAMS = ('norm_mix', 'w_in', 'wx', 'are', 'aim', 'cm', 'd_skip', 'w_glu', 'b_glu', 'w_pool',
                'pool_scale', 'w_out')


def _layer_spec(arr, l):
    return pl.BlockSpec((None,) + arr.shape[1:], lambda *_: (l,) + (0,) * (arr.ndim - 1))


def _rms(x, g):
    return x * lax.rsqrt(jnp.mean(x * x, axis=-1, keepdims=True) + EPS) * g


def _gelu_tanh(x):
    return 0.5 * x * (1.0 + jnp.tanh(0.7978845608028654 * (x + 0.044715 * (x * x * x))))


def _disc_kernel(lr_ref, li_ref, ldt_ref, br_ref, bi_ref, are_ref, aim_ref, bbr_ref, bbi_ref):
    lr = lr_ref[...]
    li = li_ref[...]
    dt = jnp.exp(ldt_ref[...])
    mag = jnp.exp(lr * dt)
    ab_re = mag * jnp.cos(li * dt)
    ab_im = mag * jnp.sin(li * dt)
    den = lr * lr + li * li
    nr = ab_re - 1.0
    coef_re = (nr * lr + ab_im * li) / den
    coef_im = (ab_im * lr - nr * li) / den
    br = br_ref[...]
    bi = bi_ref[...]
    are_ref[...] = ab_re
    aim_ref[...] = ab_im
    bbr_ref[...] = coef_re * br - coef_im * bi
    bbi_ref[...] = coef_re * bi + coef_im * br


def _discretise(lam_re, lam_im, log_dt, b_re, b_im):
    depth = lam_re.shape[0]
    rows = depth * S5_GROUPS * S5_GROUP_CH
    rep = lambda a: jnp.repeat(a.reshape(depth * S5_GROUPS, -1), S5_GROUP_CH, axis=0)
    tr = lambda b: jnp.transpose(b, (0, 1, 3, 2)).reshape(rows, S5_STATE)
    out = jax.ShapeDtypeStruct((rows, S5_STATE), F32)
    are, aim, bbr, bbi = pl.pallas_call(
        _disc_kernel, out_shape=(out, out, out, out), name="s5_discretise",
    )(rep(lam_re), rep(lam_im), rep(log_dt[..., None]), tr(b_re), tr(b_im))
    shp = (depth, S5_GROUPS, S5_GROUP_CH, S5_STATE)
    are = are.reshape(shp)[:, :, 0, :].reshape(depth, 1, S5_LANES)
    aim = aim.reshape(shp)[:, :, 0, :].reshape(depth, 1, S5_LANES)
    return are, aim, bbr.reshape(shp), bbi.reshape(shp)


def _blockdiag(w, rows_first):
    depth = w.shape[0]
    gpb = S5_KBLOCK // S5_GROUP_CH
    w5 = w.reshape(depth, S5_GROUPS // gpb, gpb, S5_GROUP_CH, S5_STATE)
    eye = jnp.eye(gpb, dtype=w.dtype)
    if rows_first:
        return jnp.einsum('lkgcn,gh->lkgchn', w5, eye).reshape(depth, -1, S5_KBLOCK, S5_NBLOCK)
    return jnp.einsum('lkgcn,gh->lkgnhc', w5, eye).reshape(depth, -1, S5_NBLOCK, S5_KBLOCK)


def _unsorted(dest, gran_refs):
    s = jnp.concatenate([g[...] for g in gran_refs], axis=0)
    slot = lax.broadcasted_iota(jnp.int32, (TILE, SORTED_ROWS), 1).astype(F32)
    perm_t = jnp.where(slot == dest, 1.0, 0.0).astype(BF16)
    return jnp.dot(perm_t, s, preferred_element_type=F32)


MIXER_INPUTS = 16


def _interleave(gens):
    while gens:
        gens = [g for g in gens if next(g, None) is not None]


def _mixer_kernel(*refs, bsz, tc, past, nsteps, halves, has_tail, unsort):
    refs = refs[1:] if unsort else refs
    inputs, rest = refs[:MIXER_INPUTS], refs[MIXER_INPUTS:]
    tail_ref = None
    if has_tail:
        tail_ref, rest = rest[0], rest[1:]
    moe = None
    if unsort:
        ngran = halves * TILE_GRANS
        moe, rest = (rest[0], rest[1:1 + ngran]), rest[1 + ngran:]
    body = functools.partial(_mixer_body, *inputs, *rest, moe=moe, bsz=bsz, tc=tc, past=past,
                             last=nsteps - 1, halves=halves)
    if not has_tail:
        body()
        return
    x1_ref = rest[0]
    i = pl.program_id(1)
    pl.when(i < nsteps)(body)

    @pl.when(i >= nsteps)
    def _():
        x1_ref[...] = tail_ref[...]


def _mixer_body(x_ref, h0re_ref, h0im_ref, hist_ref, gmix_ref, win_ref, wx_ref, are_ref, aim_ref,
                cm_ref, dskip_ref, wglu_ref, bglu_ref, wpool_ref, pscale_ref, wout_ref,
                x1_ref, hre_out, him_out, pool_out, xre, xim, sre, sim, hre, him, zbuf,
                *, moe, bsz, tc, past, last, halves):
    i = pl.program_id(1)
    rows = bsz * tc
    hist_rows = POOL_HIST * bsz

    @pl.when(i == 0)
    def _():
        hre[...] = h0re_ref[0]
        him[...] = h0im_ref[0]
        zbuf[0:hist_rows, :] = hist_ref[0]

    def chunk(hf):
        r0 = hf * rows
        rsl = slice(r0, r0 + rows)
        if len(x_ref.shape) == 3:
            x = jnp.concatenate([x_ref[:, hf * tc + t, :] for t in range(tc)], axis=0)
        else:
            x = x_ref[rsl, :]
        if moe is not None:
            dest_ref, gran_refs = moe
            x = x + _unsorted(dest_ref[rsl, :], gran_refs[hf * TILE_GRANS:(hf + 1) * TILE_GRANS])
        h = _rms(x, gmix_ref[...]).astype(BF16)
        yield 1
        proj = jnp.dot(h, win_ref[...], preferred_element_type=F32)
        u = proj[:, :S5_WIDTH]
        z = proj[:, S5_WIDTH:]
        zbuf[hist_rows + r0:hist_rows + r0 + rows, :] = z
        ub = u.astype(BF16)
        yield 2

        step_idx = (i * halves + hf) * tc + lax.broadcasted_iota(jnp.int32, (rows, 1), 0) // bsz
        pos = (step_idx + (past + 1)).astype(F32)
        pooled = []
        for gi, w in enumerate(POOL_WINDOWS):
            gs = slice(gi * POOL_GROUP_CH, (gi + 1) * POOL_GROUP_CH)
            s = zbuf[r0:r0 + hist_rows + rows, gs]
            k = 1
            while k < w:
                s = s[k * bsz:] + s[:-k * bsz]
                k *= 2
            inv_cnt = 1.0 / jnp.minimum(pos, float(w))
            pooled.append((s[-rows:] * inv_cnt - z[:, gs]).astype(BF16))
        yield 3
        pouts = [jnp.dot(pooled[gi], wpool_ref[gi], preferred_element_type=F32)
                 for gi in range(len(POOL_WINDOWS))]
        pool_mixed = (jnp.concatenate(pouts, axis=-1) * pscale_ref[...]).astype(BF16)
        x1_ref[rsl, :] = x + jnp.dot(pool_mixed, wout_ref[S5_WIDTH:, :], preferred_element_type=F32)
        yield 4

        ys = []
        for kb in range(S5_WIDTH // S5_KBLOCK):
            ns = slice(kb * S5_NBLOCK, (kb + 1) * S5_NBLOCK)
            xx = jnp.dot(ub[:, kb * S5_KBLOCK:(kb + 1) * S5_KBLOCK], wx_ref[kb],
                         preferred_element_type=F32)
            xre[rsl, ns] = xx[:, :S5_NBLOCK]
            xim[rsl, ns] = xx[:, S5_NBLOCK:]
            yield 5
            if bsz == SUBLANES:
                scan_rows(r0, 0, ns, True)
            else:
                def body(rb, c, ns=ns):
                    scan_rows(r0, pl.multiple_of(rb * SUBLANES, SUBLANES), ns, False)
                    return c
                lax.fori_loop(0, bsz // SUBLANES, body, 0)
            yield 6
            ys.append(jnp.dot(sre[rsl, ns].astype(BF16), cm_ref[kb, :S5_NBLOCK, :], preferred_element_type=F32)
                      + jnp.dot(sim[rsl, ns].astype(BF16), cm_ref[kb, S5_NBLOCK:, :],
                                preferred_element_type=F32))
        y = jnp.concatenate(ys, axis=-1) + dskip_ref[...] * u
        g = _gelu_tanh(y)
        gb = g.astype(BF16)
        yield 7
        s5_out = g * jax.nn.sigmoid(jnp.dot(gb, wglu_ref[...], preferred_element_type=F32) + bglu_ref[...])
        sb = s5_out.astype(BF16)
        yield 8
        x1_ref[rsl, :] += jnp.dot(sb, wout_ref[:S5_WIDTH, :], preferred_element_type=F32)

    assert SCAN_LANES == S5_NBLOCK

    def scan_rows(r0, b0, ls, static):
        ar = jnp.broadcast_to(are_ref[:, ls], (SUBLANES, SCAN_LANES))
        ai = jnp.broadcast_to(aim_ref[:, ls], (SUBLANES, SCAN_LANES))

        def step(t, carry):
            hr, hi = carry
            row = r0 + t * bsz + b0
            if not static:
                row = pl.multiple_of(row, SUBLANES)
            nr = ar * hr + (xre[pl.ds(row, SUBLANES), ls] - ai * hi)
            ni = ar * hi + (xim[pl.ds(row, SUBLANES), ls] + ai * hr)
            sre[pl.ds(row, SUBLANES), ls] = nr
            sim[pl.ds(row, SUBLANES), ls] = ni
            return nr, ni

        carry = (hre[pl.ds(b0, SUBLANES), ls], him[pl.ds(b0, SUBLANES), ls])
        if static:
            for t in range(tc):
                carry = step(t, carry)
        else:
            carry = lax.fori_loop(0, tc, step, carry, unroll=8)
        hre[pl.ds(b0, SUBLANES), ls] = carry[0]
        him[pl.ds(b0, SUBLANES), ls] = carry[1]

    _interleave([chunk(hf) for hf in range(halves)])

    all_rows = halves * rows

    @pl.when(i == last)
    def _():
        hre_out[0] = hre[...]
        him_out[0] = him[...]
        pool_out[0] = zbuf[all_rows + hist_rows - POOL_BUF * bsz:all_rows + hist_rows, :]

    @pl.when(i != last)
    def _():
        zbuf[0:hist_rows, :] = zbuf[all_rows:all_rows + hist_rows, :]


def _mixer(x, h0re, h0im, hist, lw, l, *, bsz, tc, past, nb, nchunks, halves, in_off, tail=None, moe=None):
    rows = bsz * tc
    assert rows == TILE and nchunks % halves == 0
    blk = halves * rows
    nsteps = nchunks // halves
    ntail = 0 if tail is None else tail.shape[0] // blk
    assert ntail == 0 or (nb == 1 and tail.shape[0] % blk == 0)
    steps = nsteps + ntail
    step = lambda i: jnp.minimum(i, nsteps - 1)
    perb = lambda shape: pl.BlockSpec(shape, lambda b, i, *_: (b,) + (0,) * (len(shape) - 1))
    in_blk = lambda b, i: in_off + b * nsteps + step(i)
    if x.ndim == 2:
        x_spec = pl.BlockSpec((blk, D_MODEL), lambda b, i, *_: (in_blk(b, i), 0))
    else:
        x_spec = pl.BlockSpec((bsz, halves * tc, D_MODEL), lambda b, i, *_: (b, step(i), 0))
    in_specs = [
        x_spec,
        perb((1, bsz, S5_LANES)), perb((1, bsz, S5_LANES)), perb((1, POOL_HIST * bsz, POOL_WIDTH)),
    ] + [_layer_spec(lw[k], l) for k in MIXER_PARAMS]
    args = [x, h0re, h0im, hist] + [lw[k] for k in MIXER_PARAMS]
    assert len(args) == MIXER_INPUTS
    if ntail:
        in_specs.append(pl.BlockSpec((blk, D_MODEL), lambda b, i, *_: (jnp.maximum(i - nsteps, 0), 0)))
        args.append(tail)
    prefetch = []
    if moe is not None:
        dest, expert_out, pos = moe
        prefetch = [pos]
        in_specs.append(pl.BlockSpec((blk, 1), lambda b, i, pos: (in_blk(b, i), 0)))
        args.append(dest)
        for hf in range(halves):
            for q in range(TILE_GRANS):
                in_specs.append(pl.BlockSpec(
                    (GRAN, D_MODEL),
                    lambda b, i, pos, hf=hf, q=q: (pos[(in_blk(b, i) * halves + hf) * TILE_GRANS + q], 0)))
                args.append(expert_out)
    out_specs = [
        pl.BlockSpec((blk, D_MODEL), lambda b, i, *_: (b * steps + i, 0)),
        perb((1, bsz, S5_LANES)), perb((1, bsz, S5_LANES)), perb((1, POOL_BUF * bsz, POOL_WIDTH)),
    ]
    out_shape = [
        jax.ShapeDtypeStruct((nb * steps * blk, D_MODEL), F32),
        jax.ShapeDtypeStruct((nb, bsz, S5_LANES), F32),
        jax.ShapeDtypeStruct((nb, bsz, S5_LANES), F32),
        jax.ShapeDtypeStruct((nb, POOL_BUF * bsz, POOL_WIDTH), F32),
    ]
    scratch = [
        pltpu.VMEM((blk, S5_LANES), F32), pltpu.VMEM((blk, S5_LANES), F32),
        pltpu.VMEM((blk, S5_LANES), F32), pltpu.VMEM((blk, S5_LANES), F32),
        pltpu.VMEM((bsz, S5_LANES), F32), pltpu.VMEM((bsz, S5_LANES), F32),
        pltpu.VMEM(((halves * tc + POOL_HIST) * bsz, POOL_WIDTH), F32),
    ]
    grid_spec = pltpu.PrefetchScalarGridSpec(
        num_scalar_prefetch=len(prefetch), grid=(nb, steps), in_specs=in_specs, out_specs=out_specs,
        scratch_shapes=scratch)
    return pl.pallas_call(
        functools.partial(_mixer_kernel, bsz=bsz, tc=tc, past=past, nsteps=nsteps, halves=halves,
                          has_tail=ntail > 0, unsort=moe is not None),
        grid_spec=grid_spec, out_shape=out_shape, name="mixer",
        compiler_params=pltpu.CompilerParams(dimension_semantics=("arbitrary", "arbitrary"),
                                             vmem_limit_bytes=VMEM_LIMIT),
    )(*prefetch, *args)


ROUTER_ROWS = 32


def _dot_nt(a, b):
    return lax.dot_general(a, b, (((1,), (1,)), ((), ())), preferred_element_type=F32)


def _argmax_rows(rows):
    best, idx = rows[0], jnp.zeros_like(rows[0])
    for k in range(1, len(rows)):
        gt = rows[k] > best
        best = jnp.where(gt, rows[k], best)
        idx = jnp.where(gt, float(k), idx)
    return best, idx


def _gating_t(lt):
    row = lambda r: lt[r:r + 1, :]
    g_rows = [row(N_EXPERTS + g) for g in range(N_EXPERT_GROUPS)]
    gmax, gidx = _argmax_rows(g_rows)
    gsum = g_rows[0] * 0.0
    for r in g_rows:
        gsum = gsum + jnp.exp(r - gmax)
    g_w = 1.0 / gsum
    el = []
    for k in range(EXPERTS_PER_GROUP):
        v = row((N_EXPERT_GROUPS - 1) * EXPERTS_PER_GROUP + k)
        for g in range(N_EXPERT_GROUPS - 2, -1, -1):
            v = jnp.where(gidx == float(g), row(g * EXPERTS_PER_GROUP + k), v)
        el.append(v)
    m1, i1 = _argmax_rows(el)
    m2, i2 = _argmax_rows([jnp.where(i1 == float(k), -jnp.inf, el[k]) for k in range(EXPERTS_PER_GROUP)])
    e2 = jnp.exp(m2 - m1)
    den = 1.0 + e2
    first = gidx * float(EXPERTS_PER_GROUP)
    return gidx, first + i1, first + i2, g_w / den, g_w * e2 / den


def _route_sort_kernel(x1_ref, tri_ref, gffn_ref, wrh_ref, wrl_ref, br_ref, sorted_ref, dest_ref, gid_ref):
    _interleave([_route_sort_tile(k, x1_ref, tri_ref, gffn_ref, wrh_ref, wrl_ref, br_ref, sorted_ref, dest_ref,
                                  gid_ref) for k in range(ROUTE_TILES)])


def _route_sort_tile(k, x1_ref, tri_ref, gffn_ref, wrh_ref, wrl_ref, br_ref, sorted_ref, dest_ref, gid_ref):
    x = x1_ref[k * TILE:(k + 1) * TILE, :]
    h2 = _rms(x, gffn_ref[...])
    hi = h2.astype(BF16)
    lo = (h2 - hi.astype(F32)).astype(BF16)
    yield 1
    lt = (_dot_nt(wrh_ref[...], hi) + _dot_nt(wrl_ref[...], hi) + _dot_nt(wrh_ref[...], lo)) + br_ref[...]
    yield 2
    gidx, ex1, ex2, gate1, gate2 = _gating_t(lt)
    yield 3

    sub8 = lax.broadcasted_iota(jnp.int32, (SUBLANES, TILE), 0).astype(F32)
    onehot = jnp.where(sub8 == gidx, 1.0, 0.0)
    rank = jnp.dot(onehot.astype(BF16), tri_ref[...], preferred_element_type=F32)
    counts = jnp.sum(onehot, axis=1, keepdims=True)
    padded = jnp.floor((counts + float(GRAN - 1)) * (1.0 / GRAN)) * float(GRAN)
    e0 = padded[0:1, :]
    e1 = e0 + padded[1:2, :]
    e2 = e1 + padded[2:3, :]
    e3 = e2 + padded[3:4, :]
    dest = jnp.where(gidx == 0.0, rank[0:1, :],
                     jnp.where(gidx == 1.0, e0 + rank[1:2, :],
                               jnp.where(gidx == 2.0, e1 + rank[2:3, :], e2 + rank[3:4, :])))
    dest_ref[k * TILE:(k + 1) * TILE, :] = jnp.broadcast_to(dest, (LANES, TILE)).T[:, 0:1]

    slot = lax.broadcasted_iota(jnp.int32, (SORTED_ROWS, TILE), 0).astype(F32)
    perm = jnp.where(slot == dest, 1.0, 0.0).astype(BF16)
    sub = lax.broadcasted_iota(jnp.int32, (N_EXPERTS, TILE), 0).astype(F32)
    comb = jnp.where(sub == ex1, gate1, 0.0) + jnp.where(sub == ex2, gate2, 0.0)
    c_hi = comb.astype(BF16)
    c_lo = (comb - c_hi.astype(F32)).astype(BF16)
    zeros = jnp.zeros((LANES - N_EXPERTS, TILE), BF16)
    gates = jnp.concatenate([c_hi, zeros, c_lo, zeros], axis=0)
    yield 4
    rows = slice(k * SORTED_ROWS, (k + 1) * SORTED_ROWS)
    sorted_ref[rows, :D_MODEL] = jnp.dot(perm, hi, preferred_element_type=F32).astype(BF16)
    sorted_ref[rows, D_MODEL:] = _dot_nt(perm, gates).astype(BF16)

    q = lax.broadcasted_iota(jnp.int32, (1, LANES), 1).astype(F32) * float(GRAN)
    gid = (jnp.where(q >= e0, 1, 0) + jnp.where(q >= e1, 1, 0)
           + jnp.where(q >= e2, 1, 0) + jnp.where(q >= e3, 1, 0))
    gid_ref[k] = gid.astype(jnp.int32)


def _route_sort(x1, lw, l):
    ntiles = x1.shape[0] // TILE
    params = [lw[k] for k in ('norm_ffn', 'wr_hi', 'wr_lo', 'b_r')]
    tri = jnp.triu(jnp.ones((TILE, TILE), BF16), k=1)
    return pl.pallas_call(
        _route_sort_kernel, grid=(ntiles // ROUTE_TILES,),
        in_specs=[pl.BlockSpec((ROUTE_TILES * TILE, D_MODEL), lambda i: (i, 0)),
                  pl.BlockSpec((TILE, TILE), lambda i: (0, 0))] + [_layer_spec(p, l) for p in params],
        out_specs=[pl.BlockSpec((ROUTE_TILES * SORTED_ROWS, PAYLOAD), lambda i: (i, 0)),
                   pl.BlockSpec((ROUTE_TILES * TILE, 1), lambda i: (i, 0)),
                   pl.BlockSpec((ROUTE_TILES, 1, LANES), lambda i: (i, 0, 0))],
        out_shape=[jax.ShapeDtypeStruct((ntiles * SORTED_ROWS, PAYLOAD), BF16),
                   jax.ShapeDtypeStruct((x1.shape[0], 1), F32),
                   jax.ShapeDtypeStruct((ntiles, 1, LANES), jnp.int32)],
        name="moe_route_sort",
        compiler_params=pltpu.CompilerParams(dimension_semantics=("arbitrary",),
                                             vmem_limit_bytes=VMEM_LIMIT),
    )(x1, tri, *params)


def _experts_kernel(src_ref, wgid_ref, wvalid_ref, *refs):
    gran_refs = refs[:ITEM_GRANS]
    w1f_ref, w3f_ref, w2f_ref, out_ref, w1_ref, w3_ref, w2_ref = refs[ITEM_GRANS:]
    j = pl.program_id(0)

    @pl.when(jnp.logical_or(j == 0, wgid_ref[j] != wgid_ref[jnp.maximum(j - 1, 0)]))
    def _():
        w1_ref[...] = w1f_ref[...].astype(BF16)
        w3_ref[...] = w3f_ref[...].astype(BF16)
        w2_ref[...] = w2f_ref[...].astype(BF16)

    @pl.when(wvalid_ref[j] > 0)
    def _():
        full = jnp.concatenate([g[...] for g in gran_refs], axis=0)
        h = full[:, :D_MODEL]
        comb = (full[:, D_MODEL:D_MODEL + LANES].astype(F32)
                + full[:, D_MODEL + LANES:].astype(F32))
        first = wgid_ref[j] * EXPERTS_PER_GROUP
        lane = lax.broadcasted_iota(jnp.int32, (ITEM_ROWS, LANES), 1)
        hids = []
        for e in range(EXPERTS_PER_GROUP):
            a = jnp.dot(h, w1_ref[e], preferred_element_type=F32)
            b = jnp.dot(h, w3_ref[e], preferred_element_type=F32)
            gate = jnp.sum(jnp.where(lane == first + e, comb, 0.0), axis=-1, keepdims=True)
            hids.append(((a * jax.nn.sigmoid(a)) * b * gate).astype(BF16))
        hid = jnp.concatenate(hids, axis=-1)
        w2g = w2_ref[...].reshape(EXPERTS_PER_GROUP * D_EXPERT, D_MODEL)
        out_ref[...] = jnp.dot(hid, w2g, preferred_element_type=F32).astype(BF16)

    @pl.when(wvalid_ref[j] == 0)
    def _():
        out_ref[...] = jnp.zeros(out_ref.shape, out_ref.dtype)


def _experts(sorted_tok, src, wgid, wvalid, lw, l):
    nitems = wgid.shape[0]
    gran_spec = lambda s: pl.BlockSpec((GRAN, PAYLOAD), lambda j, src, wg, wv: (src[j * ITEM_GRANS + s], 0))
    wspec = lambda shape: pl.BlockSpec(shape, lambda j, src, wg, wv: (l * N_EXPERT_GROUPS + wg[j], 0, 0))
    grid_spec = pltpu.PrefetchScalarGridSpec(
        num_scalar_prefetch=3, grid=(nitems,),
        in_specs=[gran_spec(s) for s in range(ITEM_GRANS)] + [
            wspec((EXPERTS_PER_GROUP, D_MODEL, D_EXPERT)), wspec((EXPERTS_PER_GROUP, D_MODEL, D_EXPERT)),
            wspec((EXPERTS_PER_GROUP, D_EXPERT, D_MODEL))],
        out_specs=pl.BlockSpec((ITEM_ROWS, D_MODEL), lambda j, src, wg, wv: (j, 0)),
        scratch_shapes=[pltpu.VMEM((EXPERTS_PER_GROUP, D_MODEL, D_EXPERT), BF16),
                        pltpu.VMEM((EXPERTS_PER_GROUP, D_MODEL, D_EXPERT), BF16),
                        pltpu.VMEM((EXPERTS_PER_GROUP, D_EXPERT, D_MODEL), BF16)])
    return pl.pallas_call(
        _experts_kernel, grid_spec=grid_spec,
        out_shape=jax.ShapeDtypeStruct((nitems * ITEM_ROWS, D_MODEL), BF16),
        name="moe_experts",
        compiler_params=pltpu.CompilerParams(dimension_semantics=("arbitrary",),
                                             vmem_limit_bytes=VMEM_LIMIT),
    )(src, wgid, wvalid, *([sorted_tok] * ITEM_GRANS), lw['w1'], lw['w3'], lw['w2'])


def _unsort_kernel(pos_ref, x1_ref, dest_ref, *refs, final):
    gran_refs = refs[:TILE_GRANS]
    gfin_ref, out_ref = refs[TILE_GRANS:]
    y = x1_ref[...] + _unsorted(dest_ref[...], gran_refs)
    if not final:
        out_ref[...] = y
    else:
        y = _rms(y, gfin_ref[...])
        bsz, tc, _ = out_ref.shape
        for t in range(tc):
            out_ref[:, t, :] = y[t * bsz:(t + 1) * bsz, :]


def _unsort(x1, dest, expert_out, pos, norm_final, *, tile_off=0, ntiles=None, batch_major=None):
    final = batch_major is not None
    if final:
        batch, t_len, bsz, tc = batch_major
        nchunks = t_len // tc
        out_spec = pl.BlockSpec((bsz, tc, D_MODEL), lambda k, pos: (k // nchunks, k % nchunks, 0))
        out_shape = jax.ShapeDtypeStruct((batch, t_len, D_MODEL), F32)
    else:
        ntiles = x1.shape[0] // TILE
        out_spec = pl.BlockSpec((TILE, D_MODEL), lambda k, pos: (k, 0))
        out_shape = jax.ShapeDtypeStruct(x1.shape, F32)
    gran_spec = lambda q: pl.BlockSpec((GRAN, D_MODEL),
                                       lambda k, pos: (pos[(k + tile_off) * TILE_GRANS + q], 0))
    grid_spec = pltpu.PrefetchScalarGridSpec(
        num_scalar_prefetch=1, grid=(ntiles,),
        in_specs=[pl.BlockSpec((TILE, D_MODEL), lambda k, pos: (k + tile_off, 0)),
                  pl.BlockSpec((TILE, 1), lambda k, pos: (k + tile_off, 0))]
                 + [gran_spec(q) for q in range(TILE_GRANS)]
                 + [pl.BlockSpec((1, D_MODEL), lambda k, pos: (0, 0))],
        out_specs=out_spec)
    return pl.pallas_call(
        functools.partial(_unsort_kernel, final=final), grid_spec=grid_spec,
        out_shape=out_shape, name="moe_unsort",
        compiler_params=pltpu.CompilerParams(dimension_semantics=("arbitrary",),
                                             vmem_limit_bytes=VMEM_LIMIT),
    )(pos, x1, dest, *([expert_out] * TILE_GRANS), norm_final)


def _dispatch_tables(gid):
    ntiles = gid.shape[0]
    ngran = ntiles * TILE_GRANS
    nitems = -(-ngran // ITEM_GRANS) + N_EXPERT_GROUPS
    g = gid[:, 0, :TILE_GRANS].reshape(ngran)
    groups = jnp.arange(N_EXPERT_GROUPS, dtype=jnp.int32)
    onehot = (g[:, None] == groups[None, :]).astype(F32)
    valid = g < N_EXPERT_GROUPS
    idx = jnp.arange(ngran, dtype=jnp.int32)
    earlier = (idx[None, :] < idx[:, None]).astype(F32)
    rank = jnp.dot(earlier, onehot, precision=lax.Precision.HIGHEST).astype(jnp.int32)
    cnt = jnp.sum(onehot, axis=0).astype(jnp.int32)
    pcnt = (cnt + (ITEM_GRANS - 1)) // ITEM_GRANS * ITEM_GRANS
    pstart = jnp.sum(jnp.where(groups[:, None] < groups[None, :], pcnt[:, None], 0), axis=0)
    pend = pstart + pcnt
    pos = jnp.sum(onehot.astype(jnp.int32) * (pstart[None, :] + rank), axis=1)
    slots = jnp.arange(nitems * ITEM_GRANS, dtype=jnp.int32)
    hit = jnp.logical_and(pos[None, :] == slots[:, None], valid[None, :])
    src = jnp.sum(jnp.where(hit, jnp.arange(ngran, dtype=jnp.int32)[None, :], 0), axis=1)
    item_start = jnp.arange(nitems, dtype=jnp.int32) * ITEM_GRANS
    last_group = jnp.max(jnp.where(pcnt > 0, jnp.arange(N_EXPERT_GROUPS, dtype=jnp.int32), 0))
    wgid = jnp.minimum(jnp.sum((item_start[:, None] >= pend[None, :]).astype(jnp.int32), axis=1), last_group)
    wvalid = (item_start < pend[-1]).astype(jnp.int32)
    return src.astype(jnp.int32), wgid.astype(jnp.int32), wvalid, jnp.where(valid, pos, 0).astype(jnp.int32)


def _moe(x1, lw, l, norm_final, final_layouts=None):
    sorted_tok, dest, gid = _route_sort(x1, lw, l)
    src, wgid, wvalid, pos = _dispatch_tables(gid)
    expert_out = _experts(sorted_tok, src, wgid, wvalid, lw, l)
    if final_layouts is None:
        return dest, expert_out, pos
    return [_unsort(x1, dest, expert_out, pos, norm_final, tile_off=off, ntiles=n, batch_major=bm)
            for off, n, bm in final_layouts]


def _router_weights(w_grp, b_grp, w_er, b_er):
    pad = ROUTER_ROWS - N_EXPERTS - N_EXPERT_GROUPS
    w = jnp.pad(jnp.concatenate([w_er, w_grp], axis=-1), ((0, 0), (0, 0), (0, pad)))
    w = jnp.transpose(w, (0, 2, 1))
    b = jnp.pad(jnp.concatenate([b_er, b_grp], axis=-1), ((0, 0), (0, pad)))[:, :, None]
    hi = w.astype(BF16)
    lo = (w - hi.astype(F32)).astype(BF16)
    return hi, lo, b


def _to_time_major(x, nb):
    bt, t, w = x.shape
    return jnp.transpose(x.reshape(nb, bt // nb, t, w), (0, 2, 1, 3)).reshape(nb, t * (bt // nb), w)


def _from_time_major(x, t):
    nb, rows, w = x.shape
    bsz = rows // t
    return jnp.transpose(x.reshape(nb, t, bsz, w), (0, 2, 1, 3)).reshape(nb * bsz, t, w)


def kernel(x_prompt, x_sample, state_ssm_re, state_ssm_im, state_pool, norm_mix, w_in, lam_re, lam_im, log_dt, b_re, b_im, c_re, c_im, d_skip, w_glu, b_glu, w_pool, pool_scale, w_out, norm_ffn, w_grp, b_grp, w_erouter, b_erouter, w1, w3, w2, norm_final):
    depth = w_in.shape[0]
    pb, pt, _ = x_prompt.shape
    sb, st, _ = x_sample.shape
    p_tc = TILE // pb
    p_tiles = pt // p_tc
    s_bsz = TILE // st
    s_nb = sb // s_bsz
    p_rows = pb * pt

    are, aim, bbr, bbi = _discretise(lam_re, lam_im, log_dt, b_re, b_im)
    wx = jnp.concatenate([_blockdiag(bbr, True), _blockdiag(bbi, True)], axis=-1).astype(BF16)
    cm = jnp.concatenate([_blockdiag(c_re, False), _blockdiag(-c_im, False)], axis=-2).astype(BF16)

    wr_hi, wr_lo, b_r = _router_weights(w_grp, b_grp, w_erouter, b_erouter)
    stack_experts = lambda w: w.reshape((depth * N_EXPERTS,) + w.shape[2:])
    vec = lambda a: a[:, None, :]
    lw = dict(norm_mix=vec(norm_mix), w_in=w_in.astype(BF16), wx=wx, are=are, aim=aim, cm=cm,
              d_skip=vec(d_skip), w_glu=w_glu.astype(BF16), b_glu=vec(b_glu), w_pool=w_pool.astype(BF16),
              pool_scale=vec(pool_scale), w_out=w_out.astype(BF16), norm_ffn=vec(norm_ffn),
              wr_hi=wr_hi, wr_lo=wr_lo, b_r=b_r,
              w1=stack_experts(w1), w3=stack_experts(w3), w2=stack_experts(w2))
    nfin = norm_final[None]

    p_zero_h = jnp.zeros((1, pb, S5_LANES), F32)
    p_zero_hist = jnp.zeros((1, POOL_HIST * pb, POOL_WIDTH), F32)
    final_layouts = [(0, p_tiles, (pb, pt, pb, p_tc)), (p_tiles, s_nb, (sb, st, s_bsz, st))]

    outs = {k: [] for k in ('p_re', 'p_im', 'p_pool', 's_re', 's_im', 's_pool')}
    x, moe = None, None
    for l in range(depth):
        h0re = state_ssm_re[l].reshape(s_nb, s_bsz, S5_LANES)
        h0im = state_ssm_im[l].reshape(s_nb, s_bsz, S5_LANES)
        hist = jnp.pad(_to_time_major(state_pool[l], s_nb), ((0, 0), ((POOL_HIST - POOL_BUF) * s_bsz, 0), (0, 0)))
        xs1, hre, him, pool = _mixer(x_sample if l == 0 else x, h0re, h0im, hist, lw, l, bsz=s_bsz, tc=st,
                                     past=POOL_BUF, nb=s_nb, nchunks=1, halves=1, in_off=p_tiles, moe=moe)
        outs['s_re'].append(hre.reshape(sb, S5_GROUPS, S5_STATE))
        outs['s_im'].append(him.reshape(sb, S5_GROUPS, S5_STATE))
        outs['s_pool'].append(_from_time_major(pool, POOL_BUF))
        x1, hre, him, pool = _mixer(x_prompt if l == 0 else x, p_zero_h, p_zero_h, p_zero_hist, lw, l,
                                    bsz=pb, tc=p_tc, past=0, nb=1, nchunks=p_tiles, halves=2, in_off=0, tail=xs1, moe=moe)
        outs['p_re'].append(hre.reshape(pb, S5_GROUPS, S5_STATE))
        outs['p_im'].append(him.reshape(pb, S5_GROUPS, S5_STATE))
        outs['p_pool'].append(_from_time_major(pool, POOL_BUF))
        if l < depth - 1:
            x, moe = x1, _moe(x1, lw, l, nfin)
        else:
            y_prompt, y_sample = _moe(x1, lw, l, nfin, final_layouts)

    return (y_prompt, y_sample, jnp.stack(outs['p_re']), jnp.stack(outs['p_im']), jnp.stack(outs['p_pool']),
            jnp.stack(outs['s_re']), jnp.stack(outs['s_im']), jnp.stack(outs['s_pool']))
```

```python
import functools

import jax
import jax.numpy as jnp
from jax import lax
from jax.experimental import pallas as pl
from jax.experimental.pallas import tpu as pltpu

D_MODEL = 1024
S5_WIDTH = 512
S5_GROUP_CH = 16
S5_GROUPS = 32
S5_STATE = 64
S5_LANES = S5_GROUPS * S5_STATE
POOL_WIDTH = 512
POOL_WINDOWS = (2, 4, 8, 16)
POOL_GROUP_CH = 128
POOL_BUF = 15
POOL_HIST = 16
N_EXPERTS = 16
EXPERTS_PER_GROUP = 4
N_EXPERT_GROUPS = 4
D_EXPERT = 256
EPS = 1e-6

SUBLANES = 8
LANES = 128
S5_KBLOCK = 128
S5_NBLOCK = S5_KBLOCK // S5_GROUP_CH * S5_STATE
SCAN_LANES = 512
VMEM_LIMIT = 56 * 1024 * 1024

TILE = 512
GRAN = 32
TILE_GRANS = TILE // GRAN + N_EXPERT_GROUPS
SORTED_ROWS = TILE_GRANS * GRAN
ITEM_GRANS = 16
ITEM_ROWS = ITEM_GRANS * GRAN
ROUTE_TILES = 2
PAYLOAD = D_MODEL + 2 * LANES

F32 = jnp.float32
BF16 = jnp.bfloat16

MIXER_PARAMS = ('norm_mix', 'w_in', 'wx', 'are', 'aim', 'cm_re', 'cm_im', 'd_skip', 'w_glu', 'b_glu', 'w_pool',
                'pool_scale', 'w_out')


def _layer_spec(arr, l):
    return pl.BlockSpec((None,) + arr.shape[1:], lambda *_: (l,) + (0,) * (arr.ndim - 1))


def _rms(x, g):
    return x * lax.rsqrt(jnp.mean(x * x, axis=-1, keepdims=True) + EPS) * g


def _gelu_tanh(x):
    return 0.5 * x * (1.0 + jnp.tanh(0.7978845608028654 * (x + 0.044715 * (x * x * x))))


def _disc_kernel(lr_ref, li_ref, ldt_ref, br_ref, bi_ref, cr_ref, ci_ref, t64_ref, t16_ref,
                 are_ref, aim_ref, wx_ref, cmre_ref, cmim_ref):
    lr = lr_ref[...]
    li = li_ref[...]
    dt = jnp.exp(ldt_ref[...])
    mag = jnp.exp(lr * dt)
    ab_re = mag * jnp.cos(li * dt)
    ab_im = mag * jnp.sin(li * dt)
    den = lr * lr + li * li
    nr = ab_re - 1.0
    coef_re = (nr * lr + ab_im * li) / den
    coef_im = (ab_im * lr - nr * li) / den
    br = br_ref[...]
    bi = bi_ref[...]
    are_ref[...] = ab_re
    aim_ref[...] = ab_im
    bb_re = (coef_re * br - coef_im * bi).astype(BF16)
    bb_im = (coef_re * bi + coef_im * br).astype(BF16)

    gpb = S5_KBLOCK // S5_GROUP_CH
    rows, lanes = wx_ref.shape[0], S5_NBLOCK
    own = ((lax.broadcasted_iota(jnp.int32, (rows, lanes), 0) // S5_GROUP_CH) % gpb
           == lax.broadcasted_iota(jnp.int32, (rows, lanes), 1) // S5_STATE)
    tile8 = lambda v, t_ref: jnp.dot(v, t_ref[...], preferred_element_type=F32)
    wx_ref[:, :lanes] = jnp.where(own, tile8(bb_re, t64_ref), 0.0).astype(BF16)
    wx_ref[:, lanes:] = jnp.where(own, tile8(bb_im, t64_ref), 0.0).astype(BF16)

    crow, clane = cmre_ref.shape
    cown = ((lax.broadcasted_iota(jnp.int32, (crow, clane), 0) // S5_STATE) % gpb
            == lax.broadcasted_iota(jnp.int32, (crow, clane), 1) // S5_GROUP_CH)
    cmre_ref[...] = jnp.where(cown, tile8(cr_ref[...].astype(BF16), t16_ref), 0.0).astype(BF16)
    cmim_ref[...] = jnp.where(cown, -tile8(ci_ref[...].astype(BF16), t16_ref), 0.0).astype(BF16)


def _discretise(lam_re, lam_im, log_dt, b_re, b_im, c_re, c_im):
    depth = lam_re.shape[0]
    rows = depth * S5_GROUPS * S5_GROUP_CH
    crows = depth * S5_GROUPS * S5_STATE
    gpb = S5_KBLOCK // S5_GROUP_CH
    nblk = S5_GROUPS // gpb
    rep = lambda a: jnp.repeat(a.reshape(depth * S5_GROUPS, -1), S5_GROUP_CH, axis=0)
    tr = lambda b: jnp.transpose(b, (0, 1, 3, 2)).reshape(-1, b.shape[2])
    t64 = jnp.tile(jnp.eye(S5_STATE, dtype=BF16), (1, gpb))
    t16 = jnp.tile(jnp.eye(S5_GROUP_CH, dtype=BF16), (1, gpb))
    are, aim, wx, cm_re, cm_im = pl.pallas_call(
        _disc_kernel, name="s5_discretise",
        out_shape=(jax.ShapeDtypeStruct((rows, S5_STATE), F32), jax.ShapeDtypeStruct((rows, S5_STATE), F32),
                   jax.ShapeDtypeStruct((rows, 2 * S5_NBLOCK), BF16),
                   jax.ShapeDtypeStruct((crows, S5_KBLOCK), BF16), jax.ShapeDtypeStruct((crows, S5_KBLOCK), BF16)),
    )(rep(lam_re), rep(lam_im), rep(log_dt[..., None]), tr(b_re), tr(b_im), tr(c_re), tr(c_im), t64, t16)
    shp = (depth, S5_GROUPS, S5_GROUP_CH, S5_STATE)
    are = are.reshape(shp)[:, :, 0, :].reshape(depth, 1, S5_LANES)
    aim = aim.reshape(shp)[:, :, 0, :].reshape(depth, 1, S5_LANES)
    return (are, aim, wx.reshape(depth, nblk, S5_KBLOCK, 2 * S5_NBLOCK),
            cm_re.reshape(depth, nblk, S5_NBLOCK, S5_KBLOCK), cm_im.reshape(depth, nblk, S5_NBLOCK, S5_KBLOCK))


def _unsorted(dest, gran_refs):
    s = jnp.concatenate([g[...] for g in gran_refs], axis=0)
    slot = lax.broadcasted_iota(jnp.int32, (TILE, SORTED_ROWS), 1).astype(F32)
    perm_t = jnp.where(slot == dest, 1.0, 0.0).astype(BF16)
    return jnp.dot(perm_t, s, preferred_element_type=F32)


MIXER_INPUTS = 4 + len(MIXER_PARAMS)


def _interleave(gens):
    while gens:
        gens = [g for g in gens if next(g, None) is not None]


def _mixer_kernel(*refs, bsz, tc, past, nsteps, halves, has_tail, unsort):
    refs = refs[1:] if unsort else refs
    inputs, rest = refs[:MIXER_INPUTS], refs[MIXER_INPUTS:]
    tail_ref = None
    if has_tail:
        tail_ref, rest = rest[0], rest[1:]
    moe = None
    if unsort:
        ngran = halves * TILE_GRANS
        moe, rest = (rest[0], rest[1:1 + ngran]), rest[1 + ngran:]
    body = functools.partial(_mixer_body, *inputs, *rest, moe=moe, bsz=bsz, tc=tc, past=past,
                             last=nsteps - 1, halves=halves)
    if not has_tail:
        body()
        return
    x1_ref = rest[0]
    i = pl.program_id(1)
    pl.when(i < nsteps)(body)

    @pl.when(i >= nsteps)
    def _():
        x1_ref[...] = tail_ref[...]


def _mixer_body(x_ref, h0re_ref, h0im_ref, hist_ref, gmix_ref, win_ref, wx_ref, are_ref, aim_ref,
                cmre_ref, cmim_ref, dskip_ref, wglu_ref, bglu_ref, wpool_ref, pscale_ref, wout_ref,
                x1_ref, hre_out, him_out, pool_out, xre, xim, sre, sim, hre, him, zbuf,
                *, moe, bsz, tc, past, last, halves):
    i = pl.program_id(1)
    rows = bsz * tc
    hist_rows = POOL_HIST * bsz

    @pl.when(i == 0)
    def _():
        hre[...] = h0re_ref[0]
        him[...] = h0im_ref[0]
        zbuf[0:hist_rows, :] = hist_ref[0]

    def chunk(hf):
        r0 = hf * rows
        rsl = slice(r0, r0 + rows)
        if len(x_ref.shape) == 3:
            x = jnp.concatenate([x_ref[:, hf * tc + t, :] for t in range(tc)], axis=0)
        else:
            x = x_ref[rsl, :]
        if moe is not None:
            dest_ref, gran_refs = moe
            x = x + _unsorted(dest_ref[rsl, :], gran_refs[hf * TILE_GRANS:(hf + 1) * TILE_GRANS])
        h = _rms(x, gmix_ref[...]).astype(BF16)
        yield 1
        proj = jnp.dot(h, win_ref[...], preferred_element_type=F32)
        u = proj[:, :S5_WIDTH]
        z = proj[:, S5_WIDTH:]
        zbuf[hist_rows + r0:hist_rows + r0 + rows, :] = z
        ub = u.astype(BF16)
        yield 2

        step_idx = (i * halves + hf) * tc + lax.broadcasted_iota(jnp.int32, (rows, 1), 0) // bsz
        pos = (step_idx + (past + 1)).astype(F32)
        pooled = []
        for gi, w in enumerate(POOL_WINDOWS):
            gs = slice(gi * POOL_GROUP_CH, (gi + 1) * POOL_GROUP_CH)
            s = zbuf[r0:r0 + hist_rows + rows, gs]
            k = 1
            while k < w:
                s = s[k * bsz:] + s[:-k * bsz]
                k *= 2
            inv_cnt = 1.0 / jnp.minimum(pos, float(w))
            pooled.append((s[-rows:] * inv_cnt - z[:, gs]).astype(BF16))
        yield 3
        pouts = [jnp.dot(pooled[gi], wpool_ref[gi], preferred_element_type=F32)
                 for gi in range(len(POOL_WINDOWS))]
        pool_mixed = (jnp.concatenate(pouts, axis=-1) * pscale_ref[...]).astype(BF16)
        x1_ref[rsl, :] = x + jnp.dot(pool_mixed, wout_ref[S5_WIDTH:, :], preferred_element_type=F32)
        yield 4

        ys = []
        for kb in range(S5_WIDTH // S5_KBLOCK):
            ns = slice(kb * S5_NBLOCK, (kb + 1) * S5_NBLOCK)
            xx = jnp.dot(ub[:, kb * S5_KBLOCK:(kb + 1) * S5_KBLOCK], wx_ref[kb],
                         preferred_element_type=F32)
            xre[rsl, ns] = xx[:, :S5_NBLOCK]
            xim[rsl, ns] = xx[:, S5_NBLOCK:]
            yield 5
            if bsz == SUBLANES:
                scan_rows(r0, 0, ns, True)
            else:
                def body(rb, c, ns=ns):
                    scan_rows(r0, pl.multiple_of(rb * SUBLANES, SUBLANES), ns, False)
                    return c
                lax.fori_loop(0, bsz // SUBLANES, body, 0)
            yield 6
            ys.append(jnp.dot(sre[rsl, ns].astype(BF16), cmre_ref[kb], preferred_element_type=F32)
                      + jnp.dot(sim[rsl, ns].astype(BF16), cmim_ref[kb], preferred_element_type=F32))
        y = jnp.concatenate(ys, axis=-1) + dskip_ref[...] * u
        g = _gelu_tanh(y)
        gb = g.astype(BF16)
        yield 7
        s5_out = g * jax.nn.sigmoid(jnp.dot(gb, wglu_ref[...], preferred_element_type=F32) + bglu_ref[...])
        sb = s5_out.astype(BF16)
        yield 8
        x1_ref[rsl, :] += jnp.dot(sb, wout_ref[:S5_WIDTH, :], preferred_element_type=F32)

    assert SCAN_LANES == S5_NBLOCK

    def scan_rows(r0, b0, ls, static):
        ar = jnp.broadcast_to(are_ref[:, ls], (SUBLANES, SCAN_LANES))
        ai = jnp.broadcast_to(aim_ref[:, ls], (SUBLANES, SCAN_LANES))

        def step(t, carry):
            hr, hi = carry
            row = r0 + t * bsz + b0
            if not static:
                row = pl.multiple_of(row, SUBLANES)
            nr = ar * hr + (xre[pl.ds(row, SUBLANES), ls] - ai * hi)
            ni = ar * hi + (xim[pl.ds(row, SUBLANES), ls] + ai * hr)
            sre[pl.ds(row, SUBLANES), ls] = nr
            sim[pl.ds(row, SUBLANES), ls] = ni
            return nr, ni

        carry = (hre[pl.ds(b0, SUBLANES), ls], him[pl.ds(b0, SUBLANES), ls])
        if static:
            for t in range(tc):
                carry = step(t, carry)
        else:
            carry = lax.fori_loop(0, tc, step, carry, unroll=8)
        hre[pl.ds(b0, SUBLANES), ls] = carry[0]
        him[pl.ds(b0, SUBLANES), ls] = carry[1]

    _interleave([chunk(hf) for hf in range(halves)])

    all_rows = halves * rows

    @pl.when(i == last)
    def _():
        hre_out[0] = hre[...]
        him_out[0] = him[...]
        pool_out[0] = zbuf[all_rows + hist_rows - POOL_BUF * bsz:all_rows + hist_rows, :]

    @pl.when(i != last)
    def _():
        zbuf[0:hist_rows, :] = zbuf[all_rows:all_rows + hist_rows, :]


def _mixer(x, h0re, h0im, hist, lw, l, *, bsz, tc, past, nb, nchunks, halves, in_off, tail=None, moe=None):
    rows = bsz * tc
    assert rows == TILE and nchunks % halves == 0
    blk = halves * rows
    nsteps = nchunks // halves
    ntail = 0 if tail is None else tail.shape[0] // blk
    assert ntail == 0 or (nb == 1 and tail.shape[0] % blk == 0)
    steps = nsteps + ntail
    step = lambda i: jnp.minimum(i, nsteps - 1)
    perb = lambda shape: pl.BlockSpec(shape, lambda b, i, *_: (b,) + (0,) * (len(shape) - 1))
    in_blk = lambda b, i: in_off + b * nsteps + step(i)
    if x.ndim == 2:
        x_spec = pl.BlockSpec((blk, D_MODEL), lambda b, i, *_: (in_blk(b, i), 0))
    else:
        x_spec = pl.BlockSpec((bsz, halves * tc, D_MODEL), lambda b, i, *_: (b, step(i), 0))
    in_specs = [
        x_spec,
        perb((1, bsz, S5_LANES)), perb((1, bsz, S5_LANES)), perb((1, POOL_HIST * bsz, POOL_WIDTH)),
    ] + [_layer_spec(lw[k], l) for k in MIXER_PARAMS]
    args = [x, h0re, h0im, hist] + [lw[k] for k in MIXER_PARAMS]
    assert len(args) == MIXER_INPUTS
    if ntail:
        in_specs.append(pl.BlockSpec((blk, D_MODEL), lambda b, i, *_: (jnp.maximum(i - nsteps, 0), 0)))
        args.append(tail)
    prefetch = []
    if moe is not None:
        dest, expert_out, pos = moe
        prefetch = [pos]
        in_specs.append(pl.BlockSpec((blk, 1), lambda b, i, pos: (in_blk(b, i), 0)))
        args.append(dest)
        for hf in range(halves):
            for q in range(TILE_GRANS):
                in_specs.append(pl.BlockSpec(
                    (GRAN, D_MODEL),
                    lambda b, i, pos, hf=hf, q=q: (pos[(in_blk(b, i) * halves + hf) * TILE_GRANS + q], 0)))
                args.append(expert_out)
    out_specs = [
        pl.BlockSpec((blk, D_MODEL), lambda b, i, *_: (b * steps + i, 0)),
        perb((1, bsz, S5_LANES)), perb((1, bsz, S5_LANES)), perb((1, POOL_BUF * bsz, POOL_WIDTH)),
    ]
    out_shape = [
        jax.ShapeDtypeStruct((nb * steps * blk, D_MODEL), F32),
        jax.ShapeDtypeStruct((nb, bsz, S5_LANES), F32),
        jax.ShapeDtypeStruct((nb, bsz, S5_LANES), F32),
        jax.ShapeDtypeStruct((nb, POOL_BUF * bsz, POOL_WIDTH), F32),
    ]
    scratch = [
        pltpu.VMEM((blk, S5_LANES), F32), pltpu.VMEM((blk, S5_LANES), F32),
        pltpu.VMEM((blk, S5_LANES), F32), pltpu.VMEM((blk, S5_LANES), F32),
        pltpu.VMEM((bsz, S5_LANES), F32), pltpu.VMEM((bsz, S5_LANES), F32),
        pltpu.VMEM(((halves * tc + POOL_HIST) * bsz, POOL_WIDTH), F32),
    ]
    grid_spec = pltpu.PrefetchScalarGridSpec(
        num_scalar_prefetch=len(prefetch), grid=(nb, steps), in_specs=in_specs, out_specs=out_specs,
        scratch_shapes=scratch)
    return pl.pallas_call(
        functools.partial(_mixer_kernel, bsz=bsz, tc=tc, past=past, nsteps=nsteps, halves=halves,
                          has_tail=ntail > 0, unsort=moe is not None),
        grid_spec=grid_spec, out_shape=out_shape, name="mixer",
        compiler_params=pltpu.CompilerParams(dimension_semantics=("arbitrary", "arbitrary"),
                                             vmem_limit_bytes=VMEM_LIMIT),
    )(*prefetch, *args)


ROUTER_ROWS = 32


def _dot_nt(a, b):
    return lax.dot_general(a, b, (((1,), (1,)), ((), ())), preferred_element_type=F32)


def _argmax_rows(rows):
    best, idx = rows[0], jnp.zeros_like(rows[0])
    for k in range(1, len(rows)):
        gt = rows[k] > best
        best = jnp.where(gt, rows[k], best)
        idx = jnp.where(gt, float(k), idx)
    return best, idx


def _gating_t(lt):
    row = lambda r: lt[r:r + 1, :]
    g_rows = [row(N_EXPERTS + g) for g in range(N_EXPERT_GROUPS)]
    gmax, gidx = _argmax_rows(g_rows)
    gsum = g_rows[0] * 0.0
    for r in g_rows:
        gsum = gsum + jnp.exp(r - gmax)
    g_w = 1.0 / gsum
    el = []
    for k in range(EXPERTS_PER_GROUP):
        v = row((N_EXPERT_GROUPS - 1) * EXPERTS_PER_GROUP + k)
        for g in range(N_EXPERT_GROUPS - 2, -1, -1):
            v = jnp.where(gidx == float(g), row(g * EXPERTS_PER_GROUP + k), v)
        el.append(v)
    m1, i1 = _argmax_rows(el)
    m2, i2 = _argmax_rows([jnp.where(i1 == float(k), -jnp.inf, el[k]) for k in range(EXPERTS_PER_GROUP)])
    e2 = jnp.exp(m2 - m1)
    den = 1.0 + e2
    first = gidx * float(EXPERTS_PER_GROUP)
    return gidx, first + i1, first + i2, g_w / den, g_w * e2 / den


def _route_sort_kernel(x1_ref, tri_ref, gffn_ref, wrh_ref, wrl_ref, br_ref, sorted_ref, dest_ref, gid_ref):
    _interleave([_route_sort_tile(k, x1_ref, tri_ref, gffn_ref, wrh_ref, wrl_ref, br_ref, sorted_ref, dest_ref,
                                  gid_ref) for k in range(ROUTE_TILES)])


def _route_sort_tile(k, x1_ref, tri_ref, gffn_ref, wrh_ref, wrl_ref, br_ref, sorted_ref, dest_ref, gid_ref):
    x = x1_ref[k * TILE:(k + 1) * TILE, :]
    h2 = _rms(x, gffn_ref[...])
    hi = h2.astype(BF16)
    lo = (h2 - hi.astype(F32)).astype(BF16)
    yield 1
    lt = (_dot_nt(wrh_ref[...], hi) + _dot_nt(wrl_ref[...], hi) + _dot_nt(wrh_ref[...], lo)) + br_ref[...]
    yield 2
    gidx, ex1, ex2, gate1, gate2 = _gating_t(lt)
    yield 3

    sub8 = lax.broadcasted_iota(jnp.int32, (SUBLANES, TILE), 0).astype(F32)
    onehot = jnp.where(sub8 == gidx, 1.0, 0.0)
    rank = jnp.dot(onehot.astype(BF16), tri_ref[...], preferred_element_type=F32)
    counts = jnp.sum(onehot, axis=1, keepdims=True)
    padded = jnp.floor((counts + float(GRAN - 1)) * (1.0 / GRAN)) * float(GRAN)
    e0 = padded[0:1, :]
    e1 = e0 + padded[1:2, :]
    e2 = e1 + padded[2:3, :]
    e3 = e2 + padded[3:4, :]
    dest = jnp.where(gidx == 0.0, rank[0:1, :],
                     jnp.where(gidx == 1.0, e0 + rank[1:2, :],
                               jnp.where(gidx == 2.0, e1 + rank[2:3, :], e2 + rank[3:4, :])))
    dest_ref[k * TILE:(k + 1) * TILE, :] = jnp.broadcast_to(dest, (LANES, TILE)).T[:, 0:1]

    slot = lax.broadcasted_iota(jnp.int32, (SORTED_ROWS, TILE), 0).astype(F32)
    perm = jnp.where(slot == dest, 1.0, 0.0).astype(BF16)
    sub = lax.broadcasted_iota(jnp.int32, (N_EXPERTS, TILE), 0).astype(F32)
    comb = jnp.where(sub == ex1, gate1, 0.0) + jnp.where(sub == ex2, gate2, 0.0)
    c_hi = comb.astype(BF16)
    c_lo = (comb - c_hi.astype(F32)).astype(BF16)
    zeros = jnp.zeros((LANES - N_EXPERTS, TILE), BF16)
    gates = jnp.concatenate([c_hi, zeros, c_lo, zeros], axis=0)
    yield 4
    rows = slice(k * SORTED_ROWS, (k + 1) * SORTED_ROWS)
    sorted_ref[rows, :D_MODEL] = jnp.dot(perm, hi, preferred_element_type=F32).astype(BF16)
    sorted_ref[rows, D_MODEL:] = _dot_nt(perm, gates).astype(BF16)

    q = lax.broadcasted_iota(jnp.int32, (1, LANES), 1).astype(F32) * float(GRAN)
    gid = (jnp.where(q >= e0, 1, 0) + jnp.where(q >= e1, 1, 0)
           + jnp.where(q >= e2, 1, 0) + jnp.where(q >= e3, 1, 0))
    gid_ref[k] = gid.astype(jnp.int32)


def _route_sort(x1, lw, l):
    ntiles = x1.shape[0] // TILE
    params = [lw[k] for k in ('norm_ffn', 'wr_hi', 'wr_lo', 'b_r')]
    tri = jnp.triu(jnp.ones((TILE, TILE), BF16), k=1)
    return pl.pallas_call(
        _route_sort_kernel, grid=(ntiles // ROUTE_TILES,),
        in_specs=[pl.BlockSpec((ROUTE_TILES * TILE, D_MODEL), lambda i: (i, 0)),
                  pl.BlockSpec((TILE, TILE), lambda i: (0, 0))] + [_layer_spec(p, l) for p in params],
        out_specs=[pl.BlockSpec((ROUTE_TILES * SORTED_ROWS, PAYLOAD), lambda i: (i, 0)),
                   pl.BlockSpec((ROUTE_TILES * TILE, 1), lambda i: (i, 0)),
                   pl.BlockSpec((ROUTE_TILES, 1, LANES), lambda i: (i, 0, 0))],
        out_shape=[jax.ShapeDtypeStruct((ntiles * SORTED_ROWS, PAYLOAD), BF16),
                   jax.ShapeDtypeStruct((x1.shape[0], 1), F32),
                   jax.ShapeDtypeStruct((ntiles, 1, LANES), jnp.int32)],
        name="moe_route_sort",
        compiler_params=pltpu.CompilerParams(dimension_semantics=("arbitrary",),
                                             vmem_limit_bytes=VMEM_LIMIT),
    )(x1, tri, *params)


def _experts_kernel(src_ref, wgid_ref, wvalid_ref, *refs):
    gran_refs = refs[:ITEM_GRANS]
    w1f_ref, w3f_ref, w2f_ref, out_ref, w1_ref, w3_ref, w2_ref = refs[ITEM_GRANS:]
    j = pl.program_id(0)

    @pl.when(jnp.logical_or(j == 0, wgid_ref[j] != wgid_ref[jnp.maximum(j - 1, 0)]))
    def _():
        w1_ref[...] = w1f_ref[...].astype(BF16)
        w3_ref[...] = w3f_ref[...].astype(BF16)
        w2_ref[...] = w2f_ref[...].astype(BF16)

    @pl.when(wvalid_ref[j] > 0)
    def _():
        full = jnp.concatenate([g[...] for g in gran_refs], axis=0)
        h = full[:, :D_MODEL]
        comb = (full[:, D_MODEL:D_MODEL + LANES].astype(F32)
                + full[:, D_MODEL + LANES:].astype(F32))
        first = wgid_ref[j] * EXPERTS_PER_GROUP
        lane = lax.broadcasted_iota(jnp.int32, (ITEM_ROWS, LANES), 1)
        hids = []
        for e in range(EXPERTS_PER_GROUP):
            a = jnp.dot(h, w1_ref[e], preferred_element_type=F32)
            b = jnp.dot(h, w3_ref[e], preferred_element_type=F32)
            gate = jnp.sum(jnp.where(lane == first + e, comb, 0.0), axis=-1, keepdims=True)
            hids.append(((a * jax.nn.sigmoid(a)) * b * gate).astype(BF16))
        hid = jnp.concatenate(hids, axis=-1)
        w2g = w2_ref[...].reshape(EXPERTS_PER_GROUP * D_EXPERT, D_MODEL)
        out_ref[...] = jnp.dot(hid, w2g, preferred_element_type=F32).astype(BF16)

    @pl.when(wvalid_ref[j] == 0)
    def _():
        out_ref[...] = jnp.zeros(out_ref.shape, out_ref.dtype)


def _experts(sorted_tok, src, wgid, wvalid, lw, l):
    nitems = wgid.shape[0]
    gran_spec = lambda s: pl.BlockSpec((GRAN, PAYLOAD), lambda j, src, wg, wv: (src[j * ITEM_GRANS + s], 0))
    wspec = lambda shape: pl.BlockSpec(shape, lambda j, src, wg, wv: (l * N_EXPERT_GROUPS + wg[j], 0, 0))
    grid_spec = pltpu.PrefetchScalarGridSpec(
        num_scalar_prefetch=3, grid=(nitems,),
        in_specs=[gran_spec(s) for s in range(ITEM_GRANS)] + [
            wspec((EXPERTS_PER_GROUP, D_MODEL, D_EXPERT)), wspec((EXPERTS_PER_GROUP, D_MODEL, D_EXPERT)),
            wspec((EXPERTS_PER_GROUP, D_EXPERT, D_MODEL))],
        out_specs=pl.BlockSpec((ITEM_ROWS, D_MODEL), lambda j, src, wg, wv: (j, 0)),
        scratch_shapes=[pltpu.VMEM((EXPERTS_PER_GROUP, D_MODEL, D_EXPERT), BF16),
                        pltpu.VMEM((EXPERTS_PER_GROUP, D_MODEL, D_EXPERT), BF16),
                        pltpu.VMEM((EXPERTS_PER_GROUP, D_EXPERT, D_MODEL), BF16)])
    return pl.pallas_call(
        _experts_kernel, grid_spec=grid_spec,
        out_shape=jax.ShapeDtypeStruct((nitems * ITEM_ROWS, D_MODEL), BF16),
        name="moe_experts",
        compiler_params=pltpu.CompilerParams(dimension_semantics=("arbitrary",),
                                             vmem_limit_bytes=VMEM_LIMIT),
    )(src, wgid, wvalid, *([sorted_tok] * ITEM_GRANS), lw['w1'], lw['w3'], lw['w2'])


def _unsort_kernel(pos_ref, x1_ref, dest_ref, *refs, final):
    gran_refs = refs[:TILE_GRANS]
    gfin_ref, out_ref = refs[TILE_GRANS:]
    y = x1_ref[...] + _unsorted(dest_ref[...], gran_refs)
    if not final:
        out_ref[...] = y
    else:
        y = _rms(y, gfin_ref[...])
        bsz, tc, _ = out_ref.shape
        for t in range(tc):
            out_ref[:, t, :] = y[t * bsz:(t + 1) * bsz, :]


def _unsort(x1, dest, expert_out, pos, norm_final, *, tile_off=0, ntiles=None, batch_major=None):
    final = batch_major is not None
    if final:
        batch, t_len, bsz, tc = batch_major
        nchunks = t_len // tc
        out_spec = pl.BlockSpec((bsz, tc, D_MODEL), lambda k, pos: (k // nchunks, k % nchunks, 0))
        out_shape = jax.ShapeDtypeStruct((batch, t_len, D_MODEL), F32)
    else:
        ntiles = x1.shape[0] // TILE
        out_spec = pl.BlockSpec((TILE, D_MODEL), lambda k, pos: (k, 0))
        out_shape = jax.ShapeDtypeStruct(x1.shape, F32)
    gran_spec = lambda q: pl.BlockSpec((GRAN, D_MODEL),
                                       lambda k, pos: (pos[(k + tile_off) * TILE_GRANS + q], 0))
    grid_spec = pltpu.PrefetchScalarGridSpec(
        num_scalar_prefetch=1, grid=(ntiles,),
        in_specs=[pl.BlockSpec((TILE, D_MODEL), lambda k, pos: (k + tile_off, 0)),
                  pl.BlockSpec((TILE, 1), lambda k, pos: (k + tile_off, 0))]
                 + [gran_spec(q) for q in range(TILE_GRANS)]
                 + [pl.BlockSpec((1, D_MODEL), lambda k, pos: (0, 0))],
        out_specs=out_spec)
    return pl.pallas_call(
        functools.partial(_unsort_kernel, final=final), grid_spec=grid_spec,
        out_shape=out_shape, name="moe_unsort",
        compiler_params=pltpu.CompilerParams(dimension_semantics=("arbitrary",),
                                             vmem_limit_bytes=VMEM_LIMIT),
    )(pos, x1, dest, *([expert_out] * TILE_GRANS), norm_final)


def _dispatch_tables(gid):
    ntiles = gid.shape[0]
    ngran = ntiles * TILE_GRANS
    nitems = -(-ngran // ITEM_GRANS) + N_EXPERT_GROUPS
    g = gid[:, 0, :TILE_GRANS].reshape(ngran)
    groups = jnp.arange(N_EXPERT_GROUPS, dtype=jnp.int32)
    onehot = (g[:, None] == groups[None, :]).astype(F32)
    valid = g < N_EXPERT_GROUPS
    idx = jnp.arange(ngran, dtype=jnp.int32)
    earlier = (idx[None, :] < idx[:, None]).astype(F32)
    rank = jnp.dot(earlier, onehot, precision=lax.Precision.HIGHEST).astype(jnp.int32)
    cnt = jnp.sum(onehot, axis=0).astype(jnp.int32)
    pcnt = (cnt + (ITEM_GRANS - 1)) // ITEM_GRANS * ITEM_GRANS
    pstart = jnp.sum(jnp.where(groups[:, None] < groups[None, :], pcnt[:, None], 0), axis=0)
    pend = pstart + pcnt
    pos = jnp.sum(onehot.astype(jnp.int32) * (pstart[None, :] + rank), axis=1)
    slots = jnp.arange(nitems * ITEM_GRANS, dtype=jnp.int32)
    hit = jnp.logical_and(pos[None, :] == slots[:, None], valid[None, :])
    src = jnp.sum(jnp.where(hit, jnp.arange(ngran, dtype=jnp.int32)[None, :], 0), axis=1)
    item_start = jnp.arange(nitems, dtype=jnp.int32) * ITEM_GRANS
    last_group = jnp.max(jnp.where(pcnt > 0, jnp.arange(N_EXPERT_GROUPS, dtype=jnp.int32), 0))
    wgid = jnp.minimum(jnp.sum((item_start[:, None] >= pend[None, :]).astype(jnp.int32), axis=1), last_group)
    wvalid = (item_start < pend[-1]).astype(jnp.int32)
    return src.astype(jnp.int32), wgid.astype(jnp.int32), wvalid, jnp.where(valid, pos, 0).astype(jnp.int32)


def _moe(x1, lw, l, norm_final, final_layouts=None):
    sorted_tok, dest, gid = _route_sort(x1, lw, l)
    src, wgid, wvalid, pos = _dispatch_tables(gid)
    expert_out = _experts(sorted_tok, src, wgid, wvalid, lw, l)
    if final_layouts is None:
        return dest, expert_out, pos
    return [_unsort(x1, dest, expert_out, pos, norm_final, tile_off=off, ntiles=n, batch_major=bm)
            for off, n, bm in final_layouts]


def _router_weights(w_grp, b_grp, w_er, b_er):
    pad = ROUTER_ROWS - N_EXPERTS - N_EXPERT_GROUPS
    w = jnp.pad(jnp.concatenate([w_er, w_grp], axis=-1), ((0, 0), (0, 0), (0, pad)))
    w = jnp.transpose(w, (0, 2, 1))
    b = jnp.pad(jnp.concatenate([b_er, b_grp], axis=-1), ((0, 0), (0, pad)))[:, :, None]
    hi = w.astype(BF16)
    lo = (w - hi.astype(F32)).astype(BF16)
    return hi, lo, b


def _to_time_major(x, nb):
    bt, t, w = x.shape
    return jnp.transpose(x.reshape(nb, bt // nb, t, w), (0, 2, 1, 3)).reshape(nb, t * (bt // nb), w)


def _from_time_major(x, t):
    nb, rows, w = x.shape
    bsz = rows // t
    return jnp.transpose(x.reshape(nb, t, bsz, w), (0, 2, 1, 3)).reshape(nb * bsz, t, w)


def kernel(x_prompt, x_sample, state_ssm_re, state_ssm_im, state_pool, norm_mix, w_in, lam_re, lam_im, log_dt, b_re, b_im, c_re, c_im, d_skip, w_glu, b_glu, w_pool, pool_scale, w_out, norm_ffn, w_grp, b_grp, w_erouter, b_erouter, w1, w3, w2, norm_final):
    depth = w_in.shape[0]
    pb, pt, _ = x_prompt.shape
    sb, st, _ = x_sample.shape
    p_tc = TILE // pb
    p_tiles = pt // p_tc
    s_bsz = TILE // st
    s_nb = sb // s_bsz
    p_rows = pb * pt

    are, aim, wx, cm_re, cm_im = _discretise(lam_re, lam_im, log_dt, b_re, b_im, c_re, c_im)

    wr_hi, wr_lo, b_r = _router_weights(w_grp, b_grp, w_erouter, b_erouter)
    stack_experts = lambda w: w.reshape((depth * N_EXPERTS,) + w.shape[2:])
    vec = lambda a: a[:, None, :]
    lw = dict(norm_mix=vec(norm_mix), w_in=w_in.astype(BF16), wx=wx, are=are, aim=aim, cm_re=cm_re, cm_im=cm_im,
              d_skip=vec(d_skip), w_glu=w_glu.astype(BF16), b_glu=vec(b_glu), w_pool=w_pool.astype(BF16),
              pool_scale=vec(pool_scale), w_out=w_out.astype(BF16), norm_ffn=vec(norm_ffn),
              wr_hi=wr_hi, wr_lo=wr_lo, b_r=b_r,
              w1=stack_experts(w1), w3=stack_experts(w3), w2=stack_experts(w2))
    nfin = norm_final[None]

    p_zero_h = jnp.zeros((1, pb, S5_LANES), F32)
    p_zero_hist = jnp.zeros((1, POOL_HIST * pb, POOL_WIDTH), F32)
    final_layouts = [(0, p_tiles, (pb, pt, pb, p_tc)), (p_tiles, s_nb, (sb, st, s_bsz, st))]

    outs = {k: [] for k in ('p_re', 'p_im', 'p_pool', 's_re', 's_im', 's_pool')}
    x, moe = None, None
    for l in range(depth):
        h0re = state_ssm_re[l].reshape(s_nb, s_bsz, S5_LANES)
        h0im = state_ssm_im[l].reshape(s_nb, s_bsz, S5_LANES)
        hist = jnp.pad(_to_time_major(state_pool[l], s_nb), ((0, 0), ((POOL_HIST - POOL_BUF) * s_bsz, 0), (0, 0)))
        xs1, hre, him, pool = _mixer(x_sample if l == 0 else x, h0re, h0im, hist, lw, l, bsz=s_bsz, tc=st,
                                     past=POOL_BUF, nb=s_nb, nchunks=1, halves=1, in_off=p_tiles, moe=moe)
        outs['s_re'].append(hre.reshape(sb, S5_GROUPS, S5_STATE))
        outs['s_im'].append(him.reshape(sb, S5_GROUPS, S5_STATE))
        outs['s_pool'].append(_from_time_major(pool, POOL_BUF))
        x1, hre, him, pool = _mixer(x_prompt if l == 0 else x, p_zero_h, p_zero_h, p_zero_hist, lw, l,
                                    bsz=pb, tc=p_tc, past=0, nb=1, nchunks=p_tiles, halves=2, in_off=0, tail=xs1, moe=moe)
        outs['p_re'].append(hre.reshape(pb, S5_GROUPS, S5_STATE))
        outs['p_im'].append(him.reshape(pb, S5_GROUPS, S5_STATE))
        outs['p_pool'].append(_from_time_major(pool, POOL_BUF))
        if l < depth - 1:
            x, moe = x1, _moe(x1, lw, l, nfin)
        else:
            y_prompt, y_sample = _moe(x1, lw, l, nfin, final_layouts)

    return (y_prompt, y_sample, jnp.stack(outs['p_re']), jnp.stack(outs['p_im']), jnp.stack(outs['p_pool']),
            jnp.stack(outs['s_re']), jnp.stack(outs['s_im']), jnp.stack(outs['s_pool']))
```

```python
import functools

import jax
import jax.numpy as jnp
from jax import lax
from jax.experimental import pallas as pl
from jax.experimental.pallas import tpu as pltpu

D_MODEL = 1024
S5_WIDTH = 512
S5_GROUP_CH = 16
S5_GROUPS = 32
S5_STATE = 64
S5_LANES = S5_GROUPS * S5_STATE
POOL_WIDTH = 512
POOL_WINDOWS = (2, 4, 8, 16)
POOL_GROUP_CH = 128
POOL_BUF = 15
POOL_HIST = 16
N_EXPERTS = 16
EXPERTS_PER_GROUP = 4
N_EXPERT_GROUPS = 4
D_EXPERT = 256
EPS = 1e-6

SUBLANES = 8
LANES = 128
S5_KBLOCK = 128
S5_NBLOCK = S5_KBLOCK // S5_GROUP_CH * S5_STATE
SCAN_LANES = 512
VMEM_LIMIT = 56 * 1024 * 1024

TILE = 512
GRAN = 32
TILE_GRANS = TILE // GRAN + N_EXPERT_GROUPS
SORTED_ROWS = TILE_GRANS * GRAN
ITEM_GRANS = 16
ITEM_ROWS = ITEM_GRANS * GRAN
ROUTE_TILES = 2
PAYLOAD = D_MODEL + 2 * LANES

F32 = jnp.float32
BF16 = jnp.bfloat16

MIXER_PARAMS = ('norm_mix', 'w_in', 'wx', 'are', 'aim', 'cm_re', 'cm_im', 'd_skip', 'w_glu', 'b_glu', 'w_pool',
                'pool_scale', 'w_out')


def _layer_spec(arr, l):
    return pl.BlockSpec((None,) + arr.shape[1:], lambda *_: (l,) + (0,) * (arr.ndim - 1))


def _rms(x, g):
    return x * lax.rsqrt(jnp.mean(x * x, axis=-1, keepdims=True) + EPS) * g


def _gelu_tanh(x):
    return 0.5 * x * (1.0 + jnp.tanh(0.7978845608028654 * (x + 0.044715 * (x * x * x))))


def _disc_kernel(lr_ref, li_ref, ldt_ref, br_ref, bi_ref, cr_ref, ci_ref, t64_ref, t16_ref,
                 are_ref, aim_ref, wx_ref, cmre_ref, cmim_ref):
    lr = lr_ref[...]
    li = li_ref[...]
    dt = jnp.exp(ldt_ref[...])
    mag = jnp.exp(lr * dt)
    ab_re = mag * jnp.cos(li * dt)
    ab_im = mag * jnp.sin(li * dt)
    den = lr * lr + li * li
    nr = ab_re - 1.0
    coef_re = (nr * lr + ab_im * li) / den
    coef_im = (ab_im * lr - nr * li) / den
    br = br_ref[...]
    bi = bi_ref[...]
    are_ref[...] = ab_re
    aim_ref[...] = ab_im
    bb_re = (coef_re * br - coef_im * bi).astype(BF16)
    bb_im = (coef_re * bi + coef_im * br).astype(BF16)

    gpb = S5_KBLOCK // S5_GROUP_CH
    rows, lanes = wx_ref.shape[0], S5_NBLOCK
    own = ((lax.broadcasted_iota(jnp.int32, (rows, lanes), 0) // S5_GROUP_CH) % gpb
           == lax.broadcasted_iota(jnp.int32, (rows, lanes), 1) // S5_STATE)
    tile8 = lambda v, t_ref: jnp.dot(v, t_ref[...], preferred_element_type=F32)
    wx_ref[:, :lanes] = jnp.where(own, tile8(bb_re, t64_ref), 0.0).astype(BF16)
    wx_ref[:, lanes:] = jnp.where(own, tile8(bb_im, t64_ref), 0.0).astype(BF16)

    crow, clane = cmre_ref.shape
    cown = ((lax.broadcasted_iota(jnp.int32, (crow, clane), 0) // S5_STATE) % gpb
            == lax.broadcasted_iota(jnp.int32, (crow, clane), 1) // S5_GROUP_CH)
    cmre_ref[...] = jnp.where(cown, tile8(cr_ref[...].astype(BF16), t16_ref), 0.0).astype(BF16)
    cmim_ref[...] = jnp.where(cown, -tile8(ci_ref[...].astype(BF16), t16_ref), 0.0).astype(BF16)


def _discretise(lam_re, lam_im, log_dt, b_re, b_im, c_re, c_im):
    depth = lam_re.shape[0]
    rows = depth * S5_GROUPS * S5_GROUP_CH
    crows = depth * S5_GROUPS * S5_STATE
    gpb = S5_KBLOCK // S5_GROUP_CH
    nblk = S5_GROUPS // gpb
    rep = lambda a: jnp.repeat(a.reshape(depth * S5_GROUPS, -1), S5_GROUP_CH, axis=0)
    tr = lambda b: jnp.transpose(b, (0, 1, 3, 2)).reshape(-1, b.shape[2])
    t64 = jnp.tile(jnp.eye(S5_STATE, dtype=BF16), (1, gpb))
    t16 = jnp.tile(jnp.eye(S5_GROUP_CH, dtype=BF16), (1, gpb))
    are, aim, wx, cm_re, cm_im = pl.pallas_call(
        _disc_kernel, name="s5_discretise",
        out_shape=(jax.ShapeDtypeStruct((rows, S5_STATE), F32), jax.ShapeDtypeStruct((rows, S5_STATE), F32),
                   jax.ShapeDtypeStruct((rows, 2 * S5_NBLOCK), BF16),
                   jax.ShapeDtypeStruct((crows, S5_KBLOCK), BF16), jax.ShapeDtypeStruct((crows, S5_KBLOCK), BF16)),
    )(rep(lam_re), rep(lam_im), rep(log_dt[..., None]), tr(b_re), tr(b_im), tr(c_re), tr(c_im), t64, t16)
    shp = (depth, S5_GROUPS, S5_GROUP_CH, S5_STATE)
    are = are.reshape(shp)[:, :, 0, :].reshape(depth, 1, S5_LANES)
    aim = aim.reshape(shp)[:, :, 0, :].reshape(depth, 1, S5_LANES)
    return (are, aim, wx.reshape(depth, nblk, S5_KBLOCK, 2 * S5_NBLOCK),
            cm_re.reshape(depth, nblk, S5_NBLOCK, S5_KBLOCK), cm_im.reshape(depth, nblk, S5_NBLOCK, S5_KBLOCK))


def _unsorted(dest, gran_refs):
    s = jnp.concatenate([g[...] for g in gran_refs], axis=0)
    slot = lax.broadcasted_iota(jnp.int32, (TILE, SORTED_ROWS), 1).astype(F32)
    perm_t = jnp.where(slot == dest, 1.0, 0.0).astype(BF16)
    return jnp.dot(perm_t, s, preferred_element_type=F32)


MIXER_INPUTS = 4 + len(MIXER_PARAMS)


def _interleave(gens):
    while gens:
        gens = [g for g in gens if next(g, None) is not None]


def _mixer_kernel(*refs, bsz, tc, past, nsteps, halves, has_tail, unsort):
    refs = refs[1:] if unsort else refs
    inputs, rest = refs[:MIXER_INPUTS], refs[MIXER_INPUTS:]
    tail_ref = None
    if has_tail:
        tail_ref, rest = rest[0], rest[1:]
    moe = None
    if unsort:
        ngran = halves * TILE_GRANS
        moe, rest = (rest[0], rest[1:1 + ngran]), rest[1 + ngran:]
    body = functools.partial(_mixer_body, *inputs, *rest, moe=moe, bsz=bsz, tc=tc, past=past,
                             last=nsteps - 1, halves=halves)
    if not has_tail:
        body()
        return
    x1_ref = rest[0]
    i = pl.program_id(1)
    pl.when(i < nsteps)(body)

    @pl.when(i >= nsteps)
    def _():
        x1_ref[...] = tail_ref[...]


def _mixer_body(x_ref, h0re_ref, h0im_ref, hist_ref, gmix_ref, win_ref, wx_ref, are_ref, aim_ref,
                cmre_ref, cmim_ref, dskip_ref, wglu_ref, bglu_ref, wpool_ref, pscale_ref, wout_ref,
                x1_ref, hre_out, him_out, pool_out, xre, xim, sre, sim, hre, him, zbuf,
                *, moe, bsz, tc, past, last, halves):
    i = pl.program_id(1)
    rows = bsz * tc
    hist_rows = POOL_HIST * bsz

    @pl.when(i == 0)
    def _():
        hre[...] = h0re_ref[0]
        him[...] = h0im_ref[0]
        zbuf[0:hist_rows, :] = hist_ref[0]

    def chunk(hf):
        r0 = hf * rows
        rsl = slice(r0, r0 + rows)
        if len(x_ref.shape) == 3:
            x = jnp.concatenate([x_ref[:, hf * tc + t, :] for t in range(tc)], axis=0)
        else:
            x = x_ref[rsl, :]
        if moe is not None:
            dest_ref, gran_refs = moe
            x = x + _unsorted(dest_ref[rsl, :], gran_refs[hf * TILE_GRANS:(hf + 1) * TILE_GRANS])
        h = _rms(x, gmix_ref[...]).astype(BF16)
        yield 1
        proj = jnp.dot(h, win_ref[...], preferred_element_type=F32)
        u = proj[:, :S5_WIDTH]
        z = proj[:, S5_WIDTH:]
        zbuf[hist_rows + r0:hist_rows + r0 + rows, :] = z
        ub = u.astype(BF16)
        yield 2

        step_idx = (i * halves + hf) * tc + lax.broadcasted_iota(jnp.int32, (rows, 1), 0) // bsz
        pos = (step_idx + (past + 1)).astype(F32)
        pooled = []
        for gi, w in enumerate(POOL_WINDOWS):
            gs = slice(gi * POOL_GROUP_CH, (gi + 1) * POOL_GROUP_CH)
            s = zbuf[r0:r0 + hist_rows + rows, gs]
            k = 1
            while k < w:
                s = s[k * bsz:] + s[:-k * bsz]
                k *= 2
            inv_cnt = 1.0 / jnp.minimum(pos, float(w))
            pooled.append((s[-rows:] * inv_cnt - z[:, gs]).astype(BF16))
        yield 3
        pouts = [jnp.dot(pooled[gi], wpool_ref[gi], preferred_element_type=F32)
                 for gi in range(len(POOL_WINDOWS))]
        pool_mixed = (jnp.concatenate(pouts, axis=-1) * pscale_ref[...]).astype(BF16)
        x1_ref[rsl, :] = x + jnp.dot(pool_mixed, wout_ref[S5_WIDTH:, :], preferred_element_type=F32)
        yield 4

        ys = []
        for kb in range(S5_WIDTH // S5_KBLOCK):
            ns = slice(kb * S5_NBLOCK, (kb + 1) * S5_NBLOCK)
            xx = jnp.dot(ub[:, kb * S5_KBLOCK:(kb + 1) * S5_KBLOCK], wx_ref[kb],
                         preferred_element_type=F32)
            xre[rsl, ns] = xx[:, :S5_NBLOCK]
            xim[rsl, ns] = xx[:, S5_NBLOCK:]
            yield 5
            if bsz == SUBLANES:
                scan_rows(r0, 0, ns, True)
            else:
                def body(rb, c, ns=ns):
                    scan_rows(r0, pl.multiple_of(rb * SUBLANES, SUBLANES), ns, False)
                    return c
                lax.fori_loop(0, bsz // SUBLANES, body, 0)
            yield 6
            ys.append(jnp.dot(sre[rsl, ns].astype(BF16), cmre_ref[kb], preferred_element_type=F32)
                      + jnp.dot(sim[rsl, ns].astype(BF16), cmim_ref[kb], preferred_element_type=F32))
        y = jnp.concatenate(ys, axis=-1) + dskip_ref[...] * u
        g = _gelu_tanh(y)
        gb = g.astype(BF16)
        yield 7
        s5_out = g * jax.nn.sigmoid(jnp.dot(gb, wglu_ref[...], preferred_element_type=F32) + bglu_ref[...])
        sb = s5_out.astype(BF16)
        yield 8
        x1_ref[rsl, :] += jnp.dot(sb, wout_ref[:S5_WIDTH, :], preferred_element_type=F32)

    assert SCAN_LANES == S5_NBLOCK

    def scan_rows(r0, b0, ls, static):
        ar = jnp.broadcast_to(are_ref[:, ls], (SUBLANES, SCAN_LANES))
        ai = jnp.broadcast_to(aim_ref[:, ls], (SUBLANES, SCAN_LANES))

        def step(t, carry):
            hr, hi = carry
            row = r0 + t * bsz + b0
            if not static:
                row = pl.multiple_of(row, SUBLANES)
            nr = ar * hr + (xre[pl.ds(row, SUBLANES), ls] - ai * hi)
            ni = ar * hi + (xim[pl.ds(row, SUBLANES), ls] + ai * hr)
            sre[pl.ds(row, SUBLANES), ls] = nr
            sim[pl.ds(row, SUBLANES), ls] = ni
            return nr, ni

        carry = (hre[pl.ds(b0, SUBLANES), ls], him[pl.ds(b0, SUBLANES), ls])
        if static:
            for t in range(tc):
                carry = step(t, carry)
        else:
            carry = lax.fori_loop(0, tc, step, carry, unroll=8)
        hre[pl.ds(b0, SUBLANES), ls] = carry[0]
        him[pl.ds(b0, SUBLANES), ls] = carry[1]

    _interleave([chunk(hf) for hf in range(halves)])

    all_rows = halves * rows

    @pl.when(i == last)
    def _():
        hre_out[0] = hre[...]
        him_out[0] = him[...]
        pool_out[0] = zbuf[all_rows + hist_rows - POOL_BUF * bsz:all_rows + hist_rows, :]

    @pl.when(i != last)
    def _():
        zbuf[0:hist_rows, :] = zbuf[all_rows:all_rows + hist_rows, :]


def _mixer(x, h0re, h0im, hist, lw, l, *, bsz, tc, past, nb, nchunks, halves, in_off, tail=None, moe=None):
    rows = bsz * tc
    assert rows == TILE and nchunks % halves == 0
    blk = halves * rows
    nsteps = nchunks // halves
    ntail = 0 if tail is None else tail.shape[0] // blk
    assert ntail == 0 or (nb == 1 and tail.shape[0] % blk == 0)
    steps = nsteps + ntail
    step = lambda i: jnp.minimum(i, nsteps - 1)
    perb = lambda shape: pl.BlockSpec(shape, lambda b, i, *_: (b,) + (0,) * (len(shape) - 1))
    in_blk = lambda b, i: in_off + b * nsteps + step(i)
    if x.ndim == 2:
        x_spec = pl.BlockSpec((blk, D_MODEL), lambda b, i, *_: (in_blk(b, i), 0))
    else:
        x_spec = pl.BlockSpec((bsz, halves * tc, D_MODEL), lambda b, i, *_: (b, step(i), 0))
    in_specs = [
        x_spec,
        perb((1, bsz, S5_LANES)), perb((1, bsz, S5_LANES)), perb((1, POOL_HIST * bsz, POOL_WIDTH)),
    ] + [_layer_spec(lw[k], l) for k in MIXER_PARAMS]
    args = [x, h0re, h0im, hist] + [lw[k] for k in MIXER_PARAMS]
    assert len(args) == MIXER_INPUTS
    if ntail:
        in_specs.append(pl.BlockSpec((blk, D_MODEL), lambda b, i, *_: (jnp.maximum(i - nsteps, 0), 0)))
        args.append(tail)
    prefetch = []
    if moe is not None:
        dest, expert_out, pos = moe
        prefetch = [pos]
        in_specs.append(pl.BlockSpec((blk, 1), lambda b, i, pos: (in_blk(b, i), 0)))
        args.append(dest)
        for hf in range(halves):
            for q in range(TILE_GRANS):
                in_specs.append(pl.BlockSpec(
                    (GRAN, D_MODEL),
                    lambda b, i, pos, hf=hf, q=q: (pos[(in_blk(b, i) * halves + hf) * TILE_GRANS + q], 0)))
                args.append(expert_out)
    out_specs = [
        pl.BlockSpec((blk, D_MODEL), lambda b, i, *_: (b * steps + i, 0)),
        perb((1, bsz, S5_LANES)), perb((1, bsz, S5_LANES)), perb((1, POOL_BUF * bsz, POOL_WIDTH)),
    ]
    out_shape = [
        jax.ShapeDtypeStruct((nb * steps * blk, D_MODEL), F32),
        jax.ShapeDtypeStruct((nb, bsz, S5_LANES), F32),
        jax.ShapeDtypeStruct((nb, bsz, S5_LANES), F32),
        jax.ShapeDtypeStruct((nb, POOL_BUF * bsz, POOL_WIDTH), F32),
    ]
    scratch = [
        pltpu.VMEM((blk, S5_LANES), F32), pltpu.VMEM((blk, S5_LANES), F32),
        pltpu.VMEM((blk, S5_LANES), F32), pltpu.VMEM((blk, S5_LANES), F32),
        pltpu.VMEM((bsz, S5_LANES), F32), pltpu.VMEM((bsz, S5_LANES), F32),
        pltpu.VMEM(((halves * tc + POOL_HIST) * bsz, POOL_WIDTH), F32),
    ]
    grid_spec = pltpu.PrefetchScalarGridSpec(
        num_scalar_prefetch=len(prefetch), grid=(nb, steps), in_specs=in_specs, out_specs=out_specs,
        scratch_shapes=scratch)
    return pl.pallas_call(
        functools.partial(_mixer_kernel, bsz=bsz, tc=tc, past=past, nsteps=nsteps, halves=halves,
                          has_tail=ntail > 0, unsort=moe is not None),
        grid_spec=grid_spec, out_shape=out_shape, name="mixer",
        compiler_params=pltpu.CompilerParams(dimension_semantics=("arbitrary", "arbitrary"),
                                             vmem_limit_bytes=VMEM_LIMIT),
    )(*prefetch, *args)


ROUTER_ROWS = 32


def _dot_nt(a, b):
    return lax.dot_general(a, b, (((1,), (1,)), ((), ())), preferred_element_type=F32)


def _argmax_rows(rows):
    best, idx = rows[0], jnp.zeros_like(rows[0])
    for k in range(1, len(rows)):
        gt = rows[k] > best
        best = jnp.where(gt, rows[k], best)
        idx = jnp.where(gt, float(k), idx)
    return best, idx


def _gating_t(lt):
    row = lambda r: lt[r:r + 1, :]
    g_rows = [row(N_EXPERTS + g) for g in range(N_EXPERT_GROUPS)]
    gmax, gidx = _argmax_rows(g_rows)
    gsum = g_rows[0] * 0.0
    for r in g_rows:
        gsum = gsum + jnp.exp(r - gmax)
    g_w = 1.0 / gsum
    el = []
    for k in range(EXPERTS_PER_GROUP):
        v = row((N_EXPERT_GROUPS - 1) * EXPERTS_PER_GROUP + k)
        for g in range(N_EXPERT_GROUPS - 2, -1, -1):
            v = jnp.where(gidx == float(g), row(g * EXPERTS_PER_GROUP + k), v)
        el.append(v)
    m1, i1 = _argmax_rows(el)
    m2, i2 = _argmax_rows([jnp.where(i1 == float(k), -jnp.inf, el[k]) for k in range(EXPERTS_PER_GROUP)])
    e2 = jnp.exp(m2 - m1)
    den = 1.0 + e2
    first = gidx * float(EXPERTS_PER_GROUP)
    return gidx, first + i1, first + i2, g_w / den, g_w * e2 / den


def _route_sort_kernel(x1_ref, tri_ref, gffn_ref, wrh_ref, wrl_ref, br_ref, sorted_ref, dest_ref, gid_ref):
    _interleave([_route_sort_tile(k, x1_ref, tri_ref, gffn_ref, wrh_ref, wrl_ref, br_ref, sorted_ref, dest_ref,
                                  gid_ref) for k in range(ROUTE_TILES)])


def _route_sort_tile(k, x1_ref, tri_ref, gffn_ref, wrh_ref, wrl_ref, br_ref, sorted_ref, dest_ref, gid_ref):
    x = x1_ref[k * TILE:(k + 1) * TILE, :]
    h2 = _rms(x, gffn_ref[...])
    hi = h2.astype(BF16)
    yield 1
    lt = _dot_nt(wrh_ref[...], hi) + _dot_nt(wrl_ref[...], hi) + br_ref[...]
    yield 2
    gidx, ex1, ex2, gate1, gate2 = _gating_t(lt)
    yield 3

    sub8 = lax.broadcasted_iota(jnp.int32, (SUBLANES, TILE), 0).astype(F32)
    onehot = jnp.where(sub8 == gidx, 1.0, 0.0)
    rank = jnp.dot(onehot.astype(BF16), tri_ref[...], preferred_element_type=F32)
    counts = jnp.sum(onehot, axis=1, keepdims=True)
    padded = jnp.floor((counts + float(GRAN - 1)) * (1.0 / GRAN)) * float(GRAN)
    e0 = padded[0:1, :]
    e1 = e0 + padded[1:2, :]
    e2 = e1 + padded[2:3, :]
    e3 = e2 + padded[3:4, :]
    dest = jnp.where(gidx == 0.0, rank[0:1, :],
                     jnp.where(gidx == 1.0, e0 + rank[1:2, :],
                               jnp.where(gidx == 2.0, e1 + rank[2:3, :], e2 + rank[3:4, :])))
    dest_ref[k * TILE:(k + 1) * TILE, :] = jnp.broadcast_to(dest, (LANES, TILE)).T[:, 0:1]

    slot = lax.broadcasted_iota(jnp.int32, (SORTED_ROWS, TILE), 0).astype(F32)
    perm = jnp.where(slot == dest, 1.0, 0.0).astype(BF16)
    sub = lax.broadcasted_iota(jnp.int32, (N_EXPERTS, TILE), 0).astype(F32)
    comb = jnp.where(sub == ex1, gate1, 0.0) + jnp.where(sub == ex2, gate2, 0.0)
    c_hi = comb.astype(BF16)
    c_lo = (comb - c_hi.astype(F32)).astype(BF16)
    zeros = jnp.zeros((LANES - N_EXPERTS, TILE), BF16)
    gates = jnp.concatenate([c_hi, zeros, c_lo, zeros], axis=0)
    yield 4
    rows = slice(k * SORTED_ROWS, (k + 1) * SORTED_ROWS)
    sorted_ref[rows, :D_MODEL] = jnp.dot(perm, hi, preferred_element_type=F32).astype(BF16)
    sorted_ref[rows, D_MODEL:] = _dot_nt(perm, gates).astype(BF16)

    q = lax.broadcasted_iota(jnp.int32, (1, LANES), 1).astype(F32) * float(GRAN)
    gid = (jnp.where(q >= e0, 1, 0) + jnp.where(q >= e1, 1, 0)
           + jnp.where(q >= e2, 1, 0) + jnp.where(q >= e3, 1, 0))
    gid_ref[k] = gid.astype(jnp.int32)


def _route_sort(x1, lw, l):
    ntiles = x1.shape[0] // TILE
    params = [lw[k] for k in ('norm_ffn', 'wr_hi', 'wr_lo', 'b_r')]
    tri = jnp.triu(jnp.ones((TILE, TILE), BF16), k=1)
    return pl.pallas_call(
        _route_sort_kernel, grid=(ntiles // ROUTE_TILES,),
        in_specs=[pl.BlockSpec((ROUTE_TILES * TILE, D_MODEL), lambda i: (i, 0)),
                  pl.BlockSpec((TILE, TILE), lambda i: (0, 0))] + [_layer_spec(p, l) for p in params],
        out_specs=[pl.BlockSpec((ROUTE_TILES * SORTED_ROWS, PAYLOAD), lambda i: (i, 0)),
                   pl.BlockSpec((ROUTE_TILES * TILE, 1), lambda i: (i, 0)),
                   pl.BlockSpec((ROUTE_TILES, 1, LANES), lambda i: (i, 0, 0))],
        out_shape=[jax.ShapeDtypeStruct((ntiles * SORTED_ROWS, PAYLOAD), BF16),
                   jax.ShapeDtypeStruct((x1.shape[0], 1), F32),
                   jax.ShapeDtypeStruct((ntiles, 1, LANES), jnp.int32)],
        name="moe_route_sort",
        compiler_params=pltpu.CompilerParams(dimension_semantics=("arbitrary",),
                                             vmem_limit_bytes=VMEM_LIMIT),
    )(x1, tri, *params)


def _experts_kernel(src_ref, wgid_ref, wvalid_ref, *refs):
    gran_refs = refs[:ITEM_GRANS]
    w1f_ref, w3f_ref, w2f_ref, out_ref, w1_ref, w3_ref, w2_ref = refs[ITEM_GRANS:]
    j = pl.program_id(0)

    @pl.when(jnp.logical_or(j == 0, wgid_ref[j] != wgid_ref[jnp.maximum(j - 1, 0)]))
    def _():
        w1_ref[...] = w1f_ref[...].astype(BF16)
        w3_ref[...] = w3f_ref[...].astype(BF16)
        w2_ref[...] = w2f_ref[...].astype(BF16)

    @pl.when(wvalid_ref[j] > 0)
    def _():
        full = jnp.concatenate([g[...] for g in gran_refs], axis=0)
        h = full[:, :D_MODEL]
        comb = (full[:, D_MODEL:D_MODEL + LANES].astype(F32)
                + full[:, D_MODEL + LANES:].astype(F32))
        first = wgid_ref[j] * EXPERTS_PER_GROUP
        lane = lax.broadcasted_iota(jnp.int32, (ITEM_ROWS, LANES), 1)
        hids = []
        for e in range(EXPERTS_PER_GROUP):
            a = jnp.dot(h, w1_ref[e], preferred_element_type=F32)
            b = jnp.dot(h, w3_ref[e], preferred_element_type=F32)
            gate = jnp.sum(jnp.where(lane == first + e, comb, 0.0), axis=-1, keepdims=True)
            hids.append(((a * jax.nn.sigmoid(a)) * b * gate).astype(BF16))
        hid = jnp.concatenate(hids, axis=-1)
        w2g = w2_ref[...].reshape(EXPERTS_PER_GROUP * D_EXPERT, D_MODEL)
        out_ref[...] = jnp.dot(hid, w2g, preferred_element_type=F32).astype(BF16)

    @pl.when(wvalid_ref[j] == 0)
    def _():
        out_ref[...] = jnp.zeros(out_ref.shape, out_ref.dtype)


def _experts(sorted_tok, src, wgid, wvalid, lw, l):
    nitems = wgid.shape[0]
    gran_spec = lambda s: pl.BlockSpec((GRAN, PAYLOAD), lambda j, src, wg, wv: (src[j * ITEM_GRANS + s], 0))
    wspec = lambda shape: pl.BlockSpec(shape, lambda j, src, wg, wv: (l * N_EXPERT_GROUPS + wg[j], 0, 0))
    grid_spec = pltpu.PrefetchScalarGridSpec(
        num_scalar_prefetch=3, grid=(nitems,),
        in_specs=[gran_spec(s) for s in range(ITEM_GRANS)] + [
            wspec((EXPERTS_PER_GROUP, D_MODEL, D_EXPERT)), wspec((EXPERTS_PER_GROUP, D_MODEL, D_EXPERT)),
            wspec((EXPERTS_PER_GROUP, D_EXPERT, D_MODEL))],
        out_specs=pl.BlockSpec((ITEM_ROWS, D_MODEL), lambda j, src, wg, wv: (j, 0)),
        scratch_shapes=[pltpu.VMEM((EXPERTS_PER_GROUP, D_MODEL, D_EXPERT), BF16),
                        pltpu.VMEM((EXPERTS_PER_GROUP, D_MODEL, D_EXPERT), BF16),
                        pltpu.VMEM((EXPERTS_PER_GROUP, D_EXPERT, D_MODEL), BF16)])
    return pl.pallas_call(
        _experts_kernel, grid_spec=grid_spec,
        out_shape=jax.ShapeDtypeStruct((nitems * ITEM_ROWS, D_MODEL), BF16),
        name="moe_experts",
        compiler_params=pltpu.CompilerParams(dimension_semantics=("arbitrary",),
                                             vmem_limit_bytes=VMEM_LIMIT),
    )(src, wgid, wvalid, *([sorted_tok] * ITEM_GRANS), lw['w1'], lw['w3'], lw['w2'])


def _unsort_kernel(pos_ref, x1_ref, dest_ref, *refs, final):
    gran_refs = refs[:TILE_GRANS]
    gfin_ref, out_ref = refs[TILE_GRANS:]
    y = x1_ref[...] + _unsorted(dest_ref[...], gran_refs)
    if not final:
        out_ref[...] = y
    else:
        y = _rms(y, gfin_ref[...])
        bsz, tc, _ = out_ref.shape
        for t in range(tc):
            out_ref[:, t, :] = y[t * bsz:(t + 1) * bsz, :]


def _unsort(x1, dest, expert_out, pos, norm_final, *, tile_off=0, ntiles=None, batch_major=None):
    final = batch_major is not None
    if final:
        batch, t_len, bsz, tc = batch_major
        nchunks = t_len // tc
        out_spec = pl.BlockSpec((bsz, tc, D_MODEL), lambda k, pos: (k // nchunks, k % nchunks, 0))
        out_shape = jax.ShapeDtypeStruct((batch, t_len, D_MODEL), F32)
    else:
        ntiles = x1.shape[0] // TILE
        out_spec = pl.BlockSpec((TILE, D_MODEL), lambda k, pos: (k, 0))
        out_shape = jax.ShapeDtypeStruct(x1.shape, F32)
    gran_spec = lambda q: pl.BlockSpec((GRAN, D_MODEL),
                                       lambda k, pos: (pos[(k + tile_off) * TILE_GRANS + q], 0))
    grid_spec = pltpu.PrefetchScalarGridSpec(
        num_scalar_prefetch=1, grid=(ntiles,),
        in_specs=[pl.BlockSpec((TILE, D_MODEL), lambda k, pos: (k + tile_off, 0)),
                  pl.BlockSpec((TILE, 1), lambda k, pos: (k + tile_off, 0))]
                 + [gran_spec(q) for q in range(TILE_GRANS)]
                 + [pl.BlockSpec((1, D_MODEL), lambda k, pos: (0, 0))],
        out_specs=out_spec)
    return pl.pallas_call(
        functools.partial(_unsort_kernel, final=final), grid_spec=grid_spec,
        out_shape=out_shape, name="moe_unsort",
        compiler_params=pltpu.CompilerParams(dimension_semantics=("arbitrary",),
                                             vmem_limit_bytes=VMEM_LIMIT),
    )(pos, x1, dest, *([expert_out] * TILE_GRANS), norm_final)


def _dispatch_tables(gid):
    ntiles = gid.shape[0]
    ngran = ntiles * TILE_GRANS
    nitems = -(-ngran // ITEM_GRANS) + N_EXPERT_GROUPS
    g = gid[:, 0, :TILE_GRANS].reshape(ngran)
    groups = jnp.arange(N_EXPERT_GROUPS, dtype=jnp.int32)
    onehot = (g[:, None] == groups[None, :]).astype(F32)
    valid = g < N_EXPERT_GROUPS
    idx = jnp.arange(ngran, dtype=jnp.int32)
    earlier = (idx[None, :] < idx[:, None]).astype(F32)
    rank = jnp.dot(earlier, onehot, precision=lax.Precision.HIGHEST).astype(jnp.int32)
    cnt = jnp.sum(onehot, axis=0).astype(jnp.int32)
    pcnt = (cnt + (ITEM_GRANS - 1)) // ITEM_GRANS * ITEM_GRANS
    pstart = jnp.sum(jnp.where(groups[:, None] < groups[None, :], pcnt[:, None], 0), axis=0)
    pend = pstart + pcnt
    pos = jnp.sum(onehot.astype(jnp.int32) * (pstart[None, :] + rank), axis=1)
    slots = jnp.arange(nitems * ITEM_GRANS, dtype=jnp.int32)
    hit = jnp.logical_and(pos[None, :] == slots[:, None], valid[None, :])
    src = jnp.sum(jnp.where(hit, jnp.arange(ngran, dtype=jnp.int32)[None, :], 0), axis=1)
    item_start = jnp.arange(nitems, dtype=jnp.int32) * ITEM_GRANS
    last_group = jnp.max(jnp.where(pcnt > 0, jnp.arange(N_EXPERT_GROUPS, dtype=jnp.int32), 0))
    wgid = jnp.minimum(jnp.sum((item_start[:, None] >= pend[None, :]).astype(jnp.int32), axis=1), last_group)
    wvalid = (item_start < pend[-1]).astype(jnp.int32)
    return src.astype(jnp.int32), wgid.astype(jnp.int32), wvalid, jnp.where(valid, pos, 0).astype(jnp.int32)


def _moe(x1, lw, l, norm_final, final_layouts=None):
    sorted_tok, dest, gid = _route_sort(x1, lw, l)
    src, wgid, wvalid, pos = _dispatch_tables(gid)
    expert_out = _experts(sorted_tok, src, wgid, wvalid, lw, l)
    if final_layouts is None:
        return dest, expert_out, pos
    return [_unsort(x1, dest, expert_out, pos, norm_final, tile_off=off, ntiles=n, batch_major=bm)
            for off, n, bm in final_layouts]


def _router_weights(w_grp, b_grp, w_er, b_er):
    pad = ROUTER_ROWS - N_EXPERTS - N_EXPERT_GROUPS
    w = jnp.pad(jnp.concatenate([w_er, w_grp], axis=-1), ((0, 0), (0, 0), (0, pad)))
    w = jnp.transpose(w, (0, 2, 1))
    b = jnp.pad(jnp.concatenate([b_er, b_grp], axis=-1), ((0, 0), (0, pad)))[:, :, None]
    hi = w.astype(BF16)
    lo = (w - hi.astype(F32)).astype(BF16)
    return hi, lo, b


def _to_time_major(x, nb):
    bt, t, w = x.shape
    return jnp.transpose(x.reshape(nb, bt // nb, t, w), (0, 2, 1, 3)).reshape(nb, t * (bt // nb), w)


def _from_time_major(x, t):
    nb, rows, w = x.shape
    bsz = rows // t
    return jnp.transpose(x.reshape(nb, t, bsz, w), (0, 2, 1, 3)).reshape(nb * bsz, t, w)


def kernel(x_prompt, x_sample, state_ssm_re, state_ssm_im, state_pool, norm_mix, w_in, lam_re, lam_im, log_dt, b_re, b_im, c_re, c_im, d_skip, w_glu, b_glu, w_pool, pool_scale, w_out, norm_ffn, w_grp, b_grp, w_erouter, b_erouter, w1, w3, w2, norm_final):
    depth = w_in.shape[0]
    pb, pt, _ = x_prompt.shape
    sb, st, _ = x_sample.shape
    p_tc = TILE // pb
    p_tiles = pt // p_tc
    s_bsz = TILE // st
    s_nb = sb // s_bsz
    p_rows = pb * pt

    are, aim, wx, cm_re, cm_im = _discretise(lam_re, lam_im, log_dt, b_re, b_im, c_re, c_im)

    wr_hi, wr_lo, b_r = _router_weights(w_grp, b_grp, w_erouter, b_erouter)
    stack_experts = lambda w: w.reshape((depth * N_EXPERTS,) + w.shape[2:])
    vec = lambda a: a[:, None, :]
    lw = dict(norm_mix=vec(norm_mix), w_in=w_in.astype(BF16), wx=wx, are=are, aim=aim, cm_re=cm_re, cm_im=cm_im,
              d_skip=vec(d_skip), w_glu=w_glu.astype(BF16), b_glu=vec(b_glu), w_pool=w_pool.astype(BF16),
              pool_scale=vec(pool_scale), w_out=w_out.astype(BF16), norm_ffn=vec(norm_ffn),
              wr_hi=wr_hi, wr_lo=wr_lo, b_r=b_r,
              w1=stack_experts(w1), w3=stack_experts(w3), w2=stack_experts(w2))
    nfin = norm_final[None]

    p_zero_h = jnp.zeros((1, pb, S5_LANES), F32)
    p_zero_hist = jnp.zeros((1, POOL_HIST * pb, POOL_WIDTH), F32)
    final_layouts = [(0, p_tiles, (pb, pt, pb, p_tc)), (p_tiles, s_nb, (sb, st, s_bsz, st))]

    outs = {k: [] for k in ('p_re', 'p_im', 'p_pool', 's_re', 's_im', 's_pool')}
    x, moe = None, None
    for l in range(depth):
        h0re = state_ssm_re[l].reshape(s_nb, s_bsz, S5_LANES)
        h0im = state_ssm_im[l].reshape(s_nb, s_bsz, S5_LANES)
        hist = jnp.pad(_to_time_major(state_pool[l], s_nb), ((0, 0), ((POOL_HIST - POOL_BUF) * s_bsz, 0), (0, 0)))
        xs1, hre, him, pool = _mixer(x_sample if l == 0 else x, h0re, h0im, hist, lw, l, bsz=s_bsz, tc=st,
                                     past=POOL_BUF, nb=s_nb, nchunks=1, halves=1, in_off=p_tiles, moe=moe)
        outs['s_re'].append(hre.reshape(sb, S5_GROUPS, S5_STATE))
        outs['s_im'].append(him.reshape(sb, S5_GROUPS, S5_STATE))
        outs['s_pool'].append(_from_time_major(pool, POOL_BUF))
        x1, hre, him, pool = _mixer(x_prompt if l == 0 else x, p_zero_h, p_zero_h, p_zero_hist, lw, l,
                                    bsz=pb, tc=p_tc, past=0, nb=1, nchunks=p_tiles, halves=2, in_off=0, tail=xs1, moe=moe)
        outs['p_re'].append(hre.reshape(pb, S5_GROUPS, S5_STATE))
        outs['p_im'].append(him.reshape(pb, S5_GROUPS, S5_STATE))
        outs['p_pool'].append(_from_time_major(pool, POOL_BUF))
        if l < depth - 1:
            x, moe = x1, _moe(x1, lw, l, nfin)
        else:
            y_prompt, y_sample = _moe(x1, lw, l, nfin, final_layouts)

    return (y_prompt, y_sample, jnp.stack(outs['p_re']), jnp.stack(outs['p_im']), jnp.stack(outs['p_pool']),
            jnp.stack(outs['s_re']), jnp.stack(outs['s_im']), jnp.stack(outs['s_pool']))
```

```python
import functools

import jax
import jax.numpy as jnp
from jax import lax
from jax.experimental import pallas as pl
from jax.experimental.pallas import tpu as pltpu

D_MODEL = 1024
S5_WIDTH = 512
S5_GROUP_CH = 16
S5_GROUPS = 32
S5_STATE = 64
S5_LANES = S5_GROUPS * S5_STATE
POOL_WIDTH = 512
POOL_WINDOWS = (2, 4, 8, 16)
POOL_GROUP_CH = 128
POOL_BUF = 15
POOL_HIST = 16
N_EXPERTS = 16
EXPERTS_PER_GROUP = 4
N_EXPERT_GROUPS = 4
D_EXPERT = 256
EPS = 1e-6

SUBLANES = 8
LANES = 128
S5_KBLOCK = 128
S5_NBLOCK = S5_KBLOCK // S5_GROUP_CH * S5_STATE
SCAN_LANES = 512
VMEM_LIMIT = 56 * 1024 * 1024

TILE = 512
GRAN = 32
TILE_GRANS = TILE // GRAN + N_EXPERT_GROUPS
SORTED_ROWS = TILE_GRANS * GRAN
ITEM_GRANS = 16
ITEM_ROWS = ITEM_GRANS * GRAN
ROUTE_TILES = 2
UNSORT_TILES = 2
PAYLOAD = D_MODEL + 2 * LANES

F32 = jnp.float32
BF16 = jnp.bfloat16

MIXER_PARAMS = ('norm_mix', 'w_in', 'wx', 'are', 'aim', 'cm_re', 'cm_im', 'd_skip', 'w_glu', 'b_glu', 'w_pool',
                'pool_scale', 'w_out')


def _layer_spec(arr, l):
    return pl.BlockSpec((None,) + arr.shape[1:], lambda *_: (l,) + (0,) * (arr.ndim - 1))


def _rms(x, g):
    return x * lax.rsqrt(jnp.mean(x * x, axis=-1, keepdims=True) + EPS) * g


def _gelu_tanh(x):
    return 0.5 * x * (1.0 + jnp.tanh(0.7978845608028654 * (x + 0.044715 * (x * x * x))))


def _disc_kernel(lr_ref, li_ref, ldt_ref, br_ref, bi_ref, cr_ref, ci_ref, t64_ref, t16_ref,
                 are_ref, aim_ref, wx_ref, cmre_ref, cmim_ref):
    lr = lr_ref[...]
    li = li_ref[...]
    dt = jnp.exp(ldt_ref[...])
    mag = jnp.exp(lr * dt)
    ab_re = mag * jnp.cos(li * dt)
    ab_im = mag * jnp.sin(li * dt)
    den = lr * lr + li * li
    nr = ab_re - 1.0
    coef_re = (nr * lr + ab_im * li) / den
    coef_im = (ab_im * lr - nr * li) / den
    br = br_ref[...]
    bi = bi_ref[...]
    are_ref[...] = ab_re
    aim_ref[...] = ab_im
    bb_re = (coef_re * br - coef_im * bi).astype(BF16)
    bb_im = (coef_re * bi + coef_im * br).astype(BF16)

    gpb = S5_KBLOCK // S5_GROUP_CH
    rows, lanes = wx_ref.shape[0], S5_NBLOCK
    own = ((lax.broadcasted_iota(jnp.int32, (rows, lanes), 0) // S5_GROUP_CH) % gpb
           == lax.broadcasted_iota(jnp.int32, (rows, lanes), 1) // S5_STATE)
    tile8 = lambda v, t_ref: jnp.dot(v, t_ref[...], preferred_element_type=F32)
    wx_ref[:, :lanes] = jnp.where(own, tile8(bb_re, t64_ref), 0.0).astype(BF16)
    wx_ref[:, lanes:] = jnp.where(own, tile8(bb_im, t64_ref), 0.0).astype(BF16)

    crow, clane = cmre_ref.shape
    cown = ((lax.broadcasted_iota(jnp.int32, (crow, clane), 0) // S5_STATE) % gpb
            == lax.broadcasted_iota(jnp.int32, (crow, clane), 1) // S5_GROUP_CH)
    cmre_ref[...] = jnp.where(cown, tile8(cr_ref[...].astype(BF16), t16_ref), 0.0).astype(BF16)
    cmim_ref[...] = jnp.where(cown, -tile8(ci_ref[...].astype(BF16), t16_ref), 0.0).astype(BF16)


def _discretise(lam_re, lam_im, log_dt, b_re, b_im, c_re, c_im):
    depth = lam_re.shape[0]
    rows = depth * S5_GROUPS * S5_GROUP_CH
    crows = depth * S5_GROUPS * S5_STATE
    gpb = S5_KBLOCK // S5_GROUP_CH
    nblk = S5_GROUPS // gpb
    rep = lambda a: jnp.repeat(a.reshape(depth * S5_GROUPS, -1), S5_GROUP_CH, axis=0)
    tr = lambda b: jnp.transpose(b, (0, 1, 3, 2)).reshape(-1, b.shape[2])
    t64 = jnp.tile(jnp.eye(S5_STATE, dtype=BF16), (1, gpb))
    t16 = jnp.tile(jnp.eye(S5_GROUP_CH, dtype=BF16), (1, gpb))
    are, aim, wx, cm_re, cm_im = pl.pallas_call(
        _disc_kernel, name="s5_discretise",
        out_shape=(jax.ShapeDtypeStruct((rows, S5_STATE), F32), jax.ShapeDtypeStruct((rows, S5_STATE), F32),
                   jax.ShapeDtypeStruct((rows, 2 * S5_NBLOCK), BF16),
                   jax.ShapeDtypeStruct((crows, S5_KBLOCK), BF16), jax.ShapeDtypeStruct((crows, S5_KBLOCK), BF16)),
    )(rep(lam_re), rep(lam_im), rep(log_dt[..., None]), tr(b_re), tr(b_im), tr(c_re), tr(c_im), t64, t16)
    shp = (depth, S5_GROUPS, S5_GROUP_CH, S5_STATE)
    are = are.reshape(shp)[:, :, 0, :].reshape(depth, 1, S5_LANES)
    aim = aim.reshape(shp)[:, :, 0, :].reshape(depth, 1, S5_LANES)
    return (are, aim, wx.reshape(depth, nblk, S5_KBLOCK, 2 * S5_NBLOCK),
            cm_re.reshape(depth, nblk, S5_NBLOCK, S5_KBLOCK), cm_im.reshape(depth, nblk, S5_NBLOCK, S5_KBLOCK))


def _unsorted(dest, gran_refs):
    s = jnp.concatenate([g[...] for g in gran_refs], axis=0)
    slot = lax.broadcasted_iota(jnp.int32, (TILE, SORTED_ROWS), 1).astype(F32)
    perm_t = jnp.where(slot == dest, 1.0, 0.0).astype(BF16)
    return jnp.dot(perm_t, s, preferred_element_type=F32)


MIXER_INPUTS = 4 + len(MIXER_PARAMS)


def _interleave(gens):
    while gens:
        gens = [g for g in gens if next(g, None) is not None]


def _mixer_kernel(*refs, bsz, tc, past, nsteps, halves, has_tail, unsort):
    refs = refs[1:] if unsort else refs
    inputs, rest = refs[:MIXER_INPUTS], refs[MIXER_INPUTS:]
    tail_ref = None
    if has_tail:
        tail_ref, rest = rest[0], rest[1:]
    moe = None
    if unsort:
        ngran = halves * TILE_GRANS
        moe, rest = (rest[0], rest[1:1 + ngran]), rest[1 + ngran:]
    body = functools.partial(_mixer_body, *inputs, *rest, moe=moe, bsz=bsz, tc=tc, past=past,
                             last=nsteps - 1, halves=halves)
    if not has_tail:
        body()
        return
    x1_ref = rest[0]
    i = pl.program_id(1)
    pl.when(i < nsteps)(body)

    @pl.when(i >= nsteps)
    def _():
        x1_ref[...] = tail_ref[...]


def _mixer_body(x_ref, h0re_ref, h0im_ref, hist_ref, gmix_ref, win_ref, wx_ref, are_ref, aim_ref,
                cmre_ref, cmim_ref, dskip_ref, wglu_ref, bglu_ref, wpool_ref, pscale_ref, wout_ref,
                x1_ref, hre_out, him_out, pool_out, xre, xim, sre, sim, hre, him, zbuf,
                *, moe, bsz, tc, past, last, halves):
    i = pl.program_id(1)
    rows = bsz * tc
    hist_rows = POOL_HIST * bsz

    @pl.when(i == 0)
    def _():
        hre[...] = h0re_ref[0]
        him[...] = h0im_ref[0]
        zbuf[0:hist_rows, :] = hist_ref[0]

    def chunk(hf):
        r0 = hf * rows
        rsl = slice(r0, r0 + rows)
        if len(x_ref.shape) == 3:
            x = jnp.concatenate([x_ref[:, hf * tc + t, :] for t in range(tc)], axis=0)
        else:
            x = x_ref[rsl, :]
        if moe is not None:
            dest_ref, gran_refs = moe
            x = x + _unsorted(dest_ref[rsl, :], gran_refs[hf * TILE_GRANS:(hf + 1) * TILE_GRANS])
        h = _rms(x, gmix_ref[...]).astype(BF16)
        yield 1
        proj = jnp.dot(h, win_ref[...], preferred_element_type=F32)
        u = proj[:, :S5_WIDTH]
        z = proj[:, S5_WIDTH:]
        zbuf[hist_rows + r0:hist_rows + r0 + rows, :] = z
        ub = u.astype(BF16)
        yield 2

        step_idx = (i * halves + hf) * tc + lax.broadcasted_iota(jnp.int32, (rows, 1), 0) // bsz
        pos = (step_idx + (past + 1)).astype(F32)
        pooled = []
        for gi, w in enumerate(POOL_WINDOWS):
            gs = slice(gi * POOL_GROUP_CH, (gi + 1) * POOL_GROUP_CH)
            s = zbuf[r0:r0 + hist_rows + rows, gs]
            k = 1
            while k < w:
                s = s[k * bsz:] + s[:-k * bsz]
                k *= 2
            inv_cnt = 1.0 / jnp.minimum(pos, float(w))
            pooled.append((s[-rows:] * inv_cnt - z[:, gs]).astype(BF16))
        yield 3
        pouts = [jnp.dot(pooled[gi], wpool_ref[gi], preferred_element_type=F32)
                 for gi in range(len(POOL_WINDOWS))]
        pool_mixed = (jnp.concatenate(pouts, axis=-1) * pscale_ref[...]).astype(BF16)
        x1_ref[rsl, :] = x + jnp.dot(pool_mixed, wout_ref[S5_WIDTH:, :], preferred_element_type=F32)
        yield 4

        ys = []
        for kb in range(S5_WIDTH // S5_KBLOCK):
            ns = slice(kb * S5_NBLOCK, (kb + 1) * S5_NBLOCK)
            xx = jnp.dot(ub[:, kb * S5_KBLOCK:(kb + 1) * S5_KBLOCK], wx_ref[kb],
                         preferred_element_type=F32)
            xre[rsl, ns] = xx[:, :S5_NBLOCK]
            xim[rsl, ns] = xx[:, S5_NBLOCK:]
            yield 5
            if bsz == SUBLANES:
                scan_rows(r0, 0, ns, True)
            else:
                def body(rb, c, ns=ns):
                    scan_rows(r0, pl.multiple_of(rb * SUBLANES, SUBLANES), ns, False)
                    return c
                lax.fori_loop(0, bsz // SUBLANES, body, 0)
            yield 6
            ys.append(jnp.dot(sre[rsl, ns].astype(BF16), cmre_ref[kb], preferred_element_type=F32)
                      + jnp.dot(sim[rsl, ns].astype(BF16), cmim_ref[kb], preferred_element_type=F32))
        y = jnp.concatenate(ys, axis=-1) + dskip_ref[...] * u
        g = _gelu_tanh(y)
        gb = g.astype(BF16)
        yield 7
        s5_out = g * jax.nn.sigmoid(jnp.dot(gb, wglu_ref[...], preferred_element_type=F32) + bglu_ref[...])
        sb = s5_out.astype(BF16)
        yield 8
        x1_ref[rsl, :] += jnp.dot(sb, wout_ref[:S5_WIDTH, :], preferred_element_type=F32)

    assert SCAN_LANES == S5_NBLOCK

    def scan_rows(r0, b0, ls, static):
        ar = jnp.broadcast_to(are_ref[:, ls], (SUBLANES, SCAN_LANES))
        ai = jnp.broadcast_to(aim_ref[:, ls], (SUBLANES, SCAN_LANES))

        def step(t, carry):
            hr, hi = carry
            row = r0 + t * bsz + b0
            if not static:
                row = pl.multiple_of(row, SUBLANES)
            nr = ar * hr + (xre[pl.ds(row, SUBLANES), ls] - ai * hi)
            ni = ar * hi + (xim[pl.ds(row, SUBLANES), ls] + ai * hr)
            sre[pl.ds(row, SUBLANES), ls] = nr
            sim[pl.ds(row, SUBLANES), ls] = ni
            return nr, ni

        carry = (hre[pl.ds(b0, SUBLANES), ls], him[pl.ds(b0, SUBLANES), ls])
        if static:
            for t in range(tc):
                carry = step(t, carry)
        else:
            carry = lax.fori_loop(0, tc, step, carry, unroll=8)
        hre[pl.ds(b0, SUBLANES), ls] = carry[0]
        him[pl.ds(b0, SUBLANES), ls] = carry[1]

    _interleave([chunk(hf) for hf in range(halves)])

    all_rows = halves * rows

    @pl.when(i == last)
    def _():
        hre_out[0] = hre[...]
        him_out[0] = him[...]
        pool_out[0] = zbuf[all_rows + hist_rows - POOL_BUF * bsz:all_rows + hist_rows, :]

    @pl.when(i != last)
    def _():
        zbuf[0:hist_rows, :] = zbuf[all_rows:all_rows + hist_rows, :]


def _mixer(x, h0re, h0im, hist, lw, l, *, bsz, tc, past, nb, nchunks, halves, in_off, tail=None, moe=None):
    rows = bsz * tc
    assert rows == TILE and nchunks % halves == 0
    blk = halves * rows
    nsteps = nchunks // halves
    ntail = 0 if tail is None else tail.shape[0] // blk
    assert ntail == 0 or (nb == 1 and tail.shape[0] % blk == 0)
    steps = nsteps + ntail
    step = lambda i: jnp.minimum(i, nsteps - 1)
    perb = lambda shape: pl.BlockSpec(shape, lambda b, i, *_: (b,) + (0,) * (len(shape) - 1))
    in_blk = lambda b, i: in_off + b * nsteps + step(i)
    if x.ndim == 2:
        x_spec = pl.BlockSpec((blk, D_MODEL), lambda b, i, *_: (in_blk(b, i), 0))
    else:
        x_spec = pl.BlockSpec((bsz, halves * tc, D_MODEL), lambda b, i, *_: (b, step(i), 0))
    in_specs = [
        x_spec,
        perb((1, bsz, S5_LANES)), perb((1, bsz, S5_LANES)), perb((1, POOL_HIST * bsz, POOL_WIDTH)),
    ] + [_layer_spec(lw[k], l) for k in MIXER_PARAMS]
    args = [x, h0re, h0im, hist] + [lw[k] for k in MIXER_PARAMS]
    assert len(args) == MIXER_INPUTS
    if ntail:
        in_specs.append(pl.BlockSpec((blk, D_MODEL), lambda b, i, *_: (jnp.maximum(i - nsteps, 0), 0)))
        args.append(tail)
    prefetch = []
    if moe is not None:
        dest, expert_out, pos = moe
        prefetch = [pos]
        in_specs.append(pl.BlockSpec((blk, 1), lambda b, i, pos: (in_blk(b, i), 0)))
        args.append(dest)
        for hf in range(halves):
            for q in range(TILE_GRANS):
                in_specs.append(pl.BlockSpec(
                    (GRAN, D_MODEL),
                    lambda b, i, pos, hf=hf, q=q: (pos[(in_blk(b, i) * halves + hf) * TILE_GRANS + q], 0)))
                args.append(expert_out)
    out_specs = [
        pl.BlockSpec((blk, D_MODEL), lambda b, i, *_: (b * steps + i, 0)),
        perb((1, bsz, S5_LANES)), perb((1, bsz, S5_LANES)), perb((1, POOL_BUF * bsz, POOL_WIDTH)),
    ]
    out_shape = [
        jax.ShapeDtypeStruct((nb * steps * blk, D_MODEL), F32),
        jax.ShapeDtypeStruct((nb, bsz, S5_LANES), F32),
        jax.ShapeDtypeStruct((nb, bsz, S5_LANES), F32),
        jax.ShapeDtypeStruct((nb, POOL_BUF * bsz, POOL_WIDTH), F32),
    ]
    scratch = [
        pltpu.VMEM((blk, S5_LANES), F32), pltpu.VMEM((blk, S5_LANES), F32),
        pltpu.VMEM((blk, S5_LANES), F32), pltpu.VMEM((blk, S5_LANES), F32),
        pltpu.VMEM((bsz, S5_LANES), F32), pltpu.VMEM((bsz, S5_LANES), F32),
        pltpu.VMEM(((halves * tc + POOL_HIST) * bsz, POOL_WIDTH), F32),
    ]
    grid_spec = pltpu.PrefetchScalarGridSpec(
        num_scalar_prefetch=len(prefetch), grid=(nb, steps), in_specs=in_specs, out_specs=out_specs,
        scratch_shapes=scratch)
    return pl.pallas_call(
        functools.partial(_mixer_kernel, bsz=bsz, tc=tc, past=past, nsteps=nsteps, halves=halves,
                          has_tail=ntail > 0, unsort=moe is not None),
        grid_spec=grid_spec, out_shape=out_shape, name="mixer",
        compiler_params=pltpu.CompilerParams(dimension_semantics=("arbitrary", "arbitrary"),
                                             vmem_limit_bytes=VMEM_LIMIT),
    )(*prefetch, *args)


ROUTER_ROWS = 32


def _dot_nt(a, b):
    return lax.dot_general(a, b, (((1,), (1,)), ((), ())), preferred_element_type=F32)


def _argmax_rows(rows):
    best, idx = rows[0], jnp.zeros_like(rows[0])
    for k in range(1, len(rows)):
        gt = rows[k] > best
        best = jnp.where(gt, rows[k], best)
        idx = jnp.where(gt, float(k), idx)
    return best, idx


def _gating_t(lt):
    row = lambda r: lt[r:r + 1, :]
    g_rows = [row(N_EXPERTS + g) for g in range(N_EXPERT_GROUPS)]
    gmax, gidx = _argmax_rows(g_rows)
    gsum = g_rows[0] * 0.0
    for r in g_rows:
        gsum = gsum + jnp.exp(r - gmax)
    g_w = 1.0 / gsum
    el = []
    for k in range(EXPERTS_PER_GROUP):
        v = row((N_EXPERT_GROUPS - 1) * EXPERTS_PER_GROUP + k)
        for g in range(N_EXPERT_GROUPS - 2, -1, -1):
            v = jnp.where(gidx == float(g), row(g * EXPERTS_PER_GROUP + k), v)
        el.append(v)
    m1, i1 = _argmax_rows(el)
    m2, i2 = _argmax_rows([jnp.where(i1 == float(k), -jnp.inf, el[k]) for k in range(EXPERTS_PER_GROUP)])
    e2 = jnp.exp(m2 - m1)
    den = 1.0 + e2
    first = gidx * float(EXPERTS_PER_GROUP)
    return gidx, first + i1, first + i2, g_w / den, g_w * e2 / den


def _route_sort_kernel(x1_ref, tri_ref, gffn_ref, wrh_ref, wrl_ref, br_ref, sorted_ref, dest_ref, gid_ref):
    _interleave([_route_sort_tile(k, x1_ref, tri_ref, gffn_ref, wrh_ref, wrl_ref, br_ref, sorted_ref, dest_ref,
                                  gid_ref) for k in range(ROUTE_TILES)])


def _route_sort_tile(k, x1_ref, tri_ref, gffn_ref, wrh_ref, wrl_ref, br_ref, sorted_ref, dest_ref, gid_ref):
    x = x1_ref[k * TILE:(k + 1) * TILE, :]
    h2 = _rms(x, gffn_ref[...])
    hi = h2.astype(BF16)
    yield 1
    lt = _dot_nt(wrh_ref[...], hi) + _dot_nt(wrl_ref[...], hi) + br_ref[...]
    yield 2
    gidx, ex1, ex2, gate1, gate2 = _gating_t(lt)
    yield 3

    sub8 = lax.broadcasted_iota(jnp.int32, (SUBLANES, TILE), 0).astype(F32)
    onehot = jnp.where(sub8 == gidx, 1.0, 0.0)
    rank = jnp.dot(onehot.astype(BF16), tri_ref[...], preferred_element_type=F32)
    counts = jnp.sum(onehot, axis=1, keepdims=True)
    padded = jnp.floor((counts + float(GRAN - 1)) * (1.0 / GRAN)) * float(GRAN)
    e0 = padded[0:1, :]
    e1 = e0 + padded[1:2, :]
    e2 = e1 + padded[2:3, :]
    e3 = e2 + padded[3:4, :]
    dest = jnp.where(gidx == 0.0, rank[0:1, :],
                     jnp.where(gidx == 1.0, e0 + rank[1:2, :],
                               jnp.where(gidx == 2.0, e1 + rank[2:3, :], e2 + rank[3:4, :])))
    dest_ref[k * TILE:(k + 1) * TILE, :] = jnp.broadcast_to(dest, (LANES, TILE)).T[:, 0:1]

    slot = lax.broadcasted_iota(jnp.int32, (SORTED_ROWS, TILE), 0).astype(F32)
    perm = jnp.where(slot == dest, 1.0, 0.0).astype(BF16)
    sub = lax.broadcasted_iota(jnp.int32, (N_EXPERTS, TILE), 0).astype(F32)
    comb = jnp.where(sub == ex1, gate1, 0.0) + jnp.where(sub == ex2, gate2, 0.0)
    c_hi = comb.astype(BF16)
    c_lo = (comb - c_hi.astype(F32)).astype(BF16)
    zeros = jnp.zeros((LANES - N_EXPERTS, TILE), BF16)
    gates = jnp.concatenate([c_hi, zeros, c_lo, zeros], axis=0)
    yield 4
    rows = slice(k * SORTED_ROWS, (k + 1) * SORTED_ROWS)
    sorted_ref[rows, :D_MODEL] = jnp.dot(perm, hi, preferred_element_type=F32).astype(BF16)
    sorted_ref[rows, D_MODEL:] = _dot_nt(perm, gates).astype(BF16)

    q = lax.broadcasted_iota(jnp.int32, (1, LANES), 1).astype(F32) * float(GRAN)
    gid = (jnp.where(q >= e0, 1, 0) + jnp.where(q >= e1, 1, 0)
           + jnp.where(q >= e2, 1, 0) + jnp.where(q >= e3, 1, 0))
    gid_ref[k] = gid.astype(jnp.int32)


def _route_sort(x1, lw, l):
    ntiles = x1.shape[0] // TILE
    params = [lw[k] for k in ('norm_ffn', 'wr_hi', 'wr_lo', 'b_r')]
    tri = jnp.triu(jnp.ones((TILE, TILE), BF16), k=1)
    return pl.pallas_call(
        _route_sort_kernel, grid=(ntiles // ROUTE_TILES,),
        in_specs=[pl.BlockSpec((ROUTE_TILES * TILE, D_MODEL), lambda i: (i, 0)),
                  pl.BlockSpec((TILE, TILE), lambda i: (0, 0))] + [_layer_spec(p, l) for p in params],
        out_specs=[pl.BlockSpec((ROUTE_TILES * SORTED_ROWS, PAYLOAD), lambda i: (i, 0)),
                   pl.BlockSpec((ROUTE_TILES * TILE, 1), lambda i: (i, 0)),
                   pl.BlockSpec((ROUTE_TILES, 1, LANES), lambda i: (i, 0, 0))],
        out_shape=[jax.ShapeDtypeStruct((ntiles * SORTED_ROWS, PAYLOAD), BF16),
                   jax.ShapeDtypeStruct((x1.shape[0], 1), F32),
                   jax.ShapeDtypeStruct((ntiles, 1, LANES), jnp.int32)],
        name="moe_route_sort",
        compiler_params=pltpu.CompilerParams(dimension_semantics=("arbitrary",),
                                             vmem_limit_bytes=VMEM_LIMIT),
    )(x1, tri, *params)


def _experts_kernel(src_ref, wgid_ref, wvalid_ref, *refs):
    gran_refs = refs[:ITEM_GRANS]
    w1f_ref, w3f_ref, w2f_ref, out_ref, w1_ref, w3_ref, w2_ref = refs[ITEM_GRANS:]
    j = pl.program_id(0)

    @pl.when(jnp.logical_or(j == 0, wgid_ref[j] != wgid_ref[jnp.maximum(j - 1, 0)]))
    def _():
        w1_ref[...] = w1f_ref[...].astype(BF16)
        w3_ref[...] = w3f_ref[...].astype(BF16)
        w2_ref[...] = w2f_ref[...].astype(BF16)

    @pl.when(wvalid_ref[j] > 0)
    def _():
        full = jnp.concatenate([g[...] for g in gran_refs], axis=0)
        h = full[:, :D_MODEL]
        comb = (full[:, D_MODEL:D_MODEL + LANES].astype(F32)
                + full[:, D_MODEL + LANES:].astype(F32))
        first = wgid_ref[j] * EXPERTS_PER_GROUP
        lane = lax.broadcasted_iota(jnp.int32, (ITEM_ROWS, LANES), 1)
        hids = []
        for e in range(EXPERTS_PER_GROUP):
            a = jnp.dot(h, w1_ref[e], preferred_element_type=F32)
            b = jnp.dot(h, w3_ref[e], preferred_element_type=F32)
            gate = jnp.sum(jnp.where(lane == first + e, comb, 0.0), axis=-1, keepdims=True)
            hids.append(((a * jax.nn.sigmoid(a)) * b * gate).astype(BF16))
        hid = jnp.concatenate(hids, axis=-1)
        w2g = w2_ref[...].reshape(EXPERTS_PER_GROUP * D_EXPERT, D_MODEL)
        out_ref[...] = jnp.dot(hid, w2g, preferred_element_type=F32).astype(BF16)

    @pl.when(wvalid_ref[j] == 0)
    def _():
        out_ref[...] = jnp.zeros(out_ref.shape, out_ref.dtype)


def _experts(sorted_tok, src, wgid, wvalid, lw, l):
    nitems = wgid.shape[0]
    gran_spec = lambda s: pl.BlockSpec((GRAN, PAYLOAD), lambda j, src, wg, wv: (src[j * ITEM_GRANS + s], 0))
    wspec = lambda shape: pl.BlockSpec(shape, lambda j, src, wg, wv: (l * N_EXPERT_GROUPS + wg[j], 0, 0))
    grid_spec = pltpu.PrefetchScalarGridSpec(
        num_scalar_prefetch=3, grid=(nitems,),
        in_specs=[gran_spec(s) for s in range(ITEM_GRANS)] + [
            wspec((EXPERTS_PER_GROUP, D_MODEL, D_EXPERT)), wspec((EXPERTS_PER_GROUP, D_MODEL, D_EXPERT)),
            wspec((EXPERTS_PER_GROUP, D_EXPERT, D_MODEL))],
        out_specs=pl.BlockSpec((ITEM_ROWS, D_MODEL), lambda j, src, wg, wv: (j, 0)),
        scratch_shapes=[pltpu.VMEM((EXPERTS_PER_GROUP, D_MODEL, D_EXPERT), BF16),
                        pltpu.VMEM((EXPERTS_PER_GROUP, D_MODEL, D_EXPERT), BF16),
                        pltpu.VMEM((EXPERTS_PER_GROUP, D_EXPERT, D_MODEL), BF16)])
    return pl.pallas_call(
        _experts_kernel, grid_spec=grid_spec,
        out_shape=jax.ShapeDtypeStruct((nitems * ITEM_ROWS, D_MODEL), BF16),
        name="moe_experts",
        compiler_params=pltpu.CompilerParams(dimension_semantics=("arbitrary",),
                                             vmem_limit_bytes=VMEM_LIMIT),
    )(src, wgid, wvalid, *([sorted_tok] * ITEM_GRANS), lw['w1'], lw['w3'], lw['w2'])


def _unsort_kernel(pos_ref, x1_ref, dest_ref, *refs, concat_time):
    ngran = UNSORT_TILES * TILE_GRANS
    gran_refs, (gfin_ref, out_ref) = refs[:ngran], refs[ngran:]
    if concat_time:
        bsz, tc = out_ref.shape[0], out_ref.shape[1] // UNSORT_TILES
    else:
        bsz, tc = out_ref.shape[0] // UNSORT_TILES, out_ref.shape[1]

    def tile(j):
        rsl = slice(j * TILE, (j + 1) * TILE)
        s = jnp.concatenate([g[...] for g in gran_refs[j * TILE_GRANS:(j + 1) * TILE_GRANS]], axis=0)
        slot = lax.broadcasted_iota(jnp.int32, (TILE, SORTED_ROWS), 1).astype(F32)
        perm_t = jnp.where(slot == dest_ref[rsl, :], 1.0, 0.0).astype(BF16)
        yield 1
        y = x1_ref[rsl, :] + jnp.dot(perm_t, s, preferred_element_type=F32)
        yield 2
        y = _rms(y, gfin_ref[...])
        yield 3
        for t in range(tc):
            if concat_time:
                out_ref[:, j * tc + t, :] = y[t * bsz:(t + 1) * bsz, :]
            else:
                out_ref[j * bsz:(j + 1) * bsz, t, :] = y[t * bsz:(t + 1) * bsz, :]

    _interleave([tile(j) for j in range(UNSORT_TILES)])


def _unsort(x1, dest, expert_out, pos, norm_final, *, tile_off, ntiles, batch_major):
    batch, t_len, bsz, tc = batch_major
    nchunks = t_len // tc
    assert ntiles % UNSORT_TILES == 0 and tile_off % UNSORT_TILES == 0
    concat_time = nchunks > 1
    if concat_time:
        assert nchunks % UNSORT_TILES == 0
        spb = nchunks // UNSORT_TILES
        out_spec = pl.BlockSpec((bsz, UNSORT_TILES * tc, D_MODEL), lambda k, pos: (k // spb, k % spb, 0))
    else:
        out_spec = pl.BlockSpec((UNSORT_TILES * bsz, tc, D_MODEL), lambda k, pos: (k, 0, 0))
    blk_off = tile_off // UNSORT_TILES
    gran_spec = lambda j, q: pl.BlockSpec(
        (GRAN, D_MODEL), lambda k, pos: (pos[(k * UNSORT_TILES + j + tile_off) * TILE_GRANS + q], 0))
    grid_spec = pltpu.PrefetchScalarGridSpec(
        num_scalar_prefetch=1, grid=(ntiles // UNSORT_TILES,),
        in_specs=[pl.BlockSpec((UNSORT_TILES * TILE, D_MODEL), lambda k, pos: (k + blk_off, 0)),
                  pl.BlockSpec((UNSORT_TILES * TILE, 1), lambda k, pos: (k + blk_off, 0))]
                 + [gran_spec(j, q) for j in range(UNSORT_TILES) for q in range(TILE_GRANS)]
                 + [pl.BlockSpec((1, D_MODEL), lambda k, pos: (0, 0))],
        out_specs=out_spec)
    return pl.pallas_call(
        functools.partial(_unsort_kernel, concat_time=concat_time), grid_spec=grid_spec,
        out_shape=jax.ShapeDtypeStruct((batch, t_len, D_MODEL), F32), name="moe_unsort",
        compiler_params=pltpu.CompilerParams(dimension_semantics=("arbitrary",),
                                             vmem_limit_bytes=VMEM_LIMIT),
    )(pos, x1, dest, *([expert_out] * (UNSORT_TILES * TILE_GRANS)), norm_final)


def _dispatch_tables(gid):
    ntiles = gid.shape[0]
    ngran = ntiles * TILE_GRANS
    nitems = -(-ngran // ITEM_GRANS) + N_EXPERT_GROUPS
    g = gid[:, 0, :TILE_GRANS].reshape(ngran)
    groups = jnp.arange(N_EXPERT_GROUPS, dtype=jnp.int32)
    onehot = (g[:, None] == groups[None, :]).astype(F32)
    valid = g < N_EXPERT_GROUPS
    idx = jnp.arange(ngran, dtype=jnp.int32)
    earlier = (idx[None, :] < idx[:, None]).astype(F32)
    rank = jnp.dot(earlier, onehot, precision=lax.Precision.HIGHEST).astype(jnp.int32)
    cnt = jnp.sum(onehot, axis=0).astype(jnp.int32)
    pcnt = (cnt + (ITEM_GRANS - 1)) // ITEM_GRANS * ITEM_GRANS
    pstart = jnp.sum(jnp.where(groups[:, None] < groups[None, :], pcnt[:, None], 0), axis=0)
    pend = pstart + pcnt
    pos = jnp.sum(onehot.astype(jnp.int32) * (pstart[None, :] + rank), axis=1)
    slots = jnp.arange(nitems * ITEM_GRANS, dtype=jnp.int32)
    hit = jnp.logical_and(pos[None, :] == slots[:, None], valid[None, :])
    src = jnp.sum(jnp.where(hit, jnp.arange(ngran, dtype=jnp.int32)[None, :], 0), axis=1)
    item_start = jnp.arange(nitems, dtype=jnp.int32) * ITEM_GRANS
    last_group = jnp.max(jnp.where(pcnt > 0, jnp.arange(N_EXPERT_GROUPS, dtype=jnp.int32), 0))
    wgid = jnp.minimum(jnp.sum((item_start[:, None] >= pend[None, :]).astype(jnp.int32), axis=1), last_group)
    wvalid = (item_start < pend[-1]).astype(jnp.int32)
    return src.astype(jnp.int32), wgid.astype(jnp.int32), wvalid, jnp.where(valid, pos, 0).astype(jnp.int32)


def _moe(x1, lw, l, norm_final, final_layouts=None):
    sorted_tok, dest, gid = _route_sort(x1, lw, l)
    src, wgid, wvalid, pos = _dispatch_tables(gid)
    expert_out = _experts(sorted_tok, src, wgid, wvalid, lw, l)
    if final_layouts is None:
        return dest, expert_out, pos
    return [_unsort(x1, dest, expert_out, pos, norm_final, tile_off=off, ntiles=n, batch_major=bm)
            for off, n, bm in final_layouts]


def _router_weights(w_grp, b_grp, w_er, b_er):
    pad = ROUTER_ROWS - N_EXPERTS - N_EXPERT_GROUPS
    w = jnp.pad(jnp.concatenate([w_er, w_grp], axis=-1), ((0, 0), (0, 0), (0, pad)))
    w = jnp.transpose(w, (0, 2, 1))
    b = jnp.pad(jnp.concatenate([b_er, b_grp], axis=-1), ((0, 0), (0, pad)))[:, :, None]
    hi = w.astype(BF16)
    lo = (w - hi.astype(F32)).astype(BF16)
    return hi, lo, b


def _to_time_major(x, nb):
    bt, t, w = x.shape
    return jnp.transpose(x.reshape(nb, bt // nb, t, w), (0, 2, 1, 3)).reshape(nb, t * (bt // nb), w)


def _from_time_major(x, t):
    nb, rows, w = x.shape
    bsz = rows // t
    return jnp.transpose(x.reshape(nb, t, bsz, w), (0, 2, 1, 3)).reshape(nb * bsz, t, w)


def kernel(x_prompt, x_sample, state_ssm_re, state_ssm_im, state_pool, norm_mix, w_in, lam_re, lam_im, log_dt, b_re, b_im, c_re, c_im, d_skip, w_glu, b_glu, w_pool, pool_scale, w_out, norm_ffn, w_grp, b_grp, w_erouter, b_erouter, w1, w3, w2, norm_final):
    depth = w_in.shape[0]
    pb, pt, _ = x_prompt.shape
    sb, st, _ = x_sample.shape
    p_tc = TILE // pb
    p_tiles = pt // p_tc
    s_bsz = TILE // st
    s_nb = sb // s_bsz
    p_rows = pb * pt

    are, aim, wx, cm_re, cm_im = _discretise(lam_re, lam_im, log_dt, b_re, b_im, c_re, c_im)

    wr_hi, wr_lo, b_r = _router_weights(w_grp, b_grp, w_erouter, b_erouter)
    stack_experts = lambda w: w.reshape((depth * N_EXPERTS,) + w.shape[2:])
    vec = lambda a: a[:, None, :]
    lw = dict(norm_mix=vec(norm_mix), w_in=w_in.astype(BF16), wx=wx, are=are, aim=aim, cm_re=cm_re, cm_im=cm_im,
              d_skip=vec(d_skip), w_glu=w_glu.astype(BF16), b_glu=vec(b_glu), w_pool=w_pool.astype(BF16),
              pool_scale=vec(pool_scale), w_out=w_out.astype(BF16), norm_ffn=vec(norm_ffn),
              wr_hi=wr_hi, wr_lo=wr_lo, b_r=b_r,
              w1=stack_experts(w1), w3=stack_experts(w3), w2=stack_experts(w2))
    nfin = norm_final[None]

    p_zero_h = jnp.zeros((1, pb, S5_LANES), F32)
    p_zero_hist = jnp.zeros((1, POOL_HIST * pb, POOL_WIDTH), F32)
    final_layouts = [(0, p_tiles, (pb, pt, pb, p_tc)), (p_tiles, s_nb, (sb, st, s_bsz, st))]

    outs = {k: [] for k in ('p_re', 'p_im', 'p_pool', 's_re', 's_im', 's_pool')}
    x, moe = None, None
    for l in range(depth):
        h0re = state_ssm_re[l].reshape(s_nb, s_bsz, S5_LANES)
        h0im = state_ssm_im[l].reshape(s_nb, s_bsz, S5_LANES)
        hist = jnp.pad(_to_time_major(state_pool[l], s_nb), ((0, 0), ((POOL_HIST - POOL_BUF) * s_bsz, 0), (0, 0)))
        xs1, hre, him, pool = _mixer(x_sample if l == 0 else x, h0re, h0im, hist, lw, l, bsz=s_bsz, tc=st,
                                     past=POOL_BUF, nb=s_nb, nchunks=1, halves=1, in_off=p_tiles, moe=moe)
        outs['s_re'].append(hre.reshape(sb, S5_GROUPS, S5_STATE))
        outs['s_im'].append(him.reshape(sb, S5_GROUPS, S5_STATE))
        outs['s_pool'].append(_from_time_major(pool, POOL_BUF))
        x1, hre, him, pool = _mixer(x_prompt if l == 0 else x, p_zero_h, p_zero_h, p_zero_hist, lw, l,
                                    bsz=pb, tc=p_tc, past=0, nb=1, nchunks=p_tiles, halves=2, in_off=0, tail=xs1, moe=moe)
        outs['p_re'].append(hre.reshape(pb, S5_GROUPS, S5_STATE))
        outs['p_im'].append(him.reshape(pb, S5_GROUPS, S5_STATE))
        outs['p_pool'].append(_from_time_major(pool, POOL_BUF))
        if l < depth - 1:
            x, moe = x1, _moe(x1, lw, l, nfin)
        else:
            y_prompt, y_sample = _moe(x1, lw, l, nfin, final_layouts)

    return (y_prompt, y_sample, jnp.stack(outs['p_re']), jnp.stack(outs['p_im']), jnp.stack(outs['p_pool']),
            jnp.stack(outs['s_re']), jnp.stack(outs['s_im']), jnp.stack(outs['s_pool']))
```

```python
import functools

import jax
import jax.numpy as jnp
from jax import lax
from jax.experimental import pallas as pl
from jax.experimental.pallas import tpu as pltpu

D_MODEL = 1024
S5_WIDTH = 512
S5_GROUP_CH = 16
S5_GROUPS = 32
S5_STATE = 64
S5_LANES = S5_GROUPS * S5_STATE
POOL_WIDTH = 512
POOL_WINDOWS = (2, 4, 8, 16)
POOL_GROUP_CH = 128
POOL_BUF = 15
POOL_HIST = 16
N_EXPERTS = 16
EXPERTS_PER_GROUP = 4
N_EXPERT_GROUPS = 4
D_EXPERT = 256
EPS = 1e-6

SUBLANES = 8
LANES = 128
S5_KBLOCK = 128
S5_NBLOCK = S5_KBLOCK // S5_GROUP_CH * S5_STATE
SCAN_LANES = 512
VMEM_LIMIT = 56 * 1024 * 1024

TILE = 512
GRAN = 32
TILE_GRANS = TILE // GRAN + N_EXPERT_GROUPS
SORTED_ROWS = TILE_GRANS * GRAN
ITEM_GRANS = 32
ITEM_ROWS = ITEM_GRANS * GRAN
ROUTE_TILES = 2
UNSORT_TILES = 2
PAYLOAD = D_MODEL + 2 * LANES

F32 = jnp.float32
BF16 = jnp.bfloat16

MIXER_PARAMS = ('norm_mix', 'w_in', 'wx', 'are', 'aim', 'cm_re', 'cm_im', 'd_skip', 'w_glu', 'b_glu', 'w_pool',
                'pool_scale', 'w_out')


def _layer_spec(arr, l):
    return pl.BlockSpec((None,) + arr.shape[1:], lambda *_: (l,) + (0,) * (arr.ndim - 1))


def _rms(x, g):
    return x * lax.rsqrt(jnp.mean(x * x, axis=-1, keepdims=True) + EPS) * g


def _gelu_tanh(x):
    return 0.5 * x * (1.0 + jnp.tanh(0.7978845608028654 * (x + 0.044715 * (x * x * x))))


def _disc_kernel(lr_ref, li_ref, ldt_ref, br_ref, bi_ref, cr_ref, ci_ref, t64_ref, t16_ref,
                 are_ref, aim_ref, wx_ref, cmre_ref, cmim_ref):
    lr = lr_ref[...]
    li = li_ref[...]
    dt = jnp.exp(ldt_ref[...])
    mag = jnp.exp(lr * dt)
    ab_re = mag * jnp.cos(li * dt)
    ab_im = mag * jnp.sin(li * dt)
    den = lr * lr + li * li
    nr = ab_re - 1.0
    coef_re = (nr * lr + ab_im * li) / den
    coef_im = (ab_im * lr - nr * li) / den
    br = br_ref[...]
    bi = bi_ref[...]
    are_ref[...] = ab_re
    aim_ref[...] = ab_im
    bb_re = (coef_re * br - coef_im * bi).astype(BF16)
    bb_im = (coef_re * bi + coef_im * br).astype(BF16)

    gpb = S5_KBLOCK // S5_GROUP_CH
    rows, lanes = wx_ref.shape[0], S5_NBLOCK
    own = ((lax.broadcasted_iota(jnp.int32, (rows, lanes), 0) // S5_GROUP_CH) % gpb
           == lax.broadcasted_iota(jnp.int32, (rows, lanes), 1) // S5_STATE)
    tile8 = lambda v, t_ref: jnp.dot(v, t_ref[...], preferred_element_type=F32)
    wx_ref[:, :lanes] = jnp.where(own, tile8(bb_re, t64_ref), 0.0).astype(BF16)
    wx_ref[:, lanes:] = jnp.where(own, tile8(bb_im, t64_ref), 0.0).astype(BF16)

    crow, clane = cmre_ref.shape
    cown = ((lax.broadcasted_iota(jnp.int32, (crow, clane), 0) // S5_STATE) % gpb
            == lax.broadcasted_iota(jnp.int32, (crow, clane), 1) // S5_GROUP_CH)
    cmre_ref[...] = jnp.where(cown, tile8(cr_ref[...].astype(BF16), t16_ref), 0.0).astype(BF16)
    cmim_ref[...] = jnp.where(cown, -tile8(ci_ref[...].astype(BF16), t16_ref), 0.0).astype(BF16)


def _discretise(lam_re, lam_im, log_dt, b_re, b_im, c_re, c_im):
    depth = lam_re.shape[0]
    rows = depth * S5_GROUPS * S5_GROUP_CH
    crows = depth * S5_GROUPS * S5_STATE
    gpb = S5_KBLOCK // S5_GROUP_CH
    nblk = S5_GROUPS // gpb
    rep = lambda a: jnp.repeat(a.reshape(depth * S5_GROUPS, -1), S5_GROUP_CH, axis=0)
    tr = lambda b: jnp.transpose(b, (0, 1, 3, 2)).reshape(-1, b.shape[2])
    t64 = jnp.tile(jnp.eye(S5_STATE, dtype=BF16), (1, gpb))
    t16 = jnp.tile(jnp.eye(S5_GROUP_CH, dtype=BF16), (1, gpb))
    are, aim, wx, cm_re, cm_im = pl.pallas_call(
        _disc_kernel, name="s5_discretise",
        out_shape=(jax.ShapeDtypeStruct((rows, S5_STATE), F32), jax.ShapeDtypeStruct((rows, S5_STATE), F32),
                   jax.ShapeDtypeStruct((rows, 2 * S5_NBLOCK), BF16),
                   jax.ShapeDtypeStruct((crows, S5_KBLOCK), BF16), jax.ShapeDtypeStruct((crows, S5_KBLOCK), BF16)),
    )(rep(lam_re), rep(lam_im), rep(log_dt[..., None]), tr(b_re), tr(b_im), tr(c_re), tr(c_im), t64, t16)
    shp = (depth, S5_GROUPS, S5_GROUP_CH, S5_STATE)
    are = are.reshape(shp)[:, :, 0, :].reshape(depth, 1, S5_LANES)
    aim = aim.reshape(shp)[:, :, 0, :].reshape(depth, 1, S5_LANES)
    return (are, aim, wx.reshape(depth, nblk, S5_KBLOCK, 2 * S5_NBLOCK),
            cm_re.reshape(depth, nblk, S5_NBLOCK, S5_KBLOCK), cm_im.reshape(depth, nblk, S5_NBLOCK, S5_KBLOCK))


def _unsorted(dest, gran_refs):
    s = jnp.concatenate([g[...] for g in gran_refs], axis=0)
    slot = lax.broadcasted_iota(jnp.int32, (TILE, SORTED_ROWS), 1).astype(F32)
    perm_t = jnp.where(slot == dest, 1.0, 0.0).astype(BF16)
    return jnp.dot(perm_t, s, preferred_element_type=F32)


MIXER_INPUTS = 4 + len(MIXER_PARAMS)


def _interleave(gens):
    while gens:
        gens = [g for g in gens if next(g, None) is not None]


def _mixer_kernel(*refs, bsz, tc, past, nsteps, halves, has_tail, unsort):
    refs = refs[1:] if unsort else refs
    inputs, rest = refs[:MIXER_INPUTS], refs[MIXER_INPUTS:]
    tail_ref = None
    if has_tail:
        tail_ref, rest = rest[0], rest[1:]
    moe = None
    if unsort:
        ngran = halves * TILE_GRANS
        moe, rest = (rest[0], rest[1:1 + ngran]), rest[1 + ngran:]
    body = functools.partial(_mixer_body, *inputs, *rest, moe=moe, bsz=bsz, tc=tc, past=past,
                             last=nsteps - 1, halves=halves)
    if not has_tail:
        body()
        return
    x1_ref = rest[0]
    i = pl.program_id(1)
    pl.when(i < nsteps)(body)

    @pl.when(i >= nsteps)
    def _():
        x1_ref[...] = tail_ref[...]


def _mixer_body(x_ref, h0re_ref, h0im_ref, hist_ref, gmix_ref, win_ref, wx_ref, are_ref, aim_ref,
                cmre_ref, cmim_ref, dskip_ref, wglu_ref, bglu_ref, wpool_ref, pscale_ref, wout_ref,
                x1_ref, hre_out, him_out, pool_out, xre, xim, sre, sim, hre, him, zbuf,
                *, moe, bsz, tc, past, last, halves):
    i = pl.program_id(1)
    rows = bsz * tc
    hist_rows = POOL_HIST * bsz

    @pl.when(i == 0)
    def _():
        hre[...] = h0re_ref[0]
        him[...] = h0im_ref[0]
        zbuf[0:hist_rows, :] = hist_ref[0]

    def chunk(hf):
        r0 = hf * rows
        rsl = slice(r0, r0 + rows)
        if len(x_ref.shape) == 3:
            x = jnp.concatenate([x_ref[:, hf * tc + t, :] for t in range(tc)], axis=0)
        else:
            x = x_ref[rsl, :]
        if moe is not None:
            dest_ref, gran_refs = moe
            x = x + _unsorted(dest_ref[rsl, :], gran_refs[hf * TILE_GRANS:(hf + 1) * TILE_GRANS])
        h = _rms(x, gmix_ref[...]).astype(BF16)
        yield 1
        proj = jnp.dot(h, win_ref[...], preferred_element_type=F32)
        u = proj[:, :S5_WIDTH]
        z = proj[:, S5_WIDTH:]
        zbuf[hist_rows + r0:hist_rows + r0 + rows, :] = z
        ub = u.astype(BF16)
        yield 2

        step_idx = (i * halves + hf) * tc + lax.broadcasted_iota(jnp.int32, (rows, 1), 0) // bsz
        pos = (step_idx + (past + 1)).astype(F32)
        pooled = []
        for gi, w in enumerate(POOL_WINDOWS):
            gs = slice(gi * POOL_GROUP_CH, (gi + 1) * POOL_GROUP_CH)
            s = zbuf[r0:r0 + hist_rows + rows, gs]
            k = 1
            while k < w:
                s = s[k * bsz:] + s[:-k * bsz]
                k *= 2
            inv_cnt = 1.0 / jnp.minimum(pos, float(w))
            pooled.append((s[-rows:] * inv_cnt - z[:, gs]).astype(BF16))
        yield 3
        pouts = [jnp.dot(pooled[gi], wpool_ref[gi], preferred_element_type=F32)
                 for gi in range(len(POOL_WINDOWS))]
        pool_mixed = (jnp.concatenate(pouts, axis=-1) * pscale_ref[...]).astype(BF16)
        x1_ref[rsl, :] = x + jnp.dot(pool_mixed, wout_ref[S5_WIDTH:, :], preferred_element_type=F32)
        yield 4

        ys = []
        for kb in range(S5_WIDTH // S5_KBLOCK):
            ns = slice(kb * S5_NBLOCK, (kb + 1) * S5_NBLOCK)
            xx = jnp.dot(ub[:, kb * S5_KBLOCK:(kb + 1) * S5_KBLOCK], wx_ref[kb],
                         preferred_element_type=F32)
            xre[rsl, ns] = xx[:, :S5_NBLOCK]
            xim[rsl, ns] = xx[:, S5_NBLOCK:]
            yield 5
            if bsz == SUBLANES:
                scan_rows(r0, 0, ns, True)
            else:
                def body(rb, c, ns=ns):
                    scan_rows(r0, pl.multiple_of(rb * SUBLANES, SUBLANES), ns, False)
                    return c
                lax.fori_loop(0, bsz // SUBLANES, body, 0)
            yield 6
            ys.append(jnp.dot(sre[rsl, ns].astype(BF16), cmre_ref[kb], preferred_element_type=F32)
                      + jnp.dot(sim[rsl, ns].astype(BF16), cmim_ref[kb], preferred_element_type=F32))
        y = jnp.concatenate(ys, axis=-1) + dskip_ref[...] * u
        g = _gelu_tanh(y)
        gb = g.astype(BF16)
        yield 7
        s5_out = g * jax.nn.sigmoid(jnp.dot(gb, wglu_ref[...], preferred_element_type=F32) + bglu_ref[...])
        sb = s5_out.astype(BF16)
        yield 8
        x1_ref[rsl, :] += jnp.dot(sb, wout_ref[:S5_WIDTH, :], preferred_element_type=F32)

    assert SCAN_LANES == S5_NBLOCK

    def scan_rows(r0, b0, ls, static):
        ar = jnp.broadcast_to(are_ref[:, ls], (SUBLANES, SCAN_LANES))
        ai = jnp.broadcast_to(aim_ref[:, ls], (SUBLANES, SCAN_LANES))

        def step(t, carry):
            hr, hi = carry
            row = r0 + t * bsz + b0
            if not static:
                row = pl.multiple_of(row, SUBLANES)
            nr = ar * hr + (xre[pl.ds(row, SUBLANES), ls] - ai * hi)
            ni = ar * hi + (xim[pl.ds(row, SUBLANES), ls] + ai * hr)
            sre[pl.ds(row, SUBLANES), ls] = nr
            sim[pl.ds(row, SUBLANES), ls] = ni
            return nr, ni

        carry = (hre[pl.ds(b0, SUBLANES), ls], him[pl.ds(b0, SUBLANES), ls])
        if static:
            for t in range(tc):
                carry = step(t, carry)
        else:
            carry = lax.fori_loop(0, tc, step, carry, unroll=8)
        hre[pl.ds(b0, SUBLANES), ls] = carry[0]
        him[pl.ds(b0, SUBLANES), ls] = carry[1]

    _interleave([chunk(hf) for hf in range(halves)])

    all_rows = halves * rows

    @pl.when(i == last)
    def _():
        hre_out[0] = hre[...]
        him_out[0] = him[...]
        pool_out[0] = zbuf[all_rows + hist_rows - POOL_BUF * bsz:all_rows + hist_rows, :]

    @pl.when(i != last)
    def _():
        zbuf[0:hist_rows, :] = zbuf[all_rows:all_rows + hist_rows, :]


def _mixer(x, h0re, h0im, hist, lw, l, *, bsz, tc, past, nb, nchunks, halves, in_off, tail=None, moe=None):
    rows = bsz * tc
    assert rows == TILE and nchunks % halves == 0
    blk = halves * rows
    nsteps = nchunks // halves
    ntail = 0 if tail is None else tail.shape[0] // blk
    assert ntail == 0 or (nb == 1 and tail.shape[0] % blk == 0)
    steps = nsteps + ntail
    step = lambda i: jnp.minimum(i, nsteps - 1)
    perb = lambda shape: pl.BlockSpec(shape, lambda b, i, *_: (b,) + (0,) * (len(shape) - 1))
    in_blk = lambda b, i: in_off + b * nsteps + step(i)
    if x.ndim == 2:
        x_spec = pl.BlockSpec((blk, D_MODEL), lambda b, i, *_: (in_blk(b, i), 0))
    else:
        x_spec = pl.BlockSpec((bsz, halves * tc, D_MODEL), lambda b, i, *_: (b, step(i), 0))
    in_specs = [
        x_spec,
        perb((1, bsz, S5_LANES)), perb((1, bsz, S5_LANES)), perb((1, POOL_HIST * bsz, POOL_WIDTH)),
    ] + [_layer_spec(lw[k], l) for k in MIXER_PARAMS]
    args = [x, h0re, h0im, hist] + [lw[k] for k in MIXER_PARAMS]
    assert len(args) == MIXER_INPUTS
    if ntail:
        in_specs.append(pl.BlockSpec((blk, D_MODEL), lambda b, i, *_: (jnp.maximum(i - nsteps, 0), 0)))
        args.append(tail)
    prefetch = []
    if moe is not None:
        dest, expert_out, pos = moe
        prefetch = [pos]
        in_specs.append(pl.BlockSpec((blk, 1), lambda b, i, pos: (in_blk(b, i), 0)))
        args.append(dest)
        for hf in range(halves):
            for q in range(TILE_GRANS):
                in_specs.append(pl.BlockSpec(
                    (GRAN, D_MODEL),
                    lambda b, i, pos, hf=hf, q=q: (pos[(in_blk(b, i) * halves + hf) * TILE_GRANS + q], 0)))
                args.append(expert_out)
    out_specs = [
        pl.BlockSpec((blk, D_MODEL), lambda b, i, *_: (b * steps + i, 0)),
        perb((1, bsz, S5_LANES)), perb((1, bsz, S5_LANES)), perb((1, POOL_BUF * bsz, POOL_WIDTH)),
    ]
    out_shape = [
        jax.ShapeDtypeStruct((nb * steps * blk, D_MODEL), F32),
        jax.ShapeDtypeStruct((nb, bsz, S5_LANES), F32),
        jax.ShapeDtypeStruct((nb, bsz, S5_LANES), F32),
        jax.ShapeDtypeStruct((nb, POOL_BUF * bsz, POOL_WIDTH), F32),
    ]
    scratch = [
        pltpu.VMEM((blk, S5_LANES), F32), pltpu.VMEM((blk, S5_LANES), F32),
        pltpu.VMEM((blk, S5_LANES), F32), pltpu.VMEM((blk, S5_LANES), F32),
        pltpu.VMEM((bsz, S5_LANES), F32), pltpu.VMEM((bsz, S5_LANES), F32),
        pltpu.VMEM(((halves * tc + POOL_HIST) * bsz, POOL_WIDTH), F32),
    ]
    grid_spec = pltpu.PrefetchScalarGridSpec(
        num_scalar_prefetch=len(prefetch), grid=(nb, steps), in_specs=in_specs, out_specs=out_specs,
        scratch_shapes=scratch)
    return pl.pallas_call(
        functools.partial(_mixer_kernel, bsz=bsz, tc=tc, past=past, nsteps=nsteps, halves=halves,
                          has_tail=ntail > 0, unsort=moe is not None),
        grid_spec=grid_spec, out_shape=out_shape, name="mixer",
        compiler_params=pltpu.CompilerParams(dimension_semantics=("arbitrary", "arbitrary"),
                                             vmem_limit_bytes=VMEM_LIMIT),
    )(*prefetch, *args)


ROUTER_ROWS = 32


def _dot_nt(a, b):
    return lax.dot_general(a, b, (((1,), (1,)), ((), ())), preferred_element_type=F32)


def _argmax_rows(rows):
    best, idx = rows[0], jnp.zeros_like(rows[0])
    for k in range(1, len(rows)):
        gt = rows[k] > best
        best = jnp.where(gt, rows[k], best)
        idx = jnp.where(gt, float(k), idx)
    return best, idx


def _gating_t(lt):
    row = lambda r: lt[r:r + 1, :]
    g_rows = [row(N_EXPERTS + g) for g in range(N_EXPERT_GROUPS)]
    gmax, gidx = _argmax_rows(g_rows)
    gsum = g_rows[0] * 0.0
    for r in g_rows:
        gsum = gsum + jnp.exp(r - gmax)
    g_w = 1.0 / gsum
    el = []
    for k in range(EXPERTS_PER_GROUP):
        v = row((N_EXPERT_GROUPS - 1) * EXPERTS_PER_GROUP + k)
        for g in range(N_EXPERT_GROUPS - 2, -1, -1):
            v = jnp.where(gidx == float(g), row(g * EXPERTS_PER_GROUP + k), v)
        el.append(v)
    m1, i1 = _argmax_rows(el)
    m2, i2 = _argmax_rows([jnp.where(i1 == float(k), -jnp.inf, el[k]) for k in range(EXPERTS_PER_GROUP)])
    e2 = jnp.exp(m2 - m1)
    den = 1.0 + e2
    first = gidx * float(EXPERTS_PER_GROUP)
    return gidx, first + i1, first + i2, g_w / den, g_w * e2 / den


def _route_sort_kernel(x1_ref, tri_ref, gffn_ref, wrh_ref, wrl_ref, br_ref, sorted_ref, dest_ref, gid_ref):
    _interleave([_route_sort_tile(k, x1_ref, tri_ref, gffn_ref, wrh_ref, wrl_ref, br_ref, sorted_ref, dest_ref,
                                  gid_ref) for k in range(ROUTE_TILES)])


def _route_sort_tile(k, x1_ref, tri_ref, gffn_ref, wrh_ref, wrl_ref, br_ref, sorted_ref, dest_ref, gid_ref):
    x = x1_ref[k * TILE:(k + 1) * TILE, :]
    h2 = _rms(x, gffn_ref[...])
    hi = h2.astype(BF16)
    yield 1
    lt = _dot_nt(wrh_ref[...], hi) + _dot_nt(wrl_ref[...], hi) + br_ref[...]
    yield 2
    gidx, ex1, ex2, gate1, gate2 = _gating_t(lt)
    yield 3

    sub8 = lax.broadcasted_iota(jnp.int32, (SUBLANES, TILE), 0).astype(F32)
    onehot = jnp.where(sub8 == gidx, 1.0, 0.0)
    rank = jnp.dot(onehot.astype(BF16), tri_ref[...], preferred_element_type=F32)
    counts = jnp.sum(onehot, axis=1, keepdims=True)
    padded = jnp.floor((counts + float(GRAN - 1)) * (1.0 / GRAN)) * float(GRAN)
    e0 = padded[0:1, :]
    e1 = e0 + padded[1:2, :]
    e2 = e1 + padded[2:3, :]
    e3 = e2 + padded[3:4, :]
    dest = jnp.where(gidx == 0.0, rank[0:1, :],
                     jnp.where(gidx == 1.0, e0 + rank[1:2, :],
                               jnp.where(gidx == 2.0, e1 + rank[2:3, :], e2 + rank[3:4, :])))
    dest_ref[k * TILE:(k + 1) * TILE, :] = jnp.broadcast_to(dest, (LANES, TILE)).T[:, 0:1]

    slot = lax.broadcasted_iota(jnp.int32, (SORTED_ROWS, TILE), 0).astype(F32)
    perm = jnp.where(slot == dest, 1.0, 0.0).astype(BF16)
    sub = lax.broadcasted_iota(jnp.int32, (N_EXPERTS, TILE), 0).astype(F32)
    comb = jnp.where(sub == ex1, gate1, 0.0) + jnp.where(sub == ex2, gate2, 0.0)
    c_hi = comb.astype(BF16)
    c_lo = (comb - c_hi.astype(F32)).astype(BF16)
    zeros = jnp.zeros((LANES - N_EXPERTS, TILE), BF16)
    gates = jnp.concatenate([c_hi, zeros, c_lo, zeros], axis=0)
    yield 4
    rows = slice(k * SORTED_ROWS, (k + 1) * SORTED_ROWS)
    sorted_ref[rows, :D_MODEL] = jnp.dot(perm, hi, preferred_element_type=F32).astype(BF16)
    sorted_ref[rows, D_MODEL:] = _dot_nt(perm, gates).astype(BF16)

    q = lax.broadcasted_iota(jnp.int32, (1, LANES), 1).astype(F32) * float(GRAN)
    gid = (jnp.where(q >= e0, 1, 0) + jnp.where(q >= e1, 1, 0)
           + jnp.where(q >= e2, 1, 0) + jnp.where(q >= e3, 1, 0))
    gid_ref[k] = gid.astype(jnp.int32)


def _route_sort(x1, lw, l):
    ntiles = x1.shape[0] // TILE
    params = [lw[k] for k in ('norm_ffn', 'wr_hi', 'wr_lo', 'b_r')]
    tri = jnp.triu(jnp.ones((TILE, TILE), BF16), k=1)
    return pl.pallas_call(
        _route_sort_kernel, grid=(ntiles // ROUTE_TILES,),
        in_specs=[pl.BlockSpec((ROUTE_TILES * TILE, D_MODEL), lambda i: (i, 0)),
                  pl.BlockSpec((TILE, TILE), lambda i: (0, 0))] + [_layer_spec(p, l) for p in params],
        out_specs=[pl.BlockSpec((ROUTE_TILES * SORTED_ROWS, PAYLOAD), lambda i: (i, 0)),
                   pl.BlockSpec((ROUTE_TILES * TILE, 1), lambda i: (i, 0)),
                   pl.BlockSpec((ROUTE_TILES, 1, LANES), lambda i: (i, 0, 0))],
        out_shape=[jax.ShapeDtypeStruct((ntiles * SORTED_ROWS, PAYLOAD), BF16),
                   jax.ShapeDtypeStruct((x1.shape[0], 1), F32),
                   jax.ShapeDtypeStruct((ntiles, 1, LANES), jnp.int32)],
        name="moe_route_sort",
        compiler_params=pltpu.CompilerParams(dimension_semantics=("arbitrary",),
                                             vmem_limit_bytes=VMEM_LIMIT),
    )(x1, tri, *params)


def _experts_kernel(src_ref, wgid_ref, wvalid_ref, *refs):
    gran_refs = refs[:ITEM_GRANS]
    w1f_ref, w3f_ref, w2f_ref, out_ref, w1_ref, w3_ref, w2_ref = refs[ITEM_GRANS:]
    j = pl.program_id(0)

    @pl.when(jnp.logical_or(j == 0, wgid_ref[j] != wgid_ref[jnp.maximum(j - 1, 0)]))
    def _():
        w1_ref[...] = w1f_ref[...].astype(BF16)
        w3_ref[...] = w3f_ref[...].astype(BF16)
        w2_ref[...] = w2f_ref[...].astype(BF16)

    @pl.when(wvalid_ref[j] > 0)
    def _():
        full = jnp.concatenate([g[...] for g in gran_refs], axis=0)
        h = full[:, :D_MODEL]
        comb = (full[:, D_MODEL:D_MODEL + LANES].astype(F32)
                + full[:, D_MODEL + LANES:].astype(F32))
        first = wgid_ref[j] * EXPERTS_PER_GROUP
        lane = lax.broadcasted_iota(jnp.int32, (ITEM_ROWS, LANES), 1)
        hids = []
        for e in range(EXPERTS_PER_GROUP):
            a = jnp.dot(h, w1_ref[e], preferred_element_type=F32)
            b = jnp.dot(h, w3_ref[e], preferred_element_type=F32)
            gate = jnp.sum(jnp.where(lane == first + e, comb, 0.0), axis=-1, keepdims=True)
            hids.append(((a * jax.nn.sigmoid(a)) * b * gate).astype(BF16))
        hid = jnp.concatenate(hids, axis=-1)
        w2g = w2_ref[...].reshape(EXPERTS_PER_GROUP * D_EXPERT, D_MODEL)
        out_ref[...] = jnp.dot(hid, w2g, preferred_element_type=F32).astype(BF16)

    @pl.when(wvalid_ref[j] == 0)
    def _():
        out_ref[...] = jnp.zeros(out_ref.shape, out_ref.dtype)


def _experts(sorted_tok, src, wgid, wvalid, lw, l):
    nitems = wgid.shape[0]
    gran_spec = lambda s: pl.BlockSpec((GRAN, PAYLOAD), lambda j, src, wg, wv: (src[j * ITEM_GRANS + s], 0))
    wspec = lambda shape: pl.BlockSpec(shape, lambda j, src, wg, wv: (l * N_EXPERT_GROUPS + wg[j], 0, 0))
    grid_spec = pltpu.PrefetchScalarGridSpec(
        num_scalar_prefetch=3, grid=(nitems,),
        in_specs=[gran_spec(s) for s in range(ITEM_GRANS)] + [
            wspec((EXPERTS_PER_GROUP, D_MODEL, D_EXPERT)), wspec((EXPERTS_PER_GROUP, D_MODEL, D_EXPERT)),
            wspec((EXPERTS_PER_GROUP, D_EXPERT, D_MODEL))],
        out_specs=pl.BlockSpec((ITEM_ROWS, D_MODEL), lambda j, src, wg, wv: (j, 0)),
        scratch_shapes=[pltpu.VMEM((EXPERTS_PER_GROUP, D_MODEL, D_EXPERT), BF16),
                        pltpu.VMEM((EXPERTS_PER_GROUP, D_MODEL, D_EXPERT), BF16),
                        pltpu.VMEM((EXPERTS_PER_GROUP, D_EXPERT, D_MODEL), BF16)])
    return pl.pallas_call(
        _experts_kernel, grid_spec=grid_spec,
        out_shape=jax.ShapeDtypeStruct((nitems * ITEM_ROWS, D_MODEL), BF16),
        name="moe_experts",
        compiler_params=pltpu.CompilerParams(dimension_semantics=("arbitrary",),
                                             vmem_limit_bytes=VMEM_LIMIT),
    )(src, wgid, wvalid, *([sorted_tok] * ITEM_GRANS), lw['w1'], lw['w3'], lw['w2'])


def _unsort_kernel(pos_ref, x1_ref, dest_ref, *refs, concat_time):
    ngran = UNSORT_TILES * TILE_GRANS
    gran_refs, (gfin_ref, out_ref) = refs[:ngran], refs[ngran:]
    if concat_time:
        bsz, tc = out_ref.shape[0], out_ref.shape[1] // UNSORT_TILES
    else:
        bsz, tc = out_ref.shape[0] // UNSORT_TILES, out_ref.shape[1]

    def tile(j):
        rsl = slice(j * TILE, (j + 1) * TILE)
        s = jnp.concatenate([g[...] for g in gran_refs[j * TILE_GRANS:(j + 1) * TILE_GRANS]], axis=0)
        slot = lax.broadcasted_iota(jnp.int32, (TILE, SORTED_ROWS), 1).astype(F32)
        perm_t = jnp.where(slot == dest_ref[rsl, :], 1.0, 0.0).astype(BF16)
        yield 1
        y = x1_ref[rsl, :] + jnp.dot(perm_t, s, preferred_element_type=F32)
        yield 2
        y = _rms(y, gfin_ref[...])
        yield 3
        for t in range(tc):
            if concat_time:
                out_ref[:, j * tc + t, :] = y[t * bsz:(t + 1) * bsz, :]
            else:
                out_ref[j * bsz:(j + 1) * bsz, t, :] = y[t * bsz:(t + 1) * bsz, :]

    _interleave([tile(j) for j in range(UNSORT_TILES)])


def _unsort(x1, dest, expert_out, pos, norm_final, *, tile_off, ntiles, batch_major):
    batch, t_len, bsz, tc = batch_major
    nchunks = t_len // tc
    assert ntiles % UNSORT_TILES == 0 and tile_off % UNSORT_TILES == 0
    concat_time = nchunks > 1
    if concat_time:
        assert nchunks % UNSORT_TILES == 0
        spb = nchunks // UNSORT_TILES
        out_spec = pl.BlockSpec((bsz, UNSORT_TILES * tc, D_MODEL), lambda k, pos: (k // spb, k % spb, 0))
    else:
        out_spec = pl.BlockSpec((UNSORT_TILES * bsz, tc, D_MODEL), lambda k, pos: (k, 0, 0))
    blk_off = tile_off // UNSORT_TILES
    gran_spec = lambda j, q: pl.BlockSpec(
        (GRAN, D_MODEL), lambda k, pos: (pos[(k * UNSORT_TILES + j + tile_off) * TILE_GRANS + q], 0))
    grid_spec = pltpu.PrefetchScalarGridSpec(
        num_scalar_prefetch=1, grid=(ntiles // UNSORT_TILES,),
        in_specs=[pl.BlockSpec((UNSORT_TILES * TILE, D_MODEL), lambda k, pos: (k + blk_off, 0)),
                  pl.BlockSpec((UNSORT_TILES * TILE, 1), lambda k, pos: (k + blk_off, 0))]
                 + [gran_spec(j, q) for j in range(UNSORT_TILES) for q in range(TILE_GRANS)]
                 + [pl.BlockSpec((1, D_MODEL), lambda k, pos: (0, 0))],
        out_specs=out_spec)
    return pl.pallas_call(
        functools.partial(_unsort_kernel, concat_time=concat_time), grid_spec=grid_spec,
        out_shape=jax.ShapeDtypeStruct((batch, t_len, D_MODEL), F32), name="moe_unsort",
        compiler_params=pltpu.CompilerParams(dimension_semantics=("arbitrary",),
                                             vmem_limit_bytes=VMEM_LIMIT),
    )(pos, x1, dest, *([expert_out] * (UNSORT_TILES * TILE_GRANS)), norm_final)


def _dispatch_tables(gid):
    ntiles = gid.shape[0]
    ngran = ntiles * TILE_GRANS
    nitems = -(-ngran // ITEM_GRANS) + N_EXPERT_GROUPS
    g = gid[:, 0, :TILE_GRANS].reshape(ngran)
    groups = jnp.arange(N_EXPERT_GROUPS, dtype=jnp.int32)
    onehot = (g[:, None] == groups[None, :]).astype(F32)
    valid = g < N_EXPERT_GROUPS
    idx = jnp.arange(ngran, dtype=jnp.int32)
    earlier = (idx[None, :] < idx[:, None]).astype(F32)
    rank = jnp.dot(earlier, onehot, precision=lax.Precision.HIGHEST).astype(jnp.int32)
    cnt = jnp.sum(onehot, axis=0).astype(jnp.int32)
    pcnt = (cnt + (ITEM_GRANS - 1)) // ITEM_GRANS * ITEM_GRANS
    pstart = jnp.sum(jnp.where(groups[:, None] < groups[None, :], pcnt[:, None], 0), axis=0)
    pend = pstart + pcnt
    pos = jnp.sum(onehot.astype(jnp.int32) * (pstart[None, :] + rank), axis=1)
    slots = jnp.arange(nitems * ITEM_GRANS, dtype=jnp.int32)
    hit = jnp.logical_and(pos[None, :] == slots[:, None], valid[None, :])
    src = jnp.sum(jnp.where(hit, jnp.arange(ngran, dtype=jnp.int32)[None, :], 0), axis=1)
    item_start = jnp.arange(nitems, dtype=jnp.int32) * ITEM_GRANS
    last_group = jnp.max(jnp.where(pcnt > 0, jnp.arange(N_EXPERT_GROUPS, dtype=jnp.int32), 0))
    wgid = jnp.minimum(jnp.sum((item_start[:, None] >= pend[None, :]).astype(jnp.int32), axis=1), last_group)
    wvalid = (item_start < pend[-1]).astype(jnp.int32)
    return src.astype(jnp.int32), wgid.astype(jnp.int32), wvalid, jnp.where(valid, pos, 0).astype(jnp.int32)


def _moe(x1, lw, l, norm_final, final_layouts=None):
    sorted_tok, dest, gid = _route_sort(x1, lw, l)
    src, wgid, wvalid, pos = _dispatch_tables(gid)
    expert_out = _experts(sorted_tok, src, wgid, wvalid, lw, l)
    if final_layouts is None:
        return dest, expert_out, pos
    return [_unsort(x1, dest, expert_out, pos, norm_final, tile_off=off, ntiles=n, batch_major=bm)
            for off, n, bm in final_layouts]


def _router_weights(w_grp, b_grp, w_er, b_er):
    pad = ROUTER_ROWS - N_EXPERTS - N_EXPERT_GROUPS
    w = jnp.pad(jnp.concatenate([w_er, w_grp], axis=-1), ((0, 0), (0, 0), (0, pad)))
    w = jnp.transpose(w, (0, 2, 1))
    b = jnp.pad(jnp.concatenate([b_er, b_grp], axis=-1), ((0, 0), (0, pad)))[:, :, None]
    hi = w.astype(BF16)
    lo = (w - hi.astype(F32)).astype(BF16)
    return hi, lo, b


def _to_time_major(x, nb):
    bt, t, w = x.shape
    return jnp.transpose(x.reshape(nb, bt // nb, t, w), (0, 2, 1, 3)).reshape(nb, t * (bt // nb), w)


def _from_time_major(x, t):
    nb, rows, w = x.shape
    bsz = rows // t
    return jnp.transpose(x.reshape(nb, t, bsz, w), (0, 2, 1, 3)).reshape(nb * bsz, t, w)


def kernel(x_prompt, x_sample, state_ssm_re, state_ssm_im, state_pool, norm_mix, w_in, lam_re, lam_im, log_dt, b_re, b_im, c_re, c_im, d_skip, w_glu, b_glu, w_pool, pool_scale, w_out, norm_ffn, w_grp, b_grp, w_erouter, b_erouter, w1, w3, w2, norm_final):
    depth = w_in.shape[0]
    pb, pt, _ = x_prompt.shape
    sb, st, _ = x_sample.shape
    p_tc = TILE // pb
    p_tiles = pt // p_tc
    s_bsz = TILE // st
    s_nb = sb // s_bsz
    p_rows = pb * pt

    are, aim, wx, cm_re, cm_im = _discretise(lam_re, lam_im, log_dt, b_re, b_im, c_re, c_im)

    wr_hi, wr_lo, b_r = _router_weights(w_grp, b_grp, w_erouter, b_erouter)
    stack_experts = lambda w: w.reshape((depth * N_EXPERTS,) + w.shape[2:])
    vec = lambda a: a[:, None, :]
    lw = dict(norm_mix=vec(norm_mix), w_in=w_in.astype(BF16), wx=wx, are=are, aim=aim, cm_re=cm_re, cm_im=cm_im,
              d_skip=vec(d_skip), w_glu=w_glu.astype(BF16), b_glu=vec(b_glu), w_pool=w_pool.astype(BF16),
              pool_scale=vec(pool_scale), w_out=w_out.astype(BF16), norm_ffn=vec(norm_ffn),
              wr_hi=wr_hi, wr_lo=wr_lo, b_r=b_r,
              w1=stack_experts(w1), w3=stack_experts(w3), w2=stack_experts(w2))
    nfin = norm_final[None]

    p_zero_h = jnp.zeros((1, pb, S5_LANES), F32)
    p_zero_hist = jnp.zeros((1, POOL_HIST * pb, POOL_WIDTH), F32)
    final_layouts = [(0, p_tiles, (pb, pt, pb, p_tc)), (p_tiles, s_nb, (sb, st, s_bsz, st))]

    outs = {k: [] for k in ('p_re', 'p_im', 'p_pool', 's_re', 's_im', 's_pool')}
    x, moe = None, None
    for l in range(depth):
        h0re = state_ssm_re[l].reshape(s_nb, s_bsz, S5_LANES)
        h0im = state_ssm_im[l].reshape(s_nb, s_bsz, S5_LANES)
        hist = jnp.pad(_to_time_major(state_pool[l], s_nb), ((0, 0), ((POOL_HIST - POOL_BUF) * s_bsz, 0), (0, 0)))
        xs1, hre, him, pool = _mixer(x_sample if l == 0 else x, h0re, h0im, hist, lw, l, bsz=s_bsz, tc=st,
                                     past=POOL_BUF, nb=s_nb, nchunks=1, halves=1, in_off=p_tiles, moe=moe)
        outs['s_re'].append(hre.reshape(sb, S5_GROUPS, S5_STATE))
        outs['s_im'].append(him.reshape(sb, S5_GROUPS, S5_STATE))
        outs['s_pool'].append(_from_time_major(pool, POOL_BUF))
        x1, hre, him, pool = _mixer(x_prompt if l == 0 else x, p_zero_h, p_zero_h, p_zero_hist, lw, l,
                                    bsz=pb, tc=p_tc, past=0, nb=1, nchunks=p_tiles, halves=2, in_off=0, tail=xs1, moe=moe)
        outs['p_re'].append(hre.reshape(pb, S5_GROUPS, S5_STATE))
        outs['p_im'].append(him.reshape(pb, S5_GROUPS, S5_STATE))
        outs['p_pool'].append(_from_time_major(pool, POOL_BUF))
        if l < depth - 1:
            x, moe = x1, _moe(x1, lw, l, nfin)
        else:
            y_prompt, y_sample = _moe(x1, lw, l, nfin, final_layouts)

    return (y_prompt, y_sample, jnp.stack(outs['p_re']), jnp.stack(outs['p_im']), jnp.stack(outs['p_pool']),
            jnp.stack(outs['s_re']), jnp.stack(outs['s_im']), jnp.stack(outs['s_pool']))
```

```python
import functools

import jax
import jax.numpy as jnp
from jax import lax
from jax.experimental import pallas as pl
from jax.experimental.pallas import tpu as pltpu

D_MODEL = 1024
S5_WIDTH = 512
S5_GROUP_CH = 16
S5_GROUPS = 32
S5_STATE = 64
S5_LANES = S5_GROUPS * S5_STATE
POOL_WIDTH = 512
POOL_WINDOWS = (2, 4, 8, 16)
POOL_GROUP_CH = 128
POOL_BUF = 15
POOL_HIST = 16
N_EXPERTS = 16
EXPERTS_PER_GROUP = 4
N_EXPERT_GROUPS = 4
D_EXPERT = 256
EPS = 1e-6

SUBLANES = 8
LANES = 128
S5_KBLOCK = 128
S5_NBLOCK = S5_KBLOCK // S5_GROUP_CH * S5_STATE
SCAN_LANES = 512
VMEM_LIMIT = 56 * 1024 * 1024

TILE = 512
GRAN = 32
TILE_GRANS = TILE // GRAN + N_EXPERT_GROUPS
SORTED_ROWS = TILE_GRANS * GRAN
ITEM_GRANS = 16
ITEM_ROWS = ITEM_GRANS * GRAN
ROUTE_TILES = 2
UNSORT_TILES = 2
PAYLOAD = D_MODEL + 2 * LANES

F32 = jnp.float32
BF16 = jnp.bfloat16

MIXER_PARAMS = ('norm_mix', 'w_in', 'wx', 'are', 'aim', 'cm_re', 'cm_im', 'd_skip', 'w_glu', 'b_glu', 'w_pool',
                'pool_scale', 'w_out')


def _layer_spec(arr, l):
    return pl.BlockSpec((None,) + arr.shape[1:], lambda *_: (l,) + (0,) * (arr.ndim - 1))


def _rms(x, g):
    return x * lax.rsqrt(jnp.mean(x * x, axis=-1, keepdims=True) + EPS) * g


def _gelu_tanh(x):
    return 0.5 * x * (1.0 + jnp.tanh(0.7978845608028654 * (x + 0.044715 * (x * x * x))))


def _disc_kernel(lr_ref, li_ref, ldt_ref, br_ref, bi_ref, cr_ref, ci_ref, t64_ref, t16_ref,
                 are_ref, aim_ref, wx_ref, cmre_ref, cmim_ref):
    lr = lr_ref[...]
    li = li_ref[...]
    dt = jnp.exp(ldt_ref[...])
    mag = jnp.exp(lr * dt)
    ab_re = mag * jnp.cos(li * dt)
    ab_im = mag * jnp.sin(li * dt)
    den = lr * lr + li * li
    nr = ab_re - 1.0
    coef_re = (nr * lr + ab_im * li) / den
    coef_im = (ab_im * lr - nr * li) / den
    br = br_ref[...]
    bi = bi_ref[...]
    are_ref[...] = ab_re
    aim_ref[...] = ab_im
    bb_re = (coef_re * br - coef_im * bi).astype(BF16)
    bb_im = (coef_re * bi + coef_im * br).astype(BF16)

    gpb = S5_KBLOCK // S5_GROUP_CH
    rows, lanes = wx_ref.shape[0], S5_NBLOCK
    own = ((lax.broadcasted_iota(jnp.int32, (rows, lanes), 0) // S5_GROUP_CH) % gpb
           == lax.broadcasted_iota(jnp.int32, (rows, lanes), 1) // S5_STATE)
    tile8 = lambda v, t_ref: jnp.dot(v, t_ref[...], preferred_element_type=F32)
    wx_ref[:, :lanes] = jnp.where(own, tile8(bb_re, t64_ref), 0.0).astype(BF16)
    wx_ref[:, lanes:] = jnp.where(own, tile8(bb_im, t64_ref), 0.0).astype(BF16)

    crow, clane = cmre_ref.shape
    cown = ((lax.broadcasted_iota(jnp.int32, (crow, clane), 0) // S5_STATE) % gpb
            == lax.broadcasted_iota(jnp.int32, (crow, clane), 1) // S5_GROUP_CH)
    cmre_ref[...] = jnp.where(cown, tile8(cr_ref[...].astype(BF16), t16_ref), 0.0).astype(BF16)
    cmim_ref[...] = jnp.where(cown, -tile8(ci_ref[...].astype(BF16), t16_ref), 0.0).astype(BF16)


def _discretise(lam_re, lam_im, log_dt, b_re, b_im, c_re, c_im):
    depth = lam_re.shape[0]
    rows = depth * S5_GROUPS * S5_GROUP_CH
    crows = depth * S5_GROUPS * S5_STATE
    gpb = S5_KBLOCK // S5_GROUP_CH
    nblk = S5_GROUPS // gpb
    rep = lambda a: jnp.repeat(a.reshape(depth * S5_GROUPS, -1), S5_GROUP_CH, axis=0)
    tr = lambda b: jnp.transpose(b, (0, 1, 3, 2)).reshape(-1, b.shape[2])
    t64 = jnp.tile(jnp.eye(S5_STATE, dtype=BF16), (1, gpb))
    t16 = jnp.tile(jnp.eye(S5_GROUP_CH, dtype=BF16), (1, gpb))
    are, aim, wx, cm_re, cm_im = pl.pallas_call(
        _disc_kernel, name="s5_discretise",
        out_shape=(jax.ShapeDtypeStruct((rows, S5_STATE), F32), jax.ShapeDtypeStruct((rows, S5_STATE), F32),
                   jax.ShapeDtypeStruct((rows, 2 * S5_NBLOCK), BF16),
                   jax.ShapeDtypeStruct((crows, S5_KBLOCK), BF16), jax.ShapeDtypeStruct((crows, S5_KBLOCK), BF16)),
    )(rep(lam_re), rep(lam_im), rep(log_dt[..., None]), tr(b_re), tr(b_im), tr(c_re), tr(c_im), t64, t16)
    shp = (depth, S5_GROUPS, S5_GROUP_CH, S5_STATE)
    are = are.reshape(shp)[:, :, 0, :].reshape(depth, 1, S5_LANES)
    aim = aim.reshape(shp)[:, :, 0, :].reshape(depth, 1, S5_LANES)
    return (are, aim, wx.reshape(depth, nblk, S5_KBLOCK, 2 * S5_NBLOCK),
            cm_re.reshape(depth, nblk, S5_NBLOCK, S5_KBLOCK), cm_im.reshape(depth, nblk, S5_NBLOCK, S5_KBLOCK))


def _unsorted(dest, gran_refs):
    s = jnp.concatenate([g[...] for g in gran_refs], axis=0)
    slot = lax.broadcasted_iota(jnp.int32, (TILE, SORTED_ROWS), 1).astype(F32)
    perm_t = jnp.where(slot == dest, 1.0, 0.0).astype(BF16)
    return jnp.dot(perm_t, s, preferred_element_type=F32)


MIXER_INPUTS = 4 + len(MIXER_PARAMS)


def _interleave(gens):
    while gens:
        gens = [g for g in gens if next(g, None) is not None]


def _mixer_kernel(*refs, bsz, tc, past, nsteps, halves, has_tail, unsort):
    refs = refs[1:] if unsort else refs
    inputs, rest = refs[:MIXER_INPUTS], refs[MIXER_INPUTS:]
    tail_ref = None
    if has_tail:
        tail_ref, rest = rest[0], rest[1:]
    moe = None
    if unsort:
        ngran = halves * TILE_GRANS
        moe, rest = (rest[0], rest[1:1 + ngran]), rest[1 + ngran:]
    body = functools.partial(_mixer_body, *inputs, *rest, moe=moe, bsz=bsz, tc=tc, past=past,
                             last=nsteps - 1, halves=halves)
    if not has_tail:
        body()
        return
    x1_ref = rest[0]
    i = pl.program_id(1)
    pl.when(i < nsteps)(body)

    @pl.when(i >= nsteps)
    def _():
        x1_ref[...] = tail_ref[...]


def _mixer_body(x_ref, h0re_ref, h0im_ref, hist_ref, gmix_ref, win_ref, wx_ref, are_ref, aim_ref,
                cmre_ref, cmim_ref, dskip_ref, wglu_ref, bglu_ref, wpool_ref, pscale_ref, wout_ref,
                x1_ref, hre_out, him_out, pool_out, xre, xim, sre, sim, hre, him, zbuf,
                *, moe, bsz, tc, past, last, halves):
    i = pl.program_id(1)
    rows = bsz * tc
    hist_rows = POOL_HIST * bsz

    @pl.when(i == 0)
    def _():
        hre[...] = h0re_ref[0]
        him[...] = h0im_ref[0]
        zbuf[0:hist_rows, :] = hist_ref[0]

    def chunk(hf):
        r0 = hf * rows
        rsl = slice(r0, r0 + rows)
        if len(x_ref.shape) == 3:
            x = jnp.concatenate([x_ref[:, hf * tc + t, :] for t in range(tc)], axis=0)
        else:
            x = x_ref[rsl, :]
        if moe is not None:
            dest_ref, gran_refs = moe
            x = x + _unsorted(dest_ref[rsl, :], gran_refs[hf * TILE_GRANS:(hf + 1) * TILE_GRANS])
        h = _rms(x, gmix_ref[...]).astype(BF16)
        yield 1
        proj = jnp.dot(h, win_ref[...], preferred_element_type=F32)
        u = proj[:, :S5_WIDTH]
        z = proj[:, S5_WIDTH:]
        zbuf[hist_rows + r0:hist_rows + r0 + rows, :] = z
        ub = u.astype(BF16)
        yield 2

        step_idx = (i * halves + hf) * tc + lax.broadcasted_iota(jnp.int32, (rows, 1), 0) // bsz
        pos = (step_idx + (past + 1)).astype(F32)
        pooled = []
        for gi, w in enumerate(POOL_WINDOWS):
            gs = slice(gi * POOL_GROUP_CH, (gi + 1) * POOL_GROUP_CH)
            s = zbuf[r0:r0 + hist_rows + rows, gs]
            k = 1
            while k < w:
                s = s[k * bsz:] + s[:-k * bsz]
                k *= 2
            inv_cnt = 1.0 / jnp.minimum(pos, float(w))
            pooled.append((s[-rows:] * inv_cnt - z[:, gs]).astype(BF16))
        yield 3
        pouts = [jnp.dot(pooled[gi], wpool_ref[gi], preferred_element_type=F32)
                 for gi in range(len(POOL_WINDOWS))]
        pool_mixed = (jnp.concatenate(pouts, axis=-1) * pscale_ref[...]).astype(BF16)
        x1_ref[rsl, :] = x + jnp.dot(pool_mixed, wout_ref[S5_WIDTH:, :], preferred_element_type=F32)
        yield 4

        ys = []
        for kb in range(S5_WIDTH // S5_KBLOCK):
            ns = slice(kb * S5_NBLOCK, (kb + 1) * S5_NBLOCK)
            xx = jnp.dot(ub[:, kb * S5_KBLOCK:(kb + 1) * S5_KBLOCK], wx_ref[kb],
                         preferred_element_type=F32)
            xre[rsl, ns] = xx[:, :S5_NBLOCK]
            xim[rsl, ns] = xx[:, S5_NBLOCK:]
            yield 5
            if bsz == SUBLANES:
                scan_rows(r0, 0, ns, True)
            else:
                def body(rb, c, ns=ns):
                    scan_rows(r0, pl.multiple_of(rb * SUBLANES, SUBLANES), ns, False)
                    return c
                lax.fori_loop(0, bsz // SUBLANES, body, 0)
            yield 6
            ys.append(jnp.dot(sre[rsl, ns].astype(BF16), cmre_ref[kb], preferred_element_type=F32)
                      + jnp.dot(sim[rsl, ns].astype(BF16), cmim_ref[kb], preferred_element_type=F32))
        y = jnp.concatenate(ys, axis=-1) + dskip_ref[...] * u
        g = _gelu_tanh(y)
        gb = g.astype(BF16)
        yield 7
        s5_out = g * jax.nn.sigmoid(jnp.dot(gb, wglu_ref[...], preferred_element_type=F32) + bglu_ref[...])
        sb = s5_out.astype(BF16)
        yield 8
        x1_ref[rsl, :] += jnp.dot(sb, wout_ref[:S5_WIDTH, :], preferred_element_type=F32)

    assert SCAN_LANES == S5_NBLOCK

    def scan_rows(r0, b0, ls, static):
        ar = jnp.broadcast_to(are_ref[:, ls], (SUBLANES, SCAN_LANES))
        ai = jnp.broadcast_to(aim_ref[:, ls], (SUBLANES, SCAN_LANES))

        def step(t, carry):
            hr, hi = carry
            row = r0 + t * bsz + b0
            if not static:
                row = pl.multiple_of(row, SUBLANES)
            nr = ar * hr + (xre[pl.ds(row, SUBLANES), ls] - ai * hi)
            ni = ar * hi + (xim[pl.ds(row, SUBLANES), ls] + ai * hr)
            sre[pl.ds(row, SUBLANES), ls] = nr
            sim[pl.ds(row, SUBLANES), ls] = ni
            return nr, ni

        carry = (hre[pl.ds(b0, SUBLANES), ls], him[pl.ds(b0, SUBLANES), ls])
        if static:
            for t in range(tc):
                carry = step(t, carry)
        else:
            carry = lax.fori_loop(0, tc, step, carry, unroll=8)
        hre[pl.ds(b0, SUBLANES), ls] = carry[0]
        him[pl.ds(b0, SUBLANES), ls] = carry[1]

    _interleave([chunk(hf) for hf in range(halves)])

    all_rows = halves * rows

    @pl.when(i == last)
    def _():
        hre_out[0] = hre[...]
        him_out[0] = him[...]
        pool_out[0] = zbuf[all_rows + hist_rows - POOL_BUF * bsz:all_rows + hist_rows, :]

    @pl.when(i != last)
    def _():
        zbuf[0:hist_rows, :] = zbuf[all_rows:all_rows + hist_rows, :]


def _mixer(x, h0re, h0im, hist, lw, l, *, bsz, tc, past, nb, nchunks, halves, in_off, tail=None, moe=None):
    rows = bsz * tc
    assert rows == TILE and nchunks % halves == 0
    blk = halves * rows
    nsteps = nchunks // halves
    ntail = 0 if tail is None else tail.shape[0] // blk
    assert ntail == 0 or (nb == 1 and tail.shape[0] % blk == 0)
    steps = nsteps + ntail
    step = lambda i: jnp.minimum(i, nsteps - 1)
    perb = lambda shape: pl.BlockSpec(shape, lambda b, i, *_: (b,) + (0,) * (len(shape) - 1))
    in_blk = lambda b, i: in_off + b * nsteps + step(i)
    if x.ndim == 2:
        x_spec = pl.BlockSpec((blk, D_MODEL), lambda b, i, *_: (in_blk(b, i), 0))
    else:
        x_spec = pl.BlockSpec((bsz, halves * tc, D_MODEL), lambda b, i, *_: (b, step(i), 0))
    in_specs = [
        x_spec,
        perb((1, bsz, S5_LANES)), perb((1, bsz, S5_LANES)), perb((1, POOL_HIST * bsz, POOL_WIDTH)),
    ] + [_layer_spec(lw[k], l) for k in MIXER_PARAMS]
    args = [x, h0re, h0im, hist] + [lw[k] for k in MIXER_PARAMS]
    assert len(args) == MIXER_INPUTS
    if ntail:
        in_specs.append(pl.BlockSpec((blk, D_MODEL), lambda b, i, *_: (jnp.maximum(i - nsteps, 0), 0)))
        args.append(tail)
    prefetch = []
    if moe is not None:
        dest, expert_out, pos = moe
        prefetch = [pos]
        in_specs.append(pl.BlockSpec((blk, 1), lambda b, i, pos: (in_blk(b, i), 0)))
        args.append(dest)
        for hf in range(halves):
            for q in range(TILE_GRANS):
                in_specs.append(pl.BlockSpec(
                    (GRAN, D_MODEL),
                    lambda b, i, pos, hf=hf, q=q: (pos[(in_blk(b, i) * halves + hf) * TILE_GRANS + q], 0)))
                args.append(expert_out)
    out_specs = [
        pl.BlockSpec((blk, D_MODEL), lambda b, i, *_: (b * steps + i, 0)),
        perb((1, bsz, S5_LANES)), perb((1, bsz, S5_LANES)), perb((1, POOL_BUF * bsz, POOL_WIDTH)),
    ]
    out_shape = [
        jax.ShapeDtypeStruct((nb * steps * blk, D_MODEL), F32),
        jax.ShapeDtypeStruct((nb, bsz, S5_LANES), F32),
        jax.ShapeDtypeStruct((nb, bsz, S5_LANES), F32),
        jax.ShapeDtypeStruct((nb, POOL_BUF * bsz, POOL_WIDTH), F32),
    ]
    scratch = [
        pltpu.VMEM((blk, S5_LANES), F32), pltpu.VMEM((blk, S5_LANES), F32),
        pltpu.VMEM((blk, S5_LANES), F32), pltpu.VMEM((blk, S5_LANES), F32),
        pltpu.VMEM((bsz, S5_LANES), F32), pltpu.VMEM((bsz, S5_LANES), F32),
        pltpu.VMEM(((halves * tc + POOL_HIST) * bsz, POOL_WIDTH), F32),
    ]
    grid_spec = pltpu.PrefetchScalarGridSpec(
        num_scalar_prefetch=len(prefetch), grid=(nb, steps), in_specs=in_specs, out_specs=out_specs,
        scratch_shapes=scratch)
    return pl.pallas_call(
        functools.partial(_mixer_kernel, bsz=bsz, tc=tc, past=past, nsteps=nsteps, halves=halves,
                          has_tail=ntail > 0, unsort=moe is not None),
        grid_spec=grid_spec, out_shape=out_shape, name="mixer",
        compiler_params=pltpu.CompilerParams(dimension_semantics=("arbitrary", "arbitrary"),
                                             vmem_limit_bytes=VMEM_LIMIT),
    )(*prefetch, *args)


ROUTER_ROWS = 32


def _dot_nt(a, b):
    return lax.dot_general(a, b, (((1,), (1,)), ((), ())), preferred_element_type=F32)


def _argmax_rows(rows):
    best, idx = rows[0], jnp.zeros_like(rows[0])
    for k in range(1, len(rows)):
        gt = rows[k] > best
        best = jnp.where(gt, rows[k], best)
        idx = jnp.where(gt, float(k), idx)
    return best, idx


def _gating_t(lt):
    row = lambda r: lt[r:r + 1, :]
    g_rows = [row(N_EXPERTS + g) for g in range(N_EXPERT_GROUPS)]
    gmax, gidx = _argmax_rows(g_rows)
    gsum = g_rows[0] * 0.0
    for r in g_rows:
        gsum = gsum + jnp.exp(r - gmax)
    g_w = 1.0 / gsum
    el = []
    for k in range(EXPERTS_PER_GROUP):
        v = row((N_EXPERT_GROUPS - 1) * EXPERTS_PER_GROUP + k)
        for g in range(N_EXPERT_GROUPS - 2, -1, -1):
            v = jnp.where(gidx == float(g), row(g * EXPERTS_PER_GROUP + k), v)
        el.append(v)
    m1, i1 = _argmax_rows(el)
    m2, i2 = _argmax_rows([jnp.where(i1 == float(k), -jnp.inf, el[k]) for k in range(EXPERTS_PER_GROUP)])
    e2 = jnp.exp(m2 - m1)
    den = 1.0 + e2
    first = gidx * float(EXPERTS_PER_GROUP)
    return gidx, first + i1, first + i2, g_w / den, g_w * e2 / den


def _route_sort_kernel(x1_ref, tri_ref, gffn_ref, wr_ref, br_ref, sorted_ref, dest_ref, gid_ref):
    _interleave([_route_sort_tile(k, x1_ref, tri_ref, gffn_ref, wr_ref, br_ref, sorted_ref, dest_ref, gid_ref)
                 for k in range(ROUTE_TILES)])


def _route_sort_tile(k, x1_ref, tri_ref, gffn_ref, wr_ref, br_ref, sorted_ref, dest_ref, gid_ref):
    x = x1_ref[k * TILE:(k + 1) * TILE, :]
    h2 = _rms(x, gffn_ref[...])
    hi = h2.astype(BF16)
    yield 1
    lt2 = _dot_nt(wr_ref[...], hi)
    lt = lt2[:ROUTER_ROWS, :] + lt2[ROUTER_ROWS:, :] + br_ref[...]
    yield 2
    gidx, ex1, ex2, gate1, gate2 = _gating_t(lt)
    yield 3

    sub8 = lax.broadcasted_iota(jnp.int32, (SUBLANES, TILE), 0).astype(F32)
    onehot = jnp.where(sub8 == gidx, 1.0, 0.0)
    rank = jnp.dot(onehot.astype(BF16), tri_ref[...], preferred_element_type=F32)
    counts = jnp.sum(onehot, axis=1, keepdims=True)
    padded = jnp.floor((counts + float(GRAN - 1)) * (1.0 / GRAN)) * float(GRAN)
    e0 = padded[0:1, :]
    e1 = e0 + padded[1:2, :]
    e2 = e1 + padded[2:3, :]
    e3 = e2 + padded[3:4, :]
    dest = jnp.where(gidx == 0.0, rank[0:1, :],
                     jnp.where(gidx == 1.0, e0 + rank[1:2, :],
                               jnp.where(gidx == 2.0, e1 + rank[2:3, :], e2 + rank[3:4, :])))
    dest_ref[k * TILE:(k + 1) * TILE, :] = jnp.broadcast_to(dest, (LANES, TILE)).T[:, 0:1]

    slot = lax.broadcasted_iota(jnp.int32, (SORTED_ROWS, TILE), 0).astype(F32)
    perm = jnp.where(slot == dest, 1.0, 0.0).astype(BF16)
    sub = lax.broadcasted_iota(jnp.int32, (N_EXPERTS, TILE), 0).astype(F32)
    comb = jnp.where(sub == ex1, gate1, 0.0) + jnp.where(sub == ex2, gate2, 0.0)
    c_hi = comb.astype(BF16)
    c_lo = (comb - c_hi.astype(F32)).astype(BF16)
    zeros = jnp.zeros((LANES - N_EXPERTS, TILE), BF16)
    gates = jnp.concatenate([c_hi, zeros, c_lo, zeros], axis=0)
    yield 4
    rows = slice(k * SORTED_ROWS, (k + 1) * SORTED_ROWS)
    sorted_ref[rows, :D_MODEL] = jnp.dot(perm, hi, preferred_element_type=F32).astype(BF16)
    sorted_ref[rows, D_MODEL:] = _dot_nt(perm, gates).astype(BF16)

    q = lax.broadcasted_iota(jnp.int32, (1, LANES), 1).astype(F32) * float(GRAN)
    gid = (jnp.where(q >= e0, 1, 0) + jnp.where(q >= e1, 1, 0)
           + jnp.where(q >= e2, 1, 0) + jnp.where(q >= e3, 1, 0))
    gid_ref[k] = gid.astype(jnp.int32)


def _route_sort(x1, lw, l):
    ntiles = x1.shape[0] // TILE
    params = [lw[k] for k in ('norm_ffn', 'wr', 'b_r')]
    tri = jnp.triu(jnp.ones((TILE, TILE), BF16), k=1)
    return pl.pallas_call(
        _route_sort_kernel, grid=(ntiles // ROUTE_TILES,),
        in_specs=[pl.BlockSpec((ROUTE_TILES * TILE, D_MODEL), lambda i: (i, 0)),
                  pl.BlockSpec((TILE, TILE), lambda i: (0, 0))] + [_layer_spec(p, l) for p in params],
        out_specs=[pl.BlockSpec((ROUTE_TILES * SORTED_ROWS, PAYLOAD), lambda i: (i, 0)),
                   pl.BlockSpec((ROUTE_TILES * TILE, 1), lambda i: (i, 0)),
                   pl.BlockSpec((ROUTE_TILES, 1, LANES), lambda i: (i, 0, 0))],
        out_shape=[jax.ShapeDtypeStruct((ntiles * SORTED_ROWS, PAYLOAD), BF16),
                   jax.ShapeDtypeStruct((x1.shape[0], 1), F32),
                   jax.ShapeDtypeStruct((ntiles, 1, LANES), jnp.int32)],
        name="moe_route_sort",
        compiler_params=pltpu.CompilerParams(dimension_semantics=("arbitrary",),
                                             vmem_limit_bytes=VMEM_LIMIT),
    )(x1, tri, *params)


def _experts_kernel(src_ref, wgid_ref, wvalid_ref, *refs):
    gran_refs = refs[:ITEM_GRANS]
    w1f_ref, w3f_ref, w2f_ref, out_ref, w1_ref, w3_ref, w2_ref = refs[ITEM_GRANS:]
    j = pl.program_id(0)

    @pl.when(jnp.logical_or(j == 0, wgid_ref[j] != wgid_ref[jnp.maximum(j - 1, 0)]))
    def _():
        w1_ref[...] = w1f_ref[...].astype(BF16)
        w3_ref[...] = w3f_ref[...].astype(BF16)
        w2_ref[...] = w2f_ref[...].astype(BF16)

    @pl.when(wvalid_ref[j] > 0)
    def _():
        full = jnp.concatenate([g[...] for g in gran_refs], axis=0)
        h = full[:, :D_MODEL]
        comb = (full[:, D_MODEL:D_MODEL + LANES].astype(F32)
                + full[:, D_MODEL + LANES:].astype(F32))
        first = wgid_ref[j] * EXPERTS_PER_GROUP
        lane = lax.broadcasted_iota(jnp.int32, (ITEM_ROWS, LANES), 1)
        hids = []
        for e in range(EXPERTS_PER_GROUP):
            a = jnp.dot(h, w1_ref[e], preferred_element_type=F32)
            b = jnp.dot(h, w3_ref[e], preferred_element_type=F32)
            gate = jnp.sum(jnp.where(lane == first + e, comb, 0.0), axis=-1, keepdims=True)
            hids.append(((a * jax.nn.sigmoid(a)) * b * gate).astype(BF16))
        hid = jnp.concatenate(hids, axis=-1)
        w2g = w2_ref[...].reshape(EXPERTS_PER_GROUP * D_EXPERT, D_MODEL)
        out_ref[...] = jnp.dot(hid, w2g, preferred_element_type=F32).astype(BF16)

    @pl.when(wvalid_ref[j] == 0)
    def _():
        out_ref[...] = jnp.zeros(out_ref.shape, out_ref.dtype)


def _experts(sorted_tok, src, wgid, wvalid, lw, l):
    nitems = wgid.shape[0]
    gran_spec = lambda s: pl.BlockSpec((GRAN, PAYLOAD), lambda j, src, wg, wv: (src[j * ITEM_GRANS + s], 0))
    wspec = lambda shape: pl.BlockSpec(shape, lambda j, src, wg, wv: (l * N_EXPERT_GROUPS + wg[j], 0, 0))
    grid_spec = pltpu.PrefetchScalarGridSpec(
        num_scalar_prefetch=3, grid=(nitems,),
        in_specs=[gran_spec(s) for s in range(ITEM_GRANS)] + [
            wspec((EXPERTS_PER_GROUP, D_MODEL, D_EXPERT)), wspec((EXPERTS_PER_GROUP, D_MODEL, D_EXPERT)),
            wspec((EXPERTS_PER_GROUP, D_EXPERT, D_MODEL))],
        out_specs=pl.BlockSpec((ITEM_ROWS, D_MODEL), lambda j, src, wg, wv: (j, 0)),
        scratch_shapes=[pltpu.VMEM((EXPERTS_PER_GROUP, D_MODEL, D_EXPERT), BF16),
                        pltpu.VMEM((EXPERTS_PER_GROUP, D_MODEL, D_EXPERT), BF16),
                        pltpu.VMEM((EXPERTS_PER_GROUP, D_EXPERT, D_MODEL), BF16)])
    return pl.pallas_call(
        _experts_kernel, grid_spec=grid_spec,
        out_shape=jax.ShapeDtypeStruct((nitems * ITEM_ROWS, D_MODEL), BF16),
        name="moe_experts",
        compiler_params=pltpu.CompilerParams(dimension_semantics=("arbitrary",),
                                             vmem_limit_bytes=VMEM_LIMIT),
    )(src, wgid, wvalid, *([sorted_tok] * ITEM_GRANS), lw['w1'], lw['w3'], lw['w2'])


def _unsort_kernel(pos_ref, x1_ref, dest_ref, *refs, concat_time):
    ngran = UNSORT_TILES * TILE_GRANS
    gran_refs, (gfin_ref, out_ref) = refs[:ngran], refs[ngran:]
    if concat_time:
        bsz, tc = out_ref.shape[0], out_ref.shape[1] // UNSORT_TILES
    else:
        bsz, tc = out_ref.shape[0] // UNSORT_TILES, out_ref.shape[1]

    def tile(j):
        rsl = slice(j * TILE, (j + 1) * TILE)
        s = jnp.concatenate([g[...] for g in gran_refs[j * TILE_GRANS:(j + 1) * TILE_GRANS]], axis=0)
        slot = lax.broadcasted_iota(jnp.int32, (TILE, SORTED_ROWS), 1).astype(F32)
        perm_t = jnp.where(slot == dest_ref[rsl, :], 1.0, 0.0).astype(BF16)
        yield 1
        y = x1_ref[rsl, :] + jnp.dot(perm_t, s, preferred_element_type=F32)
        yield 2
        y = _rms(y, gfin_ref[...])
        yield 3
        for t in range(tc):
            if concat_time:
                out_ref[:, j * tc + t, :] = y[t * bsz:(t + 1) * bsz, :]
            else:
                out_ref[j * bsz:(j + 1) * bsz, t, :] = y[t * bsz:(t + 1) * bsz, :]

    _interleave([tile(j) for j in range(UNSORT_TILES)])


def _unsort(x1, dest, expert_out, pos, norm_final, *, tile_off, ntiles, batch_major):
    batch, t_len, bsz, tc = batch_major
    nchunks = t_len // tc
    assert ntiles % UNSORT_TILES == 0 and tile_off % UNSORT_TILES == 0
    concat_time = nchunks > 1
    if concat_time:
        assert nchunks % UNSORT_TILES == 0
        spb = nchunks // UNSORT_TILES
        out_spec = pl.BlockSpec((bsz, UNSORT_TILES * tc, D_MODEL), lambda k, pos: (k // spb, k % spb, 0))
    else:
        out_spec = pl.BlockSpec((UNSORT_TILES * bsz, tc, D_MODEL), lambda k, pos: (k, 0, 0))
    blk_off = tile_off // UNSORT_TILES
    gran_spec = lambda j, q: pl.BlockSpec(
        (GRAN, D_MODEL), lambda k, pos: (pos[(k * UNSORT_TILES + j + tile_off) * TILE_GRANS + q], 0))
    grid_spec = pltpu.PrefetchScalarGridSpec(
        num_scalar_prefetch=1, grid=(ntiles // UNSORT_TILES,),
        in_specs=[pl.BlockSpec((UNSORT_TILES * TILE, D_MODEL), lambda k, pos: (k + blk_off, 0)),
                  pl.BlockSpec((UNSORT_TILES * TILE, 1), lambda k, pos: (k + blk_off, 0))]
                 + [gran_spec(j, q) for j in range(UNSORT_TILES) for q in range(TILE_GRANS)]
                 + [pl.BlockSpec((1, D_MODEL), lambda k, pos: (0, 0))],
        out_specs=out_spec)
    return pl.pallas_call(
        functools.partial(_unsort_kernel, concat_time=concat_time), grid_spec=grid_spec,
        out_shape=jax.ShapeDtypeStruct((batch, t_len, D_MODEL), F32), name="moe_unsort",
        compiler_params=pltpu.CompilerParams(dimension_semantics=("arbitrary",),
                                             vmem_limit_bytes=VMEM_LIMIT),
    )(pos, x1, dest, *([expert_out] * (UNSORT_TILES * TILE_GRANS)), norm_final)


def _dispatch_tables(gid):
    ntiles = gid.shape[0]
    ngran = ntiles * TILE_GRANS
    nitems = -(-ngran // ITEM_GRANS) + N_EXPERT_GROUPS
    g = gid[:, 0, :TILE_GRANS].reshape(ngran)
    groups = jnp.arange(N_EXPERT_GROUPS, dtype=jnp.int32)
    onehot = (g[:, None] == groups[None, :]).astype(F32)
    valid = g < N_EXPERT_GROUPS
    idx = jnp.arange(ngran, dtype=jnp.int32)
    earlier = (idx[None, :] < idx[:, None]).astype(F32)
    rank = jnp.dot(earlier, onehot, precision=lax.Precision.HIGHEST).astype(jnp.int32)
    cnt = jnp.sum(onehot, axis=0).astype(jnp.int32)
    pcnt = (cnt + (ITEM_GRANS - 1)) // ITEM_GRANS * ITEM_GRANS
    pstart = jnp.sum(jnp.where(groups[:, None] < groups[None, :], pcnt[:, None], 0), axis=0)
    pend = pstart + pcnt
    pos = jnp.sum(onehot.astype(jnp.int32) * (pstart[None, :] + rank), axis=1)
    slots = jnp.arange(nitems * ITEM_GRANS, dtype=jnp.int32)
    hit = jnp.logical_and(pos[None, :] == slots[:, None], valid[None, :])
    src = jnp.sum(jnp.where(hit, jnp.arange(ngran, dtype=jnp.int32)[None, :], 0), axis=1)
    item_start = jnp.arange(nitems, dtype=jnp.int32) * ITEM_GRANS
    last_group = jnp.max(jnp.where(pcnt > 0, jnp.arange(N_EXPERT_GROUPS, dtype=jnp.int32), 0))
    wgid = jnp.minimum(jnp.sum((item_start[:, None] >= pend[None, :]).astype(jnp.int32), axis=1), last_group)
    wvalid = (item_start < pend[-1]).astype(jnp.int32)
    return src.astype(jnp.int32), wgid.astype(jnp.int32), wvalid, jnp.where(valid, pos, 0).astype(jnp.int32)


def _moe(x1, lw, l, norm_final, final_layouts=None):
    sorted_tok, dest, gid = _route_sort(x1, lw, l)
    src, wgid, wvalid, pos = _dispatch_tables(gid)
    expert_out = _experts(sorted_tok, src, wgid, wvalid, lw, l)
    if final_layouts is None:
        return dest, expert_out, pos
    return [_unsort(x1, dest, expert_out, pos, norm_final, tile_off=off, ntiles=n, batch_major=bm)
            for off, n, bm in final_layouts]


def _router_weights(w_grp, b_grp, w_er, b_er):
    pad = ROUTER_ROWS - N_EXPERTS - N_EXPERT_GROUPS
    w = jnp.pad(jnp.concatenate([w_er, w_grp], axis=-1), ((0, 0), (0, 0), (0, pad)))
    w = jnp.transpose(w, (0, 2, 1))
    b = jnp.pad(jnp.concatenate([b_er, b_grp], axis=-1), ((0, 0), (0, pad)))[:, :, None]
    hi = w.astype(BF16)
    lo = (w - hi.astype(F32)).astype(BF16)
    return jnp.concatenate([hi, lo], axis=1), b


def _to_time_major(x, nb):
    bt, t, w = x.shape
    return jnp.transpose(x.reshape(nb, bt // nb, t, w), (0, 2, 1, 3)).reshape(nb, t * (bt // nb), w)


def _from_time_major(x, t):
    nb, rows, w = x.shape
    bsz = rows // t
    return jnp.transpose(x.reshape(nb, t, bsz, w), (0, 2, 1, 3)).reshape(nb * bsz, t, w)


def kernel(x_prompt, x_sample, state_ssm_re, state_ssm_im, state_pool, norm_mix, w_in, lam_re, lam_im, log_dt, b_re, b_im, c_re, c_im, d_skip, w_glu, b_glu, w_pool, pool_scale, w_out, norm_ffn, w_grp, b_grp, w_erouter, b_erouter, w1, w3, w2, norm_final):
    depth = w_in.shape[0]
    pb, pt, _ = x_prompt.shape
    sb, st, _ = x_sample.shape
    p_tc = TILE // pb
    p_tiles = pt // p_tc
    s_bsz = TILE // st
    s_nb = sb // s_bsz
    p_rows = pb * pt

    are, aim, wx, cm_re, cm_im = _discretise(lam_re, lam_im, log_dt, b_re, b_im, c_re, c_im)

    wr, b_r = _router_weights(w_grp, b_grp, w_erouter, b_erouter)
    stack_experts = lambda w: w.reshape((depth * N_EXPERTS,) + w.shape[2:])
    vec = lambda a: a[:, None, :]
    lw = dict(norm_mix=vec(norm_mix), w_in=w_in.astype(BF16), wx=wx, are=are, aim=aim, cm_re=cm_re, cm_im=cm_im,
              d_skip=vec(d_skip), w_glu=w_glu.astype(BF16), b_glu=vec(b_glu), w_pool=w_pool.astype(BF16),
              pool_scale=vec(pool_scale), w_out=w_out.astype(BF16), norm_ffn=vec(norm_ffn),
              wr=wr, b_r=b_r,
              w1=stack_experts(w1), w3=stack_experts(w3), w2=stack_experts(w2))
    nfin = norm_final[None]

    p_zero_h = jnp.zeros((1, pb, S5_LANES), F32)
    p_zero_hist = jnp.zeros((1, POOL_HIST * pb, POOL_WIDTH), F32)
    final_layouts = [(0, p_tiles, (pb, pt, pb, p_tc)), (p_tiles, s_nb, (sb, st, s_bsz, st))]

    outs = {k: [] for k in ('p_re', 'p_im', 'p_pool', 's_re', 's_im', 's_pool')}
    x, moe = None, None
    for l in range(depth):
        h0re = state_ssm_re[l].reshape(s_nb, s_bsz, S5_LANES)
        h0im = state_ssm_im[l].reshape(s_nb, s_bsz, S5_LANES)
        hist = jnp.pad(_to_time_major(state_pool[l], s_nb), ((0, 0), ((POOL_HIST - POOL_BUF) * s_bsz, 0), (0, 0)))
        xs1, hre, him, pool = _mixer(x_sample if l == 0 else x, h0re, h0im, hist, lw, l, bsz=s_bsz, tc=st,
                                     past=POOL_BUF, nb=s_nb, nchunks=1, halves=1, in_off=p_tiles, moe=moe)
        outs['s_re'].append(hre.reshape(sb, S5_GROUPS, S5_STATE))
        outs['s_im'].append(him.reshape(sb, S5_GROUPS, S5_STATE))
        outs['s_pool'].append(_from_time_major(pool, POOL_BUF))
        x1, hre, him, pool = _mixer(x_prompt if l == 0 else x, p_zero_h, p_zero_h, p_zero_hist, lw, l,
                                    bsz=pb, tc=p_tc, past=0, nb=1, nchunks=p_tiles, halves=2, in_off=0, tail=xs1, moe=moe)
        outs['p_re'].append(hre.reshape(pb, S5_GROUPS, S5_STATE))
        outs['p_im'].append(him.reshape(pb, S5_GROUPS, S5_STATE))
        outs['p_pool'].append(_from_time_major(pool, POOL_BUF))
        if l < depth - 1:
            x, moe = x1, _moe(x1, lw, l, nfin)
        else:
            y_prompt, y_sample = _moe(x1, lw, l, nfin, final_layouts)

    return (y_prompt, y_sample, jnp.stack(outs['p_re']), jnp.stack(outs['p_im']), jnp.stack(outs['p_pool']),
            jnp.stack(outs['s_re']), jnp.stack(outs['s_im']), jnp.stack(outs['s_pool']))
```
